```python
import math
import jax, jax.numpy as jnp
from jax import lax
import numpy as np

D_MODEL = 1024
BATCH = 2
SEQ = 8192
DEPTH = 2
DEC_BATCH = 128
DEC_SEQ = 4
PAST_LEN = 8192
PAGE_SIZE = 128

N_EVEN = (DEPTH + 1) // 2
N_ODD = DEPTH // 2
N_MOD = 9
EPS = 1e-6
NEG = -1e30
GLA_HEADS = 4
GLA_DK = D_MODEL // 16
GLA_DV = D_MODEL // 8
GLA_GATE_RANK = 16
GLA_TAU = 16.0
GLA_CHUNK = 64
POOL_WINDOWS = (2, 4, 8, 16)
POOL_GROUP = D_MODEL // 8
POOL_WIDTH = len(POOL_WINDOWS) * POOL_GROUP
POOL_HIST = max(POOL_WINDOWS) - 1
MLA_HEADS = 8
MLA_Q_RANK = 3 * D_MODEL // 8
MLA_KV_RANK = D_MODEL // 4
MLA_NOPE = D_MODEL // 16
MLA_ROPE = D_MODEL // 32
MLA_V = D_MODEL // 16
MLA_SCALE = (MLA_NOPE + MLA_ROPE) ** -0.5
ROPE_BASE = 10000.0
Q_BLOCK = 128
CONV_WIDTH = 31
CONV_CH = D_MODEL // 2
CONV_HIST = CONV_WIDTH - 1
D_FF = 2816
EV_IN_SIZES = (GLA_HEADS * GLA_DK, GLA_HEADS * GLA_DK, GLA_HEADS * GLA_DV, GLA_HEADS * GLA_DV, GLA_GATE_RANK, POOL_WIDTH)
EV_IN = sum(EV_IN_SIZES)
EV_OUT = GLA_HEADS * GLA_DV + POOL_WIDTH
OD_IN_SIZES = (MLA_Q_RANK, MLA_KV_RANK, MLA_ROPE, 2 * CONV_CH)
OD_IN = sum(OD_IN_SIZES)
OD_OUT = MLA_HEADS * MLA_V + CONV_CH

kernel_name = 'hybrid_gla_pool_mla_conv_decoder_step'


def split_cols(z, sizes):
    idx = np.cumsum(sizes)[:-1].tolist()
    return jnp.split(z, idx, axis=-1)


def rmsnorm(x, g):
    xf = x.astype(jnp.float32)
    y = xf * lax.rsqrt(jnp.mean(xf * xf, axis=-1, keepdims=True) + EPS)
    return (y * g.astype(jnp.float32)).astype(x.dtype)


def layernorm(x, g, b):
    xf = x.astype(jnp.float32)
    mu = jnp.mean(xf, axis=-1, keepdims=True)
    var = jnp.mean(jnp.square(xf - mu), axis=-1, keepdims=True)
    y = (xf - mu) * lax.rsqrt(var + EPS)
    return (y * g.astype(jnp.float32) + b.astype(jnp.float32)).astype(x.dtype)


def rope(x, pos):
    half = x.shape[-1] // 2
    freqs = ROPE_BASE ** (-jnp.arange(half, dtype=jnp.float32) / half)
    ang = pos.astype(jnp.float32)[:, None] * freqs[None, :]
    shape = (pos.shape[0],) + (1,) * (x.ndim - 3) + (half,)
    cos = jnp.cos(ang).reshape(shape)
    sin = jnp.sin(ang).reshape(shape)
    xf = x.astype(jnp.float32)
    x1, x2 = xf[..., :half], xf[..., half:]
    return jnp.concatenate([x1 * cos - x2 * sin, x2 * cos + x1 * sin], axis=-1).astype(x.dtype)


def modulate(x, g, shift, scale):
    return rmsnorm(x, g) * (1.0 + scale[:, None, :]) + shift[:, None, :]


def swiglu(h, w1, w3, w2):
    return (jax.nn.silu(h @ w1) * (h @ w3)) @ w2


def gla_recurrence(q, k, v, log_a, s0):
    B, T, H, _ = q.shape
    DV = v.shape[-1]
    C = GLA_CHUNK if T % GLA_CHUNK == 0 else T
    n = T // C

    def chunks(t):
        return t.astype(jnp.float32).reshape(B, n, C, H, t.shape[-1]).transpose(1, 0, 3, 2, 4)

    causal = jnp.tril(jnp.ones((C, C), dtype=bool))

    def step(S, inp):
        qc, kc, vc, ac = inp
        b = jnp.cumsum(ac, axis=2)
        b_last = b[:, :, -1:, :]
        qi = qc * jnp.exp(b)
        ki = kc * jnp.exp(-b)
        att = jnp.where(causal, jnp.einsum('bhtk,bhsk->bhts', qi, ki), 0.0)
        o = jnp.einsum('bhtk,bhkv->bhtv', qi, S) + jnp.einsum('bhts,bhsv->bhtv', att, vc)
        S = jnp.exp(b_last)[:, :, 0, :, None] * S + jnp.einsum('bhsk,bhsv->bhkv', kc * jnp.exp(b_last - b), vc)
        return S, o

    S, o = lax.scan(step, s0.astype(jnp.float32), (chunks(q), chunks(k), chunks(v), chunks(log_a)))
    o = o.transpose(1, 0, 3, 2, 4).reshape(B, T, H, DV)
    return o.astype(v.dtype), S.astype(s0.dtype)


def pool_mix(u, hist, p0, pool_w, pool_scale):
    B, T, _ = u.shape
    L = POOL_HIST
    full = jnp.concatenate([hist, u], axis=1)
    cs = jnp.cumsum(full.astype(jnp.float32), axis=1)
    cs = jnp.concatenate([jnp.zeros((B, 1, POOL_WIDTH), jnp.float32), cs], axis=1)
    t = jnp.arange(T)
    means = []
    for g, w in enumerate(POOL_WINDOWS):
        sl = slice(g * POOL_GROUP, (g + 1) * POOL_GROUP)
        win = cs[:, L + 1:, sl] - cs[:, L + 1 - w:L + 1 - w + T, sl]
        cnt = jnp.minimum(p0 + t + 1, w).astype(jnp.float32)
        means.append(win / cnt[None, :, None])
    pooled = jnp.concatenate(means, axis=-1).astype(u.dtype) - u
    mixed = jnp.einsum('btgc,gcd->btgd', pooled.reshape(B, T, len(POOL_WINDOWS), POOL_GROUP), pool_w)
    return mixed.reshape(B, T, POOL_WIDTH) * pool_scale, full[:, -L:]


def even_mixer(h, s0, hist, p0, w_in, gate_w2, gate_b, out_norm, pool_w, pool_scale, w_out):
    B, T, _ = h.shape
    q, k, v, r, g_low, u = split_cols(h @ w_in, EV_IN_SIZES)
    log_a = jax.nn.log_sigmoid((g_low @ gate_w2 + gate_b).astype(jnp.float32)) / GLA_TAU
    hd = lambda t, d: t.reshape(B, T, GLA_HEADS, d)
    o, S = gla_recurrence(hd(q, GLA_DK) * (GLA_DK ** -0.5), hd(k, GLA_DK), hd(v, GLA_DV), hd(log_a, GLA_DK), s0)
    o = rmsnorm(o, out_norm.reshape(GLA_HEADS, GLA_DV)).reshape(B, T, GLA_HEADS * GLA_DV) * jax.nn.silu(r)
    pooled, hist_new = pool_mix(u, hist, p0, pool_w, pool_scale)
    y = jnp.concatenate([o, pooled], axis=-1) @ w_out
    return y, S, hist_new


def mla_scores(q_lat, q_rope, ckv, kr):
    s = jnp.einsum('bqhr,bkr->bhqk', q_lat, ckv, preferred_element_type=jnp.float32)
    s = s + jnp.einsum('bqhd,bkd->bhqk', q_rope, kr, preferred_element_type=jnp.float32)
    return s * MLA_SCALE


def mla_prompt(q_lat, q_rope, ckv, kr):
    B, T, H, R = q_lat.shape
    blk = Q_BLOCK if T % Q_BLOCK == 0 else T
    nb = T // blk
    kpos = jnp.arange(T)

    def one(args):
        ql, qr, qpos = args
        s = mla_scores(ql, qr, ckv, kr)
        s = jnp.where(kpos[None, :] <= qpos[:, None], s, NEG)
        p = jax.nn.softmax(s, axis=-1).astype(ckv.dtype)
        return jnp.einsum('bhqk,bkr->bqhr', p, ckv)

    to_blocks = lambda t: t.reshape((B, nb, blk) + t.shape[2:]).swapaxes(0, 1)
    out = lax.map(one, (to_blocks(q_lat), to_blocks(q_rope), kpos.reshape(nb, blk)))
    return out.swapaxes(0, 1).reshape(B, T, H, R)


def mla_sample(q_lat, q_rope, ckv, kr, past_ckv, past_kr):
    T = q_lat.shape[1]
    P = past_ckv.shape[1]
    s_past = mla_scores(q_lat, q_rope, past_ckv, past_kr)
    s_new = mla_scores(q_lat, q_rope, ckv, kr)
    s_new = jnp.where(jnp.tril(jnp.ones((T, T), dtype=bool)), s_new, NEG)
    p = jax.nn.softmax(jnp.concatenate([s_past, s_new], axis=-1), axis=-1).astype(ckv.dtype)
    return (jnp.einsum('bhqk,bkr->bqhr', p[..., :P], past_ckv)
            + jnp.einsum('bhqk,bkr->bqhr', p[..., P:], ckv))


def odd_mixer(h, conv_hist, p0, past, w_in, q_norm, w_uq, kv_norm, w_uk, w_uv,
              conv_w, conv_b, conv_norm_g, conv_norm_b, w_out):
    B, T, _ = h.shape
    cq, ckv_raw, kr_raw, glu = split_cols(h @ w_in, OD_IN_SIZES)
    pos = p0 + jnp.arange(T)
    q = (rmsnorm(cq, q_norm) @ w_uq).reshape(B, T, MLA_HEADS, MLA_NOPE + MLA_ROPE)
    q_nope, q_rope = q[..., :MLA_NOPE], rope(q[..., MLA_NOPE:], pos)
    q_lat = jnp.einsum('bthd,rhd->bthr', q_nope, w_uk)
    ckv = rmsnorm(ckv_raw, kv_norm)
    kr = rope(kr_raw, pos)
    if past is None:
        lat = mla_prompt(q_lat, q_rope, ckv, kr)
    else:
        lat = mla_sample(q_lat, q_rope, ckv, kr, past[0], past[1])
    attn = jnp.einsum('bthr,rhv->bthv', lat, w_uv).reshape(B, T, MLA_HEADS * MLA_V)
    a, gt = jnp.split(glu, 2, axis=-1)
    u = a * jax.nn.sigmoid(gt)
    full = jnp.concatenate([conv_hist, u], axis=1)
    cv = lax.conv_general_dilated(full, conv_w[:, None, :], window_strides=(1,), padding='VALID',
                                  dimension_numbers=('NWC', 'WIO', 'NWC'), feature_group_count=CONV_CH)
    cv = jax.nn.silu(layernorm(cv + conv_b, conv_norm_g, conv_norm_b))
    y = jnp.concatenate([attn, cv], axis=-1) @ w_out
    return y, ckv, kr, full[:, -CONV_HIST:]


def forward(x, c, p0, gla_init, pool_init, conv_init, mla_past, p):
    B = x.shape[0]
    new_gla, new_pool, new_ckv, new_kr, new_conv = [], [], [], [], []
    for layer in range(DEPTH):
        mod = (jax.nn.silu(c) @ p['ada_w'][layer] + p['ada_b'][layer]).reshape(B, N_MOD, D_MODEL)
        h = modulate(x, p['norm_g'][layer, 0], mod[:, 0], mod[:, 1])
        x = x + 0.5 * mod[:, 2][:, None, :] * swiglu(h, p['ffn_w1'][layer, 0], p['ffn_w3'][layer, 0], p['ffn_w2'][layer, 0])
        h = modulate(x, p['norm_g'][layer, 1], mod[:, 3], mod[:, 4])
        i = layer // 2
        if layer % 2 == 0:
            y, S, hist = even_mixer(h, gla_init[i], pool_init[i], p0, p['ev_w_in'][i], p['ev_gate_w2'][i], p['ev_gate_b'][i],
                                    p['ev_out_norm'][i], p['ev_pool_w'][i], p['ev_pool_scale'][i], p['ev_w_out'][i])
            new_gla.append(S)
            new_pool.append(hist)
        else:
            past = None if mla_past is None else mla_past[i]
            y, ckv, kr, chist = odd_mixer(h, conv_init[i], p0, past, p['od_w_in'][i], p['od_q_norm'][i], p['od_w_uq'][i],
                                          p['od_kv_norm'][i], p['od_w_uk'][i], p['od_w_uv'][i], p['od_conv_w'][i],
                                          p['od_conv_b'][i], p['od_conv_norm_g'][i], p['od_conv_norm_b'][i], p['od_w_out'][i])
            new_ckv.append(ckv)
            new_kr.append(kr)
            new_conv.append(chist)
        x = x + mod[:, 5][:, None, :] * y
        h = modulate(x, p['norm_g'][layer, 2], mod[:, 6], mod[:, 7])
        x = x + 0.5 * mod[:, 8][:, None, :] * swiglu(h, p['ffn_w1'][layer, 1], p['ffn_w3'][layer, 1], p['ffn_w2'][layer, 1])
    y = rmsnorm(x, p['final_norm'])
    return (y, jnp.stack(new_gla), jnp.stack(new_pool), jnp.stack(new_ckv), jnp.stack(new_kr), jnp.stack(new_conv))


def setup_inputs(seed: int = 0) -> dict:
    key = jax.random.key(seed)
    ks = iter(jax.random.split(key, 64))
    nrm = lambda shape, s: jax.random.normal(next(ks), shape, jnp.float32) * s
    D = D_MODEL
    n_pages = PAST_LEN // PAGE_SIZE
    n_used = DEC_BATCH * n_pages
    n_pool = (5 * n_used + 3) // 4
    x_prompt = nrm((BATCH, SEQ, D), 1.0)
    x_sample = nrm((DEC_BATCH, DEC_SEQ, D), 1.0)
    state_gla = nrm((N_EVEN, DEC_BATCH, GLA_HEADS, GLA_DK, GLA_DV), 0.5)
    state_pool = nrm((N_EVEN, DEC_BATCH, POOL_HIST, POOL_WIDTH), 1.0)
    cache_ckv = nrm((N_ODD, n_pool, PAGE_SIZE, MLA_KV_RANK), 1.0)
    cache_krope = nrm((N_ODD, n_pool, PAGE_SIZE, MLA_ROPE), 1.0)
    state_conv = nrm((N_ODD, DEC_BATCH, CONV_HIST, CONV_CH), 0.5)
    page_table = jax.random.permutation(next(ks), n_pool)[:n_used].astype(jnp.int32).reshape(DEC_BATCH, n_pages)
    c_prompt = nrm((BATCH, D), 1.0)
    c_sample = nrm((DEC_BATCH, D), 1.0)
    return {
        'x_prompt': x_prompt, 'x_sample': x_sample,
        'state_gla': state_gla, 'state_pool': state_pool,
        'cache_ckv': cache_ckv, 'cache_krope': cache_krope, 'state_conv': state_conv,
        'page_table': page_table, 'c_prompt': c_prompt, 'c_sample': c_sample,
        'ada_w': nrm((DEPTH, D, N_MOD * D), D ** -0.5),
        'ada_b': nrm((DEPTH, N_MOD * D), 0.02),
        'norm_g': 1.0 + nrm((DEPTH, 3, D), 0.02),
        'ffn_w1': nrm((DEPTH, 2, D, D_FF), D ** -0.5),
        'ffn_w3': nrm((DEPTH, 2, D, D_FF), D ** -0.5),
        'ffn_w2': nrm((DEPTH, 2, D_FF, D), D_FF ** -0.5),
        'ev_w_in': nrm((N_EVEN, D, EV_IN), D ** -0.5),
        'ev_gate_w2': nrm((N_EVEN, GLA_GATE_RANK, GLA_HEADS * GLA_DK), GLA_GATE_RANK ** -0.5),
        'ev_gate_b': nrm((N_EVEN, GLA_HEADS * GLA_DK), 0.02),
        'ev_out_norm': 1.0 + nrm((N_EVEN, GLA_HEADS * GLA_DV), 0.02),
        'ev_pool_w': nrm((N_EVEN, len(POOL_WINDOWS), POOL_GROUP, POOL_GROUP), POOL_GROUP ** -0.5),
        'ev_pool_scale': 1.0 + nrm((N_EVEN, POOL_WIDTH), 0.1),
        'ev_w_out': nrm((N_EVEN, EV_OUT, D), EV_OUT ** -0.5),
        'od_w_in': nrm((N_ODD, D, OD_IN), D ** -0.5),
        'od_q_norm': 1.0 + nrm((N_ODD, MLA_Q_RANK), 0.02),
        'od_w_uq': nrm((N_ODD, MLA_Q_RANK, MLA_HEADS * (MLA_NOPE + MLA_ROPE)), MLA_Q_RANK ** -0.5),
        'od_kv_norm': 1.0 + nrm((N_ODD, MLA_KV_RANK), 0.02),
        'od_w_uk': nrm((N_ODD, MLA_KV_RANK, MLA_HEADS, MLA_NOPE), MLA_KV_RANK ** -0.5),
        'od_w_uv': nrm((N_ODD, MLA_KV_RANK, MLA_HEADS, MLA_V), MLA_KV_RANK ** -0.5),
        'od_conv_w': nrm((N_ODD, CONV_WIDTH, CONV_CH), CONV_WIDTH ** -0.5),
        'od_conv_b': nrm((N_ODD, CONV_CH), 0.02),
        'od_conv_norm_g': 1.0 + nrm((N_ODD, CONV_CH), 0.02),
        'od_conv_norm_b': nrm((N_ODD, CONV_CH), 0.02),
        'od_w_out': nrm((N_ODD, OD_OUT, D), OD_OUT ** -0.5),
        'final_norm': 1.0 + nrm((D,), 0.02),
    }


def reference(x_prompt, x_sample, state_gla, state_pool, cache_ckv, cache_krope, state_conv, page_table,
              c_prompt, c_sample, ada_w, ada_b, norm_g, ffn_w1, ffn_w3, ffn_w2,
              ev_w_in, ev_gate_w2, ev_gate_b, ev_out_norm, ev_pool_w, ev_pool_scale, ev_w_out,
              od_w_in, od_q_norm, od_w_uq, od_kv_norm, od_w_uk, od_w_uv, od_conv_w, od_conv_b,
              od_conv_norm_g, od_conv_norm_b, od_w_out, final_norm):
    p = dict(ada_w=ada_w, ada_b=ada_b, norm_g=norm_g, ffn_w1=ffn_w1, ffn_w3=ffn_w3, ffn_w2=ffn_w2,
             ev_w_in=ev_w_in, ev_gate_w2=ev_gate_w2, ev_gate_b=ev_gate_b, ev_out_norm=ev_out_norm,
             ev_pool_w=ev_pool_w, ev_pool_scale=ev_pool_scale, ev_w_out=ev_w_out,
             od_w_in=od_w_in, od_q_norm=od_q_norm, od_w_uq=od_w_uq, od_kv_norm=od_kv_norm,
             od_w_uk=od_w_uk, od_w_uv=od_w_uv, od_conv_w=od_conv_w, od_conv_b=od_conv_b,
             od_conv_norm_g=od_conv_norm_g, od_conv_norm_b=od_conv_norm_b, od_w_out=od_w_out,
             final_norm=final_norm)
    B = x_prompt.shape[0]
    DB = x_sample.shape[0]
    dt = x_prompt.dtype
    gla0 = jnp.zeros((N_EVEN, B, GLA_HEADS, GLA_DK, GLA_DV), state_gla.dtype)
    pool0 = jnp.zeros((N_EVEN, B, POOL_HIST, POOL_WIDTH), dt)
    conv0 = jnp.zeros((N_ODD, B, CONV_HIST, CONV_CH), dt)
    y_prompt, gla_p, pool_p, ckv_p, kr_p, conv_p = forward(x_prompt, c_prompt, 0, gla0, pool0, conv0, None, p)
    past_len = page_table.shape[1] * cache_ckv.shape[2]
    mla_past = [(cache_ckv[i, page_table].reshape(DB, past_len, MLA_KV_RANK),
                 cache_krope[i, page_table].reshape(DB, past_len, MLA_ROPE)) for i in range(N_ODD)]
    y_sample, gla_s, pool_s, ckv_s, kr_s, conv_s = forward(x_sample, c_sample, past_len, state_gla, state_pool,
                                                           state_conv, mla_past, p)
    return (y_prompt, y_sample, gla_p, gla_s, pool_p, pool_s, ckv_p, ckv_s, kr_p, kr_s, conv_p, conv_s)
```

```python
import functools

import jax
import jax.numpy as jnp
import numpy as np
from jax import lax
from jax.experimental import pallas as pl
from jax.experimental.pallas import tpu as pltpu

F32 = jnp.float32
BF16 = jnp.bfloat16

EPS = 1e-6
NEG = -1e30
N_MOD = 9
GLA_HEADS = 4
GLA_DK = 64
GLA_DV = 128
GLA_GATE_RANK = 16
GLA_TAU = 16.0
GLA_CHUNK = 64
POOL_WINDOWS = (2, 4, 8, 16)
POOL_GROUP = 128
POOL_WIDTH = 512
POOL_HIST = 15
MLA_HEADS = 8
MLA_Q_RANK = 384
MLA_KV_RANK = 256
MLA_NOPE = 64
MLA_ROPE = 32
MLA_V = 64
MLA_SCALE = (MLA_NOPE + MLA_ROPE) ** -0.5
ROPE_BASE = 10000.0
CONV_WIDTH = 31
CONV_CH = 512
CONV_HIST = 30

LANES = 128
ROPE_LANE0 = 64
POOL_PAD = 16
CONV_PAD = 32
VMEM_LIMIT = 52 * 2 ** 20
TOKEN_TILE = 512
ATTN_TILE = 512
FF_TILE = 1408
PAGES_PER_STEP = 16
GLA_SAMPLE_BATCHES = 4


def _cp(*sem):
    return pltpu.CompilerParams(dimension_semantics=sem, vmem_limit_bytes=VMEM_LIMIT)


def _dot(a, b):
    return jnp.dot(a, b, preferred_element_type=F32)


def _dot_nt(a, b):
    return lax.dot_general(a, b, (((1,), (1,)), ((), ())), preferred_element_type=F32)


def _dot_tn(a, b):
    return lax.dot_general(a, b, (((0,), (0,)), ((), ())), preferred_element_type=F32)


def _silu(x):
    return x * jax.nn.sigmoid(x)


def _rms(x, g):
    return x * lax.rsqrt(jnp.mean(x * x, axis=-1, keepdims=True) + EPS) * g


def _modulated(x_ref, g_ref, sh_ref, sc_ref):
    x = x_ref[...]
    h = _rms(x, g_ref[...]) * (1.0 + sc_ref[...]) + sh_ref[...]
    return h.reshape(x.shape[0] * x.shape[1], x.shape[2]).astype(BF16)


class _Group:
    def __init__(self, prompt, lead, rows, tile):
        self.prompt = prompt
        self.grid = (lead, rows // tile) if prompt else (1, 1)
        self.block = (1, tile) if prompt else (lead, rows)

    def act(self, width):
        return pl.BlockSpec(self.block + (width,), lambda b, t, *_: (b, t, 0))

    def mod(self, k):
        if self.prompt:
            return pl.BlockSpec((None, None, 1, self.dm), lambda b, t, *_: (b, k, 0, 0))
        return pl.BlockSpec((None, self.block[1], self.dm), lambda b, t, *_: (k, 0, 0))

    def pos(self):
        if self.prompt:
            return pl.BlockSpec((1, self.block[1], LANES), lambda b, t, *_: (0, t, 0))
        return pl.BlockSpec((self.block[0], 1, LANES), lambda b, t, *_: (0, 0, 0))

    dm = 1024


def _full(a):
    nd = a.ndim
    return pl.BlockSpec(a.shape, lambda *_: (0,) * nd)


def _sds(shape, dtype=F32):
    return jax.ShapeDtypeStruct(shape, dtype)


def _ada_kernel(c_ref, w_ref, b_ref, o_ref):
    c = c_ref[...]
    o_ref[...] = _dot(_silu(c).astype(BF16), w_ref[...].astype(BF16)) + b_ref[...]


def _ada(c_all, ada_w, ada_b):
    depth, dm, n = ada_w.shape
    m = c_all.shape[0]
    tn = dm
    return pl.pallas_call(
        _ada_kernel,
        grid=(depth, n // tn),
        in_specs=[pl.BlockSpec((m, dm), lambda l, j: (0, 0)),
                  pl.BlockSpec((None, dm, tn), lambda l, j: (l, 0, j)),
                  pl.BlockSpec((None, 1, tn), lambda l, j: (l, 0, j))],
        out_specs=pl.BlockSpec((None, m, tn), lambda l, j: (l, 0, j)),
        out_shape=_sds((depth, m, n)),
        compiler_params=_cp("parallel", "parallel"),
    )(c_all, ada_w, ada_b.reshape(depth, 1, n))


def _ffn_kernel(*refs, final):
    if final:
        x_ref, sh_ref, sc_ref, gt_ref, g_ref, w1_ref, w3_ref, w2_ref, fn_ref, o_ref, h_scr, acc_scr = refs
    else:
        x_ref, sh_ref, sc_ref, gt_ref, g_ref, w1_ref, w3_ref, w2_ref, o_ref, h_scr, acc_scr = refs
    j = pl.program_id(2)

    @pl.when(j == 0)
    def _():
        h_scr[...] = _modulated(x_ref, g_ref, sh_ref, sc_ref)
        acc_scr[...] = jnp.zeros_like(acc_scr)

    h = h_scr[...]
    a = _dot(h, w1_ref[...])
    b = _dot(h, w3_ref[...])
    acc_scr[...] += _dot((_silu(a) * b).astype(BF16), w2_ref[...])

    @pl.when(j == pl.num_programs(2) - 1)
    def _():
        x = x_ref[...]
        xn = x + 0.5 * gt_ref[...] * acc_scr[...].reshape(x.shape)
        o_ref[...] = _rms(xn, fn_ref[...]) if final else xn


def _ffn(grp, x, mod, k0, g, w1, w3, w2, final_g=None):
    dm, dff = w1.shape
    tf = FF_TILE if dff % FF_TILE == 0 else dff
    rows = grp.block[0] * grp.block[1]
    final = final_g is not None
    ins = [x, mod, mod, mod, g, w1, w3, w2] + ([final_g] if final else [])
    specs = [grp.act(dm), grp.mod(k0), grp.mod(k0 + 1), grp.mod(k0 + 2), _full(g),
             pl.BlockSpec((dm, tf), lambda b, t, j: (0, j)),
             pl.BlockSpec((dm, tf), lambda b, t, j: (0, j)),
             pl.BlockSpec((tf, dm), lambda b, t, j: (j, 0))] + ([_full(final_g)] if final else [])
    return pl.pallas_call(
        functools.partial(_ffn_kernel, final=final),
        grid=grp.grid + (dff // tf,),
        in_specs=specs,
        out_specs=grp.act(dm),
        out_shape=_sds(x.shape),
        scratch_shapes=[pltpu.VMEM((rows, dm), BF16), pltpu.VMEM((rows, dm), F32)],
        compiler_params=_cp("parallel", "parallel", "arbitrary"),
    )(*ins)


def _log_sigmoid(x):
    return jnp.minimum(x, 0.0) - jnp.log(1.0 + jnp.exp(-jnp.abs(x)))


def _even_in_kernel(x_ref, sh_ref, sc_ref, g_ref, wq_ref, wg_ref, w2_ref, gb_ref, wu_ref,
                    q_ref, k_ref, v_ref, r_ref, la_ref, u_ref):
    h = _modulated(x_ref, g_ref, sh_ref, sc_ref)
    lead = x_ref.shape[:2]
    hk = GLA_HEADS * GLA_DK
    hv = GLA_HEADS * GLA_DV
    z = _dot(h, wq_ref[...])
    q_ref[...] = (z[:, :hk] * GLA_DK ** -0.5).reshape(lead + (hk,))
    k_ref[...] = z[:, hk:2 * hk].reshape(lead + (hk,))
    v_ref[...] = z[:, 2 * hk:2 * hk + hv].reshape(lead + (hv,))
    r_ref[...] = z[:, 2 * hk + hv:].reshape(lead + (hv,))
    g_low = _dot(h, wg_ref[...]).astype(BF16)
    gate = _dot(g_low, w2_ref[...]) + gb_ref[...]
    la_ref[...] = (_log_sigmoid(gate) / GLA_TAU).reshape(lead + (hk,))
    u_ref[...] = _dot(h, wu_ref[...]).reshape(lead + (POOL_WIDTH,))


def _even_in(grp, x, mod, g, wq, wg, w2, gb, wu):
    hk = GLA_HEADS * GLA_DK
    hv = GLA_HEADS * GLA_DV
    widths = (hk, hk, hv, hv, hk, POOL_WIDTH)
    return pl.pallas_call(
        _even_in_kernel,
        grid=grp.grid,
        in_specs=[grp.act(x.shape[-1]), grp.mod(3), grp.mod(4), _full(g), _full(wq), _full(wg),
                  _full(w2), _full(gb), _full(wu)],
        out_specs=[grp.act(w) for w in widths],
        out_shape=[_sds(x.shape[:2] + (w,)) for w in widths],
        compiler_params=_cp("parallel", "parallel"),
    )(x, mod, mod, g, wq, wg, w2, gb, wu)


def _cumsum_rows(tril, x):
    hi = x.astype(BF16)
    r1 = x - hi.astype(F32)
    mid = r1.astype(BF16)
    lo = (r1 - mid.astype(F32)).astype(BF16)
    return _dot(tril, hi) + _dot(tril, mid) + _dot(tril, lo)


def _gla_kernel(*refs, nbb, tt, has_s0):
    if has_s0:
        q_ref, k_ref, v_ref, la_ref, s0_ref, o_ref, so_ref, s_scr = refs
    else:
        q_ref, k_ref, v_ref, la_ref, o_ref, so_ref, s_scr = refs
    t = pl.program_id(1)
    c_rows = GLA_CHUNK

    @pl.when(t == 0)
    def _():
        s_scr[...] = s0_ref[...] if has_s0 else jnp.zeros_like(s_scr)

    row = lax.broadcasted_iota(jnp.int32, (c_rows, c_rows), 0)
    col = lax.broadcasted_iota(jnp.int32, (c_rows, c_rows), 1)
    causal = col <= row
    tril = jnp.where(causal, 1.0, 0.0).astype(BF16)
    lane = lax.broadcasted_iota(jnp.int32, (c_rows, LANES), 1)
    srow = lax.broadcasted_iota(jnp.int32, (LANES, LANES), 0)

    def chunk(c, carry):
        rows = pl.ds(pl.multiple_of(c * c_rows, c_rows), c_rows)
        for n in range(nbb):
            for p in range(GLA_HEADS // 2):
                sl = slice(LANES * p, LANES * (p + 1))
                b = _cumsum_rows(tril, la_ref[n, rows, sl])
                b_last = b[c_rows - 1:c_rows, :]
                k = k_ref[n, rows, sl]
                qi = q_ref[n, rows, sl] * jnp.exp(b)
                ki = (k * jnp.exp(-b)).astype(BF16)
                kd = (k * jnp.exp(b_last - b)).astype(BF16)
                s_old = s_scr[n, p]
                s_bf = s_old.astype(BF16)
                decay = jnp.transpose(jnp.broadcast_to(jnp.exp(b_last), (LANES, LANES)))
                upd = []
                for hh in range(2):
                    mine = (lane < GLA_DK) if hh == 0 else (lane >= GLA_DK)
                    qh = jnp.where(mine, qi, 0.0).astype(BF16)
                    att = jnp.where(causal, _dot_nt(qh, ki), 0.0).astype(BF16)
                    vs = slice(GLA_DV * (2 * p + hh), GLA_DV * (2 * p + hh + 1))
                    vh = v_ref[n, rows, vs].astype(BF16)
                    o_ref[n, rows, vs] = _dot(qh, s_bf) + _dot(att, vh)
                    upd.append(_dot_tn(kd, vh))
                s_scr[n, p] = decay * s_old + jnp.where(srow < GLA_DK, upd[0], upd[1])
        return carry

    lax.fori_loop(0, tt // c_rows, chunk, 0)

    @pl.when(t == pl.num_programs(1) - 1)
    def _():
        so_ref[...] = s_scr[...]


def _gla(q, k, v, la, s0, nbb, tt):
    nb, nt, hk = q.shape
    hv = v.shape[-1]
    has_s0 = s0 is not None
    act = lambda w: pl.BlockSpec((nbb, tt, w), lambda b, t: (b, t, 0))
    st = pl.BlockSpec((nbb, GLA_HEADS // 2, LANES, GLA_DV), lambda b, t: (b, 0, 0, 0))
    return pl.pallas_call(
        functools.partial(_gla_kernel, nbb=nbb, tt=tt, has_s0=has_s0),
        grid=(nb // nbb, nt // tt),
        in_specs=[act(hk), act(hk), act(hv), act(hk)] + ([st] if has_s0 else []),
        out_specs=[act(hv), st],
        out_shape=[_sds((nb, nt, hv)), _sds((nb, GLA_HEADS // 2, LANES, GLA_DV))],
        scratch_shapes=[pltpu.VMEM((nbb, GLA_HEADS // 2, LANES, GLA_DV), F32)],
        compiler_params=_cp("parallel", "arbitrary"),
    )(*([q, k, v, la] + ([s0] if has_s0 else [])))


def _gla_gate(o, r, on):
    outs = []
    for h in range(GLA_HEADS):
        sl = slice(GLA_DV * h, GLA_DV * (h + 1))
        outs.append(_rms(o[:, sl], on[:, sl]))
    return jnp.concatenate(outs, axis=-1) * _silu(r)


def _even_mix_out(x_ref, gt_ref, og, pooled, pw_ref, ps_ref, wo_ref, xo_ref):
    mixed = [_dot(pooled[g].astype(BF16), pw_ref[g]) for g in range(len(POOL_WINDOWS))]
    mixed = jnp.concatenate(mixed, axis=-1) * ps_ref[...]
    n_o = og.shape[-1]
    y = _dot(og.astype(BF16), wo_ref[:n_o, :]) + _dot(mixed.astype(BF16), wo_ref[n_o:, :])
    x = x_ref[...]
    xo_ref[...] = x + gt_ref[...] * y.reshape(x.shape)


def _even_out_prompt_kernel(x_ref, o_ref, r_ref, u_ref, gt_ref, on_ref, pw_ref, ps_ref, wo_ref,
                            xo_ref, ho_ref, hb, *, tm, p0):
    t = pl.program_id(1)

    @pl.when(t == 0)
    def _():
        hb[0:POOL_PAD, :] = jnp.zeros((POOL_PAD, POOL_WIDTH), F32)

    u = u_ref[0]
    hb[POOL_PAD:POOL_PAD + tm, :] = u
    pos = t * tm + lax.broadcasted_iota(jnp.int32, (tm, 1), 0)
    pooled = []
    for g, w in enumerate(POOL_WINDOWS):
        sl = slice(POOL_GROUP * g, POOL_GROUP * (g + 1))
        win = hb[POOL_PAD - (w - 1):POOL_PAD - (w - 1) + tm, sl]
        for j in range(w - 2, -1, -1):
            win = win + hb[POOL_PAD - j:POOL_PAD - j + tm, sl]
        cnt = jnp.minimum(p0 + pos + 1, w).astype(F32)
        pooled.append(win / cnt - u[:, sl])
    og = _gla_gate(o_ref[0], r_ref[0], on_ref[...])
    _even_mix_out(x_ref, gt_ref, og, pooled, pw_ref, ps_ref, wo_ref, xo_ref)
    hb[0:POOL_PAD, :] = hb[tm:tm + POOL_PAD, :]

    @pl.when(t == pl.num_programs(1) - 1)
    def _():
        ho_ref[0] = hb[0:POOL_PAD, :]


def _even_out_sample_kernel(x_ref, o_ref, r_ref, u_ref, hi_ref, gt_ref, on_ref, pw_ref, ps_ref, wo_ref,
                            xo_ref, hn_ref, *, p0):
    steps, nb = u_ref.shape[:2]

    def slab(i, sl):
        return hi_ref[i, :, sl] if i < POOL_HIST else u_ref[i - POOL_HIST, :, sl]

    pooled = []
    for g, w in enumerate(POOL_WINDOWS):
        sl = slice(POOL_GROUP * g, POOL_GROUP * (g + 1))
        rows = []
        for t in range(steps):
            win = slab(POOL_HIST + t - (w - 1), sl)
            for j in range(w - 2, -1, -1):
                win = win + slab(POOL_HIST + t - j, sl)
            rows.append(win / float(min(p0 + t + 1, w)) - u_ref[t, :, sl])
        pooled.append(jnp.concatenate(rows, axis=0))
    flat = lambda ref: ref[...].reshape(steps * nb, ref.shape[-1])
    og = _gla_gate(flat(o_ref), flat(r_ref), on_ref[...])
    _even_mix_out(x_ref, gt_ref, og, pooled, pw_ref, ps_ref, wo_ref, xo_ref)
    for i in range(POOL_HIST):
        hn_ref[i] = slab(steps + i, slice(None))


def _even_out(grp, x, o, r, u, hist, mod, on, pw, ps, wo, p0):
    dm = x.shape[-1]
    common = [mod, on, pw, ps, wo]
    common_specs = [grp.mod(5), _full(on), _full(pw), _full(ps), _full(wo)]
    acts = [grp.act(dm), grp.act(o.shape[-1]), grp.act(r.shape[-1]), grp.act(u.shape[-1])]
    if grp.prompt:
        tm = grp.block[1]
        return pl.pallas_call(
            functools.partial(_even_out_prompt_kernel, tm=tm, p0=p0),
            grid=grp.grid,
            in_specs=acts + common_specs,
            out_specs=[grp.act(dm), pl.BlockSpec((1, POOL_PAD, POOL_WIDTH), lambda b, t: (b, 0, 0))],
            out_shape=[_sds(x.shape), _sds((x.shape[0], POOL_PAD, POOL_WIDTH))],
            scratch_shapes=[pltpu.VMEM((POOL_PAD + tm, POOL_WIDTH), F32)],
            compiler_params=_cp("parallel", "arbitrary"),
        )(x, o, r, u, *common)
    return pl.pallas_call(
        functools.partial(_even_out_sample_kernel, p0=p0),
        grid=grp.grid,
        in_specs=acts + [_full(hist)] + common_specs,
        out_specs=[grp.act(dm), _full(hist)],
        out_shape=[_sds(x.shape), _sds(hist.shape)],
        compiler_params=_cp("parallel", "arbitrary"),
    )(x, o, r, u, hist, *common)


def _rope_slab(x, cs, s1, s2, lead):
    shp = lead + (LANES,)
    back = pltpu.roll(x, LANES - MLA_ROPE // 2, 1).reshape(shp)
    fwd = pltpu.roll(x, MLA_ROPE // 2, 1).reshape(shp)
    out = x.reshape(shp) * cs + back * s1 + fwd * s2
    return out.reshape(x.shape)


def _odd_in_kernel(*refs, sample):
    (x_ref, sh_ref, sc_ref, g_ref, wcq_ref, wckv_ref, wkr_ref, wga_ref, wgg_ref, qn_ref, wuq_ref,
     kvn_ref, cs_ref, s1_ref, s2_ref) = refs[:15]
    if sample:
        wkl_ref, ckv_ref, kr_ref, uc_ref, q_ref, ql_ref = refs[15:]
    else:
        wuk_ref, wuv_ref, ckv_ref, kr_ref, uc_ref, q_ref, k_ref, v_ref = refs[15:]
    lead = x_ref.shape[:2]
    h = _modulated(x_ref, g_ref, sh_ref, sc_ref)
    cs, s1, s2 = cs_ref[...], s1_ref[...], s2_ref[...]

    cq = _rms(_dot(h, wcq_ref[...]), qn_ref[...]).astype(BF16)
    q = _dot(cq, wuq_ref[...]) * MLA_SCALE
    q = jnp.concatenate(
        [_rope_slab(q[:, LANES * i:LANES * (i + 1)], cs, s1, s2, lead) for i in range(MLA_HEADS)], axis=-1)
    q_ref[...] = q.astype(BF16).reshape(lead + (MLA_HEADS * LANES,))

    ckv = _rms(_dot(h, wckv_ref[...]), kvn_ref[...])
    ckv_ref[...] = ckv.reshape(lead + (MLA_KV_RANK,))
    kr = _rope_slab(_dot(h, wkr_ref[...]), cs, s1, s2, lead)
    kr_ref[...] = kr.reshape(lead + (LANES,))
    uc_ref[...] = (_dot(h, wga_ref[...]) * jax.nn.sigmoid(_dot(h, wgg_ref[...]))).reshape(lead + (CONV_CH,))

    if sample:
        qb = q.astype(BF16)
        for i in range(MLA_HEADS):
            ql = _dot(qb[:, LANES * i:LANES * (i + 1)], wkl_ref[i])
            ql_ref[:, :, MLA_KV_RANK * i:MLA_KV_RANK * (i + 1)] = ql.astype(BF16).reshape(lead + (MLA_KV_RANK,))
    else:
        cb = ckv.astype(BF16)
        kn = _dot(cb, wuk_ref[...])
        kn = jnp.concatenate([kn[:, LANES * i:LANES * (i + 1)] + kr for i in range(MLA_HEADS)], axis=-1)
        k_ref[...] = kn.astype(BF16).reshape(lead + (MLA_HEADS * LANES,))
        v_ref[...] = _dot(cb, wuv_ref[...]).astype(BF16).reshape(lead + (MLA_HEADS * LANES,))


def _odd_in(grp, x, mod, g, w, tabs, sample):
    lead = x.shape[:2]
    hl = MLA_HEADS * LANES
    ins = [x, mod, mod, g, w['wcq'], w['wckv'], w['wkr'], w['wga'], w['wgg'], w['qn'], w['wuq'], w['kvn'], *tabs]
    specs = [grp.act(x.shape[-1]), grp.mod(3), grp.mod(4)] + [_full(a) for a in ins[3:12]] + [grp.pos()] * 3
    outs = [(MLA_KV_RANK, F32), (LANES, F32), (CONV_CH, F32), (hl, BF16)]
    if sample:
        ins.append(w['wkl'])
        outs.append((MLA_HEADS * MLA_KV_RANK, BF16))
    else:
        ins += [w['wuk'], w['wuv']]
        outs += [(hl, BF16), (hl, BF16)]
    specs += [_full(a) for a in ins[15:]]
    return pl.pallas_call(
        functools.partial(_odd_in_kernel, sample=sample),
        grid=grp.grid,
        in_specs=specs,
        out_specs=[grp.act(wd) for wd, _ in outs],
        out_shape=[_sds(lead + (wd,), dt) for wd, dt in outs],
        compiler_params=_cp("parallel", "parallel"),
    )(*ins)


def _softmax_step(s, m_prev, l_prev):
    m_new = jnp.maximum(m_prev, jnp.max(s, axis=-1, keepdims=True))
    alpha = jnp.exp(m_prev - m_new)
    p = jnp.exp(s - m_new[:, :1])
    return p, alpha, m_new, alpha * l_prev + jnp.sum(p, axis=-1, keepdims=True)


def _attn_prompt_kernel(q_ref, k_ref, v_ref, o_ref, m_scr, l_scr, acc_scr, *, tq):
    qi = pl.program_id(1)
    ki = pl.program_id(2)

    @pl.when(ki == 0)
    def _():
        m_scr[...] = jnp.full_like(m_scr, NEG)
        l_scr[...] = jnp.zeros_like(l_scr)
        acc_scr[...] = jnp.zeros_like(acc_scr)

    def update(diagonal):
        if diagonal:
            row = lax.broadcasted_iota(jnp.int32, (tq, tq), 0)
            col = lax.broadcasted_iota(jnp.int32, (tq, tq), 1)
        for h in range(MLA_HEADS):
            sl = slice(LANES * h, LANES * (h + 1))
            s = _dot_nt(q_ref[0, :, sl], k_ref[0, :, sl])
            if diagonal:
                s = jnp.where(col <= row, s, NEG)
            p, alpha, m_new, l_new = _softmax_step(s, m_scr[:, sl], l_scr[:, sl])
            acc_scr[:, sl] = alpha * acc_scr[:, sl] + _dot(p.astype(BF16), v_ref[0, :, sl])
            m_scr[:, sl] = m_new
            l_scr[:, sl] = l_new

    @pl.when(ki < qi)
    def _():
        update(False)

    @pl.when(ki == qi)
    def _():
        update(True)
        o_ref[0] = (acc_scr[...] / l_scr[...]).astype(BF16)


def _attn_prompt(q, k, v, tq):
    nb, nt, hl = q.shape
    kv = pl.BlockSpec((1, tq, hl), lambda b, i, j: (b, jnp.minimum(i, j), 0))
    qs = pl.BlockSpec((1, tq, hl), lambda b, i, j: (b, i, 0))
    return pl.pallas_call(
        functools.partial(_attn_prompt_kernel, tq=tq),
        grid=(nb, nt // tq, nt // tq),
        in_specs=[qs, kv, kv],
        out_specs=qs,
        out_shape=_sds(q.shape, BF16),
        scratch_shapes=[pltpu.VMEM((tq, hl), F32)] * 3,
        compiler_params=_cp("parallel", "parallel", "arbitrary"),
    )(q, k, v)


def _attn_sample_kernel(pt_ref, ql_ref, qr_ref, cn_ref, kn_ref, *refs, npg):
    ckv_refs, kr_refs = refs[:npg], refs[npg:2 * npg]
    o_ref, m_scr, l_scr, acc_scr = refs[2 * npg:]
    j = pl.program_id(1)
    ql = ql_ref[0]
    qr = qr_ref[0]

    @pl.when(j == 0)
    def _():
        m_scr[...] = jnp.full_like(m_scr, NEG)
        l_scr[...] = jnp.zeros_like(l_scr)
        acc_scr[...] = jnp.zeros_like(acc_scr)

    pages = [ckv_refs[i][...].astype(BF16) for i in range(npg)]
    s = jnp.concatenate(
        [_dot_nt(ql, pages[i]) + _dot_nt(qr, kr_refs[i][...].astype(BF16)) for i in range(npg)], axis=-1)
    p, alpha, m_new, l_new = _softmax_step(s, m_scr[...], l_scr[...])
    pb = p.astype(BF16)
    acc = alpha[:, :1] * acc_scr[...]
    for i in range(npg):
        acc = acc + _dot(pb[:, LANES * i:LANES * (i + 1)], pages[i])
    m_scr[...] = m_new
    l_scr[...] = l_new
    acc_scr[...] = acc

    @pl.when(j == pl.num_programs(1) - 1)
    def _():
        rows = ql.shape[0]
        n_new = cn_ref.shape[1]
        pad = jnp.zeros((LANES - n_new, MLA_KV_RANK), F32)
        cn = jnp.concatenate([cn_ref[0], pad], axis=0).astype(BF16)
        kn = jnp.concatenate([kn_ref[0], pad[:, :MLA_ROPE]], axis=0).astype(BF16)
        s2 = _dot_nt(ql, cn) + _dot_nt(qr, kn)
        step = lax.shift_right_logical(lax.broadcasted_iota(jnp.int32, (rows, LANES), 0), MLA_HEADS.bit_length() - 1)
        col = lax.broadcasted_iota(jnp.int32, (rows, LANES), 1)
        s2 = jnp.where(col <= step, s2, NEG)
        p2, alpha2, _, l2 = _softmax_step(s2, m_scr[...], l_scr[...])
        acc2 = alpha2[:, :1] * acc_scr[...] + _dot(p2.astype(BF16), cn)
        o_ref[0] = acc2 / l2[:, :1]


def _attn_sample(page_table, ql, qr, cn, kn, cache_ckv, cache_kr, layer):
    nb, rows, rank = ql.shape
    n_pages = page_table.shape[1]
    npg = PAGES_PER_STEP if n_pages % PAGES_PER_STEP == 0 else n_pages
    page = cache_ckv.shape[2]

    def pg(width, i):
        return pl.BlockSpec((None, None, page, width), lambda b, j, pt: (layer, pt[b, j * npg + i], 0, 0))

    per_b = lambda a: pl.BlockSpec((1,) + a.shape[1:], lambda b, j, pt: (b, 0, 0))
    return pl.pallas_call(
        functools.partial(_attn_sample_kernel, npg=npg),
        grid_spec=pltpu.PrefetchScalarGridSpec(
            num_scalar_prefetch=1,
            grid=(nb, n_pages // npg),
            in_specs=[per_b(ql), per_b(qr), per_b(cn), per_b(kn)]
            + [pg(rank, i) for i in range(npg)] + [pg(cache_kr.shape[-1], i) for i in range(npg)],
            out_specs=pl.BlockSpec((1, rows, rank), lambda b, j, pt: (b, 0, 0)),
            scratch_shapes=[pltpu.VMEM((rows, LANES), F32), pltpu.VMEM((rows, LANES), F32),
                            pltpu.VMEM((rows, rank), F32)]),
        out_shape=_sds((nb, rows, rank)),
        compiler_params=_cp("parallel", "arbitrary"),
    )(page_table, ql, qr, cn, kn, *([cache_ckv] * npg), *([cache_kr] * npg))


def _conv_norm_act(cv, cb_ref, lg_ref, lb_ref):
    cv = cv + cb_ref[...]
    mu = jnp.mean(cv, axis=-1, keepdims=True)
    d = cv - mu
    y = d * lax.rsqrt(jnp.mean(d * d, axis=-1, keepdims=True) + EPS)
    return _silu(y * lg_ref[...] + lb_ref[...])


def _odd_out_prompt_kernel(x_ref, a_ref, uc_ref, gt_ref, cw_ref, cb_ref, lg_ref, lb_ref, woa_ref, woc_ref,
                           xo_ref, ho_ref, hb, *, tm):
    t = pl.program_id(1)

    @pl.when(t == 0)
    def _():
        hb[0:CONV_PAD, :] = jnp.zeros((CONV_PAD, CONV_CH), F32)

    hb[CONV_PAD:CONV_PAD + tm, :] = uc_ref[0]
    base = CONV_PAD - CONV_HIST
    cv = cw_ref[0:1, :] * hb[base:base + tm, :]
    for j in range(1, CONV_WIDTH):
        cv = cv + cw_ref[j:j + 1, :] * hb[base + j:base + j + tm, :]
    cv = _conv_norm_act(cv, cb_ref, lg_ref, lb_ref)
    y = _dot(a_ref[0], woa_ref[...]) + _dot(cv.astype(BF16), woc_ref[...])
    xo_ref[0] = x_ref[0] + gt_ref[...] * y
    hb[0:CONV_PAD, :] = hb[tm:tm + CONV_PAD, :]

    @pl.when(t == pl.num_programs(1) - 1)
    def _():
        ho_ref[0] = hb[0:CONV_PAD, :]


def _odd_out_sample_kernel(x_ref, lat_ref, uc_ref, hi_ref, gt_ref, cw_ref, cb_ref, lg_ref, lb_ref, wuv_ref,
                           woa_ref, woc_ref, xo_ref, hn_ref):
    steps, nb = uc_ref.shape[:2]

    def slab(i):
        return hi_ref[i] if i < CONV_HIST else uc_ref[i - CONV_HIST]

    rows = []
    for t in range(steps):
        cv = cw_ref[0:1, :] * slab(t)
        for j in range(1, CONV_WIDTH):
            cv = cv + cw_ref[j:j + 1, :] * slab(t + j)
        rows.append(cv)
    cv = _conv_norm_act(jnp.concatenate(rows, axis=0), cb_ref, lg_ref, lb_ref)
    lat = lat_ref[...].reshape(steps * nb, lat_ref.shape[-1]).astype(BF16)
    attn = _dot(lat, wuv_ref[...]).astype(BF16)
    y = _dot(attn, woa_ref[...]) + _dot(cv.astype(BF16), woc_ref[...])
    x = x_ref[...]
    xo_ref[...] = x + gt_ref[...] * y.reshape(x.shape)
    for i in range(CONV_HIST):
        hn_ref[i] = slab(steps + i)


def _odd_out(grp, x, a, uc, hist, mod, w):
    dm = x.shape[-1]
    conv = [w['cw'], w['cb'], w['lg'], w['lb']]
    if grp.prompt:
        tm = grp.block[1]
        ins = [x, a, uc, mod] + conv + [w['woa_pad'], w['woc']]
        return pl.pallas_call(
            functools.partial(_odd_out_prompt_kernel, tm=tm),
            grid=grp.grid,
            in_specs=[grp.act(dm), grp.act(a.shape[-1]), grp.act(CONV_CH), grp.mod(5)] + [_full(v) for v in ins[4:]],
            out_specs=[grp.act(dm), pl.BlockSpec((1, CONV_PAD, CONV_CH), lambda b, t: (b, 0, 0))],
            out_shape=[_sds(x.shape), _sds((x.shape[0], CONV_PAD, CONV_CH))],
            scratch_shapes=[pltpu.VMEM((CONV_PAD + tm, CONV_CH), F32)],
            compiler_params=_cp("parallel", "arbitrary"),
        )(*ins)
    ins = [x, a, uc, hist, mod] + conv + [w['wuv_bd'], w['woa'], w['woc']]
    return pl.pallas_call(
        _odd_out_sample_kernel,
        grid=grp.grid,
        in_specs=[grp.act(dm), grp.act(a.shape[-1]), grp.act(CONV_CH), _full(hist), grp.mod(5)]
        + [_full(v) for v in ins[5:]],
        out_specs=[grp.act(dm), _full(hist)],
        out_shape=[_sds(x.shape), _sds(hist.shape)],
        compiler_params=_cp("parallel", "arbitrary"),
    )(*ins)


def _head_pad(w, heads, width, offset=0):
    kdim = w.shape[0]
    w = w.reshape(kdim, heads, width)
    w = jnp.pad(w, ((0, 0), (0, 0), (offset, LANES - width - offset)))
    return w.reshape(kdim, heads * LANES)


def _rope_tables(pos):
    half = MLA_ROPE // 2
    freqs = ROPE_BASE ** (-jnp.arange(half, dtype=F32) / half)
    ang = pos.astype(F32)[..., None] * freqs
    cos, sin = jnp.cos(ang), jnp.sin(ang)
    zeros = jnp.zeros_like(cos)
    lead = jnp.ones(pos.shape + (ROPE_LANE0,), F32)
    tail = jnp.zeros(pos.shape + (LANES - ROPE_LANE0 - MLA_ROPE,), F32)
    cs = jnp.concatenate([lead, cos, cos, tail], axis=-1)
    s1 = jnp.concatenate([0 * lead, -sin, zeros, tail], axis=-1)
    s2 = jnp.concatenate([0 * lead, zeros, sin, tail], axis=-1)
    return cs, s1, s2


def _even_weights(w_in, gate_w2, gate_b, out_norm, pool_w, pool_scale, w_out):
    hk = GLA_HEADS * GLA_DK
    hv = GLA_HEADS * GLA_DV
    n_main = 2 * hk + 2 * hv
    wg = jnp.pad(w_in[:, n_main:n_main + GLA_GATE_RANK], ((0, 0), (0, LANES - GLA_GATE_RANK)))
    w2 = jnp.pad(gate_w2, ((0, LANES - GLA_GATE_RANK), (0, 0)))
    return dict(wq=w_in[:, :n_main].astype(BF16), wg=wg.astype(BF16), w2=w2.astype(BF16),
                gb=gate_b[None], wu=w_in[:, n_main + GLA_GATE_RANK:].astype(BF16),
                on=out_norm[None], pw=pool_w.astype(BF16), ps=pool_scale[None], wo=w_out.astype(BF16))


def _odd_weights(w_in, q_norm, w_uq, kv_norm, w_uk, w_uv, conv_w, conv_b, ln_g, ln_b, w_out):
    c0, c1, c2 = MLA_Q_RANK, MLA_Q_RANK + MLA_KV_RANK, MLA_Q_RANK + MLA_KV_RANK + MLA_ROPE
    n_attn = MLA_HEADS * MLA_V
    wkr = jnp.pad(w_in[:, c1:c2], ((0, 0), (ROPE_LANE0, LANES - ROPE_LANE0 - MLA_ROPE)))
    wuk = w_uk.reshape(MLA_KV_RANK, MLA_HEADS * MLA_NOPE)
    wuv = w_uv.reshape(MLA_KV_RANK, MLA_HEADS * MLA_V)
    wkl = jnp.pad(jnp.transpose(w_uk, (1, 2, 0)), ((0, 0), (0, LANES - MLA_NOPE), (0, 0)))
    eye = jnp.eye(MLA_HEADS, dtype=F32)
    wuv_bd = (eye[:, None, :, None] * jnp.transpose(w_uv, (1, 0, 2))[:, :, None, :]).reshape(
        MLA_HEADS * MLA_KV_RANK, n_attn)
    woa = w_out[:n_attn]
    woa_pad = jnp.pad(woa.reshape(MLA_HEADS, MLA_V, -1), ((0, 0), (0, LANES - MLA_V), (0, 0))).reshape(
        MLA_HEADS * LANES, -1)
    return dict(wcq=w_in[:, :c0].astype(BF16), wckv=w_in[:, c0:c1].astype(BF16), wkr=wkr.astype(BF16),
                wga=w_in[:, c2:c2 + CONV_CH].astype(BF16), wgg=w_in[:, c2 + CONV_CH:].astype(BF16),
                qn=q_norm[None], wuq=_head_pad(w_uq, MLA_HEADS, MLA_NOPE + MLA_ROPE).astype(BF16),
                kvn=kv_norm[None], wuk=_head_pad(wuk, MLA_HEADS, MLA_NOPE).astype(BF16),
                wuv=_head_pad(wuv, MLA_HEADS, MLA_V).astype(BF16), wkl=wkl.astype(BF16),
                wuv_bd=wuv_bd.astype(BF16), cw=conv_w, cb=conv_b[None], lg=ln_g[None], lb=ln_b[None],
                woa=woa.astype(BF16), woa_pad=woa_pad.astype(BF16), woc=w_out[n_attn:].astype(BF16))


def _tm(x):
    return jnp.swapaxes(x, 0, 1)


def kernel(x_prompt, x_sample, state_gla, state_pool, cache_ckv, cache_krope, state_conv, page_table, c_prompt, c_sample, ada_w, ada_b, norm_g, ffn_w1, ffn_w3, ffn_w2, ev_w_in, ev_gate_w2, ev_gate_b, ev_out_norm, ev_pool_w, ev_pool_scale, ev_w_out, od_w_in, od_q_norm, od_w_uq, od_kv_norm, od_w_uk, od_w_uv, od_conv_w, od_conv_b, od_conv_norm_g, od_conv_norm_b, od_w_out, final_norm):
    nbp, seq, dm = x_prompt.shape
    nbs, steps, _ = x_sample.shape
    depth = ada_w.shape[0]
    past_len = page_table.shape[1] * cache_ckv.shape[2]
    tile = min(TOKEN_TILE, seq)
    gp = _Group(True, nbp, seq, tile)
    gs = _Group(False, steps, nbs, nbs)

    n_c = nbp + nbs
    c_all = jnp.pad(jnp.concatenate([c_prompt, c_sample], axis=0), ((0, -n_c % 8), (0, 0)))
    mod = _ada(c_all, ada_w, ada_b)
    mod_p = mod[:, :nbp].reshape(depth, nbp, N_MOD, 1, dm)
    mod_s = jnp.swapaxes(mod[:, nbp:n_c].reshape(depth, nbs, N_MOD, dm), 1, 2)

    w1, w3, w2 = ffn_w1.astype(BF16), ffn_w3.astype(BF16), ffn_w2.astype(BF16)
    tabs_p = _rope_tables(jnp.arange(seq)[None])
    tabs_s = _rope_tables(past_len + jnp.arange(steps)[:, None])

    xp = x_prompt
    xs = _tm(x_sample)
    gla_p, gla_s, pool_p, pool_s, ckv_p, ckv_s, kr_p, kr_s, conv_p, conv_s = ([] for _ in range(10))
    gla_tt = min(TOKEN_TILE, seq)

    for layer in range(depth):
        i = layer // 2
        ng = norm_g[layer]
        last = layer == depth - 1
        xp = _ffn(gp, xp, mod_p[layer], 0, ng[0:1], w1[layer, 0], w3[layer, 0], w2[layer, 0])
        xs = _ffn(gs, xs, mod_s[layer], 0, ng[0:1], w1[layer, 0], w3[layer, 0], w2[layer, 0])
        if layer % 2 == 0:
            w = _even_weights(ev_w_in[i], ev_gate_w2[i], ev_gate_b[i], ev_out_norm[i], ev_pool_w[i],
                              ev_pool_scale[i], ev_w_out[i])
            proj = (ng[1:2], w['wq'], w['wg'], w['w2'], w['gb'], w['wu'])
            out_w = (w['on'], w['pw'], w['ps'], w['wo'])
            q, k, v, r, la, u = _even_in(gp, xp, mod_p[layer], *proj)
            o, s_fin = _gla(q, k, v, la, None, nbp, gla_tt)
            xp, hist = _even_out(gp, xp, o, r, u, None, mod_p[layer], *out_w, 0)
            gla_p.append(s_fin.reshape(nbp, GLA_HEADS, GLA_DK, GLA_DV))
            pool_p.append(hist[:, POOL_PAD - POOL_HIST:])
            q, k, v, r, la, u = _even_in(gs, xs, mod_s[layer], *proj)
            chunked = lambda a: jnp.pad(_tm(a), ((0, 0), (0, GLA_CHUNK - steps), (0, 0)))
            s0 = state_gla[i].reshape(nbs, GLA_HEADS // 2, LANES, GLA_DV)
            o, s_fin = _gla(chunked(q), chunked(k), chunked(v), chunked(la), s0,
                            GLA_SAMPLE_BATCHES if nbs % GLA_SAMPLE_BATCHES == 0 else 1, GLA_CHUNK)
            xs, hist = _even_out(gs, xs, _tm(o[:, :steps]), r, u, _tm(state_pool[i]), mod_s[layer], *out_w, past_len)
            gla_s.append(s_fin.reshape(nbs, GLA_HEADS, GLA_DK, GLA_DV))
            pool_s.append(_tm(hist))
        else:
            w = _odd_weights(od_w_in[i], od_q_norm[i], od_w_uq[i], od_kv_norm[i], od_w_uk[i], od_w_uv[i],
                             od_conv_w[i], od_conv_b[i], od_conv_norm_g[i], od_conv_norm_b[i], od_w_out[i])
            rope_lanes = slice(ROPE_LANE0, ROPE_LANE0 + MLA_ROPE)
            ckv, kr, uc, q, k, v = _odd_in(gp, xp, mod_p[layer], ng[1:2], w, tabs_p, False)
            attn = _attn_prompt(q, k, v, min(ATTN_TILE, seq))
            xp, hist = _odd_out(gp, xp, attn, uc, None, mod_p[layer], w)
            ckv_p.append(ckv)
            kr_p.append(kr[..., rope_lanes])
            conv_p.append(hist[:, CONV_PAD - CONV_HIST:])
            ckv, kr, uc, q, ql = _odd_in(gs, xs, mod_s[layer], ng[1:2], w, tabs_s, True)
            kr = kr[..., rope_lanes]
            qr = q.reshape(steps, nbs, MLA_HEADS, LANES)[..., rope_lanes]
            qr = _tm(qr).reshape(nbs, steps * MLA_HEADS, MLA_ROPE)
            ql = _tm(ql).reshape(nbs, steps * MLA_HEADS, MLA_KV_RANK)
            pad8 = lambda a: jnp.pad(_tm(a), ((0, 0), (0, -steps % 8), (0, 0)))
            lat = _attn_sample(page_table, ql, qr, pad8(ckv), pad8(kr), cache_ckv, cache_krope, i)
            lat = _tm(lat.reshape(nbs, steps, MLA_HEADS * MLA_KV_RANK))
            xs, hist = _odd_out(gs, xs, lat, uc, _tm(state_conv[i]), mod_s[layer], w)
            ckv_s.append(_tm(ckv))
            kr_s.append(_tm(kr))
            conv_s.append(_tm(hist))
        fin = final_norm[None] if last else None
        xp = _ffn(gp, xp, mod_p[layer], 6, ng[2:3], w1[layer, 1], w3[layer, 1], w2[layer, 1], fin)
        xs = _ffn(gs, xs, mod_s[layer], 6, ng[2:3], w1[layer, 1], w3[layer, 1], w2[layer, 1], fin)

    st = jnp.stack
    return (xp, _tm(xs), st(gla_p), st(gla_s), st(pool_p), st(pool_s), st(ckv_p), st(ckv_s),
            st(kr_p), st(kr_s), st(conv_p), st(conv_s))
```

```python
import functools

import jax
import jax.numpy as jnp
import numpy as np
from jax import lax
from jax.experimental import pallas as pl
from jax.experimental.pallas import tpu as pltpu

F32 = jnp.float32
BF16 = jnp.bfloat16

EPS = 1e-6
NEG = -1e30
N_MOD = 9
GLA_HEADS = 4
GLA_DK = 64
GLA_DV = 128
GLA_GATE_RANK = 16
GLA_TAU = 16.0
GLA_CHUNK = 64
POOL_WINDOWS = (2, 4, 8, 16)
POOL_GROUP = 128
POOL_WIDTH = 512
POOL_HIST = 15
MLA_HEADS = 8
MLA_Q_RANK = 384
MLA_KV_RANK = 256
MLA_NOPE = 64
MLA_ROPE = 32
MLA_V = 64
MLA_SCALE = (MLA_NOPE + MLA_ROPE) ** -0.5
LOG2E = 1.4426950408889634
ROPE_BASE = 10000.0
CONV_WIDTH = 31
CONV_CH = 512
CONV_HIST = 30

LANES = 128
SUBLANES = 8
CONV_ROWS = 32
ROPE_LANE0 = 64
POOL_PAD = 16
CONV_PAD = 32
VMEM_LIMIT = 52 * 2 ** 20
TOKEN_TILE = 512
ATTN_TILE = 512
FF_TILE = 1408
PAGES_PER_STEP = 64
GLA_SAMPLE_BATCHES = 4


def _cp(*sem):
    return pltpu.CompilerParams(dimension_semantics=sem, vmem_limit_bytes=VMEM_LIMIT)


def _dot(a, b):
    return jnp.dot(a, b, preferred_element_type=F32)


def _dot_nt(a, b):
    return lax.dot_general(a, b, (((1,), (1,)), ((), ())), preferred_element_type=F32)


def _dot_tn(a, b):
    return lax.dot_general(a, b, (((0,), (0,)), ((), ())), preferred_element_type=F32)


def _silu(x):
    return x * jax.nn.sigmoid(x)


def _rms(x, g):
    return x * lax.rsqrt(jnp.mean(x * x, axis=-1, keepdims=True) + EPS) * g


def _modulated(x_ref, g_ref, sh_ref, sc_ref):
    x = x_ref[...]
    h = _rms(x, g_ref[...]) * (1.0 + sc_ref[...]) + sh_ref[...]
    return h.reshape(x.shape[0] * x.shape[1], x.shape[2]).astype(BF16)


class _Group:
    def __init__(self, prompt, lead, rows, tile):
        self.prompt = prompt
        self.grid = (lead, rows // tile) if prompt else (1, 1)
        self.block = (1, tile) if prompt else (lead, rows)

    def act(self, width):
        return pl.BlockSpec(self.block + (width,), lambda b, t, *_: (b, t, 0))

    def mod(self, k):
        if self.prompt:
            return pl.BlockSpec((None, None, 1, self.dm), lambda b, t, *_: (b, k, 0, 0))
        return pl.BlockSpec((None, self.block[1], self.dm), lambda b, t, *_: (k, 0, 0))

    def pos(self):
        if self.prompt:
            return pl.BlockSpec((1, self.block[1], LANES), lambda b, t, *_: (0, t, 0))
        return pl.BlockSpec((self.block[0], 1, LANES), lambda b, t, *_: (0, 0, 0))

    dm = 1024


def _full(a):
    nd = a.ndim
    return pl.BlockSpec(a.shape, lambda *_: (0,) * nd)


def _sds(shape, dtype=F32):
    return jax.ShapeDtypeStruct(shape, dtype)


def _ada_kernel(c_ref, w_ref, b_ref, o_ref):
    c = c_ref[...]
    o_ref[...] = _dot(_silu(c).astype(BF16), w_ref[...].astype(BF16)) + b_ref[...]


def _ada(c_all, ada_w, ada_b):
    depth, dm, n = ada_w.shape
    m = c_all.shape[0]
    tn = dm
    return pl.pallas_call(
        _ada_kernel,
        grid=(depth, n // tn),
        in_specs=[pl.BlockSpec((m, dm), lambda l, j: (0, 0)),
                  pl.BlockSpec((None, dm, tn), lambda l, j: (l, 0, j)),
                  pl.BlockSpec((None, 1, tn), lambda l, j: (l, 0, j))],
        out_specs=pl.BlockSpec((None, m, tn), lambda l, j: (l, 0, j)),
        out_shape=_sds((depth, m, n)),
        compiler_params=_cp("parallel", "parallel"),
    )(c_all, ada_w, ada_b.reshape(depth, 1, n))


def _ffn_kernel(*refs, final):
    if final:
        x_ref, sh_ref, sc_ref, gt_ref, g_ref, w1_ref, w3_ref, w2_ref, fn_ref, o_ref, h_scr, acc_scr = refs
    else:
        x_ref, sh_ref, sc_ref, gt_ref, g_ref, w1_ref, w3_ref, w2_ref, o_ref, h_scr, acc_scr = refs
    j = pl.program_id(2)

    @pl.when(j == 0)
    def _():
        h_scr[...] = _modulated(x_ref, g_ref, sh_ref, sc_ref)
        acc_scr[...] = jnp.zeros_like(acc_scr)

    h = h_scr[...]
    a = _dot(h, w1_ref[...])
    b = _dot(h, w3_ref[...])
    acc_scr[...] += _dot((_silu(a) * b).astype(BF16), w2_ref[...])

    @pl.when(j == pl.num_programs(2) - 1)
    def _():
        x = x_ref[...]
        xn = x + 0.5 * gt_ref[...] * acc_scr[...].reshape(x.shape)
        o_ref[...] = _rms(xn, fn_ref[...]) if final else xn


def _ffn(grp, x, mod, k0, g, w1, w3, w2, final_g=None):
    dm, dff = w1.shape
    tf = FF_TILE if dff % FF_TILE == 0 else dff
    rows = grp.block[0] * grp.block[1]
    final = final_g is not None
    ins = [x, mod, mod, mod, g, w1, w3, w2] + ([final_g] if final else [])
    specs = [grp.act(dm), grp.mod(k0), grp.mod(k0 + 1), grp.mod(k0 + 2), _full(g),
             pl.BlockSpec((dm, tf), lambda b, t, j: (0, j)),
             pl.BlockSpec((dm, tf), lambda b, t, j: (0, j)),
             pl.BlockSpec((tf, dm), lambda b, t, j: (j, 0))] + ([_full(final_g)] if final else [])
    return pl.pallas_call(
        functools.partial(_ffn_kernel, final=final),
        grid=grp.grid + (dff // tf,),
        in_specs=specs,
        out_specs=grp.act(dm),
        out_shape=_sds(x.shape),
        scratch_shapes=[pltpu.VMEM((rows, dm), BF16), pltpu.VMEM((rows, dm), F32)],
        compiler_params=_cp("parallel", "parallel", "arbitrary"),
    )(*ins)


def _log_sigmoid(x):
    return jnp.minimum(x, 0.0) - jnp.log(1.0 + jnp.exp(-jnp.abs(x)))


def _even_in_kernel(x_ref, sh_ref, sc_ref, g_ref, wq_ref, wg_ref, w2_ref, gb_ref, wu_ref,
                    q_ref, k_ref, v_ref, r_ref, la_ref, u_ref):
    h = _modulated(x_ref, g_ref, sh_ref, sc_ref)
    lead = x_ref.shape[:2]
    hk = GLA_HEADS * GLA_DK
    hv = GLA_HEADS * GLA_DV
    z = _dot(h, wq_ref[...])
    q_ref[...] = (z[:, :hk] * GLA_DK ** -0.5).reshape(lead + (hk,))
    k_ref[...] = z[:, hk:2 * hk].reshape(lead + (hk,))
    v_ref[...] = z[:, 2 * hk:2 * hk + hv].reshape(lead + (hv,))
    r_ref[...] = z[:, 2 * hk + hv:].reshape(lead + (hv,))
    g_low = _dot(h, wg_ref[...]).astype(BF16)
    gate = _dot(g_low, w2_ref[...]) + gb_ref[...]
    la_ref[...] = (_log_sigmoid(gate) / GLA_TAU).reshape(lead + (hk,))
    u_ref[...] = _dot(h, wu_ref[...]).reshape(lead + (POOL_WIDTH,))


def _even_in(grp, x, mod, g, wq, wg, w2, gb, wu):
    hk = GLA_HEADS * GLA_DK
    hv = GLA_HEADS * GLA_DV
    widths = (hk, hk, hv, hv, hk, POOL_WIDTH)
    return pl.pallas_call(
        _even_in_kernel,
        grid=grp.grid,
        in_specs=[grp.act(x.shape[-1]), grp.mod(3), grp.mod(4), _full(g), _full(wq), _full(wg),
                  _full(w2), _full(gb), _full(wu)],
        out_specs=[grp.act(w) for w in widths],
        out_shape=[_sds(x.shape[:2] + (w,)) for w in widths],
        compiler_params=_cp("parallel", "parallel"),
    )(x, mod, mod, g, wq, wg, w2, gb, wu)


def _cumsum_rows(tril, x):
    hi = x.astype(BF16)
    r1 = x - hi.astype(F32)
    mid = r1.astype(BF16)
    lo = (r1 - mid.astype(F32)).astype(BF16)
    return _dot(tril, hi) + _dot(tril, mid) + _dot(tril, lo)


def _gla_kernel(*refs, nbb, tt, has_s0):
    if has_s0:
        q_ref, k_ref, v_ref, la_ref, s0_ref, o_ref, so_ref, s_scr = refs
    else:
        q_ref, k_ref, v_ref, la_ref, o_ref, so_ref, s_scr = refs
    t = pl.program_id(1)
    c_rows = GLA_CHUNK

    @pl.when(t == 0)
    def _():
        s_scr[...] = s0_ref[...] if has_s0 else jnp.zeros_like(s_scr)

    row = lax.broadcasted_iota(jnp.int32, (c_rows, c_rows), 0)
    col = lax.broadcasted_iota(jnp.int32, (c_rows, c_rows), 1)
    causal = col <= row
    tril = jnp.where(causal, 1.0, 0.0).astype(BF16)
    lane = lax.broadcasted_iota(jnp.int32, (c_rows, LANES), 1)
    srow = lax.broadcasted_iota(jnp.int32, (LANES, LANES), 0)

    def chunk(c, carry):
        rows = pl.ds(pl.multiple_of(c * c_rows, c_rows), c_rows)
        for n in range(nbb):
            for p in range(GLA_HEADS // 2):
                sl = slice(LANES * p, LANES * (p + 1))
                b = _cumsum_rows(tril, la_ref[n, rows, sl])
                b_last = b[c_rows - 1:c_rows, :]
                k = k_ref[n, rows, sl]
                qi = q_ref[n, rows, sl] * jnp.exp(b)
                ki = (k * jnp.exp(-b)).astype(BF16)
                kd = (k * jnp.exp(b_last - b)).astype(BF16)
                s_old = s_scr[n, p]
                s_bf = s_old.astype(BF16)
                decay = jnp.transpose(jnp.broadcast_to(jnp.exp(b_last), (LANES, LANES)))
                upd = []
                for hh in range(2):
                    mine = (lane < GLA_DK) if hh == 0 else (lane >= GLA_DK)
                    qh = jnp.where(mine, qi, 0.0).astype(BF16)
                    att = jnp.where(causal, _dot_nt(qh, ki), 0.0).astype(BF16)
                    vs = slice(GLA_DV * (2 * p + hh), GLA_DV * (2 * p + hh + 1))
                    vh = v_ref[n, rows, vs].astype(BF16)
                    o_ref[n, rows, vs] = _dot(qh, s_bf) + _dot(att, vh)
                    upd.append(_dot_tn(kd, vh))
                s_scr[n, p] = decay * s_old + jnp.where(srow < GLA_DK, upd[0], upd[1])
        return carry

    lax.fori_loop(0, tt // c_rows, chunk, 0)

    @pl.when(t == pl.num_programs(1) - 1)
    def _():
        so_ref[...] = s_scr[...]


def _gla(q, k, v, la, s0, nbb, tt):
    nb, nt, hk = q.shape
    hv = v.shape[-1]
    has_s0 = s0 is not None
    act = lambda w: pl.BlockSpec((nbb, tt, w), lambda b, t: (b, t, 0))
    st = pl.BlockSpec((nbb, GLA_HEADS // 2, LANES, GLA_DV), lambda b, t: (b, 0, 0, 0))
    return pl.pallas_call(
        functools.partial(_gla_kernel, nbb=nbb, tt=tt, has_s0=has_s0),
        grid=(nb // nbb, nt // tt),
        in_specs=[act(hk), act(hk), act(hv), act(hk)] + ([st] if has_s0 else []),
        out_specs=[act(hv), st],
        out_shape=[_sds((nb, nt, hv)), _sds((nb, GLA_HEADS // 2, LANES, GLA_DV))],
        scratch_shapes=[pltpu.VMEM((nbb, GLA_HEADS // 2, LANES, GLA_DV), F32)],
        compiler_params=_cp("parallel", "arbitrary"),
    )(*([q, k, v, la] + ([s0] if has_s0 else [])))


def _gla_gate(o, r, on):
    outs = []
    for h in range(GLA_HEADS):
        sl = slice(GLA_DV * h, GLA_DV * (h + 1))
        outs.append(_rms(o[:, sl], on[:, sl]))
    return jnp.concatenate(outs, axis=-1) * _silu(r)


def _even_mix_out(x_ref, gt_ref, og, pooled, pw_ref, ps_ref, wo_ref, xo_ref):
    mixed = [_dot(pooled[g].astype(BF16), pw_ref[g]) for g in range(len(POOL_WINDOWS))]
    mixed = jnp.concatenate(mixed, axis=-1) * ps_ref[...]
    n_o = og.shape[-1]
    y = _dot(og.astype(BF16), wo_ref[:n_o, :]) + _dot(mixed.astype(BF16), wo_ref[n_o:, :])
    x = x_ref[...]
    xo_ref[...] = x + gt_ref[...] * y.reshape(x.shape)


def _even_out_prompt_kernel(x_ref, o_ref, r_ref, u_ref, gt_ref, on_ref, pw_ref, ps_ref, wo_ref,
                            xo_ref, ho_ref, hb, *, tm, p0):
    t = pl.program_id(1)

    @pl.when(t == 0)
    def _():
        hb[0:POOL_PAD, :] = jnp.zeros((POOL_PAD, POOL_WIDTH), F32)

    u = u_ref[0]
    hb[POOL_PAD:POOL_PAD + tm, :] = u
    pos = t * tm + lax.broadcasted_iota(jnp.int32, (tm, 1), 0)
    pooled = []
    for g, w in enumerate(POOL_WINDOWS):
        sl = slice(POOL_GROUP * g, POOL_GROUP * (g + 1))
        win = hb[POOL_PAD - (w - 1):POOL_PAD - (w - 1) + tm, sl]
        for j in range(w - 2, -1, -1):
            win = win + hb[POOL_PAD - j:POOL_PAD - j + tm, sl]
        cnt = jnp.minimum(p0 + pos + 1, w).astype(F32)
        pooled.append(win / cnt - u[:, sl])
    og = _gla_gate(o_ref[0], r_ref[0], on_ref[...])
    _even_mix_out(x_ref, gt_ref, og, pooled, pw_ref, ps_ref, wo_ref, xo_ref)
    hb[0:POOL_PAD, :] = hb[tm:tm + POOL_PAD, :]

    @pl.when(t == pl.num_programs(1) - 1)
    def _():
        ho_ref[0] = hb[0:POOL_PAD, :]


def _even_out_sample_kernel(x_ref, o_ref, r_ref, u_ref, hi_ref, gt_ref, on_ref, pw_ref, ps_ref, wo_ref,
                            xo_ref, hn_ref, *, p0):
    steps, nb = u_ref.shape[:2]

    def slab(i, sl):
        return hi_ref[i, :, sl] if i < POOL_HIST else u_ref[i - POOL_HIST, :, sl]

    pooled = []
    for g, w in enumerate(POOL_WINDOWS):
        sl = slice(POOL_GROUP * g, POOL_GROUP * (g + 1))
        rows = []
        for t in range(steps):
            win = slab(POOL_HIST + t - (w - 1), sl)
            for j in range(w - 2, -1, -1):
                win = win + slab(POOL_HIST + t - j, sl)
            rows.append(win / float(min(p0 + t + 1, w)) - u_ref[t, :, sl])
        pooled.append(jnp.concatenate(rows, axis=0))
    flat = lambda ref: ref[...].reshape(steps * nb, ref.shape[-1])
    og = _gla_gate(flat(o_ref), flat(r_ref), on_ref[...])
    _even_mix_out(x_ref, gt_ref, og, pooled, pw_ref, ps_ref, wo_ref, xo_ref)
    for i in range(POOL_HIST):
        hn_ref[i] = slab(steps + i, slice(None))


def _even_out(grp, x, o, r, u, hist, mod, on, pw, ps, wo, p0):
    dm = x.shape[-1]
    common = [mod, on, pw, ps, wo]
    common_specs = [grp.mod(5), _full(on), _full(pw), _full(ps), _full(wo)]
    acts = [grp.act(dm), grp.act(o.shape[-1]), grp.act(r.shape[-1]), grp.act(u.shape[-1])]
    if grp.prompt:
        tm = grp.block[1]
        return pl.pallas_call(
            functools.partial(_even_out_prompt_kernel, tm=tm, p0=p0),
            grid=grp.grid,
            in_specs=acts + common_specs,
            out_specs=[grp.act(dm), pl.BlockSpec((1, POOL_PAD, POOL_WIDTH), lambda b, t: (b, 0, 0))],
            out_shape=[_sds(x.shape), _sds((x.shape[0], POOL_PAD, POOL_WIDTH))],
            scratch_shapes=[pltpu.VMEM((POOL_PAD + tm, POOL_WIDTH), F32)],
            compiler_params=_cp("parallel", "arbitrary"),
        )(x, o, r, u, *common)
    return pl.pallas_call(
        functools.partial(_even_out_sample_kernel, p0=p0),
        grid=grp.grid,
        in_specs=acts + [_full(hist)] + common_specs,
        out_specs=[grp.act(dm), _full(hist)],
        out_shape=[_sds(x.shape), _sds(hist.shape)],
        compiler_params=_cp("parallel", "arbitrary"),
    )(x, o, r, u, hist, *common)


def _rope_slab(x, cs, s1, s2, lead):
    shp = lead + (LANES,)
    back = pltpu.roll(x, LANES - MLA_ROPE // 2, 1).reshape(shp)
    fwd = pltpu.roll(x, MLA_ROPE // 2, 1).reshape(shp)
    out = x.reshape(shp) * cs + back * s1 + fwd * s2
    return out.reshape(x.shape)


def _odd_in_kernel(*refs, sample):
    (x_ref, sh_ref, sc_ref, g_ref, wcq_ref, wckv_ref, wkr_ref, wga_ref, wgg_ref, qn_ref, wuq_ref,
     kvn_ref, cs_ref, s1_ref, s2_ref) = refs[:15]
    if sample:
        wkl_ref, ckv_ref, kr_ref, uc_ref, q_ref, ql_ref = refs[15:]
    else:
        wuk_ref, wuv_ref, ckv_ref, kr_ref, uc_ref, q_ref, k_ref, v_ref = refs[15:]
    lead = x_ref.shape[:2]
    h = _modulated(x_ref, g_ref, sh_ref, sc_ref)
    cs, s1, s2 = cs_ref[...], s1_ref[...], s2_ref[...]

    cq = _rms(_dot(h, wcq_ref[...]), qn_ref[...]).astype(BF16)
    q = _dot(cq, wuq_ref[...]) * (MLA_SCALE * LOG2E)
    q = jnp.concatenate(
        [_rope_slab(q[:, LANES * i:LANES * (i + 1)], cs, s1, s2, lead) for i in range(MLA_HEADS)], axis=-1)
    q_ref[...] = q.astype(BF16).reshape(lead + (MLA_HEADS * LANES,))

    ckv = _rms(_dot(h, wckv_ref[...]), kvn_ref[...])
    ckv_ref[...] = ckv.reshape(lead + (MLA_KV_RANK,))
    kr = _rope_slab(_dot(h, wkr_ref[...]), cs, s1, s2, lead)
    kr_ref[...] = kr.reshape(lead + (LANES,))
    uc_ref[...] = (_dot(h, wga_ref[...]) * jax.nn.sigmoid(_dot(h, wgg_ref[...]))).reshape(lead + (CONV_CH,))

    if sample:
        qb = q.astype(BF16)
        for i in range(MLA_HEADS):
            ql = _dot(qb[:, LANES * i:LANES * (i + 1)], wkl_ref[i])
            ql_ref[:, :, MLA_KV_RANK * i:MLA_KV_RANK * (i + 1)] = ql.astype(BF16).reshape(lead + (MLA_KV_RANK,))
    else:
        cb = ckv.astype(BF16)
        kn = _dot(cb, wuk_ref[...])
        kn = jnp.concatenate([kn[:, LANES * i:LANES * (i + 1)] + kr for i in range(MLA_HEADS)], axis=-1)
        k_ref[...] = kn.astype(BF16).reshape(lead + (MLA_HEADS * LANES,))
        vlane = lax.broadcasted_iota(jnp.int32, (1, MLA_HEADS * LANES), 1) & (LANES - 1)
        v = _dot(cb, wuv_ref[...]) + jnp.where(vlane == MLA_V, 1.0, 0.0)
        v_ref[...] = v.astype(BF16).reshape(lead + (MLA_HEADS * LANES,))


def _odd_in(grp, x, mod, g, w, tabs, sample):
    lead = x.shape[:2]
    hl = MLA_HEADS * LANES
    ins = [x, mod, mod, g, w['wcq'], w['wckv'], w['wkr'], w['wga'], w['wgg'], w['qn'], w['wuq'], w['kvn'], *tabs]
    specs = [grp.act(x.shape[-1]), grp.mod(3), grp.mod(4)] + [_full(a) for a in ins[3:12]] + [grp.pos()] * 3
    outs = [(MLA_KV_RANK, F32), (LANES, F32), (CONV_CH, F32), (hl, BF16)]
    if sample:
        ins.append(w['wkl'])
        outs.append((MLA_HEADS * MLA_KV_RANK, BF16))
    else:
        ins += [w['wuk'], w['wuv']]
        outs += [(hl, BF16), (hl, BF16)]
    specs += [_full(a) for a in ins[15:]]
    return pl.pallas_call(
        functools.partial(_odd_in_kernel, sample=sample),
        grid=grp.grid,
        in_specs=specs,
        out_specs=[grp.act(wd) for wd, _ in outs],
        out_shape=[_sds(lead + (wd,), dt) for wd, dt in outs],
        compiler_params=_cp("parallel", "parallel"),
    )(*ins)


def _softmax_step(s, m_prev):
    m_new = jnp.maximum(m_prev, jnp.max(s, axis=-1, keepdims=True))
    return jnp.exp2(s - m_new[:, :1]), jnp.exp2(m_prev - m_new), m_new


def _attn_prompt_kernel(qi_ref, ki_ref, q_ref, k_ref, v_ref, o_ref, m_scr, acc_scr, *, tq):
    step = pl.program_id(1)
    qi = qi_ref[step]
    ki = ki_ref[step]

    @pl.when(ki == 0)
    def _():
        m_scr[...] = jnp.full_like(m_scr, NEG)
        acc_scr[...] = jnp.zeros_like(acc_scr)

    def update(diagonal):
        if diagonal:
            row = lax.broadcasted_iota(jnp.int32, (tq, tq), 0)
            col = lax.broadcasted_iota(jnp.int32, (tq, tq), 1)
        for h in range(MLA_HEADS):
            sl = slice(LANES * h, LANES * (h + 1))
            s = _dot_nt(q_ref[0, :, sl], k_ref[0, :, sl])
            if diagonal:
                s = jnp.where(col <= row, s, NEG)
            p, alpha, m_new = _softmax_step(s, m_scr[:, sl])
            acc_scr[:, sl] = alpha * acc_scr[:, sl] + _dot(p.astype(BF16), v_ref[0, :, sl])
            m_scr[:, sl] = m_new

    @pl.when(ki < qi)
    def _():
        update(False)

    @pl.when(ki == qi)
    def _():
        update(True)
        for h in range(MLA_HEADS):
            sl = slice(LANES * h, LANES * (h + 1))
            acc = acc_scr[:, sl]
            o_ref[0, :, sl] = (acc / acc[:, MLA_V:MLA_V + 1]).astype(BF16)


def _attn_prompt(q, k, v, tq):
    nb, nt, hl = q.shape
    pairs = [(i, j) for i in range(nt // tq) for j in range(i + 1)]
    qi_tab = jnp.asarray([p[0] for p in pairs], jnp.int32)
    ki_tab = jnp.asarray([p[1] for p in pairs], jnp.int32)
    qs = pl.BlockSpec((1, tq, hl), lambda b, s, qt, kt: (b, qt[s], 0))
    kv = pl.BlockSpec((1, tq, hl), lambda b, s, qt, kt: (b, kt[s], 0))
    return pl.pallas_call(
        functools.partial(_attn_prompt_kernel, tq=tq),
        grid_spec=pltpu.PrefetchScalarGridSpec(
            num_scalar_prefetch=2,
            grid=(nb, len(pairs)),
            in_specs=[qs, kv, kv],
            out_specs=qs,
            scratch_shapes=[pltpu.VMEM((tq, hl), F32)] * 2),
        out_shape=_sds(q.shape, BF16),
        compiler_params=_cp("parallel", "arbitrary"),
    )(qi_tab, ki_tab, q, k, v)


def _attn_sample_kernel(pt_ref, ql_ref, qr_ref, cn_ref, kn_ref, *refs, npg):
    ckv_refs, kr_refs = refs[:npg], refs[npg:2 * npg]
    o_ref, m_scr, l_scr, acc_scr = refs[2 * npg:]
    j = pl.program_id(1)
    ql = ql_ref[0]
    qr = qr_ref[0]

    @pl.when(j == 0)
    def _():
        m_scr[...] = jnp.full_like(m_scr, NEG)
        l_scr[...] = jnp.zeros_like(l_scr)
        acc_scr[...] = jnp.zeros_like(acc_scr)

    pages = [ckv_refs[i][...].astype(BF16) for i in range(npg)]
    s = jnp.concatenate(
        [_dot_nt(ql, pages[i]) + _dot(qr, kr_refs[i][...].astype(BF16)) for i in range(npg)], axis=-1)
    p, alpha, m_new = _softmax_step(s, m_scr[...])
    pb = p.astype(BF16)
    acc = alpha[:, :1] * acc_scr[...]
    for i in range(npg):
        acc = acc + _dot(pb[:, LANES * i:LANES * (i + 1)], pages[i])
    m_scr[...] = m_new
    l_scr[...] = alpha * l_scr[...] + jnp.sum(p, axis=-1, keepdims=True)
    acc_scr[...] = acc

    @pl.when(j == pl.num_programs(1) - 1)
    def _():
        rows = ql.shape[0]
        n_new = cn_ref.shape[1]
        pad = jnp.zeros((LANES - n_new, MLA_KV_RANK), F32)
        cn = jnp.concatenate([cn_ref[0], pad], axis=0).astype(BF16)
        kn = jnp.concatenate([kn_ref[0], pad[:, :MLA_ROPE]], axis=0).astype(BF16)
        s2 = _dot_nt(ql, cn) + _dot_nt(qr, kn)
        step = lax.shift_right_logical(lax.broadcasted_iota(jnp.int32, (rows, LANES), 0), MLA_HEADS.bit_length() - 1)
        col = lax.broadcasted_iota(jnp.int32, (rows, LANES), 1)
        s2 = jnp.where(col <= step, s2, NEG)
        p2, alpha2, _ = _softmax_step(s2, m_scr[...])
        l2 = alpha2 * l_scr[...] + jnp.sum(p2, axis=-1, keepdims=True)
        acc2 = alpha2[:, :1] * acc_scr[...] + _dot(p2.astype(BF16), cn)
        o_ref[0] = acc2 / l2[:, :1]


def _attn_sample(page_table, ql, qr, cn, kn, cache_ckv, cache_kr, layer):
    nb, rows, rank = ql.shape
    n_pages = page_table.shape[1]
    npg = PAGES_PER_STEP if n_pages % PAGES_PER_STEP == 0 else n_pages

    def pg(cache, i):
        return pl.BlockSpec((None, None) + cache.shape[2:], lambda b, j, pt: (layer, pt[b, j * npg + i], 0, 0))

    per_b = lambda a: pl.BlockSpec((1,) + a.shape[1:], lambda b, j, pt: (b, 0, 0))
    return pl.pallas_call(
        functools.partial(_attn_sample_kernel, npg=npg),
        grid_spec=pltpu.PrefetchScalarGridSpec(
            num_scalar_prefetch=1,
            grid=(nb, n_pages // npg),
            in_specs=[per_b(ql), per_b(qr), per_b(cn), per_b(kn)]
            + [pg(cache_ckv, i) for i in range(npg)] + [pg(cache_kr, i) for i in range(npg)],
            out_specs=pl.BlockSpec((1, rows, rank), lambda b, j, pt: (b, 0, 0)),
            scratch_shapes=[pltpu.VMEM((rows, LANES), F32), pltpu.VMEM((rows, LANES), F32),
                            pltpu.VMEM((rows, rank), F32)]),
        out_shape=_sds((nb, rows, rank)),
        compiler_params=_cp("parallel", "arbitrary"),
    )(page_table, ql, qr, cn, kn, *([cache_ckv] * npg), *([cache_kr] * npg))


def _conv_norm_act(cv, cb_ref, lg_ref, lb_ref):
    cv = cv + cb_ref[...]
    mu = jnp.mean(cv, axis=-1, keepdims=True)
    d = cv - mu
    y = d * lax.rsqrt(jnp.mean(d * d, axis=-1, keepdims=True) + EPS)
    return _silu(y * lg_ref[...] + lb_ref[...])


def _odd_out_prompt_kernel(x_ref, a_ref, uc_ref, gt_ref, cw_ref, cb_ref, lg_ref, lb_ref, woa_ref, woc_ref,
                           xo_ref, ho_ref, hb, hs, cvb, *, tm):
    t = pl.program_id(1)

    @pl.when(t == 0)
    def _():
        hb[0:CONV_PAD, :] = jnp.zeros((CONV_PAD, CONV_CH), F32)

    hb[CONV_PAD:CONV_PAD + tm, :] = uc_ref[0]
    for sft in range(1, SUBLANES):
        hs[sft - 1] = hb[sft:sft + hs.shape[1], :]
    base = CONV_PAD - CONV_HIST

    def chunk(c, carry):
        acc = None
        for j in range(CONV_WIDTH):
            whole, sft = divmod(base + j, SUBLANES)
            rows = pl.ds(pl.multiple_of(c * CONV_ROWS + whole * SUBLANES, SUBLANES), CONV_ROWS)
            term = cw_ref[j:j + 1, :] * (hb[rows, :] if sft == 0 else hs[sft - 1, rows, :])
            acc = term if acc is None else acc + term
        cvb[pl.ds(pl.multiple_of(c * CONV_ROWS, CONV_ROWS), CONV_ROWS), :] = acc
        return carry

    lax.fori_loop(0, tm // CONV_ROWS, chunk, 0)
    cv = _conv_norm_act(cvb[...], cb_ref, lg_ref, lb_ref)
    y = _dot(a_ref[0], woa_ref[...]) + _dot(cv.astype(BF16), woc_ref[...])
    xo_ref[0] = x_ref[0] + gt_ref[...] * y
    hb[0:CONV_PAD, :] = hb[tm:tm + CONV_PAD, :]

    @pl.when(t == pl.num_programs(1) - 1)
    def _():
        ho_ref[0] = hb[0:CONV_PAD, :]


def _odd_out_sample_kernel(x_ref, lat_ref, uc_ref, hi_ref, gt_ref, cw_ref, cb_ref, lg_ref, lb_ref, wuv_ref,
                           woa_ref, woc_ref, xo_ref, hn_ref):
    steps, nb = uc_ref.shape[:2]

    def slab(i):
        return hi_ref[i] if i < CONV_HIST else uc_ref[i - CONV_HIST]

    rows = []
    for t in range(steps):
        cv = cw_ref[0:1, :] * slab(t)
        for j in range(1, CONV_WIDTH):
            cv = cv + cw_ref[j:j + 1, :] * slab(t + j)
        rows.append(cv)
    cv = _conv_norm_act(jnp.concatenate(rows, axis=0), cb_ref, lg_ref, lb_ref)
    lat = lat_ref[...].reshape(steps * nb, lat_ref.shape[-1]).astype(BF16)
    attn = _dot(lat, wuv_ref[...]).astype(BF16)
    y = _dot(attn, woa_ref[...]) + _dot(cv.astype(BF16), woc_ref[...])
    x = x_ref[...]
    xo_ref[...] = x + gt_ref[...] * y.reshape(x.shape)
    for i in range(CONV_HIST):
        hn_ref[i] = slab(steps + i)


def _odd_out(grp, x, a, uc, hist, mod, w):
    dm = x.shape[-1]
    conv = [w['cw'], w['cb'], w['lg'], w['lb']]
    if grp.prompt:
        tm = grp.block[1]
        ins = [x, a, uc, mod] + conv + [w['woa_pad'], w['woc']]
        return pl.pallas_call(
            functools.partial(_odd_out_prompt_kernel, tm=tm),
            grid=grp.grid,
            in_specs=[grp.act(dm), grp.act(a.shape[-1]), grp.act(CONV_CH), grp.mod(5)] + [_full(v) for v in ins[4:]],
            out_specs=[grp.act(dm), pl.BlockSpec((1, CONV_PAD, CONV_CH), lambda b, t: (b, 0, 0))],
            out_shape=[_sds(x.shape), _sds((x.shape[0], CONV_PAD, CONV_CH))],
            scratch_shapes=[pltpu.VMEM((CONV_PAD + tm, CONV_CH), F32),
                            pltpu.VMEM((SUBLANES - 1, CONV_PAD + tm - SUBLANES, CONV_CH), F32),
                            pltpu.VMEM((tm, CONV_CH), F32)],
            compiler_params=_cp("parallel", "arbitrary"),
        )(*ins)
    ins = [x, a, uc, hist, mod] + conv + [w['wuv_bd'], w['woa'], w['woc']]
    return pl.pallas_call(
        _odd_out_sample_kernel,
        grid=grp.grid,
        in_specs=[grp.act(dm), grp.act(a.shape[-1]), grp.act(CONV_CH), _full(hist), grp.mod(5)]
        + [_full(v) for v in ins[5:]],
        out_specs=[grp.act(dm), _full(hist)],
        out_shape=[_sds(x.shape), _sds(hist.shape)],
        compiler_params=_cp("parallel", "arbitrary"),
    )(*ins)


def _head_pad(w, heads, width, offset=0):
    kdim = w.shape[0]
    w = w.reshape(kdim, heads, width)
    w = jnp.pad(w, ((0, 0), (0, 0), (offset, LANES - width - offset)))
    return w.reshape(kdim, heads * LANES)


def _rope_tables(pos):
    half = MLA_ROPE // 2
    freqs = ROPE_BASE ** (-jnp.arange(half, dtype=F32) / half)
    ang = pos.astype(F32)[..., None] * freqs
    cos, sin = jnp.cos(ang), jnp.sin(ang)
    zeros = jnp.zeros_like(cos)
    lead = jnp.ones(pos.shape + (ROPE_LANE0,), F32)
    tail = jnp.zeros(pos.shape + (LANES - ROPE_LANE0 - MLA_ROPE,), F32)
    cs = jnp.concatenate([lead, cos, cos, tail], axis=-1)
    s1 = jnp.concatenate([0 * lead, -sin, zeros, tail], axis=-1)
    s2 = jnp.concatenate([0 * lead, zeros, sin, tail], axis=-1)
    return cs, s1, s2


def _even_weights(w_in, gate_w2, gate_b, out_norm, pool_w, pool_scale, w_out):
    hk = GLA_HEADS * GLA_DK
    hv = GLA_HEADS * GLA_DV
    n_main = 2 * hk + 2 * hv
    wg = jnp.pad(w_in[:, n_main:n_main + GLA_GATE_RANK], ((0, 0), (0, LANES - GLA_GATE_RANK)))
    w2 = jnp.pad(gate_w2, ((0, LANES - GLA_GATE_RANK), (0, 0)))
    return dict(wq=w_in[:, :n_main].astype(BF16), wg=wg.astype(BF16), w2=w2.astype(BF16),
                gb=gate_b[None], wu=w_in[:, n_main + GLA_GATE_RANK:].astype(BF16),
                on=out_norm[None], pw=pool_w.astype(BF16), ps=pool_scale[None], wo=w_out.astype(BF16))


def _odd_weights(w_in, q_norm, w_uq, kv_norm, w_uk, w_uv, conv_w, conv_b, ln_g, ln_b, w_out):
    c0, c1, c2 = MLA_Q_RANK, MLA_Q_RANK + MLA_KV_RANK, MLA_Q_RANK + MLA_KV_RANK + MLA_ROPE
    n_attn = MLA_HEADS * MLA_V
    wkr = jnp.pad(w_in[:, c1:c2], ((0, 0), (ROPE_LANE0, LANES - ROPE_LANE0 - MLA_ROPE)))
    wuk = w_uk.reshape(MLA_KV_RANK, MLA_HEADS * MLA_NOPE)
    wuv = w_uv.reshape(MLA_KV_RANK, MLA_HEADS * MLA_V)
    wkl = jnp.pad(jnp.transpose(w_uk, (1, 2, 0)), ((0, 0), (0, LANES - MLA_NOPE), (0, 0)))
    eye = jnp.eye(MLA_HEADS, dtype=F32)
    wuv_bd = (eye[:, None, :, None] * jnp.transpose(w_uv, (1, 0, 2))[:, :, None, :]).reshape(
        MLA_HEADS * MLA_KV_RANK, n_attn)
    woa = w_out[:n_attn]
    woa_pad = jnp.pad(woa.reshape(MLA_HEADS, MLA_V, -1), ((0, 0), (0, LANES - MLA_V), (0, 0))).reshape(
        MLA_HEADS * LANES, -1)
    return dict(wcq=w_in[:, :c0].astype(BF16), wckv=w_in[:, c0:c1].astype(BF16), wkr=wkr.astype(BF16),
                wga=w_in[:, c2:c2 + CONV_CH].astype(BF16), wgg=w_in[:, c2 + CONV_CH:].astype(BF16),
                qn=q_norm[None], wuq=_head_pad(w_uq, MLA_HEADS, MLA_NOPE + MLA_ROPE).astype(BF16),
                kvn=kv_norm[None], wuk=_head_pad(wuk, MLA_HEADS, MLA_NOPE).astype(BF16),
                wuv=_head_pad(wuv, MLA_HEADS, MLA_V).astype(BF16), wkl=wkl.astype(BF16),
                wuv_bd=wuv_bd.astype(BF16), cw=conv_w, cb=conv_b[None], lg=ln_g[None], lb=ln_b[None],
                woa=woa.astype(BF16), woa_pad=woa_pad.astype(BF16), woc=w_out[n_attn:].astype(BF16))


def _tm(x):
    return jnp.swapaxes(x, 0, 1)


def kernel(x_prompt, x_sample, state_gla, state_pool, cache_ckv, cache_krope, state_conv, page_table, c_prompt, c_sample, ada_w, ada_b, norm_g, ffn_w1, ffn_w3, ffn_w2, ev_w_in, ev_gate_w2, ev_gate_b, ev_out_norm, ev_pool_w, ev_pool_scale, ev_w_out, od_w_in, od_q_norm, od_w_uq, od_kv_norm, od_w_uk, od_w_uv, od_conv_w, od_conv_b, od_conv_norm_g, od_conv_norm_b, od_w_out, final_norm):
    nbp, seq, dm = x_prompt.shape
    nbs, steps, _ = x_sample.shape
    depth = ada_w.shape[0]
    past_len = page_table.shape[1] * cache_ckv.shape[2]
    tile = min(TOKEN_TILE, seq)
    gp = _Group(True, nbp, seq, tile)
    gs = _Group(False, steps, nbs, nbs)

    n_c = nbp + nbs
    c_all = jnp.pad(jnp.concatenate([c_prompt, c_sample], axis=0), ((0, -n_c % 8), (0, 0)))
    mod = _ada(c_all, ada_w, ada_b)
    mod_p = mod[:, :nbp].reshape(depth, nbp, N_MOD, 1, dm)
    mod_s = jnp.swapaxes(mod[:, nbp:n_c].reshape(depth, nbs, N_MOD, dm), 1, 2)

    w1, w3, w2 = ffn_w1.astype(BF16), ffn_w3.astype(BF16), ffn_w2.astype(BF16)
    tabs_p = _rope_tables(jnp.arange(seq)[None])
    tabs_s = _rope_tables(past_len + jnp.arange(steps)[:, None])

    xp = x_prompt
    xs = _tm(x_sample)
    gla_p, gla_s, pool_p, pool_s, ckv_p, ckv_s, kr_p, kr_s, conv_p, conv_s = ([] for _ in range(10))
    gla_tt = min(TOKEN_TILE, seq)

    for layer in range(depth):
        i = layer // 2
        ng = norm_g[layer]
        last = layer == depth - 1
        xp = _ffn(gp, xp, mod_p[layer], 0, ng[0:1], w1[layer, 0], w3[layer, 0], w2[layer, 0])
        xs = _ffn(gs, xs, mod_s[layer], 0, ng[0:1], w1[layer, 0], w3[layer, 0], w2[layer, 0])
        if layer % 2 == 0:
            w = _even_weights(ev_w_in[i], ev_gate_w2[i], ev_gate_b[i], ev_out_norm[i], ev_pool_w[i],
                              ev_pool_scale[i], ev_w_out[i])
            proj = (ng[1:2], w['wq'], w['wg'], w['w2'], w['gb'], w['wu'])
            out_w = (w['on'], w['pw'], w['ps'], w['wo'])
            q, k, v, r, la, u = _even_in(gp, xp, mod_p[layer], *proj)
            o, s_fin = _gla(q, k, v, la, None, nbp, gla_tt)
            xp, hist = _even_out(gp, xp, o, r, u, None, mod_p[layer], *out_w, 0)
            gla_p.append(s_fin.reshape(nbp, GLA_HEADS, GLA_DK, GLA_DV))
            pool_p.append(hist[:, POOL_PAD - POOL_HIST:])
            q, k, v, r, la, u = _even_in(gs, xs, mod_s[layer], *proj)
            chunked = lambda a: jnp.pad(_tm(a), ((0, 0), (0, GLA_CHUNK - steps), (0, 0)))
            s0 = state_gla[i].reshape(nbs, GLA_HEADS // 2, LANES, GLA_DV)
            o, s_fin = _gla(chunked(q), chunked(k), chunked(v), chunked(la), s0,
                            GLA_SAMPLE_BATCHES if nbs % GLA_SAMPLE_BATCHES == 0 else 1, GLA_CHUNK)
            xs, hist = _even_out(gs, xs, _tm(o[:, :steps]), r, u, _tm(state_pool[i]), mod_s[layer], *out_w, past_len)
            gla_s.append(s_fin.reshape(nbs, GLA_HEADS, GLA_DK, GLA_DV))
            pool_s.append(_tm(hist))
        else:
            w = _odd_weights(od_w_in[i], od_q_norm[i], od_w_uq[i], od_kv_norm[i], od_w_uk[i], od_w_uv[i],
                             od_conv_w[i], od_conv_b[i], od_conv_norm_g[i], od_conv_norm_b[i], od_w_out[i])
            rope_lanes = slice(ROPE_LANE0, ROPE_LANE0 + MLA_ROPE)
            ckv, kr, uc, q, k, v = _odd_in(gp, xp, mod_p[layer], ng[1:2], w, tabs_p, False)
            attn = _attn_prompt(q, k, v, min(ATTN_TILE, seq))
            xp, hist = _odd_out(gp, xp, attn, uc, None, mod_p[layer], w)
            ckv_p.append(ckv)
            kr_p.append(kr[..., rope_lanes])
            conv_p.append(hist[:, CONV_PAD - CONV_HIST:])
            ckv, kr, uc, q, ql = _odd_in(gs, xs, mod_s[layer], ng[1:2], w, tabs_s, True)
            kr = kr[..., rope_lanes]
            qr = q.reshape(steps, nbs, MLA_HEADS, LANES)[..., rope_lanes]
            qr = _tm(qr).reshape(nbs, steps * MLA_HEADS, MLA_ROPE)
            ql = _tm(ql).reshape(nbs, steps * MLA_HEADS, MLA_KV_RANK)
            pad8 = lambda a: jnp.pad(_tm(a), ((0, 0), (0, -steps % 8), (0, 0)))
            cache_kr_t = jnp.swapaxes(cache_krope, 2, 3)
            lat = _attn_sample(page_table, ql, qr, pad8(ckv), pad8(kr), cache_ckv, cache_kr_t, i)
            lat = _tm(lat.reshape(nbs, steps, MLA_HEADS * MLA_KV_RANK))
            xs, hist = _odd_out(gs, xs, lat, uc, _tm(state_conv[i]), mod_s[layer], w)
            ckv_s.append(_tm(ckv))
            kr_s.append(_tm(kr))
            conv_s.append(_tm(hist))
        fin = final_norm[None] if last else None
        xp = _ffn(gp, xp, mod_p[layer], 6, ng[2:3], w1[layer, 1], w3[layer, 1], w2[layer, 1], fin)
        xs = _ffn(gs, xs, mod_s[layer], 6, ng[2:3], w1[layer, 1], w3[layer, 1], w2[layer, 1], fin)

    st = jnp.stack
    return (xp, _tm(xs), st(gla_p), st(gla_s), st(pool_p), st(pool_s), st(ckv_p), st(ckv_s),
            st(kr_p), st(kr_s), st(conv_p), st(conv_s))
```

```python
import functools

import jax
import jax.numpy as jnp
import numpy as np
from jax import lax
from jax.experimental import pallas as pl
from jax.experimental.pallas import tpu as pltpu

F32 = jnp.float32
BF16 = jnp.bfloat16

EPS = 1e-6
NEG = -1e30
N_MOD = 9
GLA_HEADS = 4
GLA_DK = 64
GLA_DV = 128
GLA_GATE_RANK = 16
GLA_TAU = 16.0
GLA_CHUNK = 64
POOL_WINDOWS = (2, 4, 8, 16)
POOL_GROUP = 128
POOL_WIDTH = 512
POOL_HIST = 15
MLA_HEADS = 8
MLA_Q_RANK = 384
MLA_KV_RANK = 256
MLA_NOPE = 64
MLA_ROPE = 32
MLA_V = 64
MLA_SCALE = (MLA_NOPE + MLA_ROPE) ** -0.5
LOG2E = 1.4426950408889634
ROPE_BASE = 10000.0
CONV_WIDTH = 31
CONV_CH = 512
CONV_HIST = 30

LANES = 128
SUBLANES = 8
CONV_ROWS = 32
ROPE_LANE0 = 64
POOL_PAD = 16
CONV_PAD = 32
VMEM_LIMIT = 52 * 2 ** 20
TOKEN_TILE = 1024
ATTN_TILE = 1024
FFN_TOKEN_TILE = 512
FF_TILE = 1408
GLA_SAMPLE_BATCHES = 4


def _cp(*sem):
    return pltpu.CompilerParams(dimension_semantics=sem, vmem_limit_bytes=VMEM_LIMIT)


def _dot(a, b):
    return jnp.dot(a, b, preferred_element_type=F32)


def _dot_nt(a, b):
    return lax.dot_general(a, b, (((1,), (1,)), ((), ())), preferred_element_type=F32)


def _dot_tn(a, b):
    return lax.dot_general(a, b, (((0,), (0,)), ((), ())), preferred_element_type=F32)


def _silu(x):
    return x * jax.nn.sigmoid(x)


def _rms(x, g):
    return x * lax.rsqrt(jnp.mean(x * x, axis=-1, keepdims=True) + EPS) * g


def _modulated(x_ref, g_ref, sh_ref, sc_ref):
    x = x_ref[...]
    h = _rms(x, g_ref[...]) * (1.0 + sc_ref[...]) + sh_ref[...]
    return h.reshape(x.shape[0] * x.shape[1], x.shape[2]).astype(BF16)


class _Group:
    def __init__(self, prompt, lead, rows, tile):
        self.prompt = prompt
        self.grid = (lead, rows // tile) if prompt else (1, 1)
        self.block = (1, tile) if prompt else (lead, rows)

    def act(self, width):
        return pl.BlockSpec(self.block + (width,), lambda b, t, *_: (b, t, 0))

    def mod(self, k):
        if self.prompt:
            return pl.BlockSpec((None, None, 1, self.dm), lambda b, t, *_: (b, k, 0, 0))
        return pl.BlockSpec((None, self.block[1], self.dm), lambda b, t, *_: (k, 0, 0))

    def pos(self):
        if self.prompt:
            return pl.BlockSpec((1, self.block[1], LANES), lambda b, t, *_: (0, t, 0))
        return pl.BlockSpec((self.block[0], 1, LANES), lambda b, t, *_: (0, 0, 0))

    dm = 1024


def _full(a):
    nd = a.ndim
    return pl.BlockSpec(a.shape, lambda *_: (0,) * nd)


def _sds(shape, dtype=F32):
    return jax.ShapeDtypeStruct(shape, dtype)


def _ada_kernel(c_ref, w_ref, b_ref, o_ref):
    c = c_ref[...]
    o_ref[...] = _dot(_silu(c).astype(BF16), w_ref[...].astype(BF16)) + b_ref[...]


def _ada(c_all, ada_w, ada_b):
    depth, dm, n = ada_w.shape
    m = c_all.shape[0]
    tn = dm
    return pl.pallas_call(
        _ada_kernel,
        grid=(depth, n // tn),
        in_specs=[pl.BlockSpec((m, dm), lambda l, j: (0, 0)),
                  pl.BlockSpec((None, dm, tn), lambda l, j: (l, 0, j)),
                  pl.BlockSpec((None, 1, tn), lambda l, j: (l, 0, j))],
        out_specs=pl.BlockSpec((None, m, tn), lambda l, j: (l, 0, j)),
        out_shape=_sds((depth, m, n)),
        compiler_params=_cp("parallel", "parallel"),
    )(c_all, ada_w, ada_b.reshape(depth, 1, n))


def _ffn_kernel(*refs, final):
    if final:
        x_ref, sh_ref, sc_ref, gt_ref, g_ref, w1_ref, w3_ref, w2_ref, fn_ref, o_ref, h_scr, acc_scr = refs
    else:
        x_ref, sh_ref, sc_ref, gt_ref, g_ref, w1_ref, w3_ref, w2_ref, o_ref, h_scr, acc_scr = refs
    j = pl.program_id(2)

    @pl.when(j == 0)
    def _():
        h_scr[...] = _modulated(x_ref, g_ref, sh_ref, sc_ref)
        acc_scr[...] = jnp.zeros_like(acc_scr)

    h = h_scr[...]
    a = _dot(h, w1_ref[...])
    b = _dot(h, w3_ref[...])
    acc_scr[...] += _dot((_silu(a) * b).astype(BF16), w2_ref[...])

    @pl.when(j == pl.num_programs(2) - 1)
    def _():
        x = x_ref[...]
        xn = x + 0.5 * gt_ref[...] * acc_scr[...].reshape(x.shape)
        o_ref[...] = _rms(xn, fn_ref[...]) if final else xn


def _ffn(grp, x, mod, k0, g, w1, w3, w2, final_g=None):
    dm, dff = w1.shape
    tf = FF_TILE if dff % FF_TILE == 0 else dff
    rows = grp.block[0] * grp.block[1]
    final = final_g is not None
    ins = [x, mod, mod, mod, g, w1, w3, w2] + ([final_g] if final else [])
    specs = [grp.act(dm), grp.mod(k0), grp.mod(k0 + 1), grp.mod(k0 + 2), _full(g),
             pl.BlockSpec((dm, tf), lambda b, t, j: (0, j)),
             pl.BlockSpec((dm, tf), lambda b, t, j: (0, j)),
             pl.BlockSpec((tf, dm), lambda b, t, j: (j, 0))] + ([_full(final_g)] if final else [])
    return pl.pallas_call(
        functools.partial(_ffn_kernel, final=final),
        grid=grp.grid + (dff // tf,),
        in_specs=specs,
        out_specs=grp.act(dm),
        out_shape=_sds(x.shape),
        scratch_shapes=[pltpu.VMEM((rows, dm), BF16), pltpu.VMEM((rows, dm), F32)],
        compiler_params=_cp("parallel", "parallel", "arbitrary"),
    )(*ins)


def _log_sigmoid(x):
    return jnp.minimum(x, 0.0) - jnp.log(1.0 + jnp.exp(-jnp.abs(x)))


def _even_in_kernel(x_ref, sh_ref, sc_ref, g_ref, wq_ref, wg_ref, w2_ref, gb_ref, wu_ref,
                    q_ref, k_ref, v_ref, r_ref, la_ref, u_ref):
    h = _modulated(x_ref, g_ref, sh_ref, sc_ref)
    lead = x_ref.shape[:2]
    hk = GLA_HEADS * GLA_DK
    hv = GLA_HEADS * GLA_DV
    z = _dot(h, wq_ref[...])
    q_ref[...] = (z[:, :hk] * GLA_DK ** -0.5).reshape(lead + (hk,))
    k_ref[...] = z[:, hk:2 * hk].reshape(lead + (hk,))
    v_ref[...] = z[:, 2 * hk:2 * hk + hv].reshape(lead + (hv,))
    r_ref[...] = z[:, 2 * hk + hv:].reshape(lead + (hv,))
    g_low = _dot(h, wg_ref[...]).astype(BF16)
    gate = _dot(g_low, w2_ref[...]) + gb_ref[...]
    la_ref[...] = (_log_sigmoid(gate) / GLA_TAU).reshape(lead + (hk,))
    u_ref[...] = _dot(h, wu_ref[...]).reshape(lead + (POOL_WIDTH,))


def _even_in(grp, x, mod, g, wq, wg, w2, gb, wu):
    hk = GLA_HEADS * GLA_DK
    hv = GLA_HEADS * GLA_DV
    widths = (hk, hk, hv, hv, hk, POOL_WIDTH)
    return pl.pallas_call(
        _even_in_kernel,
        grid=grp.grid,
        in_specs=[grp.act(x.shape[-1]), grp.mod(3), grp.mod(4), _full(g), _full(wq), _full(wg),
                  _full(w2), _full(gb), _full(wu)],
        out_specs=[grp.act(w) for w in widths],
        out_shape=[_sds(x.shape[:2] + (w,)) for w in widths],
        compiler_params=_cp("parallel", "parallel"),
    )(x, mod, mod, g, wq, wg, w2, gb, wu)


def _cumsum_rows(tril, x):
    hi = x.astype(BF16)
    r1 = x - hi.astype(F32)
    mid = r1.astype(BF16)
    lo = (r1 - mid.astype(F32)).astype(BF16)
    return _dot(tril, hi) + _dot(tril, mid) + _dot(tril, lo)


def _gla_kernel(*refs, nbb, tt, has_s0):
    if has_s0:
        q_ref, k_ref, v_ref, la_ref, s0_ref, o_ref, so_ref, s_scr = refs
    else:
        q_ref, k_ref, v_ref, la_ref, o_ref, so_ref, s_scr = refs
    t = pl.program_id(1)
    c_rows = GLA_CHUNK

    @pl.when(t == 0)
    def _():
        s_scr[...] = s0_ref[...] if has_s0 else jnp.zeros_like(s_scr)

    row = lax.broadcasted_iota(jnp.int32, (c_rows, c_rows), 0)
    col = lax.broadcasted_iota(jnp.int32, (c_rows, c_rows), 1)
    causal = col <= row
    tril = jnp.where(causal, 1.0, 0.0).astype(BF16)
    lane = lax.broadcasted_iota(jnp.int32, (c_rows, LANES), 1)
    srow = lax.broadcasted_iota(jnp.int32, (LANES, LANES), 0)

    def chunk(c, carry):
        rows = pl.ds(pl.multiple_of(c * c_rows, c_rows), c_rows)
        for n in range(nbb):
            for p in range(GLA_HEADS // 2):
                sl = slice(LANES * p, LANES * (p + 1))
                b = _cumsum_rows(tril, la_ref[n, rows, sl])
                b_last = b[c_rows - 1:c_rows, :]
                k = k_ref[n, rows, sl]
                qi = q_ref[n, rows, sl] * jnp.exp(b)
                ki = (k * jnp.exp(-b)).astype(BF16)
                kd = (k * jnp.exp(b_last - b)).astype(BF16)
                s_old = s_scr[n, p]
                s_bf = s_old.astype(BF16)
                decay = jnp.transpose(jnp.broadcast_to(jnp.exp(b_last), (LANES, LANES)))
                upd = []
                for hh in range(2):
                    mine = (lane < GLA_DK) if hh == 0 else (lane >= GLA_DK)
                    qh = jnp.where(mine, qi, 0.0).astype(BF16)
                    att = jnp.where(causal, _dot_nt(qh, ki), 0.0).astype(BF16)
                    vs = slice(GLA_DV * (2 * p + hh), GLA_DV * (2 * p + hh + 1))
                    vh = v_ref[n, rows, vs].astype(BF16)
                    o_ref[n, rows, vs] = _dot(qh, s_bf) + _dot(att, vh)
                    upd.append(_dot_tn(kd, vh))
                s_scr[n, p] = decay * s_old + jnp.where(srow < GLA_DK, upd[0], upd[1])
        return carry

    lax.fori_loop(0, tt // c_rows, chunk, 0, unroll=min(2, tt // c_rows))

    @pl.when(t == pl.num_programs(1) - 1)
    def _():
        so_ref[...] = s_scr[...]


def _gla(q, k, v, la, s0, nbb, tt):
    nb, nt, hk = q.shape
    hv = v.shape[-1]
    has_s0 = s0 is not None
    act = lambda w: pl.BlockSpec((nbb, tt, w), lambda b, t: (b, t, 0))
    st = pl.BlockSpec((nbb, GLA_HEADS // 2, LANES, GLA_DV), lambda b, t: (b, 0, 0, 0))
    return pl.pallas_call(
        functools.partial(_gla_kernel, nbb=nbb, tt=tt, has_s0=has_s0),
        grid=(nb // nbb, nt // tt),
        in_specs=[act(hk), act(hk), act(hv), act(hk)] + ([st] if has_s0 else []),
        out_specs=[act(hv), st],
        out_shape=[_sds((nb, nt, hv)), _sds((nb, GLA_HEADS // 2, LANES, GLA_DV))],
        scratch_shapes=[pltpu.VMEM((nbb, GLA_HEADS // 2, LANES, GLA_DV), F32)],
        compiler_params=_cp("parallel", "arbitrary"),
    )(*([q, k, v, la] + ([s0] if has_s0 else [])))


def _gla_gate(o, r, on):
    outs = []
    for h in range(GLA_HEADS):
        sl = slice(GLA_DV * h, GLA_DV * (h + 1))
        outs.append(_rms(o[:, sl], on[:, sl]))
    return jnp.concatenate(outs, axis=-1) * _silu(r)


def _even_mix_out(x_ref, gt_ref, og, pooled, pw_ref, ps_ref, wo_ref, xo_ref):
    mixed = [_dot(pooled[g].astype(BF16), pw_ref[g]) for g in range(len(POOL_WINDOWS))]
    mixed = jnp.concatenate(mixed, axis=-1) * ps_ref[...]
    n_o = og.shape[-1]
    y = _dot(og.astype(BF16), wo_ref[:n_o, :]) + _dot(mixed.astype(BF16), wo_ref[n_o:, :])
    x = x_ref[...]
    xo_ref[...] = x + gt_ref[...] * y.reshape(x.shape)


def _even_out_prompt_kernel(x_ref, o_ref, r_ref, u_ref, gt_ref, on_ref, pw_ref, ps_ref, wo_ref,
                            xo_ref, ho_ref, hb, *, tm, p0):
    t = pl.program_id(1)

    @pl.when(t == 0)
    def _():
        hb[0:POOL_PAD, :] = jnp.zeros((POOL_PAD, POOL_WIDTH), F32)

    u = u_ref[0]
    hb[POOL_PAD:POOL_PAD + tm, :] = u
    pos = t * tm + lax.broadcasted_iota(jnp.int32, (tm, 1), 0)
    pooled = []
    for g, w in enumerate(POOL_WINDOWS):
        sl = slice(POOL_GROUP * g, POOL_GROUP * (g + 1))
        win = hb[POOL_PAD - (w - 1):POOL_PAD - (w - 1) + tm, sl]
        for j in range(w - 2, -1, -1):
            win = win + hb[POOL_PAD - j:POOL_PAD - j + tm, sl]
        cnt = jnp.minimum(p0 + pos + 1, w).astype(F32)
        pooled.append(win / cnt - u[:, sl])
    og = _gla_gate(o_ref[0], r_ref[0], on_ref[...])
    _even_mix_out(x_ref, gt_ref, og, pooled, pw_ref, ps_ref, wo_ref, xo_ref)
    hb[0:POOL_PAD, :] = hb[tm:tm + POOL_PAD, :]

    @pl.when(t == pl.num_programs(1) - 1)
    def _():
        ho_ref[0] = hb[0:POOL_PAD, :]


def _even_out_sample_kernel(x_ref, o_ref, r_ref, u_ref, hi_ref, gt_ref, on_ref, pw_ref, ps_ref, wo_ref,
                            xo_ref, hn_ref, *, p0):
    steps, nb = u_ref.shape[:2]

    def slab(i, sl):
        return hi_ref[i, :, sl] if i < POOL_HIST else u_ref[i - POOL_HIST, :, sl]

    pooled = []
    for g, w in enumerate(POOL_WINDOWS):
        sl = slice(POOL_GROUP * g, POOL_GROUP * (g + 1))
        rows = []
        for t in range(steps):
            win = slab(POOL_HIST + t - (w - 1), sl)
            for j in range(w - 2, -1, -1):
                win = win + slab(POOL_HIST + t - j, sl)
            rows.append(win / float(min(p0 + t + 1, w)) - u_ref[t, :, sl])
        pooled.append(jnp.concatenate(rows, axis=0))
    flat = lambda ref: ref[...].reshape(steps * nb, ref.shape[-1])
    og = _gla_gate(flat(o_ref), flat(r_ref), on_ref[...])
    _even_mix_out(x_ref, gt_ref, og, pooled, pw_ref, ps_ref, wo_ref, xo_ref)
    for i in range(POOL_HIST):
        hn_ref[i] = slab(steps + i, slice(None))


def _even_out(grp, x, o, r, u, hist, mod, on, pw, ps, wo, p0):
    dm = x.shape[-1]
    common = [mod, on, pw, ps, wo]
    common_specs = [grp.mod(5), _full(on), _full(pw), _full(ps), _full(wo)]
    acts = [grp.act(dm), grp.act(o.shape[-1]), grp.act(r.shape[-1]), grp.act(u.shape[-1])]
    if grp.prompt:
        tm = grp.block[1]
        return pl.pallas_call(
            functools.partial(_even_out_prompt_kernel, tm=tm, p0=p0),
            grid=grp.grid,
            in_specs=acts + common_specs,
            out_specs=[grp.act(dm), pl.BlockSpec((1, POOL_PAD, POOL_WIDTH), lambda b, t: (b, 0, 0))],
            out_shape=[_sds(x.shape), _sds((x.shape[0], POOL_PAD, POOL_WIDTH))],
            scratch_shapes=[pltpu.VMEM((POOL_PAD + tm, POOL_WIDTH), F32)],
            compiler_params=_cp("parallel", "arbitrary"),
        )(x, o, r, u, *common)
    return pl.pallas_call(
        functools.partial(_even_out_sample_kernel, p0=p0),
        grid=grp.grid,
        in_specs=acts + [_full(hist)] + common_specs,
        out_specs=[grp.act(dm), _full(hist)],
        out_shape=[_sds(x.shape), _sds(hist.shape)],
        compiler_params=_cp("parallel", "arbitrary"),
    )(x, o, r, u, hist, *common)


def _rope_slab(x, cs, s1, s2, lead):
    shp = lead + (LANES,)
    back = pltpu.roll(x, LANES - MLA_ROPE // 2, 1).reshape(shp)
    fwd = pltpu.roll(x, MLA_ROPE // 2, 1).reshape(shp)
    out = x.reshape(shp) * cs + back * s1 + fwd * s2
    return out.reshape(x.shape)


def _odd_in_kernel(*refs, sample):
    (x_ref, sh_ref, sc_ref, g_ref, wcq_ref, wckv_ref, wkr_ref, wga_ref, wgg_ref, qn_ref, wuq_ref,
     kvn_ref, cs_ref, s1_ref, s2_ref) = refs[:15]
    if sample:
        wkl_ref, ckv_ref, kr_ref, uc_ref, q_ref, ql_ref = refs[15:]
    else:
        wuk_ref, wuv_ref, ckv_ref, kr_ref, uc_ref, q_ref, k_ref, v_ref = refs[15:]
    lead = x_ref.shape[:2]
    h = _modulated(x_ref, g_ref, sh_ref, sc_ref)
    cs, s1, s2 = cs_ref[...], s1_ref[...], s2_ref[...]

    cq = _rms(_dot(h, wcq_ref[...]), qn_ref[...]).astype(BF16)
    q = _dot(cq, wuq_ref[...]) * (MLA_SCALE * LOG2E)
    q = jnp.concatenate(
        [_rope_slab(q[:, LANES * i:LANES * (i + 1)], cs, s1, s2, lead) for i in range(MLA_HEADS)], axis=-1)
    q_ref[...] = q.astype(BF16).reshape(lead + (MLA_HEADS * LANES,))

    ckv = _rms(_dot(h, wckv_ref[...]), kvn_ref[...])
    ckv_ref[...] = ckv.reshape(lead + (MLA_KV_RANK,))
    kr = _rope_slab(_dot(h, wkr_ref[...]), cs, s1, s2, lead)
    kr_ref[...] = kr.reshape(lead + (LANES,))
    uc_ref[...] = (_dot(h, wga_ref[...]) * jax.nn.sigmoid(_dot(h, wgg_ref[...]))).reshape(lead + (CONV_CH,))

    if sample:
        qb = q.astype(BF16)
        for i in range(MLA_HEADS):
            ql = _dot(qb[:, LANES * i:LANES * (i + 1)], wkl_ref[i])
            ql_ref[:, :, MLA_KV_RANK * i:MLA_KV_RANK * (i + 1)] = ql.astype(BF16).reshape(lead + (MLA_KV_RANK,))
    else:
        cb = ckv.astype(BF16)
        kn = _dot(cb, wuk_ref[...])
        kn = jnp.concatenate([kn[:, LANES * i:LANES * (i + 1)] + kr for i in range(MLA_HEADS)], axis=-1)
        k_ref[...] = kn.astype(BF16).reshape(lead + (MLA_HEADS * LANES,))
        vlane = lax.broadcasted_iota(jnp.int32, (1, MLA_HEADS * LANES), 1) & (LANES - 1)
        v = _dot(cb, wuv_ref[...]) + jnp.where(vlane == MLA_V, 1.0, 0.0)
        v_ref[...] = v.astype(BF16).reshape(lead + (MLA_HEADS * LANES,))


def _odd_in(grp, x, mod, g, w, tabs, sample):
    lead = x.shape[:2]
    hl = MLA_HEADS * LANES
    ins = [x, mod, mod, g, w['wcq'], w['wckv'], w['wkr'], w['wga'], w['wgg'], w['qn'], w['wuq'], w['kvn'], *tabs]
    specs = [grp.act(x.shape[-1]), grp.mod(3), grp.mod(4)] + [_full(a) for a in ins[3:12]] + [grp.pos()] * 3
    outs = [(MLA_KV_RANK, F32), (LANES, F32), (CONV_CH, F32), (hl, BF16)]
    if sample:
        ins.append(w['wkl'])
        outs.append((MLA_HEADS * MLA_KV_RANK, BF16))
    else:
        ins += [w['wuk'], w['wuv']]
        outs += [(hl, BF16), (hl, BF16)]
    specs += [_full(a) for a in ins[15:]]
    return pl.pallas_call(
        functools.partial(_odd_in_kernel, sample=sample),
        grid=grp.grid,
        in_specs=specs,
        out_specs=[grp.act(wd) for wd, _ in outs],
        out_shape=[_sds(lead + (wd,), dt) for wd, dt in outs],
        compiler_params=_cp("parallel", "parallel"),
    )(*ins)


def _softmax_step(s, m_prev):
    m_new = jnp.maximum(m_prev, jnp.max(s, axis=-1, keepdims=True))
    return jnp.exp2(s - m_new[:, :1]), jnp.exp2(m_prev - m_new), m_new


def _attn_prompt_kernel(qi_ref, ki_ref, q_ref, k_ref, v_ref, o_ref, m_scr, acc_scr, *, tq):
    step = pl.program_id(1)
    qi = qi_ref[step]
    ki = ki_ref[step]

    @pl.when(ki == 0)
    def _():
        m_scr[...] = jnp.full_like(m_scr, NEG)
        acc_scr[...] = jnp.zeros_like(acc_scr)

    def update(diagonal):
        if diagonal:
            row = lax.broadcasted_iota(jnp.int32, (tq, tq), 0)
            col = lax.broadcasted_iota(jnp.int32, (tq, tq), 1)
        for h in range(MLA_HEADS):
            sl = slice(LANES * h, LANES * (h + 1))
            s = _dot_nt(q_ref[0, :, sl], k_ref[0, :, sl])
            if diagonal:
                s = jnp.where(col <= row, s, NEG)
            p, alpha, m_new = _softmax_step(s, m_scr[:, sl])
            acc_scr[:, sl] = alpha * acc_scr[:, sl] + _dot(p.astype(BF16), v_ref[0, :, sl])
            m_scr[:, sl] = m_new

    @pl.when(ki < qi)
    def _():
        update(False)

    @pl.when(ki == qi)
    def _():
        update(True)
        for h in range(MLA_HEADS):
            sl = slice(LANES * h, LANES * (h + 1))
            acc = acc_scr[:, sl]
            o_ref[0, :, sl] = (acc / acc[:, MLA_V:MLA_V + 1]).astype(BF16)


def _attn_prompt(q, k, v, tq):
    nb, nt, hl = q.shape
    pairs = [(i, j) for i in range(nt // tq) for j in range(i + 1)]
    qi_tab = jnp.asarray([p[0] for p in pairs], jnp.int32)
    ki_tab = jnp.asarray([p[1] for p in pairs], jnp.int32)
    qs = pl.BlockSpec((1, tq, hl), lambda b, s, qt, kt: (b, qt[s], 0))
    kv = pl.BlockSpec((1, tq, hl), lambda b, s, qt, kt: (b, kt[s], 0))
    return pl.pallas_call(
        functools.partial(_attn_prompt_kernel, tq=tq),
        grid_spec=pltpu.PrefetchScalarGridSpec(
            num_scalar_prefetch=2,
            grid=(nb, len(pairs)),
            in_specs=[qs, kv, kv],
            out_specs=qs,
            scratch_shapes=[pltpu.VMEM((tq, hl), F32)] * 2),
        out_shape=_sds(q.shape, BF16),
        compiler_params=_cp("parallel", "arbitrary"),
    )(qi_tab, ki_tab, q, k, v)


def _attn_sample_kernel(pt_ref, ql_ref, qr_ref, cn_ref, kn_ref, ckv_hbm, kr_hbm, o_ref,
                        ckv_buf, kr_buf, pg_scr, s_scr, p_scr, sem, *, npg, layer):
    b = pl.program_id(0)
    nb = pl.num_programs(0)
    slot = lax.rem(b, 2)
    ql = ql_ref[0]
    qr = qr_ref[0]
    rows = ql.shape[0]

    def page_copies(batch, i, sl):
        page = pt_ref[batch, i]
        return (pltpu.make_async_copy(ckv_hbm.at[layer, page], ckv_buf.at[sl, i], sem.at[0, sl]),
                pltpu.make_async_copy(kr_hbm.at[layer, page], kr_buf.at[sl, i], sem.at[1, sl]))

    def start_page(batch, i, sl):
        for cp in page_copies(batch, i, sl):
            cp.start()

    def wait_page(i, carry):
        for cp in page_copies(b, i, slot):
            cp.wait()
        return carry

    def start_first(i, carry):
        start_page(0, i, 0)
        return carry

    @pl.when(b == 0)
    def _():
        lax.fori_loop(0, npg, start_first, 0)

    lax.fori_loop(0, npg, wait_page, 0)

    def score_pages(prefetch):
        def body(i, carry):
            if prefetch:
                start_page(b + 1, i, 1 - slot)
            page = ckv_buf[slot, i].astype(BF16)
            pg_scr[i] = page
            s_scr[i] = _dot_nt(ql, page) + _dot(qr, kr_buf[slot, i].astype(BF16))
            return carry
        lax.fori_loop(0, npg, body, 0, unroll=True)

    @pl.when(b + 1 < nb)
    def _():
        score_pages(True)

    @pl.when(b + 1 == nb)
    def _():
        score_pages(False)

    n_new = cn_ref.shape[1]
    pad = jnp.zeros((LANES - n_new, MLA_KV_RANK), F32)
    cn = jnp.concatenate([cn_ref[0], pad], axis=0).astype(BF16)
    kn = jnp.concatenate([kn_ref[0], pad[:, :MLA_ROPE]], axis=0).astype(BF16)
    step = lax.shift_right_logical(lax.broadcasted_iota(jnp.int32, (rows, LANES), 0), MLA_HEADS.bit_length() - 1)
    col = lax.broadcasted_iota(jnp.int32, (rows, LANES), 1)
    s_new = jnp.where(col <= step, _dot_nt(ql, cn) + _dot_nt(qr, kn), NEG)

    s = s_scr[...]
    m = jnp.maximum(jnp.max(jnp.max(s, axis=0), axis=-1, keepdims=True),
                    jnp.max(s_new, axis=-1, keepdims=True))
    p = jnp.exp2(s - m)
    p_new = jnp.exp2(s_new - m)
    denom = jnp.sum(jnp.sum(p, axis=0), axis=-1, keepdims=True) + jnp.sum(p_new, axis=-1, keepdims=True)
    p_scr[...] = p.astype(BF16)

    def weigh_page(i, acc):
        return acc + _dot(p_scr[i], pg_scr[i])

    acc = lax.fori_loop(0, npg, weigh_page, _dot(p_new.astype(BF16), cn), unroll=True)
    o_ref[0] = acc / denom


def _attn_sample(page_table, ql, qr, cn, kn, cache_ckv, cache_kr, layer):
    nb, rows, rank = ql.shape
    npg = page_table.shape[1]
    page, rope = cache_ckv.shape[2], cache_kr.shape[2]
    assert page == LANES, "one cache page must fill one lane tile of scores"
    per_b = lambda a: pl.BlockSpec((1,) + a.shape[1:], lambda b, pt: (b, 0, 0))
    hbm = pl.BlockSpec(memory_space=pl.ANY)
    return pl.pallas_call(
        functools.partial(_attn_sample_kernel, npg=npg, layer=layer),
        grid_spec=pltpu.PrefetchScalarGridSpec(
            num_scalar_prefetch=1,
            grid=(nb,),
            in_specs=[per_b(ql), per_b(qr), per_b(cn), per_b(kn), hbm, hbm],
            out_specs=pl.BlockSpec((1, rows, rank), lambda b, pt: (b, 0, 0)),
            scratch_shapes=[pltpu.VMEM((2, npg, page, rank), F32),
                            pltpu.VMEM((2, npg, rope, page), F32),
                            pltpu.VMEM((npg, page, rank), BF16),
                            pltpu.VMEM((npg, rows, page), F32),
                            pltpu.VMEM((npg, rows, page), BF16),
                            pltpu.SemaphoreType.DMA((2, 2))]),
        out_shape=_sds((nb, rows, rank)),
        compiler_params=_cp("arbitrary"),
    )(page_table, ql, qr, cn, kn, cache_ckv, cache_kr)


def _conv_norm_act(cv, cb_ref, lg_ref, lb_ref):
    cv = cv + cb_ref[...]
    mu = jnp.mean(cv, axis=-1, keepdims=True)
    d = cv - mu
    y = d * lax.rsqrt(jnp.mean(d * d, axis=-1, keepdims=True) + EPS)
    return _silu(y * lg_ref[...] + lb_ref[...])


def _odd_out_prompt_kernel(x_ref, a_ref, uc_ref, gt_ref, cw_ref, cb_ref, lg_ref, lb_ref, woa_ref, woc_ref,
                           xo_ref, ho_ref, hb, hs, cvb, *, tm):
    t = pl.program_id(1)

    @pl.when(t == 0)
    def _():
        hb[0:CONV_PAD, :] = jnp.zeros((CONV_PAD, CONV_CH), F32)

    hb[CONV_PAD:CONV_PAD + tm, :] = uc_ref[0]
    for sft in range(1, SUBLANES):
        hs[sft - 1] = hb[sft:sft + hs.shape[1], :]
    base = CONV_PAD - CONV_HIST

    def chunk(c, carry):
        acc = None
        for j in range(CONV_WIDTH):
            whole, sft = divmod(base + j, SUBLANES)
            rows = pl.ds(pl.multiple_of(c * CONV_ROWS + whole * SUBLANES, SUBLANES), CONV_ROWS)
            term = cw_ref[j:j + 1, :] * (hb[rows, :] if sft == 0 else hs[sft - 1, rows, :])
            acc = term if acc is None else acc + term
        cvb[pl.ds(pl.multiple_of(c * CONV_ROWS, CONV_ROWS), CONV_ROWS), :] = acc
        return carry

    lax.fori_loop(0, tm // CONV_ROWS, chunk, 0)
    cv = _conv_norm_act(cvb[...], cb_ref, lg_ref, lb_ref)
    y = _dot(a_ref[0], woa_ref[...]) + _dot(cv.astype(BF16), woc_ref[...])
    xo_ref[0] = x_ref[0] + gt_ref[...] * y
    hb[0:CONV_PAD, :] = hb[tm:tm + CONV_PAD, :]

    @pl.when(t == pl.num_programs(1) - 1)
    def _():
        ho_ref[0] = hb[0:CONV_PAD, :]


def _odd_out_sample_kernel(x_ref, lat_ref, uc_ref, hi_ref, gt_ref, cw_ref, cb_ref, lg_ref, lb_ref, wuv_ref,
                           woa_ref, woc_ref, xo_ref, hn_ref):
    steps, nb = uc_ref.shape[:2]

    def slab(i):
        return hi_ref[i] if i < CONV_HIST else uc_ref[i - CONV_HIST]

    rows = []
    for t in range(steps):
        cv = cw_ref[0:1, :] * slab(t)
        for j in range(1, CONV_WIDTH):
            cv = cv + cw_ref[j:j + 1, :] * slab(t + j)
        rows.append(cv)
    cv = _conv_norm_act(jnp.concatenate(rows, axis=0), cb_ref, lg_ref, lb_ref)
    lat = lat_ref[...].reshape(steps * nb, lat_ref.shape[-1]).astype(BF16)
    attn = _dot(lat, wuv_ref[...]).astype(BF16)
    y = _dot(attn, woa_ref[...]) + _dot(cv.astype(BF16), woc_ref[...])
    x = x_ref[...]
    xo_ref[...] = x + gt_ref[...] * y.reshape(x.shape)
    for i in range(CONV_HIST):
        hn_ref[i] = slab(steps + i)


def _odd_out(grp, x, a, uc, hist, mod, w):
    dm = x.shape[-1]
    conv = [w['cw'], w['cb'], w['lg'], w['lb']]
    if grp.prompt:
        tm = grp.block[1]
        ins = [x, a, uc, mod] + conv + [w['woa_pad'], w['woc']]
        return pl.pallas_call(
            functools.partial(_odd_out_prompt_kernel, tm=tm),
            grid=grp.grid,
            in_specs=[grp.act(dm), grp.act(a.shape[-1]), grp.act(CONV_CH), grp.mod(5)] + [_full(v) for v in ins[4:]],
            out_specs=[grp.act(dm), pl.BlockSpec((1, CONV_PAD, CONV_CH), lambda b, t: (b, 0, 0))],
            out_shape=[_sds(x.shape), _sds((x.shape[0], CONV_PAD, CONV_CH))],
            scratch_shapes=[pltpu.VMEM((CONV_PAD + tm, CONV_CH), F32),
                            pltpu.VMEM((SUBLANES - 1, CONV_PAD + tm - SUBLANES, CONV_CH), F32),
                            pltpu.VMEM((tm, CONV_CH), F32)],
            compiler_params=_cp("parallel", "arbitrary"),
        )(*ins)
    ins = [x, a, uc, hist, mod] + conv + [w['wuv_bd'], w['woa'], w['woc']]
    return pl.pallas_call(
        _odd_out_sample_kernel,
        grid=grp.grid,
        in_specs=[grp.act(dm), grp.act(a.shape[-1]), grp.act(CONV_CH), _full(hist), grp.mod(5)]
        + [_full(v) for v in ins[5:]],
        out_specs=[grp.act(dm), _full(hist)],
        out_shape=[_sds(x.shape), _sds(hist.shape)],
        compiler_params=_cp("parallel", "arbitrary"),
    )(*ins)


def _head_pad(w, heads, width, offset=0):
    kdim = w.shape[0]
    w = w.reshape(kdim, heads, width)
    w = jnp.pad(w, ((0, 0), (0, 0), (offset, LANES - width - offset)))
    return w.reshape(kdim, heads * LANES)


def _rope_tables(pos):
    half = MLA_ROPE // 2
    freqs = ROPE_BASE ** (-jnp.arange(half, dtype=F32) / half)
    ang = pos.astype(F32)[..., None] * freqs
    cos, sin = jnp.cos(ang), jnp.sin(ang)
    zeros = jnp.zeros_like(cos)
    lead = jnp.ones(pos.shape + (ROPE_LANE0,), F32)
    tail = jnp.zeros(pos.shape + (LANES - ROPE_LANE0 - MLA_ROPE,), F32)
    cs = jnp.concatenate([lead, cos, cos, tail], axis=-1)
    s1 = jnp.concatenate([0 * lead, -sin, zeros, tail], axis=-1)
    s2 = jnp.concatenate([0 * lead, zeros, sin, tail], axis=-1)
    return cs, s1, s2


def _even_weights(w_in, gate_w2, gate_b, out_norm, pool_w, pool_scale, w_out):
    hk = GLA_HEADS * GLA_DK
    hv = GLA_HEADS * GLA_DV
    n_main = 2 * hk + 2 * hv
    wg = jnp.pad(w_in[:, n_main:n_main + GLA_GATE_RANK], ((0, 0), (0, LANES - GLA_GATE_RANK)))
    w2 = jnp.pad(gate_w2, ((0, LANES - GLA_GATE_RANK), (0, 0)))
    return dict(wq=w_in[:, :n_main].astype(BF16), wg=wg.astype(BF16), w2=w2.astype(BF16),
                gb=gate_b[None], wu=w_in[:, n_main + GLA_GATE_RANK:].astype(BF16),
                on=out_norm[None], pw=pool_w.astype(BF16), ps=pool_scale[None], wo=w_out.astype(BF16))


def _odd_weights(w_in, q_norm, w_uq, kv_norm, w_uk, w_uv, conv_w, conv_b, ln_g, ln_b, w_out):
    c0, c1, c2 = MLA_Q_RANK, MLA_Q_RANK + MLA_KV_RANK, MLA_Q_RANK + MLA_KV_RANK + MLA_ROPE
    n_attn = MLA_HEADS * MLA_V
    wkr = jnp.pad(w_in[:, c1:c2], ((0, 0), (ROPE_LANE0, LANES - ROPE_LANE0 - MLA_ROPE)))
    wuk = w_uk.reshape(MLA_KV_RANK, MLA_HEADS * MLA_NOPE)
    wuv = w_uv.reshape(MLA_KV_RANK, MLA_HEADS * MLA_V)
    wkl = jnp.pad(jnp.transpose(w_uk, (1, 2, 0)), ((0, 0), (0, LANES - MLA_NOPE), (0, 0)))
    eye = jnp.eye(MLA_HEADS, dtype=F32)
    wuv_bd = (eye[:, None, :, None] * jnp.transpose(w_uv, (1, 0, 2))[:, :, None, :]).reshape(
        MLA_HEADS * MLA_KV_RANK, n_attn)
    woa = w_out[:n_attn]
    woa_pad = jnp.pad(woa.reshape(MLA_HEADS, MLA_V, -1), ((0, 0), (0, LANES - MLA_V), (0, 0))).reshape(
        MLA_HEADS * LANES, -1)
    return dict(wcq=w_in[:, :c0].astype(BF16), wckv=w_in[:, c0:c1].astype(BF16), wkr=wkr.astype(BF16),
                wga=w_in[:, c2:c2 + CONV_CH].astype(BF16), wgg=w_in[:, c2 + CONV_CH:].astype(BF16),
                qn=q_norm[None], wuq=_head_pad(w_uq, MLA_HEADS, MLA_NOPE + MLA_ROPE).astype(BF16),
                kvn=kv_norm[None], wuk=_head_pad(wuk, MLA_HEADS, MLA_NOPE).astype(BF16),
                wuv=_head_pad(wuv, MLA_HEADS, MLA_V).astype(BF16), wkl=wkl.astype(BF16),
                wuv_bd=wuv_bd.astype(BF16), cw=conv_w, cb=conv_b[None], lg=ln_g[None], lb=ln_b[None],
                woa=woa.astype(BF16), woa_pad=woa_pad.astype(BF16), woc=w_out[n_attn:].astype(BF16))


def _tm(x):
    return jnp.swapaxes(x, 0, 1)


def kernel(x_prompt, x_sample, state_gla, state_pool, cache_ckv, cache_krope, state_conv, page_table, c_prompt, c_sample, ada_w, ada_b, norm_g, ffn_w1, ffn_w3, ffn_w2, ev_w_in, ev_gate_w2, ev_gate_b, ev_out_norm, ev_pool_w, ev_pool_scale, ev_w_out, od_w_in, od_q_norm, od_w_uq, od_kv_norm, od_w_uk, od_w_uv, od_conv_w, od_conv_b, od_conv_norm_g, od_conv_norm_b, od_w_out, final_norm):
    nbp, seq, dm = x_prompt.shape
    nbs, steps, _ = x_sample.shape
    depth = ada_w.shape[0]
    past_len = page_table.shape[1] * cache_ckv.shape[2]
    tile = min(TOKEN_TILE, seq)
    gp = _Group(True, nbp, seq, tile)
    gf = _Group(True, nbp, seq, min(FFN_TOKEN_TILE, seq))
    gs = _Group(False, steps, nbs, nbs)

    n_c = nbp + nbs
    c_all = jnp.pad(jnp.concatenate([c_prompt, c_sample], axis=0), ((0, -n_c % 8), (0, 0)))
    mod = _ada(c_all, ada_w, ada_b)
    mod_p = mod[:, :nbp].reshape(depth, nbp, N_MOD, 1, dm)
    mod_s = jnp.swapaxes(mod[:, nbp:n_c].reshape(depth, nbs, N_MOD, dm), 1, 2)

    w1, w3, w2 = ffn_w1.astype(BF16), ffn_w3.astype(BF16), ffn_w2.astype(BF16)
    tabs_p = _rope_tables(jnp.arange(seq)[None])
    tabs_s = _rope_tables(past_len + jnp.arange(steps)[:, None])

    xp = x_prompt
    xs = _tm(x_sample)
    gla_p, gla_s, pool_p, pool_s, ckv_p, ckv_s, kr_p, kr_s, conv_p, conv_s = ([] for _ in range(10))
    gla_tt = min(TOKEN_TILE, seq)

    for layer in range(depth):
        i = layer // 2
        ng = norm_g[layer]
        last = layer == depth - 1
        xp = _ffn(gf, xp, mod_p[layer], 0, ng[0:1], w1[layer, 0], w3[layer, 0], w2[layer, 0])
        xs = _ffn(gs, xs, mod_s[layer], 0, ng[0:1], w1[layer, 0], w3[layer, 0], w2[layer, 0])
        if layer % 2 == 0:
            w = _even_weights(ev_w_in[i], ev_gate_w2[i], ev_gate_b[i], ev_out_norm[i], ev_pool_w[i],
                              ev_pool_scale[i], ev_w_out[i])
            proj = (ng[1:2], w['wq'], w['wg'], w['w2'], w['gb'], w['wu'])
            out_w = (w['on'], w['pw'], w['ps'], w['wo'])
            q, k, v, r, la, u = _even_in(gp, xp, mod_p[layer], *proj)
            o, s_fin = _gla(q, k, v, la, None, nbp, gla_tt)
            xp, hist = _even_out(gp, xp, o, r, u, None, mod_p[layer], *out_w, 0)
            gla_p.append(s_fin.reshape(nbp, GLA_HEADS, GLA_DK, GLA_DV))
            pool_p.append(hist[:, POOL_PAD - POOL_HIST:])
            q, k, v, r, la, u = _even_in(gs, xs, mod_s[layer], *proj)
            chunked = lambda a: jnp.pad(_tm(a), ((0, 0), (0, GLA_CHUNK - steps), (0, 0)))
            s0 = state_gla[i].reshape(nbs, GLA_HEADS // 2, LANES, GLA_DV)
            o, s_fin = _gla(chunked(q), chunked(k), chunked(v), chunked(la), s0,
                            GLA_SAMPLE_BATCHES if nbs % GLA_SAMPLE_BATCHES == 0 else 1, GLA_CHUNK)
            xs, hist = _even_out(gs, xs, _tm(o[:, :steps]), r, u, _tm(state_pool[i]), mod_s[layer], *out_w, past_len)
            gla_s.append(s_fin.reshape(nbs, GLA_HEADS, GLA_DK, GLA_DV))
            pool_s.append(_tm(hist))
        else:
            w = _odd_weights(od_w_in[i], od_q_norm[i], od_w_uq[i], od_kv_norm[i], od_w_uk[i], od_w_uv[i],
                             od_conv_w[i], od_conv_b[i], od_conv_norm_g[i], od_conv_norm_b[i], od_w_out[i])
            rope_lanes = slice(ROPE_LANE0, ROPE_LANE0 + MLA_ROPE)
            ckv, kr, uc, q, k, v = _odd_in(gp, xp, mod_p[layer], ng[1:2], w, tabs_p, False)
            attn = _attn_prompt(q, k, v, min(ATTN_TILE, seq))
            xp, hist = _odd_out(gp, xp, attn, uc, None, mod_p[layer], w)
            ckv_p.append(ckv)
            kr_p.append(kr[..., rope_lanes])
            conv_p.append(hist[:, CONV_PAD - CONV_HIST:])
            ckv, kr, uc, q, ql = _odd_in(gs, xs, mod_s[layer], ng[1:2], w, tabs_s, True)
            kr = kr[..., rope_lanes]
            qr = q.reshape(steps, nbs, MLA_HEADS, LANES)[..., rope_lanes]
            qr = _tm(qr).reshape(nbs, steps * MLA_HEADS, MLA_ROPE)
            ql = _tm(ql).reshape(nbs, steps * MLA_HEADS, MLA_KV_RANK)
            pad8 = lambda a: jnp.pad(_tm(a), ((0, 0), (0, -steps % 8), (0, 0)))
            cache_kr_t = jnp.swapaxes(cache_krope, 2, 3)
            lat = _attn_sample(page_table, ql, qr, pad8(ckv), pad8(kr), cache_ckv, cache_kr_t, i)
            lat = _tm(lat.reshape(nbs, steps, MLA_HEADS * MLA_KV_RANK))
            xs, hist = _odd_out(gs, xs, lat, uc, _tm(state_conv[i]), mod_s[layer], w)
            ckv_s.append(_tm(ckv))
            kr_s.append(_tm(kr))
            conv_s.append(_tm(hist))
        fin = final_norm[None] if last else None
        xp = _ffn(gf, xp, mod_p[layer], 6, ng[2:3], w1[layer, 1], w3[layer, 1], w2[layer, 1], fin)
        xs = _ffn(gs, xs, mod_s[layer], 6, ng[2:3], w1[layer, 1], w3[layer, 1], w2[layer, 1], fin)

    st = jnp.stack
    return (xp, _tm(xs), st(gla_p), st(gla_s), st(pool_p), st(pool_s), st(ckv_p), st(ckv_s),
            st(kr_p), st(kr_s), st(conv_p), st(conv_s))
```

```python
import functools

import jax
import jax.numpy as jnp
import numpy as np
from jax import lax
from jax.experimental import pallas as pl
from jax.experimental.pallas import tpu as pltpu

F32 = jnp.float32
BF16 = jnp.bfloat16

EPS = 1e-6
NEG = -1e30
N_MOD = 9
GLA_HEADS = 4
GLA_DK = 64
GLA_DV = 128
GLA_GATE_RANK = 16
GLA_TAU = 16.0
GLA_CHUNK = 64
POOL_WINDOWS = (2, 4, 8, 16)
POOL_GROUP = 128
POOL_WIDTH = 512
POOL_HIST = 15
MLA_HEADS = 8
MLA_Q_RANK = 384
MLA_KV_RANK = 256
MLA_NOPE = 64
MLA_ROPE = 32
MLA_V = 64
MLA_SCALE = (MLA_NOPE + MLA_ROPE) ** -0.5
LOG2E = 1.4426950408889634
ROPE_BASE = 10000.0
CONV_WIDTH = 31
CONV_CH = 512
CONV_HIST = 30

LANES = 128
SUBLANES = 8
CONV_ROWS = 32
ROPE_LANE0 = 64
POOL_PAD = 16
CONV_PAD = 32
VMEM_LIMIT = 52 * 2 ** 20
TOKEN_TILE = 1024
ATTN_TILE = 1024
FFN_TOKEN_TILE = 1024
FF_SUB = 512
GLA_SAMPLE_BATCHES = 4


def _cp(*sem):
    return pltpu.CompilerParams(dimension_semantics=sem, vmem_limit_bytes=VMEM_LIMIT)


def _dot(a, b):
    return jnp.dot(a, b, preferred_element_type=F32)


def _dot_nt(a, b):
    return lax.dot_general(a, b, (((1,), (1,)), ((), ())), preferred_element_type=F32)


def _dot_tn(a, b):
    return lax.dot_general(a, b, (((0,), (0,)), ((), ())), preferred_element_type=F32)


def _silu(x):
    return x * jax.nn.sigmoid(x)


def _rms(x, g):
    return x * lax.rsqrt(jnp.mean(x * x, axis=-1, keepdims=True) + EPS) * g


def _modulated(x_ref, g_ref, sh_ref, sc_ref):
    x = x_ref[...]
    h = _rms(x, g_ref[...]) * (1.0 + sc_ref[...]) + sh_ref[...]
    return h.reshape(x.shape[0] * x.shape[1], x.shape[2]).astype(BF16)


class _Group:
    def __init__(self, prompt, lead, rows, tile):
        self.prompt = prompt
        self.grid = (lead, rows // tile) if prompt else (1, 1)
        self.block = (1, tile) if prompt else (lead, rows)

    def act(self, width):
        return pl.BlockSpec(self.block + (width,), lambda b, t, *_: (b, t, 0))

    def mod(self, k):
        if self.prompt:
            return pl.BlockSpec((None, None, 1, self.dm), lambda b, t, *_: (b, k, 0, 0))
        return pl.BlockSpec((None, self.block[1], self.dm), lambda b, t, *_: (k, 0, 0))

    def pos(self):
        if self.prompt:
            return pl.BlockSpec((1, self.block[1], LANES), lambda b, t, *_: (0, t, 0))
        return pl.BlockSpec((self.block[0], 1, LANES), lambda b, t, *_: (0, 0, 0))

    dm = 1024


def _full(a):
    nd = a.ndim
    return pl.BlockSpec(a.shape, lambda *_: (0,) * nd)


def _sds(shape, dtype=F32):
    return jax.ShapeDtypeStruct(shape, dtype)


def _ada_kernel(c_ref, w_ref, b_ref, o_ref):
    c = c_ref[...]
    o_ref[...] = _dot(_silu(c).astype(BF16), w_ref[...].astype(BF16)) + b_ref[...]


def _ada(c_all, ada_w, ada_b):
    depth, dm, n = ada_w.shape
    m = c_all.shape[0]
    tn = dm
    return pl.pallas_call(
        _ada_kernel,
        grid=(depth, n // tn),
        in_specs=[pl.BlockSpec((m, dm), lambda l, j: (0, 0)),
                  pl.BlockSpec((None, dm, tn), lambda l, j: (l, 0, j)),
                  pl.BlockSpec((None, 1, tn), lambda l, j: (l, 0, j))],
        out_specs=pl.BlockSpec((None, m, tn), lambda l, j: (l, 0, j)),
        out_shape=_sds((depth, m, n)),
        compiler_params=_cp("parallel", "parallel"),
    )(c_all, ada_w, ada_b.reshape(depth, 1, n))


def _ffn_kernel(*refs, final):
    if final:
        x_ref, sh_ref, sc_ref, gt_ref, g_ref, w1_ref, w3_ref, w2_ref, fn_ref, o_ref = refs
    else:
        x_ref, sh_ref, sc_ref, gt_ref, g_ref, w1_ref, w3_ref, w2_ref, o_ref = refs
    h = _modulated(x_ref, g_ref, sh_ref, sc_ref)
    dff = w1_ref.shape[1]
    y = None
    for c0 in range(0, dff, FF_SUB):
        cols = slice(c0, min(c0 + FF_SUB, dff))
        a = _dot(h, w1_ref[:, cols])
        b = _dot(h, w3_ref[:, cols])
        part = _dot((_silu(a) * b).astype(BF16), w2_ref[cols, :])
        y = part if y is None else y + part
    x = x_ref[...]
    xn = x + 0.5 * gt_ref[...] * y.reshape(x.shape)
    o_ref[...] = _rms(xn, fn_ref[...]) if final else xn


def _resident(a, lead):
    idx = tuple(lead) + (0, 0)
    return pl.BlockSpec((None,) * len(lead) + a.shape[-2:], lambda *_: idx, pipeline_mode=pl.Buffered(1))


def _ffn(grp, x, mod, k0, g, w1, w3, w2, which, final_g=None):
    dm = x.shape[-1]
    final = final_g is not None
    ins = [x, mod, mod, mod, g, w1, w3, w2] + ([final_g] if final else [])
    specs = [grp.act(dm), grp.mod(k0), grp.mod(k0 + 1), grp.mod(k0 + 2), _full(g),
             _resident(w1, which), _resident(w3, which), _resident(w2, which)] + ([_full(final_g)] if final else [])
    return pl.pallas_call(
        functools.partial(_ffn_kernel, final=final),
        grid=grp.grid,
        in_specs=specs,
        out_specs=grp.act(dm),
        out_shape=_sds(x.shape),
        compiler_params=_cp("parallel", "parallel"),
    )(*ins)


def _log_sigmoid(x):
    return jnp.minimum(x, 0.0) - jnp.log(1.0 + jnp.exp(-jnp.abs(x)))


def _even_in_kernel(x_ref, sh_ref, sc_ref, g_ref, wq_ref, wg_ref, w2_ref, gb_ref, wu_ref,
                    q_ref, k_ref, v_ref, r_ref, la_ref, u_ref):
    h = _modulated(x_ref, g_ref, sh_ref, sc_ref)
    lead = x_ref.shape[:2]
    hk = GLA_HEADS * GLA_DK
    hv = GLA_HEADS * GLA_DV
    z = _dot(h, wq_ref[...])
    q_ref[...] = (z[:, :hk] * GLA_DK ** -0.5).reshape(lead + (hk,))
    k_ref[...] = z[:, hk:2 * hk].reshape(lead + (hk,))
    v_ref[...] = z[:, 2 * hk:2 * hk + hv].reshape(lead + (hv,))
    r_ref[...] = z[:, 2 * hk + hv:].reshape(lead + (hv,))
    g_low = _dot(h, wg_ref[...]).astype(BF16)
    gate = _dot(g_low, w2_ref[...]) + gb_ref[...]
    la_ref[...] = (_log_sigmoid(gate) / GLA_TAU).reshape(lead + (hk,))
    u_ref[...] = _dot(h, wu_ref[...]).reshape(lead + (POOL_WIDTH,))


def _even_in(grp, x, mod, g, wq, wg, w2, gb, wu):
    hk = GLA_HEADS * GLA_DK
    hv = GLA_HEADS * GLA_DV
    widths = (hk, hk, hv, hv, hk, POOL_WIDTH)
    return pl.pallas_call(
        _even_in_kernel,
        grid=grp.grid,
        in_specs=[grp.act(x.shape[-1]), grp.mod(3), grp.mod(4), _full(g), _full(wq), _full(wg),
                  _full(w2), _full(gb), _full(wu)],
        out_specs=[grp.act(w) for w in widths],
        out_shape=[_sds(x.shape[:2] + (w,)) for w in widths],
        compiler_params=_cp("parallel", "parallel"),
    )(x, mod, mod, g, wq, wg, w2, gb, wu)


def _cumsum_rows(tril, x):
    hi = x.astype(BF16)
    r1 = x - hi.astype(F32)
    mid = r1.astype(BF16)
    lo = (r1 - mid.astype(F32)).astype(BF16)
    return _dot(tril, hi) + _dot(tril, mid) + _dot(tril, lo)


def _gla_kernel(*refs, nbb, tt, has_s0):
    if has_s0:
        q_ref, k_ref, v_ref, la_ref, s0_ref, o_ref, so_ref, s_scr = refs
    else:
        q_ref, k_ref, v_ref, la_ref, o_ref, so_ref, s_scr = refs
    t = pl.program_id(1)
    c_rows = GLA_CHUNK

    @pl.when(t == 0)
    def _():
        s_scr[...] = s0_ref[...] if has_s0 else jnp.zeros_like(s_scr)

    row = lax.broadcasted_iota(jnp.int32, (c_rows, c_rows), 0)
    col = lax.broadcasted_iota(jnp.int32, (c_rows, c_rows), 1)
    causal = col <= row
    tril = jnp.where(causal, 1.0, 0.0).astype(BF16)
    lane = lax.broadcasted_iota(jnp.int32, (c_rows, LANES), 1)
    srow = lax.broadcasted_iota(jnp.int32, (LANES, LANES), 0)

    def chunk(c, carry):
        rows = pl.ds(pl.multiple_of(c * c_rows, c_rows), c_rows)
        for n in range(nbb):
            for p in range(GLA_HEADS // 2):
                sl = slice(LANES * p, LANES * (p + 1))
                b = _cumsum_rows(tril, la_ref[n, rows, sl])
                b_last = b[c_rows - 1:c_rows, :]
                k = k_ref[n, rows, sl]
                qi = q_ref[n, rows, sl] * jnp.exp(b)
                ki = (k * jnp.exp(-b)).astype(BF16)
                kd = (k * jnp.exp(b_last - b)).astype(BF16)
                s_old = s_scr[n, p]
                s_bf = s_old.astype(BF16)
                decay = jnp.transpose(jnp.broadcast_to(jnp.exp(b_last), (LANES, LANES)))
                upd = []
                for hh in range(2):
                    mine = (lane < GLA_DK) if hh == 0 else (lane >= GLA_DK)
                    qh = jnp.where(mine, qi, 0.0).astype(BF16)
                    att = jnp.where(causal, _dot_nt(qh, ki), 0.0).astype(BF16)
                    vs = slice(GLA_DV * (2 * p + hh), GLA_DV * (2 * p + hh + 1))
                    vh = v_ref[n, rows, vs].astype(BF16)
                    o_ref[n, rows, vs] = _dot(qh, s_bf) + _dot(att, vh)
                    upd.append(_dot_tn(kd, vh))
                s_scr[n, p] = decay * s_old + jnp.where(srow < GLA_DK, upd[0], upd[1])
        return carry

    lax.fori_loop(0, tt // c_rows, chunk, 0, unroll=min(2, tt // c_rows))

    @pl.when(t == pl.num_programs(1) - 1)
    def _():
        so_ref[...] = s_scr[...]


def _gla(q, k, v, la, s0, nbb, tt):
    nb, nt, hk = q.shape
    hv = v.shape[-1]
    has_s0 = s0 is not None
    act = lambda w: pl.BlockSpec((nbb, tt, w), lambda b, t: (b, t, 0))
    st = pl.BlockSpec((nbb, GLA_HEADS // 2, LANES, GLA_DV), lambda b, t: (b, 0, 0, 0))
    return pl.pallas_call(
        functools.partial(_gla_kernel, nbb=nbb, tt=tt, has_s0=has_s0),
        grid=(nb // nbb, nt // tt),
        in_specs=[act(hk), act(hk), act(hv), act(hk)] + ([st] if has_s0 else []),
        out_specs=[act(hv), st],
        out_shape=[_sds((nb, nt, hv)), _sds((nb, GLA_HEADS // 2, LANES, GLA_DV))],
        scratch_shapes=[pltpu.VMEM((nbb, GLA_HEADS // 2, LANES, GLA_DV), F32)],
        compiler_params=_cp("parallel", "arbitrary"),
    )(*([q, k, v, la] + ([s0] if has_s0 else [])))


def _gla_gate(o, r, on):
    outs = []
    for h in range(GLA_HEADS):
        sl = slice(GLA_DV * h, GLA_DV * (h + 1))
        outs.append(_rms(o[:, sl], on[:, sl]))
    return jnp.concatenate(outs, axis=-1) * _silu(r)


def _even_mix_out(x_ref, gt_ref, og, pooled, pw_ref, ps_ref, wo_ref, xo_ref):
    mixed = [_dot(pooled[g].astype(BF16), pw_ref[g]) for g in range(len(POOL_WINDOWS))]
    mixed = jnp.concatenate(mixed, axis=-1) * ps_ref[...]
    n_o = og.shape[-1]
    y = _dot(og.astype(BF16), wo_ref[:n_o, :]) + _dot(mixed.astype(BF16), wo_ref[n_o:, :])
    x = x_ref[...]
    xo_ref[...] = x + gt_ref[...] * y.reshape(x.shape)


def _even_out_prompt_kernel(x_ref, o_ref, r_ref, u_ref, gt_ref, on_ref, pw_ref, ps_ref, wo_ref,
                            xo_ref, ho_ref, hb, *, tm, p0):
    t = pl.program_id(1)

    @pl.when(t == 0)
    def _():
        hb[0:POOL_PAD, :] = jnp.zeros((POOL_PAD, POOL_WIDTH), F32)

    u = u_ref[0]
    hb[POOL_PAD:POOL_PAD + tm, :] = u
    pos = t * tm + lax.broadcasted_iota(jnp.int32, (tm, 1), 0)
    pooled = []
    for g, w in enumerate(POOL_WINDOWS):
        sl = slice(POOL_GROUP * g, POOL_GROUP * (g + 1))
        win = hb[POOL_PAD - (w - 1):POOL_PAD - (w - 1) + tm, sl]
        for j in range(w - 2, -1, -1):
            win = win + hb[POOL_PAD - j:POOL_PAD - j + tm, sl]
        cnt = jnp.minimum(p0 + pos + 1, w).astype(F32)
        pooled.append(win / cnt - u[:, sl])
    og = _gla_gate(o_ref[0], r_ref[0], on_ref[...])
    _even_mix_out(x_ref, gt_ref, og, pooled, pw_ref, ps_ref, wo_ref, xo_ref)
    hb[0:POOL_PAD, :] = hb[tm:tm + POOL_PAD, :]

    @pl.when(t == pl.num_programs(1) - 1)
    def _():
        ho_ref[0] = hb[0:POOL_PAD, :]


def _even_out_sample_kernel(x_ref, o_ref, r_ref, u_ref, hi_ref, gt_ref, on_ref, pw_ref, ps_ref, wo_ref,
                            xo_ref, hn_ref, *, p0):
    steps, nb = u_ref.shape[:2]

    def slab(i, sl):
        return hi_ref[i, :, sl] if i < POOL_HIST else u_ref[i - POOL_HIST, :, sl]

    pooled = []
    for g, w in enumerate(POOL_WINDOWS):
        sl = slice(POOL_GROUP * g, POOL_GROUP * (g + 1))
        rows = []
        for t in range(steps):
            win = slab(POOL_HIST + t - (w - 1), sl)
            for j in range(w - 2, -1, -1):
                win = win + slab(POOL_HIST + t - j, sl)
            rows.append(win / float(min(p0 + t + 1, w)) - u_ref[t, :, sl])
        pooled.append(jnp.concatenate(rows, axis=0))
    flat = lambda ref: ref[...].reshape(steps * nb, ref.shape[-1])
    og = _gla_gate(flat(o_ref), flat(r_ref), on_ref[...])
    _even_mix_out(x_ref, gt_ref, og, pooled, pw_ref, ps_ref, wo_ref, xo_ref)
    for i in range(POOL_HIST):
        hn_ref[i] = slab(steps + i, slice(None))


def _even_out(grp, x, o, r, u, hist, mod, on, pw, ps, wo, p0):
    dm = x.shape[-1]
    common = [mod, on, pw, ps, wo]
    common_specs = [grp.mod(5), _full(on), _full(pw), _full(ps), _full(wo)]
    acts = [grp.act(dm), grp.act(o.shape[-1]), grp.act(r.shape[-1]), grp.act(u.shape[-1])]
    if grp.prompt:
        tm = grp.block[1]
        return pl.pallas_call(
            functools.partial(_even_out_prompt_kernel, tm=tm, p0=p0),
            grid=grp.grid,
            in_specs=acts + common_specs,
            out_specs=[grp.act(dm), pl.BlockSpec((1, POOL_PAD, POOL_WIDTH), lambda b, t: (b, 0, 0))],
            out_shape=[_sds(x.shape), _sds((x.shape[0], POOL_PAD, POOL_WIDTH))],
            scratch_shapes=[pltpu.VMEM((POOL_PAD + tm, POOL_WIDTH), F32)],
            compiler_params=_cp("parallel", "arbitrary"),
        )(x, o, r, u, *common)
    return pl.pallas_call(
        functools.partial(_even_out_sample_kernel, p0=p0),
        grid=grp.grid,
        in_specs=acts + [_full(hist)] + common_specs,
        out_specs=[grp.act(dm), _full(hist)],
        out_shape=[_sds(x.shape), _sds(hist.shape)],
        compiler_params=_cp("parallel", "arbitrary"),
    )(x, o, r, u, hist, *common)


def _rope_slab(x, cs, s1, s2, lead):
    shp = lead + (LANES,)
    back = pltpu.roll(x, LANES - MLA_ROPE // 2, 1).reshape(shp)
    fwd = pltpu.roll(x, MLA_ROPE // 2, 1).reshape(shp)
    out = x.reshape(shp) * cs + back * s1 + fwd * s2
    return out.reshape(x.shape)


def _odd_in_kernel(*refs, sample):
    (x_ref, sh_ref, sc_ref, g_ref, wcq_ref, wckv_ref, wkr_ref, wga_ref, wgg_ref, qn_ref, wuq_ref,
     kvn_ref, cs_ref, s1_ref, s2_ref) = refs[:15]
    if sample:
        wkl_ref, ckv_ref, kr_ref, uc_ref, q_ref, ql_ref = refs[15:]
    else:
        wuk_ref, wuv_ref, ckv_ref, kr_ref, uc_ref, q_ref, k_ref, v_ref = refs[15:]
    lead = x_ref.shape[:2]
    h = _modulated(x_ref, g_ref, sh_ref, sc_ref)
    cs, s1, s2 = cs_ref[...], s1_ref[...], s2_ref[...]

    cq = _rms(_dot(h, wcq_ref[...]), qn_ref[...]).astype(BF16)
    q = _dot(cq, wuq_ref[...]) * (MLA_SCALE * LOG2E)
    q = jnp.concatenate(
        [_rope_slab(q[:, LANES * i:LANES * (i + 1)], cs, s1, s2, lead) for i in range(MLA_HEADS)], axis=-1)
    q_ref[...] = q.astype(BF16).reshape(lead + (MLA_HEADS * LANES,))

    ckv = _rms(_dot(h, wckv_ref[...]), kvn_ref[...])
    ckv_ref[...] = ckv.reshape(lead + (MLA_KV_RANK,))
    kr = _rope_slab(_dot(h, wkr_ref[...]), cs, s1, s2, lead)
    kr_ref[...] = kr.reshape(lead + (LANES,))
    uc_ref[...] = (_dot(h, wga_ref[...]) * jax.nn.sigmoid(_dot(h, wgg_ref[...]))).reshape(lead + (CONV_CH,))

    if sample:
        qb = q.astype(BF16)
        for i in range(MLA_HEADS):
            ql = _dot(qb[:, LANES * i:LANES * (i + 1)], wkl_ref[i])
            ql_ref[:, :, MLA_KV_RANK * i:MLA_KV_RANK * (i + 1)] = ql.astype(BF16).reshape(lead + (MLA_KV_RANK,))
    else:
        cb = ckv.astype(BF16)
        kn = _dot(cb, wuk_ref[...])
        kn = jnp.concatenate([kn[:, LANES * i:LANES * (i + 1)] + kr for i in range(MLA_HEADS)], axis=-1)
        k_ref[...] = kn.astype(BF16).reshape(lead + (MLA_HEADS * LANES,))
        vlane = lax.broadcasted_iota(jnp.int32, (1, MLA_HEADS * LANES), 1) & (LANES - 1)
        v = _dot(cb, wuv_ref[...]) + jnp.where(vlane == MLA_V, 1.0, 0.0)
        v_ref[...] = v.astype(BF16).reshape(lead + (MLA_HEADS * LANES,))


def _odd_in(grp, x, mod, g, w, tabs, sample):
    lead = x.shape[:2]
    hl = MLA_HEADS * LANES
    ins = [x, mod, mod, g, w['wcq'], w['wckv'], w['wkr'], w['wga'], w['wgg'], w['qn'], w['wuq'], w['kvn'], *tabs]
    specs = [grp.act(x.shape[-1]), grp.mod(3), grp.mod(4)] + [_full(a) for a in ins[3:12]] + [grp.pos()] * 3
    outs = [(MLA_KV_RANK, F32), (LANES, F32), (CONV_CH, F32), (hl, BF16)]
    if sample:
        ins.append(w['wkl'])
        outs.append((MLA_HEADS * MLA_KV_RANK, BF16))
    else:
        ins += [w['wuk'], w['wuv']]
        outs += [(hl, BF16), (hl, BF16)]
    specs += [_full(a) for a in ins[15:]]
    return pl.pallas_call(
        functools.partial(_odd_in_kernel, sample=sample),
        grid=grp.grid,
        in_specs=specs,
        out_specs=[grp.act(wd) for wd, _ in outs],
        out_shape=[_sds(lead + (wd,), dt) for wd, dt in outs],
        compiler_params=_cp("parallel", "parallel"),
    )(*ins)


def _softmax_step(s, m_prev):
    m_new = jnp.maximum(m_prev, jnp.max(s, axis=-1, keepdims=True))
    return jnp.exp2(s - m_new[:, :1]), jnp.exp2(m_prev - m_new), m_new


def _attn_prompt_kernel(qi_ref, ki_ref, q_ref, k_ref, v_ref, o_ref, m_scr, acc_scr, *, tq):
    step = pl.program_id(1)
    qi = qi_ref[step]
    ki = ki_ref[step]

    @pl.when(ki == 0)
    def _():
        m_scr[...] = jnp.full_like(m_scr, NEG)
        acc_scr[...] = jnp.zeros_like(acc_scr)

    def update(diagonal):
        if diagonal:
            row = lax.broadcasted_iota(jnp.int32, (tq, tq), 0)
            col = lax.broadcasted_iota(jnp.int32, (tq, tq), 1)
        for h in range(MLA_HEADS):
            sl = slice(LANES * h, LANES * (h + 1))
            s = _dot_nt(q_ref[0, :, sl], k_ref[0, :, sl])
            if diagonal:
                s = jnp.where(col <= row, s, NEG)
            p, alpha, m_new = _softmax_step(s, m_scr[:, sl])
            acc_scr[:, sl] = alpha * acc_scr[:, sl] + _dot(p.astype(BF16), v_ref[0, :, sl])
            m_scr[:, sl] = m_new

    @pl.when(ki < qi)
    def _():
        update(False)

    @pl.when(ki == qi)
    def _():
        update(True)
        for h in range(MLA_HEADS):
            sl = slice(LANES * h, LANES * (h + 1))
            acc = acc_scr[:, sl]
            o_ref[0, :, sl] = (acc / acc[:, MLA_V:MLA_V + 1]).astype(BF16)


def _attn_prompt(q, k, v, tq):
    nb, nt, hl = q.shape
    pairs = [(i, j) for i in range(nt // tq) for j in range(i + 1)]
    qi_tab = jnp.asarray([p[0] for p in pairs], jnp.int32)
    ki_tab = jnp.asarray([p[1] for p in pairs], jnp.int32)
    qs = pl.BlockSpec((1, tq, hl), lambda b, s, qt, kt: (b, qt[s], 0))
    kv = pl.BlockSpec((1, tq, hl), lambda b, s, qt, kt: (b, kt[s], 0))
    return pl.pallas_call(
        functools.partial(_attn_prompt_kernel, tq=tq),
        grid_spec=pltpu.PrefetchScalarGridSpec(
            num_scalar_prefetch=2,
            grid=(nb, len(pairs)),
            in_specs=[qs, kv, kv],
            out_specs=qs,
            scratch_shapes=[pltpu.VMEM((tq, hl), F32)] * 2),
        out_shape=_sds(q.shape, BF16),
        compiler_params=_cp("parallel", "arbitrary"),
    )(qi_tab, ki_tab, q, k, v)


def _attn_sample_kernel(pt_ref, ql_ref, qr_ref, cn_ref, kn_ref, ckv_hbm, kr_hbm, o_ref,
                        ckv_buf, kr_buf, pg_scr, s_scr, p_scr, sem, *, npg, layer):
    b = pl.program_id(0)
    nb = pl.num_programs(0)
    slot = lax.rem(b, 2)
    ql = ql_ref[0]
    qr = qr_ref[0]
    rows = ql.shape[0]

    def page_copies(page, i, sl):
        return (pltpu.make_async_copy(ckv_hbm.at[layer, page], ckv_buf.at[sl, i], sem.at[0, sl]),
                pltpu.make_async_copy(kr_hbm.at[layer, page], kr_buf.at[sl, i], sem.at[1, sl]))

    def start_page(batch, i, sl):
        for cp in page_copies(pt_ref[batch, i], i, sl):
            cp.start()

    def start_first(i, carry):
        start_page(0, i, 0)
        return carry

    @pl.when(b == 0)
    def _():
        lax.fori_loop(0, npg, start_first, 0)

    for i in range(npg):
        for cp in page_copies(0, i, slot):
            cp.wait()

    def score_pages(prefetch):
        for i in range(npg):
            if prefetch:
                for k in range(2 * i, min(2 * i + 2, npg)):
                    start_page(b + 1, k, 1 - slot)
            page = ckv_buf[slot, i].astype(BF16)
            pg_scr[i] = page
            s_scr[i] = _dot_nt(ql, page) + _dot(qr, kr_buf[slot, i].astype(BF16))

    @pl.when(b + 1 < nb)
    def _():
        score_pages(True)

    @pl.when(b + 1 == nb)
    def _():
        score_pages(False)

    n_new = cn_ref.shape[1]
    pad = jnp.zeros((LANES - n_new, MLA_KV_RANK), F32)
    cn = jnp.concatenate([cn_ref[0], pad], axis=0).astype(BF16)
    kn = jnp.concatenate([kn_ref[0], pad[:, :MLA_ROPE]], axis=0).astype(BF16)
    step = lax.shift_right_logical(lax.broadcasted_iota(jnp.int32, (rows, LANES), 0), MLA_HEADS.bit_length() - 1)
    col = lax.broadcasted_iota(jnp.int32, (rows, LANES), 1)
    s_new = jnp.where(col <= step, _dot_nt(ql, cn) + _dot_nt(qr, kn), NEG)

    s = s_scr[...]
    m = jnp.maximum(jnp.max(jnp.max(s, axis=0), axis=-1, keepdims=True),
                    jnp.max(s_new, axis=-1, keepdims=True))
    p = jnp.exp2(s - m)
    p_new = jnp.exp2(s_new - m)
    denom = jnp.sum(jnp.sum(p, axis=0), axis=-1, keepdims=True) + jnp.sum(p_new, axis=-1, keepdims=True)
    p_scr[...] = p.astype(BF16)

    def weigh_page(i, acc):
        return acc + _dot(p_scr[i], pg_scr[i])

    acc = lax.fori_loop(0, npg, weigh_page, _dot(p_new.astype(BF16), cn), unroll=True)
    o_ref[0] = acc / denom


def _attn_sample(page_table, ql, qr, cn, kn, cache_ckv, cache_kr, layer):
    nb, rows, rank = ql.shape
    npg = page_table.shape[1]
    page, rope = cache_ckv.shape[2], cache_kr.shape[2]
    assert page == LANES, "one cache page must fill one lane tile of scores"
    per_b = lambda a: pl.BlockSpec((1,) + a.shape[1:], lambda b, pt: (b, 0, 0))
    hbm = pl.BlockSpec(memory_space=pl.ANY)
    return pl.pallas_call(
        functools.partial(_attn_sample_kernel, npg=npg, layer=layer),
        grid_spec=pltpu.PrefetchScalarGridSpec(
            num_scalar_prefetch=1,
            grid=(nb,),
            in_specs=[per_b(ql), per_b(qr), per_b(cn), per_b(kn), hbm, hbm],
            out_specs=pl.BlockSpec((1, rows, rank), lambda b, pt: (b, 0, 0)),
            scratch_shapes=[pltpu.VMEM((2, npg, page, rank), F32),
                            pltpu.VMEM((2, npg, rope, page), F32),
                            pltpu.VMEM((npg, page, rank), BF16),
                            pltpu.VMEM((npg, rows, page), F32),
                            pltpu.VMEM((npg, rows, page), BF16),
                            pltpu.SemaphoreType.DMA((2, 2))]),
        out_shape=_sds((nb, rows, rank)),
        compiler_params=_cp("arbitrary"),
    )(page_table, ql, qr, cn, kn, cache_ckv, cache_kr)


def _conv_norm_act(cv, cb_ref, lg_ref, lb_ref):
    cv = cv + cb_ref[...]
    mu = jnp.mean(cv, axis=-1, keepdims=True)
    d = cv - mu
    y = d * lax.rsqrt(jnp.mean(d * d, axis=-1, keepdims=True) + EPS)
    return _silu(y * lg_ref[...] + lb_ref[...])


def _odd_out_prompt_kernel(x_ref, a_ref, uc_ref, gt_ref, cw_ref, cb_ref, lg_ref, lb_ref, woa_ref, woc_ref,
                           xo_ref, ho_ref, hb, hs, cvb, *, tm):
    t = pl.program_id(1)

    @pl.when(t == 0)
    def _():
        hb[0:CONV_PAD, :] = jnp.zeros((CONV_PAD, CONV_CH), F32)

    hb[CONV_PAD:CONV_PAD + tm, :] = uc_ref[0]
    for sft in range(1, SUBLANES):
        hs[sft - 1] = hb[sft:sft + hs.shape[1], :]
    base = CONV_PAD - CONV_HIST

    for r0 in range(0, tm, CONV_ROWS):
        acc = None
        for j in range(CONV_WIDTH):
            whole, sft = divmod(base + j, SUBLANES)
            rows = slice(r0 + whole * SUBLANES, r0 + whole * SUBLANES + CONV_ROWS)
            term = cw_ref[j:j + 1, :] * (hb[rows, :] if sft == 0 else hs[sft - 1, rows, :])
            acc = term if acc is None else acc + term
        cvb[r0:r0 + CONV_ROWS, :] = acc
    cv = _conv_norm_act(cvb[...], cb_ref, lg_ref, lb_ref)
    y = _dot(a_ref[0], woa_ref[...]) + _dot(cv.astype(BF16), woc_ref[...])
    xo_ref[0] = x_ref[0] + gt_ref[...] * y
    hb[0:CONV_PAD, :] = hb[tm:tm + CONV_PAD, :]

    @pl.when(t == pl.num_programs(1) - 1)
    def _():
        ho_ref[0] = hb[0:CONV_PAD, :]


def _odd_out_sample_kernel(x_ref, lat_ref, uc_ref, hi_ref, gt_ref, cw_ref, cb_ref, lg_ref, lb_ref, wuv_ref,
                           woa_ref, woc_ref, xo_ref, hn_ref):
    steps, nb = uc_ref.shape[:2]

    def slab(i):
        return hi_ref[i] if i < CONV_HIST else uc_ref[i - CONV_HIST]

    rows = []
    for t in range(steps):
        cv = cw_ref[0:1, :] * slab(t)
        for j in range(1, CONV_WIDTH):
            cv = cv + cw_ref[j:j + 1, :] * slab(t + j)
        rows.append(cv)
    cv = _conv_norm_act(jnp.concatenate(rows, axis=0), cb_ref, lg_ref, lb_ref)
    lat = lat_ref[...].reshape(steps * nb, lat_ref.shape[-1]).astype(BF16)
    attn = _dot(lat, wuv_ref[...]).astype(BF16)
    y = _dot(attn, woa_ref[...]) + _dot(cv.astype(BF16), woc_ref[...])
    x = x_ref[...]
    xo_ref[...] = x + gt_ref[...] * y.reshape(x.shape)
    for i in range(CONV_HIST):
        hn_ref[i] = slab(steps + i)


def _odd_out(grp, x, a, uc, hist, mod, w):
    dm = x.shape[-1]
    conv = [w['cw'], w['cb'], w['lg'], w['lb']]
    if grp.prompt:
        tm = grp.block[1]
        ins = [x, a, uc, mod] + conv + [w['woa_pad'], w['woc']]
        return pl.pallas_call(
            functools.partial(_odd_out_prompt_kernel, tm=tm),
            grid=grp.grid,
            in_specs=[grp.act(dm), grp.act(a.shape[-1]), grp.act(CONV_CH), grp.mod(5)] + [_full(v) for v in ins[4:]],
            out_specs=[grp.act(dm), pl.BlockSpec((1, CONV_PAD, CONV_CH), lambda b, t: (b, 0, 0))],
            out_shape=[_sds(x.shape), _sds((x.shape[0], CONV_PAD, CONV_CH))],
            scratch_shapes=[pltpu.VMEM((CONV_PAD + tm, CONV_CH), F32),
                            pltpu.VMEM((SUBLANES - 1, CONV_PAD + tm - SUBLANES, CONV_CH), F32),
                            pltpu.VMEM((tm, CONV_CH), F32)],
            compiler_params=_cp("parallel", "arbitrary"),
        )(*ins)
    ins = [x, a, uc, hist, mod] + conv + [w['wuv_bd'], w['woa'], w['woc']]
    return pl.pallas_call(
        _odd_out_sample_kernel,
        grid=grp.grid,
        in_specs=[grp.act(dm), grp.act(a.shape[-1]), grp.act(CONV_CH), _full(hist), grp.mod(5)]
        + [_full(v) for v in ins[5:]],
        out_specs=[grp.act(dm), _full(hist)],
        out_shape=[_sds(x.shape), _sds(hist.shape)],
        compiler_params=_cp("parallel", "arbitrary"),
    )(*ins)


def _head_pad(w, heads, width, offset=0):
    kdim = w.shape[0]
    w = w.reshape(kdim, heads, width)
    w = jnp.pad(w, ((0, 0), (0, 0), (offset, LANES - width - offset)))
    return w.reshape(kdim, heads * LANES)


def _rope_tables(pos):
    half = MLA_ROPE // 2
    freqs = ROPE_BASE ** (-jnp.arange(half, dtype=F32) / half)
    ang = pos.astype(F32)[..., None] * freqs
    cos, sin = jnp.cos(ang), jnp.sin(ang)
    zeros = jnp.zeros_like(cos)
    lead = jnp.ones(pos.shape + (ROPE_LANE0,), F32)
    tail = jnp.zeros(pos.shape + (LANES - ROPE_LANE0 - MLA_ROPE,), F32)
    cs = jnp.concatenate([lead, cos, cos, tail], axis=-1)
    s1 = jnp.concatenate([0 * lead, -sin, zeros, tail], axis=-1)
    s2 = jnp.concatenate([0 * lead, zeros, sin, tail], axis=-1)
    return cs, s1, s2


def _even_weights(w_in, gate_w2, gate_b, out_norm, pool_w, pool_scale, w_out):
    hk = GLA_HEADS * GLA_DK
    hv = GLA_HEADS * GLA_DV
    n_main = 2 * hk + 2 * hv
    wg = jnp.pad(w_in[:, n_main:n_main + GLA_GATE_RANK], ((0, 0), (0, LANES - GLA_GATE_RANK)))
    w2 = jnp.pad(gate_w2, ((0, LANES - GLA_GATE_RANK), (0, 0)))
    return dict(wq=w_in[:, :n_main].astype(BF16), wg=wg.astype(BF16), w2=w2.astype(BF16),
                gb=gate_b[None], wu=w_in[:, n_main + GLA_GATE_RANK:].astype(BF16),
                on=out_norm[None], pw=pool_w.astype(BF16), ps=pool_scale[None], wo=w_out.astype(BF16))


def _odd_weights(w_in, q_norm, w_uq, kv_norm, w_uk, w_uv, conv_w, conv_b, ln_g, ln_b, w_out):
    c0, c1, c2 = MLA_Q_RANK, MLA_Q_RANK + MLA_KV_RANK, MLA_Q_RANK + MLA_KV_RANK + MLA_ROPE
    n_attn = MLA_HEADS * MLA_V
    wkr = jnp.pad(w_in[:, c1:c2], ((0, 0), (ROPE_LANE0, LANES - ROPE_LANE0 - MLA_ROPE)))
    wuk = w_uk.reshape(MLA_KV_RANK, MLA_HEADS * MLA_NOPE)
    wuv = w_uv.reshape(MLA_KV_RANK, MLA_HEADS * MLA_V)
    wkl = jnp.pad(jnp.transpose(w_uk, (1, 2, 0)), ((0, 0), (0, LANES - MLA_NOPE), (0, 0)))
    eye = jnp.eye(MLA_HEADS, dtype=F32)
    wuv_bd = (eye[:, None, :, None] * jnp.transpose(w_uv, (1, 0, 2))[:, :, None, :]).reshape(
        MLA_HEADS * MLA_KV_RANK, n_attn)
    woa = w_out[:n_attn]
    woa_pad = jnp.pad(woa.reshape(MLA_HEADS, MLA_V, -1), ((0, 0), (0, LANES - MLA_V), (0, 0))).reshape(
        MLA_HEADS * LANES, -1)
    return dict(wcq=w_in[:, :c0].astype(BF16), wckv=w_in[:, c0:c1].astype(BF16), wkr=wkr.astype(BF16),
                wga=w_in[:, c2:c2 + CONV_CH].astype(BF16), wgg=w_in[:, c2 + CONV_CH:].astype(BF16),
                qn=q_norm[None], wuq=_head_pad(w_uq, MLA_HEADS, MLA_NOPE + MLA_ROPE).astype(BF16),
                kvn=kv_norm[None], wuk=_head_pad(wuk, MLA_HEADS, MLA_NOPE).astype(BF16),
                wuv=_head_pad(wuv, MLA_HEADS, MLA_V).astype(BF16), wkl=wkl.astype(BF16),
                wuv_bd=wuv_bd.astype(BF16), cw=conv_w, cb=conv_b[None], lg=ln_g[None], lb=ln_b[None],
                woa=woa.astype(BF16), woa_pad=woa_pad.astype(BF16), woc=w_out[n_attn:].astype(BF16))


def _tm(x):
    return jnp.swapaxes(x, 0, 1)


def kernel(x_prompt, x_sample, state_gla, state_pool, cache_ckv, cache_krope, state_conv, page_table, c_prompt, c_sample, ada_w, ada_b, norm_g, ffn_w1, ffn_w3, ffn_w2, ev_w_in, ev_gate_w2, ev_gate_b, ev_out_norm, ev_pool_w, ev_pool_scale, ev_w_out, od_w_in, od_q_norm, od_w_uq, od_kv_norm, od_w_uk, od_w_uv, od_conv_w, od_conv_b, od_conv_norm_g, od_conv_norm_b, od_w_out, final_norm):
    nbp, seq, dm = x_prompt.shape
    nbs, steps, _ = x_sample.shape
    depth = ada_w.shape[0]
    past_len = page_table.shape[1] * cache_ckv.shape[2]
    tile = min(TOKEN_TILE, seq)
    gp = _Group(True, nbp, seq, tile)
    gf = _Group(True, nbp, seq, min(FFN_TOKEN_TILE, seq))
    gs = _Group(False, steps, nbs, nbs)

    n_c = nbp + nbs
    c_all = jnp.pad(jnp.concatenate([c_prompt, c_sample], axis=0), ((0, -n_c % 8), (0, 0)))
    mod = _ada(c_all, ada_w, ada_b)
    mod_p = mod[:, :nbp].reshape(depth, nbp, N_MOD, 1, dm)
    mod_s = jnp.swapaxes(mod[:, nbp:n_c].reshape(depth, nbs, N_MOD, dm), 1, 2)

    w1, w3, w2 = ffn_w1.astype(BF16), ffn_w3.astype(BF16), ffn_w2.astype(BF16)
    tabs_p = _rope_tables(jnp.arange(seq)[None])
    tabs_s = _rope_tables(past_len + jnp.arange(steps)[:, None])

    xp = x_prompt
    xs = _tm(x_sample)
    gla_p, gla_s, pool_p, pool_s, ckv_p, ckv_s, kr_p, kr_s, conv_p, conv_s = ([] for _ in range(10))
    gla_tt = min(TOKEN_TILE, seq)

    for layer in range(depth):
        i = layer // 2
        ng = norm_g[layer]
        last = layer == depth - 1
        xp = _ffn(gf, xp, mod_p[layer], 0, ng[0:1], w1, w3, w2, (layer, 0))
        xs = _ffn(gs, xs, mod_s[layer], 0, ng[0:1], w1, w3, w2, (layer, 0))
        if layer % 2 == 0:
            w = _even_weights(ev_w_in[i], ev_gate_w2[i], ev_gate_b[i], ev_out_norm[i], ev_pool_w[i],
                              ev_pool_scale[i], ev_w_out[i])
            proj = (ng[1:2], w['wq'], w['wg'], w['w2'], w['gb'], w['wu'])
            out_w = (w['on'], w['pw'], w['ps'], w['wo'])
            q, k, v, r, la, u = _even_in(gp, xp, mod_p[layer], *proj)
            o, s_fin = _gla(q, k, v, la, None, nbp, gla_tt)
            xp, hist = _even_out(gp, xp, o, r, u, None, mod_p[layer], *out_w, 0)
            gla_p.append(s_fin.reshape(nbp, GLA_HEADS, GLA_DK, GLA_DV))
            pool_p.append(hist[:, POOL_PAD - POOL_HIST:])
            q, k, v, r, la, u = _even_in(gs, xs, mod_s[layer], *proj)
            chunked = lambda a: jnp.pad(_tm(a), ((0, 0), (0, GLA_CHUNK - steps), (0, 0)))
            s0 = state_gla[i].reshape(nbs, GLA_HEADS // 2, LANES, GLA_DV)
            o, s_fin = _gla(chunked(q), chunked(k), chunked(v), chunked(la), s0,
                            GLA_SAMPLE_BATCHES if nbs % GLA_SAMPLE_BATCHES == 0 else 1, GLA_CHUNK)
            xs, hist = _even_out(gs, xs, _tm(o[:, :steps]), r, u, _tm(state_pool[i]), mod_s[layer], *out_w, past_len)
            gla_s.append(s_fin.reshape(nbs, GLA_HEADS, GLA_DK, GLA_DV))
            pool_s.append(_tm(hist))
        else:
            w = _odd_weights(od_w_in[i], od_q_norm[i], od_w_uq[i], od_kv_norm[i], od_w_uk[i], od_w_uv[i],
                             od_conv_w[i], od_conv_b[i], od_conv_norm_g[i], od_conv_norm_b[i], od_w_out[i])
            rope_lanes = slice(ROPE_LANE0, ROPE_LANE0 + MLA_ROPE)
            ckv, kr, uc, q, k, v = _odd_in(gp, xp, mod_p[layer], ng[1:2], w, tabs_p, False)
            attn = _attn_prompt(q, k, v, min(ATTN_TILE, seq))
            xp, hist = _odd_out(gp, xp, attn, uc, None, mod_p[layer], w)
            ckv_p.append(ckv)
            kr_p.append(kr[..., rope_lanes])
            conv_p.append(hist[:, CONV_PAD - CONV_HIST:])
            ckv, kr, uc, q, ql = _odd_in(gs, xs, mod_s[layer], ng[1:2], w, tabs_s, True)
            kr = kr[..., rope_lanes]
            qr = q.reshape(steps, nbs, MLA_HEADS, LANES)[..., rope_lanes]
            qr = _tm(qr).reshape(nbs, steps * MLA_HEADS, MLA_ROPE)
            ql = _tm(ql).reshape(nbs, steps * MLA_HEADS, MLA_KV_RANK)
            pad8 = lambda a: jnp.pad(_tm(a), ((0, 0), (0, -steps % 8), (0, 0)))
            cache_kr_t = jnp.swapaxes(cache_krope, 2, 3)
            lat = _attn_sample(page_table, ql, qr, pad8(ckv), pad8(kr), cache_ckv, cache_kr_t, i)
            lat = _tm(lat.reshape(nbs, steps, MLA_HEADS * MLA_KV_RANK))
            xs, hist = _odd_out(gs, xs, lat, uc, _tm(state_conv[i]), mod_s[layer], w)
            ckv_s.append(_tm(ckv))
            kr_s.append(_tm(kr))
            conv_s.append(_tm(hist))
        fin = final_norm[None] if last else None
        xp = _ffn(gf, xp, mod_p[layer], 6, ng[2:3], w1, w3, w2, (layer, 1), fin)
        xs = _ffn(gs, xs, mod_s[layer], 6, ng[2:3], w1, w3, w2, (layer, 1), fin)

    st = jnp.stack
    return (xp, _tm(xs), st(gla_p), st(gla_s), st(pool_p), st(pool_s), st(ckv_p), st(ckv_s),
            st(kr_p), st(kr_s), st(conv_p), st(conv_s))
```

```python
import functools

import jax
import jax.numpy as jnp
import numpy as np
from jax import lax
from jax.experimental import pallas as pl
from jax.experimental.pallas import tpu as pltpu

F32 = jnp.float32
BF16 = jnp.bfloat16

EPS = 1e-6
NEG = -1e30
N_MOD = 9
GLA_HEADS = 4
GLA_DK = 64
GLA_DV = 128
GLA_GATE_RANK = 16
GLA_TAU = 16.0
GLA_CHUNK = 64
POOL_WINDOWS = (2, 4, 8, 16)
POOL_GROUP = 128
POOL_WIDTH = 512
POOL_HIST = 15
MLA_HEADS = 8
MLA_Q_RANK = 384
MLA_KV_RANK = 256
MLA_NOPE = 64
MLA_ROPE = 32
MLA_V = 64
MLA_SCALE = (MLA_NOPE + MLA_ROPE) ** -0.5
LOG2E = 1.4426950408889634
ROPE_BASE = 10000.0
CONV_WIDTH = 31
CONV_CH = 512
CONV_HIST = 30

LANES = 128
SUBLANES = 8
CONV_ROWS = 32
ROPE_LANE0 = 64
VT_ROWS = 80
POOL_PAD = 16
CONV_PAD = 32
VMEM_LIMIT = 52 * 2 ** 20
TOKEN_TILE = 1024
ATTN_TILE = 1024
ATTN_DEPTH = 3
ATTN_QCHUNK = 512
FFN_TOKEN_TILE = 1024
FF_SUB = 512
GLA_SAMPLE_BATCHES = 4


def _cp(*sem):
    return pltpu.CompilerParams(dimension_semantics=sem, vmem_limit_bytes=VMEM_LIMIT)


def _dot(a, b):
    return jnp.dot(a, b, preferred_element_type=F32)


def _dot_nt(a, b):
    return lax.dot_general(a, b, (((1,), (1,)), ((), ())), preferred_element_type=F32)


def _dot_tn(a, b):
    return lax.dot_general(a, b, (((0,), (0,)), ((), ())), preferred_element_type=F32)


def _silu(x):
    return x * jax.nn.sigmoid(x)


def _rms(x, g):
    return x * lax.rsqrt(jnp.mean(x * x, axis=-1, keepdims=True) + EPS) * g


def _modulated(x_ref, g_ref, sh_ref, sc_ref):
    x = x_ref[...]
    h = _rms(x, g_ref[...]) * (1.0 + sc_ref[...]) + sh_ref[...]
    return h.reshape(x.shape[0] * x.shape[1], x.shape[2]).astype(BF16)


class _Group:
    def __init__(self, prompt, lead, rows, tile):
        self.prompt = prompt
        self.grid = (lead, rows // tile) if prompt else (1, 1)
        self.block = (1, tile) if prompt else (lead, rows)

    def act(self, width):
        return pl.BlockSpec(self.block + (width,), lambda b, t, *_: (b, t, 0))

    def mod(self, k):
        if self.prompt:
            return pl.BlockSpec((None, None, 1, self.dm), lambda b, t, *_: (b, k, 0, 0))
        return pl.BlockSpec((None, self.block[1], self.dm), lambda b, t, *_: (k, 0, 0))

    def pos(self):
        if self.prompt:
            return pl.BlockSpec((1, self.block[1], LANES), lambda b, t, *_: (0, t, 0))
        return pl.BlockSpec((self.block[0], 1, LANES), lambda b, t, *_: (0, 0, 0))

    dm = 1024


def _full(a):
    nd = a.ndim
    return pl.BlockSpec(a.shape, lambda *_: (0,) * nd)


def _sds(shape, dtype=F32):
    return jax.ShapeDtypeStruct(shape, dtype)


def _ada_kernel(c_ref, w_ref, b_ref, o_ref):
    c = c_ref[...]
    o_ref[...] = _dot(_silu(c).astype(BF16), w_ref[...].astype(BF16)) + b_ref[...]


def _ada(c_all, ada_w, ada_b):
    depth, dm, n = ada_w.shape
    m = c_all.shape[0]
    tn = dm
    return pl.pallas_call(
        _ada_kernel,
        grid=(depth, n // tn),
        in_specs=[pl.BlockSpec((m, dm), lambda l, j: (0, 0)),
                  pl.BlockSpec((None, dm, tn), lambda l, j: (l, 0, j)),
                  pl.BlockSpec((None, 1, tn), lambda l, j: (l, 0, j))],
        out_specs=pl.BlockSpec((None, m, tn), lambda l, j: (l, 0, j)),
        out_shape=_sds((depth, m, n)),
        compiler_params=_cp("parallel", "parallel"),
    )(c_all, ada_w, ada_b.reshape(depth, 1, n))


def _ffn_kernel(*refs, final):
    if final:
        x_ref, sh_ref, sc_ref, gt_ref, g_ref, w1_ref, w3_ref, w2_ref, fn_ref, o_ref = refs
    else:
        x_ref, sh_ref, sc_ref, gt_ref, g_ref, w1_ref, w3_ref, w2_ref, o_ref = refs
    h = _modulated(x_ref, g_ref, sh_ref, sc_ref)
    dff = w1_ref.shape[1]
    y = None
    for c0 in range(0, dff, FF_SUB):
        cols = slice(c0, min(c0 + FF_SUB, dff))
        a = _dot(h, w1_ref[:, cols])
        b = _dot(h, w3_ref[:, cols])
        part = _dot((_silu(a) * b).astype(BF16), w2_ref[cols, :])
        y = part if y is None else y + part
    x = x_ref[...]
    xn = x + 0.5 * gt_ref[...] * y.reshape(x.shape)
    o_ref[...] = _rms(xn, fn_ref[...]) if final else xn


def _resident(a, lead):
    idx = tuple(lead) + (0, 0)
    return pl.BlockSpec((None,) * len(lead) + a.shape[-2:], lambda *_: idx, pipeline_mode=pl.Buffered(1))


def _ffn(grp, x, mod, k0, g, w1, w3, w2, which, final_g=None):
    dm = x.shape[-1]
    final = final_g is not None
    ins = [x, mod, mod, mod, g, w1, w3, w2] + ([final_g] if final else [])
    specs = [grp.act(dm), grp.mod(k0), grp.mod(k0 + 1), grp.mod(k0 + 2), _full(g),
             _resident(w1, which), _resident(w3, which), _resident(w2, which)] + ([_full(final_g)] if final else [])
    return pl.pallas_call(
        functools.partial(_ffn_kernel, final=final),
        grid=grp.grid,
        in_specs=specs,
        out_specs=grp.act(dm),
        out_shape=_sds(x.shape),
        compiler_params=_cp("parallel", "parallel"),
    )(*ins)


def _log_sigmoid(x):
    return jnp.minimum(x, 0.0) - jnp.log(1.0 + jnp.exp(-jnp.abs(x)))


def _even_in_kernel(x_ref, sh_ref, sc_ref, g_ref, wq_ref, wg_ref, w2_ref, gb_ref, wu_ref,
                    q_ref, k_ref, v_ref, r_ref, la_ref, u_ref):
    h = _modulated(x_ref, g_ref, sh_ref, sc_ref)
    lead = x_ref.shape[:2]
    hk = GLA_HEADS * GLA_DK
    hv = GLA_HEADS * GLA_DV
    z = _dot(h, wq_ref[...])
    q_ref[...] = (z[:, :hk] * GLA_DK ** -0.5).reshape(lead + (hk,))
    k_ref[...] = z[:, hk:2 * hk].reshape(lead + (hk,))
    v_ref[...] = z[:, 2 * hk:2 * hk + hv].reshape(lead + (hv,))
    r_ref[...] = z[:, 2 * hk + hv:].reshape(lead + (hv,))
    g_low = _dot(h, wg_ref[...]).astype(BF16)
    gate = _dot(g_low, w2_ref[...]) + gb_ref[...]
    la_ref[...] = (_log_sigmoid(gate) / GLA_TAU).reshape(lead + (hk,))
    u_ref[...] = _dot(h, wu_ref[...]).reshape(lead + (POOL_WIDTH,))


def _even_in(grp, x, mod, g, wq, wg, w2, gb, wu):
    hk = GLA_HEADS * GLA_DK
    hv = GLA_HEADS * GLA_DV
    widths = (hk, hk, hv, hv, hk, POOL_WIDTH)
    return pl.pallas_call(
        _even_in_kernel,
        grid=grp.grid,
        in_specs=[grp.act(x.shape[-1]), grp.mod(3), grp.mod(4), _full(g), _full(wq), _full(wg),
                  _full(w2), _full(gb), _full(wu)],
        out_specs=[grp.act(w) for w in widths],
        out_shape=[_sds(x.shape[:2] + (w,)) for w in widths],
        compiler_params=_cp("parallel", "parallel"),
    )(x, mod, mod, g, wq, wg, w2, gb, wu)


def _cumsum_rows(tril, x):
    hi = x.astype(BF16)
    r1 = x - hi.astype(F32)
    mid = r1.astype(BF16)
    lo = (r1 - mid.astype(F32)).astype(BF16)
    return _dot(tril, hi) + _dot(tril, mid) + _dot(tril, lo)


def _gla_kernel(*refs, nbb, tt, has_s0):
    if has_s0:
        q_ref, k_ref, v_ref, la_ref, s0_ref, o_ref, so_ref, s_scr = refs
    else:
        q_ref, k_ref, v_ref, la_ref, o_ref, so_ref, s_scr = refs
    t = pl.program_id(1)
    c_rows = GLA_CHUNK

    @pl.when(t == 0)
    def _():
        s_scr[...] = s0_ref[...] if has_s0 else jnp.zeros_like(s_scr)

    row = lax.broadcasted_iota(jnp.int32, (c_rows, c_rows), 0)
    col = lax.broadcasted_iota(jnp.int32, (c_rows, c_rows), 1)
    causal = col <= row
    tril = jnp.where(causal, 1.0, 0.0).astype(BF16)
    lane = lax.broadcasted_iota(jnp.int32, (c_rows, LANES), 1)
    srow = lax.broadcasted_iota(jnp.int32, (LANES, LANES), 0)

    def chunk(c, carry):
        rows = pl.ds(pl.multiple_of(c * c_rows, c_rows), c_rows)
        for n in range(nbb):
            for p in range(GLA_HEADS // 2):
                sl = slice(LANES * p, LANES * (p + 1))
                b = _cumsum_rows(tril, la_ref[n, rows, sl])
                b_last = b[c_rows - 1:c_rows, :]
                k = k_ref[n, rows, sl]
                qi = q_ref[n, rows, sl] * jnp.exp(b)
                ki = (k * jnp.exp(-b)).astype(BF16)
                kd = (k * jnp.exp(b_last - b)).astype(BF16)
                s_old = s_scr[n, p]
                s_bf = s_old.astype(BF16)
                decay = jnp.transpose(jnp.broadcast_to(jnp.exp(b_last), (LANES, LANES)))
                upd = []
                for hh in range(2):
                    mine = (lane < GLA_DK) if hh == 0 else (lane >= GLA_DK)
                    qh = jnp.where(mine, qi, 0.0).astype(BF16)
                    att = jnp.where(causal, _dot_nt(qh, ki), 0.0).astype(BF16)
                    vs = slice(GLA_DV * (2 * p + hh), GLA_DV * (2 * p + hh + 1))
                    vh = v_ref[n, rows, vs].astype(BF16)
                    o_ref[n, rows, vs] = _dot(qh, s_bf) + _dot(att, vh)
                    upd.append(_dot_tn(kd, vh))
                s_scr[n, p] = decay * s_old + jnp.where(srow < GLA_DK, upd[0], upd[1])
        return carry

    lax.fori_loop(0, tt // c_rows, chunk, 0, unroll=min(2, tt // c_rows))

    @pl.when(t == pl.num_programs(1) - 1)
    def _():
        so_ref[...] = s_scr[...]


def _gla(q, k, v, la, s0, nbb, tt):
    nb, nt, hk = q.shape
    hv = v.shape[-1]
    has_s0 = s0 is not None
    act = lambda w: pl.BlockSpec((nbb, tt, w), lambda b, t: (b, t, 0))
    st = pl.BlockSpec((nbb, GLA_HEADS // 2, LANES, GLA_DV), lambda b, t: (b, 0, 0, 0))
    return pl.pallas_call(
        functools.partial(_gla_kernel, nbb=nbb, tt=tt, has_s0=has_s0),
        grid=(nb // nbb, nt // tt),
        in_specs=[act(hk), act(hk), act(hv), act(hk)] + ([st] if has_s0 else []),
        out_specs=[act(hv), st],
        out_shape=[_sds((nb, nt, hv)), _sds((nb, GLA_HEADS // 2, LANES, GLA_DV))],
        scratch_shapes=[pltpu.VMEM((nbb, GLA_HEADS // 2, LANES, GLA_DV), F32)],
        compiler_params=_cp("parallel", "arbitrary"),
    )(*([q, k, v, la] + ([s0] if has_s0 else [])))


def _gla_gate(o, r, on):
    outs = []
    for h in range(GLA_HEADS):
        sl = slice(GLA_DV * h, GLA_DV * (h + 1))
        outs.append(_rms(o[:, sl], on[:, sl]))
    return jnp.concatenate(outs, axis=-1) * _silu(r)


def _even_mix_out(x_ref, gt_ref, og, pooled, pw_ref, ps_ref, wo_ref, xo_ref):
    mixed = [_dot(pooled[g].astype(BF16), pw_ref[g]) for g in range(len(POOL_WINDOWS))]
    mixed = jnp.concatenate(mixed, axis=-1) * ps_ref[...]
    n_o = og.shape[-1]
    y = _dot(og.astype(BF16), wo_ref[:n_o, :]) + _dot(mixed.astype(BF16), wo_ref[n_o:, :])
    x = x_ref[...]
    xo_ref[...] = x + gt_ref[...] * y.reshape(x.shape)


def _even_out_prompt_kernel(x_ref, o_ref, r_ref, u_ref, gt_ref, on_ref, pw_ref, ps_ref, wo_ref,
                            xo_ref, ho_ref, hb, *, tm, p0):
    t = pl.program_id(1)

    @pl.when(t == 0)
    def _():
        hb[0:POOL_PAD, :] = jnp.zeros((POOL_PAD, POOL_WIDTH), F32)

    u = u_ref[0]
    hb[POOL_PAD:POOL_PAD + tm, :] = u
    pos = t * tm + lax.broadcasted_iota(jnp.int32, (tm, 1), 0)
    pooled = []
    for g, w in enumerate(POOL_WINDOWS):
        sl = slice(POOL_GROUP * g, POOL_GROUP * (g + 1))
        win = hb[POOL_PAD - (w - 1):POOL_PAD - (w - 1) + tm, sl]
        for j in range(w - 2, -1, -1):
            win = win + hb[POOL_PAD - j:POOL_PAD - j + tm, sl]
        cnt = jnp.minimum(p0 + pos + 1, w).astype(F32)
        pooled.append(win / cnt - u[:, sl])
    og = _gla_gate(o_ref[0], r_ref[0], on_ref[...])
    _even_mix_out(x_ref, gt_ref, og, pooled, pw_ref, ps_ref, wo_ref, xo_ref)
    hb[0:POOL_PAD, :] = hb[tm:tm + POOL_PAD, :]

    @pl.when(t == pl.num_programs(1) - 1)
    def _():
        ho_ref[0] = hb[0:POOL_PAD, :]


def _even_out_sample_kernel(x_ref, o_ref, r_ref, u_ref, hi_ref, gt_ref, on_ref, pw_ref, ps_ref, wo_ref,
                            xo_ref, hn_ref, *, p0):
    steps, nb = u_ref.shape[:2]

    def slab(i, sl):
        return hi_ref[i, :, sl] if i < POOL_HIST else u_ref[i - POOL_HIST, :, sl]

    pooled = []
    for g, w in enumerate(POOL_WINDOWS):
        sl = slice(POOL_GROUP * g, POOL_GROUP * (g + 1))
        rows = []
        for t in range(steps):
            win = slab(POOL_HIST + t - (w - 1), sl)
            for j in range(w - 2, -1, -1):
                win = win + slab(POOL_HIST + t - j, sl)
            rows.append(win / float(min(p0 + t + 1, w)) - u_ref[t, :, sl])
        pooled.append(jnp.concatenate(rows, axis=0))
    flat = lambda ref: ref[...].reshape(steps * nb, ref.shape[-1])
    og = _gla_gate(flat(o_ref), flat(r_ref), on_ref[...])
    _even_mix_out(x_ref, gt_ref, og, pooled, pw_ref, ps_ref, wo_ref, xo_ref)
    for i in range(POOL_HIST):
        hn_ref[i] = slab(steps + i, slice(None))


def _even_out(grp, x, o, r, u, hist, mod, on, pw, ps, wo, p0):
    dm = x.shape[-1]
    common = [mod, on, pw, ps, wo]
    common_specs = [grp.mod(5), _full(on), _full(pw), _full(ps), _full(wo)]
    acts = [grp.act(dm), grp.act(o.shape[-1]), grp.act(r.shape[-1]), grp.act(u.shape[-1])]
    if grp.prompt:
        tm = grp.block[1]
        return pl.pallas_call(
            functools.partial(_even_out_prompt_kernel, tm=tm, p0=p0),
            grid=grp.grid,
            in_specs=acts + common_specs,
            out_specs=[grp.act(dm), pl.BlockSpec((1, POOL_PAD, POOL_WIDTH), lambda b, t: (b, 0, 0))],
            out_shape=[_sds(x.shape), _sds((x.shape[0], POOL_PAD, POOL_WIDTH))],
            scratch_shapes=[pltpu.VMEM((POOL_PAD + tm, POOL_WIDTH), F32)],
            compiler_params=_cp("parallel", "arbitrary"),
        )(x, o, r, u, *common)
    return pl.pallas_call(
        functools.partial(_even_out_sample_kernel, p0=p0),
        grid=grp.grid,
        in_specs=acts + [_full(hist)] + common_specs,
        out_specs=[grp.act(dm), _full(hist)],
        out_shape=[_sds(x.shape), _sds(hist.shape)],
        compiler_params=_cp("parallel", "arbitrary"),
    )(x, o, r, u, hist, *common)


def _rope_slab(x, cs, s1, s2, lead):
    shp = lead + (LANES,)
    back = pltpu.roll(x, LANES - MLA_ROPE // 2, 1).reshape(shp)
    fwd = pltpu.roll(x, MLA_ROPE // 2, 1).reshape(shp)
    out = x.reshape(shp) * cs + back * s1 + fwd * s2
    return out.reshape(x.shape)


def _odd_in_kernel(*refs, sample):
    (x_ref, sh_ref, sc_ref, g_ref, wcq_ref, wckv_ref, wkr_ref, wga_ref, wgg_ref, qn_ref, wuq_ref,
     kvn_ref, cs_ref, s1_ref, s2_ref) = refs[:15]
    if sample:
        wkl_ref, ckv_ref, kr_ref, uc_ref, q_ref, ql_ref = refs[15:]
    else:
        wuk_ref, wuvt_ref, ckv_ref, kr_ref, uc_ref, q_ref, k_ref, vt_ref = refs[15:]
    lead = x_ref.shape[:2]
    h = _modulated(x_ref, g_ref, sh_ref, sc_ref)
    cs, s1, s2 = cs_ref[...], s1_ref[...], s2_ref[...]

    cq = _rms(_dot(h, wcq_ref[...]), qn_ref[...]).astype(BF16)
    q = _dot(cq, wuq_ref[...]) * (MLA_SCALE * LOG2E)
    q = jnp.concatenate(
        [_rope_slab(q[:, LANES * i:LANES * (i + 1)], cs, s1, s2, lead) for i in range(MLA_HEADS)], axis=-1)
    q_ref[...] = q.astype(BF16).reshape(lead + (MLA_HEADS * LANES,))

    ckv = _rms(_dot(h, wckv_ref[...]), kvn_ref[...])
    ckv_ref[...] = ckv.reshape(lead + (MLA_KV_RANK,))
    kr = _rope_slab(_dot(h, wkr_ref[...]), cs, s1, s2, lead)
    kr_ref[...] = kr.reshape(lead + (LANES,))
    uc_ref[...] = (_dot(h, wga_ref[...]) * jax.nn.sigmoid(_dot(h, wgg_ref[...]))).reshape(lead + (CONV_CH,))

    if sample:
        qb = q.astype(BF16)
        for i in range(MLA_HEADS):
            ql = _dot(qb[:, LANES * i:LANES * (i + 1)], wkl_ref[i])
            ql_ref[:, :, MLA_KV_RANK * i:MLA_KV_RANK * (i + 1)] = ql.astype(BF16).reshape(lead + (MLA_KV_RANK,))
    else:
        cb = ckv.astype(BF16)
        kn = _dot(cb, wuk_ref[...])
        kn = jnp.concatenate([kn[:, LANES * i:LANES * (i + 1)] + kr for i in range(MLA_HEADS)], axis=-1)
        k_ref[...] = kn.astype(BF16).reshape(lead + (MLA_HEADS * LANES,))
        one_row = lax.broadcasted_iota(jnp.int32, (MLA_HEADS, VT_ROWS, 1), 1) == MLA_V
        ones = jnp.where(one_row, 1.0, 0.0).reshape(MLA_HEADS * VT_ROWS, 1)
        vt_ref[0] = (_dot_nt(wuvt_ref[...], cb) + ones).astype(BF16)


def _odd_in(grp, x, mod, g, w, tabs, sample):
    lead = x.shape[:2]
    hl = MLA_HEADS * LANES
    ins = [x, mod, mod, g, w['wcq'], w['wckv'], w['wkr'], w['wga'], w['wgg'], w['qn'], w['wuq'], w['kvn'], *tabs]
    specs = [grp.act(x.shape[-1]), grp.mod(3), grp.mod(4)] + [_full(a) for a in ins[3:12]] + [grp.pos()] * 3
    outs = [(MLA_KV_RANK, F32), (LANES, F32), (CONV_CH, F32), (hl, BF16)]
    if sample:
        ins.append(w['wkl'])
        outs.append((MLA_HEADS * MLA_KV_RANK, BF16))
    else:
        ins += [w['wuk'], w['wuvt']]
        outs.append((hl, BF16))
    specs += [_full(a) for a in ins[15:]]
    out_specs = [grp.act(wd) for wd, _ in outs]
    out_shape = [_sds(lead + (wd,), dt) for wd, dt in outs]
    if not sample:
        vt_rows = MLA_HEADS * VT_ROWS
        out_specs.append(pl.BlockSpec((1, vt_rows, grp.block[1]), lambda b, t: (b, 0, t)))
        out_shape.append(_sds((lead[0], vt_rows, lead[1]), BF16))
    return pl.pallas_call(
        functools.partial(_odd_in_kernel, sample=sample),
        grid=grp.grid,
        in_specs=specs,
        out_specs=out_specs,
        out_shape=out_shape,
        compiler_params=_cp("parallel", "parallel"),
    )(*ins)


def _attn_prompt_kernel(qi_ref, ki_ref, q_ref, k_ref, vt_ref, o_ref, m_scr, acc_scr, s_scr, p_scr, *, tq):
    step = pl.program_id(1)
    qi = qi_ref[step]
    ki = ki_ref[step]

    @pl.when(ki == 0)
    def _():
        m_scr[...] = jnp.full_like(m_scr, NEG)
        acc_scr[...] = jnp.zeros_like(acc_scr)

    def update(diagonal):
        width = s_scr.shape[2]
        items = [(h, c) for h in range(MLA_HEADS) for c in range(0, tq, width)]

        def scores(item, buf):
            h, c = item
            sl = slice(LANES * h, LANES * (h + 1))
            st = _dot_nt(k_ref[0, :, sl], q_ref[0, c:c + width, sl])
            if diagonal:
                key = lax.broadcasted_iota(jnp.int32, (tq, width), 0)
                qry = lax.broadcasted_iota(jnp.int32, (tq, width), 1) + c
                st = jnp.where(key <= qry, st, NEG)
            s_scr[buf] = st

        def absorb(item, buf):
            h, c = item
            vr = slice(VT_ROWS * h, VT_ROWS * (h + 1))
            qc = slice(c, c + width)
            m_prev = m_scr[h:h + 1, qc]
            m_new = jnp.maximum(m_prev, jnp.max(s_scr[buf], axis=0, keepdims=True))
            p_scr[buf] = jnp.exp2(s_scr[buf] - m_new).astype(BF16)
            acc_scr[vr, qc] = jnp.exp2(m_prev - m_new) * acc_scr[vr, qc] + _dot(vt_ref[0, vr, :], p_scr[buf])
            m_scr[h:h + 1, qc] = m_new

        depth = s_scr.shape[0]
        for i in range(depth - 1):
            scores(items[i], i)
        for i, item in enumerate(items):
            if i + depth - 1 < len(items):
                scores(items[i + depth - 1], (i + depth - 1) % depth)
            absorb(item, i % depth)

    @pl.when(ki < qi)
    def _():
        update(False)

    @pl.when(ki == qi)
    def _():
        update(True)
        for h in range(0, MLA_HEADS, 2):
            pair = []
            for hh in (h, h + 1):
                acc = acc_scr[VT_ROWS * hh:VT_ROWS * (hh + 1), :]
                pair.append(acc[:MLA_V] / acc[MLA_V:MLA_V + 1])
            o_ref[0, :, MLA_V * h:MLA_V * (h + 2)] = jnp.transpose(jnp.concatenate(pair, axis=0)).astype(BF16)


def _attn_prompt(q, k, vt, tq):
    nb, nt, hl = q.shape
    vt_rows = vt.shape[1]
    n_out = MLA_HEADS * MLA_V
    pairs = [(i, j) for i in range(nt // tq) for j in range(i + 1)]
    qi_tab = jnp.asarray([p[0] for p in pairs], jnp.int32)
    ki_tab = jnp.asarray([p[1] for p in pairs], jnp.int32)
    return pl.pallas_call(
        functools.partial(_attn_prompt_kernel, tq=tq),
        grid_spec=pltpu.PrefetchScalarGridSpec(
            num_scalar_prefetch=2,
            grid=(nb, len(pairs)),
            in_specs=[pl.BlockSpec((1, tq, hl), lambda b, s, qt, kt: (b, qt[s], 0)),
                      pl.BlockSpec((1, tq, hl), lambda b, s, qt, kt: (b, kt[s], 0)),
                      pl.BlockSpec((1, vt_rows, tq), lambda b, s, qt, kt: (b, 0, kt[s]))],
            out_specs=pl.BlockSpec((1, tq, n_out), lambda b, s, qt, kt: (b, qt[s], 0)),
            scratch_shapes=[pltpu.VMEM((MLA_HEADS, tq), F32), pltpu.VMEM((vt_rows, tq), F32),
                            pltpu.VMEM((ATTN_DEPTH, tq, min(ATTN_QCHUNK, tq)), F32),
                            pltpu.VMEM((ATTN_DEPTH, tq, min(ATTN_QCHUNK, tq)), BF16)]),
        out_shape=_sds((nb, nt, n_out), BF16),
        compiler_params=_cp("parallel", "arbitrary"),
    )(qi_tab, ki_tab, q, k, vt)


def _attn_sample_kernel(pt_ref, ql_ref, qr_ref, cn_ref, kn_ref, ckv_hbm, kr_hbm, o_ref,
                        ckv_buf, kr_buf, pg_scr, s_scr, p_scr, sem, *, npg, layer):
    b = pl.program_id(0)
    nb = pl.num_programs(0)
    slot = lax.rem(b, 2)
    ql = ql_ref[0]
    qr = qr_ref[0]
    rows = ql.shape[0]

    def page_copies(page, i, sl):
        return (pltpu.make_async_copy(ckv_hbm.at[layer, page], ckv_buf.at[sl, i], sem.at[0, sl]),
                pltpu.make_async_copy(kr_hbm.at[layer, page], kr_buf.at[sl, i], sem.at[1, sl]))

    def start_page(batch, i, sl):
        for cp in page_copies(pt_ref[batch, i], i, sl):
            cp.start()

    def start_first(i, carry):
        start_page(0, i, 0)
        return carry

    @pl.when(b == 0)
    def _():
        lax.fori_loop(0, npg, start_first, 0)

    for i in range(npg):
        for cp in page_copies(0, i, slot):
            cp.wait()

    def score_pages(prefetch):
        for i in range(npg):
            if prefetch:
                for k in range(2 * i, min(2 * i + 2, npg)):
                    start_page(b + 1, k, 1 - slot)
            page = ckv_buf[slot, i].astype(BF16)
            pg_scr[i] = page
            s_scr[i] = _dot_nt(ql, page) + _dot(qr, kr_buf[slot, i].astype(BF16))

    @pl.when(b + 1 < nb)
    def _():
        score_pages(True)

    @pl.when(b + 1 == nb)
    def _():
        score_pages(False)

    n_new = cn_ref.shape[1]
    pad = jnp.zeros((LANES - n_new, MLA_KV_RANK), F32)
    cn = jnp.concatenate([cn_ref[0], pad], axis=0).astype(BF16)
    kn = jnp.concatenate([kn_ref[0], pad[:, :MLA_ROPE]], axis=0).astype(BF16)
    step = lax.shift_right_logical(lax.broadcasted_iota(jnp.int32, (rows, LANES), 0), MLA_HEADS.bit_length() - 1)
    col = lax.broadcasted_iota(jnp.int32, (rows, LANES), 1)
    s_new = jnp.where(col <= step, _dot_nt(ql, cn) + _dot_nt(qr, kn), NEG)

    s = s_scr[...]
    m = jnp.maximum(jnp.max(jnp.max(s, axis=0), axis=-1, keepdims=True),
                    jnp.max(s_new, axis=-1, keepdims=True))
    p = jnp.exp2(s - m)
    p_new = jnp.exp2(s_new - m)
    denom = jnp.sum(jnp.sum(p, axis=0), axis=-1, keepdims=True) + jnp.sum(p_new, axis=-1, keepdims=True)
    p_scr[...] = p.astype(BF16)

    def weigh_page(i, acc):
        return acc + _dot(p_scr[i], pg_scr[i])

    acc = lax.fori_loop(0, npg, weigh_page, _dot(p_new.astype(BF16), cn), unroll=True)
    o_ref[0] = acc / denom


def _attn_sample(page_table, ql, qr, cn, kn, cache_ckv, cache_kr, layer):
    nb, rows, rank = ql.shape
    npg = page_table.shape[1]
    page, rope = cache_ckv.shape[2], cache_kr.shape[2]
    assert page == LANES, "one cache page must fill one lane tile of scores"
    per_b = lambda a: pl.BlockSpec((1,) + a.shape[1:], lambda b, pt: (b, 0, 0))
    hbm = pl.BlockSpec(memory_space=pl.ANY)
    return pl.pallas_call(
        functools.partial(_attn_sample_kernel, npg=npg, layer=layer),
        grid_spec=pltpu.PrefetchScalarGridSpec(
            num_scalar_prefetch=1,
            grid=(nb,),
            in_specs=[per_b(ql), per_b(qr), per_b(cn), per_b(kn), hbm, hbm],
            out_specs=pl.BlockSpec((1, rows, rank), lambda b, pt: (b, 0, 0)),
            scratch_shapes=[pltpu.VMEM((2, npg, page, rank), F32),
                            pltpu.VMEM((2, npg, rope, page), F32),
                            pltpu.VMEM((npg, page, rank), BF16),
                            pltpu.VMEM((npg, rows, page), F32),
                            pltpu.VMEM((npg, rows, page), BF16),
                            pltpu.SemaphoreType.DMA((2, 2))]),
        out_shape=_sds((nb, rows, rank)),
        compiler_params=_cp("arbitrary"),
    )(page_table, ql, qr, cn, kn, cache_ckv, cache_kr)


def _conv_norm_act(cv, cb_ref, lg_ref, lb_ref):
    cv = cv + cb_ref[...]
    mu = jnp.mean(cv, axis=-1, keepdims=True)
    d = cv - mu
    y = d * lax.rsqrt(jnp.mean(d * d, axis=-1, keepdims=True) + EPS)
    return _silu(y * lg_ref[...] + lb_ref[...])


def _odd_out_prompt_kernel(x_ref, a_ref, uc_ref, gt_ref, cw_ref, cb_ref, lg_ref, lb_ref, woa_ref, woc_ref,
                           xo_ref, ho_ref, hb, hs, cvb, *, tm):
    t = pl.program_id(1)

    @pl.when(t == 0)
    def _():
        hb[0:CONV_PAD, :] = jnp.zeros((CONV_PAD, CONV_CH), F32)

    hb[CONV_PAD:CONV_PAD + tm, :] = uc_ref[0]
    for sft in range(1, SUBLANES):
        hs[sft - 1] = hb[sft:sft + hs.shape[1], :]
    base = CONV_PAD - CONV_HIST

    for r0 in range(0, tm, CONV_ROWS):
        acc = None
        for j in range(CONV_WIDTH):
            whole, sft = divmod(base + j, SUBLANES)
            rows = slice(r0 + whole * SUBLANES, r0 + whole * SUBLANES + CONV_ROWS)
            term = cw_ref[j:j + 1, :] * (hb[rows, :] if sft == 0 else hs[sft - 1, rows, :])
            acc = term if acc is None else acc + term
        cvb[r0:r0 + CONV_ROWS, :] = acc
    cv = _conv_norm_act(cvb[...], cb_ref, lg_ref, lb_ref)
    y = _dot(a_ref[0], woa_ref[...]) + _dot(cv.astype(BF16), woc_ref[...])
    xo_ref[0] = x_ref[0] + gt_ref[...] * y
    hb[0:CONV_PAD, :] = hb[tm:tm + CONV_PAD, :]

    @pl.when(t == pl.num_programs(1) - 1)
    def _():
        ho_ref[0] = hb[0:CONV_PAD, :]


def _odd_out_sample_kernel(x_ref, lat_ref, uc_ref, hi_ref, gt_ref, cw_ref, cb_ref, lg_ref, lb_ref, wuv_ref,
                           woa_ref, woc_ref, xo_ref, hn_ref):
    steps, nb = uc_ref.shape[:2]

    def slab(i):
        return hi_ref[i] if i < CONV_HIST else uc_ref[i - CONV_HIST]

    rows = []
    for t in range(steps):
        cv = cw_ref[0:1, :] * slab(t)
        for j in range(1, CONV_WIDTH):
            cv = cv + cw_ref[j:j + 1, :] * slab(t + j)
        rows.append(cv)
    cv = _conv_norm_act(jnp.concatenate(rows, axis=0), cb_ref, lg_ref, lb_ref)
    lat = lat_ref[...].reshape(steps * nb, lat_ref.shape[-1]).astype(BF16)
    attn = _dot(lat, wuv_ref[...]).astype(BF16)
    y = _dot(attn, woa_ref[...]) + _dot(cv.astype(BF16), woc_ref[...])
    x = x_ref[...]
    xo_ref[...] = x + gt_ref[...] * y.reshape(x.shape)
    for i in range(CONV_HIST):
        hn_ref[i] = slab(steps + i)


def _odd_out(grp, x, a, uc, hist, mod, w):
    dm = x.shape[-1]
    conv = [w['cw'], w['cb'], w['lg'], w['lb']]
    if grp.prompt:
        tm = grp.block[1]
        ins = [x, a, uc, mod] + conv + [w['woa'], w['woc']]
        return pl.pallas_call(
            functools.partial(_odd_out_prompt_kernel, tm=tm),
            grid=grp.grid,
            in_specs=[grp.act(dm), grp.act(a.shape[-1]), grp.act(CONV_CH), grp.mod(5)] + [_full(v) for v in ins[4:]],
            out_specs=[grp.act(dm), pl.BlockSpec((1, CONV_PAD, CONV_CH), lambda b, t: (b, 0, 0))],
            out_shape=[_sds(x.shape), _sds((x.shape[0], CONV_PAD, CONV_CH))],
            scratch_shapes=[pltpu.VMEM((CONV_PAD + tm, CONV_CH), F32),
                            pltpu.VMEM((SUBLANES - 1, CONV_PAD + tm - SUBLANES, CONV_CH), F32),
                            pltpu.VMEM((tm, CONV_CH), F32)],
            compiler_params=_cp("parallel", "arbitrary"),
        )(*ins)
    ins = [x, a, uc, hist, mod] + conv + [w['wuv_bd'], w['woa'], w['woc']]
    return pl.pallas_call(
        _odd_out_sample_kernel,
        grid=grp.grid,
        in_specs=[grp.act(dm), grp.act(a.shape[-1]), grp.act(CONV_CH), _full(hist), grp.mod(5)]
        + [_full(v) for v in ins[5:]],
        out_specs=[grp.act(dm), _full(hist)],
        out_shape=[_sds(x.shape), _sds(hist.shape)],
        compiler_params=_cp("parallel", "arbitrary"),
    )(*ins)


def _head_pad(w, heads, width, offset=0):
    kdim = w.shape[0]
    w = w.reshape(kdim, heads, width)
    w = jnp.pad(w, ((0, 0), (0, 0), (offset, LANES - width - offset)))
    return w.reshape(kdim, heads * LANES)


def _rope_tables(pos):
    half = MLA_ROPE // 2
    freqs = ROPE_BASE ** (-jnp.arange(half, dtype=F32) / half)
    ang = pos.astype(F32)[..., None] * freqs
    cos, sin = jnp.cos(ang), jnp.sin(ang)
    zeros = jnp.zeros_like(cos)
    lead = jnp.ones(pos.shape + (ROPE_LANE0,), F32)
    tail = jnp.zeros(pos.shape + (LANES - ROPE_LANE0 - MLA_ROPE,), F32)
    cs = jnp.concatenate([lead, cos, cos, tail], axis=-1)
    s1 = jnp.concatenate([0 * lead, -sin, zeros, tail], axis=-1)
    s2 = jnp.concatenate([0 * lead, zeros, sin, tail], axis=-1)
    return cs, s1, s2


def _even_weights(w_in, gate_w2, gate_b, out_norm, pool_w, pool_scale, w_out):
    hk = GLA_HEADS * GLA_DK
    hv = GLA_HEADS * GLA_DV
    n_main = 2 * hk + 2 * hv
    wg = jnp.pad(w_in[:, n_main:n_main + GLA_GATE_RANK], ((0, 0), (0, LANES - GLA_GATE_RANK)))
    w2 = jnp.pad(gate_w2, ((0, LANES - GLA_GATE_RANK), (0, 0)))
    return dict(wq=w_in[:, :n_main].astype(BF16), wg=wg.astype(BF16), w2=w2.astype(BF16),
                gb=gate_b[None], wu=w_in[:, n_main + GLA_GATE_RANK:].astype(BF16),
                on=out_norm[None], pw=pool_w.astype(BF16), ps=pool_scale[None], wo=w_out.astype(BF16))


def _odd_weights(w_in, q_norm, w_uq, kv_norm, w_uk, w_uv, conv_w, conv_b, ln_g, ln_b, w_out):
    c0, c1, c2 = MLA_Q_RANK, MLA_Q_RANK + MLA_KV_RANK, MLA_Q_RANK + MLA_KV_RANK + MLA_ROPE
    n_attn = MLA_HEADS * MLA_V
    wkr = jnp.pad(w_in[:, c1:c2], ((0, 0), (ROPE_LANE0, LANES - ROPE_LANE0 - MLA_ROPE)))
    wuk = w_uk.reshape(MLA_KV_RANK, MLA_HEADS * MLA_NOPE)
    wkl = jnp.pad(jnp.transpose(w_uk, (1, 2, 0)), ((0, 0), (0, LANES - MLA_NOPE), (0, 0)))
    eye = jnp.eye(MLA_HEADS, dtype=F32)
    wuv_bd = (eye[:, None, :, None] * jnp.transpose(w_uv, (1, 0, 2))[:, :, None, :]).reshape(
        MLA_HEADS * MLA_KV_RANK, n_attn)
    wuvt = jnp.pad(jnp.transpose(w_uv, (1, 2, 0)), ((0, 0), (0, VT_ROWS - MLA_V), (0, 0))).reshape(
        MLA_HEADS * VT_ROWS, MLA_KV_RANK)
    return dict(wcq=w_in[:, :c0].astype(BF16), wckv=w_in[:, c0:c1].astype(BF16), wkr=wkr.astype(BF16),
                wga=w_in[:, c2:c2 + CONV_CH].astype(BF16), wgg=w_in[:, c2 + CONV_CH:].astype(BF16),
                qn=q_norm[None], wuq=_head_pad(w_uq, MLA_HEADS, MLA_NOPE + MLA_ROPE).astype(BF16),
                kvn=kv_norm[None], wuk=_head_pad(wuk, MLA_HEADS, MLA_NOPE).astype(BF16),
                wuvt=wuvt.astype(BF16), wkl=wkl.astype(BF16),
                wuv_bd=wuv_bd.astype(BF16), cw=conv_w, cb=conv_b[None], lg=ln_g[None], lb=ln_b[None],
                woa=w_out[:n_attn].astype(BF16), woc=w_out[n_attn:].astype(BF16))


def _tm(x):
    return jnp.swapaxes(x, 0, 1)


def kernel(x_prompt, x_sample, state_gla, state_pool, cache_ckv, cache_krope, state_conv, page_table, c_prompt, c_sample, ada_w, ada_b, norm_g, ffn_w1, ffn_w3, ffn_w2, ev_w_in, ev_gate_w2, ev_gate_b, ev_out_norm, ev_pool_w, ev_pool_scale, ev_w_out, od_w_in, od_q_norm, od_w_uq, od_kv_norm, od_w_uk, od_w_uv, od_conv_w, od_conv_b, od_conv_norm_g, od_conv_norm_b, od_w_out, final_norm):
    nbp, seq, dm = x_prompt.shape
    nbs, steps, _ = x_sample.shape
    depth = ada_w.shape[0]
    past_len = page_table.shape[1] * cache_ckv.shape[2]
    tile = min(TOKEN_TILE, seq)
    gp = _Group(True, nbp, seq, tile)
    gf = _Group(True, nbp, seq, min(FFN_TOKEN_TILE, seq))
    gs = _Group(False, steps, nbs, nbs)

    n_c = nbp + nbs
    c_all = jnp.pad(jnp.concatenate([c_prompt, c_sample], axis=0), ((0, -n_c % 8), (0, 0)))
    mod = _ada(c_all, ada_w, ada_b)
    mod_p = mod[:, :nbp].reshape(depth, nbp, N_MOD, 1, dm)
    mod_s = jnp.swapaxes(mod[:, nbp:n_c].reshape(depth, nbs, N_MOD, dm), 1, 2)

    w1, w3, w2 = ffn_w1.astype(BF16), ffn_w3.astype(BF16), ffn_w2.astype(BF16)
    tabs_p = _rope_tables(jnp.arange(seq)[None])
    tabs_s = _rope_tables(past_len + jnp.arange(steps)[:, None])

    xp = x_prompt
    xs = _tm(x_sample)
    gla_p, gla_s, pool_p, pool_s, ckv_p, ckv_s, kr_p, kr_s, conv_p, conv_s = ([] for _ in range(10))
    gla_tt = min(TOKEN_TILE, seq)

    for layer in range(depth):
        i = layer // 2
        ng = norm_g[layer]
        last = layer == depth - 1
        xp = _ffn(gf, xp, mod_p[layer], 0, ng[0:1], w1, w3, w2, (layer, 0))
        xs = _ffn(gs, xs, mod_s[layer], 0, ng[0:1], w1, w3, w2, (layer, 0))
        if layer % 2 == 0:
            w = _even_weights(ev_w_in[i], ev_gate_w2[i], ev_gate_b[i], ev_out_norm[i], ev_pool_w[i],
                              ev_pool_scale[i], ev_w_out[i])
            proj = (ng[1:2], w['wq'], w['wg'], w['w2'], w['gb'], w['wu'])
            out_w = (w['on'], w['pw'], w['ps'], w['wo'])
            q, k, v, r, la, u = _even_in(gp, xp, mod_p[layer], *proj)
            o, s_fin = _gla(q, k, v, la, None, nbp, gla_tt)
            xp, hist = _even_out(gp, xp, o, r, u, None, mod_p[layer], *out_w, 0)
            gla_p.append(s_fin.reshape(nbp, GLA_HEADS, GLA_DK, GLA_DV))
            pool_p.append(hist[:, POOL_PAD - POOL_HIST:])
            q, k, v, r, la, u = _even_in(gs, xs, mod_s[layer], *proj)
            chunked = lambda a: jnp.pad(_tm(a), ((0, 0), (0, GLA_CHUNK - steps), (0, 0)))
            s0 = state_gla[i].reshape(nbs, GLA_HEADS // 2, LANES, GLA_DV)
            o, s_fin = _gla(chunked(q), chunked(k), chunked(v), chunked(la), s0,
                            GLA_SAMPLE_BATCHES if nbs % GLA_SAMPLE_BATCHES == 0 else 1, GLA_CHUNK)
            xs, hist = _even_out(gs, xs, _tm(o[:, :steps]), r, u, _tm(state_pool[i]), mod_s[layer], *out_w, past_len)
            gla_s.append(s_fin.reshape(nbs, GLA_HEADS, GLA_DK, GLA_DV))
            pool_s.append(_tm(hist))
        else:
            w = _odd_weights(od_w_in[i], od_q_norm[i], od_w_uq[i], od_kv_norm[i], od_w_uk[i], od_w_uv[i],
                             od_conv_w[i], od_conv_b[i], od_conv_norm_g[i], od_conv_norm_b[i], od_w_out[i])
            rope_lanes = slice(ROPE_LANE0, ROPE_LANE0 + MLA_ROPE)
            ckv, kr, uc, q, k, v = _odd_in(gp, xp, mod_p[layer], ng[1:2], w, tabs_p, False)
            attn = _attn_prompt(q, k, v, min(ATTN_TILE, seq))
            xp, hist = _odd_out(gp, xp, attn, uc, None, mod_p[layer], w)
            ckv_p.append(ckv)
            kr_p.append(kr[..., rope_lanes])
            conv_p.append(hist[:, CONV_PAD - CONV_HIST:])
            ckv, kr, uc, q, ql = _odd_in(gs, xs, mod_s[layer], ng[1:2], w, tabs_s, True)
            kr = kr[..., rope_lanes]
            qr = q.reshape(steps, nbs, MLA_HEADS, LANES)[..., rope_lanes]
            qr = _tm(qr).reshape(nbs, steps * MLA_HEADS, MLA_ROPE)
            ql = _tm(ql).reshape(nbs, steps * MLA_HEADS, MLA_KV_RANK)
            pad8 = lambda a: jnp.pad(_tm(a), ((0, 0), (0, -steps % 8), (0, 0)))
            cache_kr_t = jnp.swapaxes(cache_krope, 2, 3)
            lat = _attn_sample(page_table, ql, qr, pad8(ckv), pad8(kr), cache_ckv, cache_kr_t, i)
            lat = _tm(lat.reshape(nbs, steps, MLA_HEADS * MLA_KV_RANK))
            xs, hist = _odd_out(gs, xs, lat, uc, _tm(state_conv[i]), mod_s[layer], w)
            ckv_s.append(_tm(ckv))
            kr_s.append(_tm(kr))
            conv_s.append(_tm(hist))
        fin = final_norm[None] if last else None
        xp = _ffn(gf, xp, mod_p[layer], 6, ng[2:3], w1, w3, w2, (layer, 1), fin)
        xs = _ffn(gs, xs, mod_s[layer], 6, ng[2:3], w1, w3, w2, (layer, 1), fin)

    st = jnp.stack
    return (xp, _tm(xs), st(gla_p), st(gla_s), st(pool_p), st(pool_s), st(ckv_p), st(ckv_s),
            st(kr_p), st(kr_s), st(conv_p), st(conv_s))
```

```python
import functools

import jax
import jax.numpy as jnp
import numpy as np
from jax import lax
from jax.experimental import pallas as pl
from jax.experimental.pallas import tpu as pltpu

F32 = jnp.float32
BF16 = jnp.bfloat16

EPS = 1e-6
NEG = -1e30
N_MOD = 9
GLA_HEADS = 4
GLA_DK = 64
GLA_DV = 128
GLA_GATE_RANK = 16
GLA_TAU = 16.0
GLA_CHUNK = 64
POOL_WINDOWS = (2, 4, 8, 16)
POOL_GROUP = 128
POOL_WIDTH = 512
POOL_HIST = 15
MLA_HEADS = 8
MLA_Q_RANK = 384
MLA_KV_RANK = 256
MLA_NOPE = 64
MLA_ROPE = 32
MLA_V = 64
MLA_SCALE = (MLA_NOPE + MLA_ROPE) ** -0.5
LOG2E = 1.4426950408889634
ROPE_BASE = 10000.0
CONV_WIDTH = 31
CONV_CH = 512
CONV_HIST = 30

LANES = 128
SUBLANES = 8
CONV_ROWS = 32
CONV_BLOCK = 256
EVEN_BLOCK = 256
ROPE_LANE0 = 64
VT_ROWS = 80
POOL_PAD = 16
CONV_PAD = 32
VMEM_LIMIT = 52 * 2 ** 20
TOKEN_TILE = 1024
ATTN_TILE = 1024
ATTN_DEPTH = 3
ATTN_QCHUNK = 512
FFN_TOKEN_TILE = 1024
FF_SUB = 512
GLA_SAMPLE_BATCHES = 4


def _cp(*sem):
    return pltpu.CompilerParams(dimension_semantics=sem, vmem_limit_bytes=VMEM_LIMIT)


def _dot(a, b):
    return jnp.dot(a, b, preferred_element_type=F32)


def _dot_nt(a, b):
    return lax.dot_general(a, b, (((1,), (1,)), ((), ())), preferred_element_type=F32)


def _dot_tn(a, b):
    return lax.dot_general(a, b, (((0,), (0,)), ((), ())), preferred_element_type=F32)


def _silu(x):
    return x * jax.nn.sigmoid(x)


def _rms(x, g):
    return x * lax.rsqrt(jnp.mean(x * x, axis=-1, keepdims=True) + EPS) * g


def _modulated(x_ref, g_ref, sh_ref, sc_ref):
    x = x_ref[...]
    h = _rms(x, g_ref[...]) * (1.0 + sc_ref[...]) + sh_ref[...]
    return h.reshape(x.shape[0] * x.shape[1], x.shape[2]).astype(BF16)


class _Group:
    def __init__(self, prompt, lead, rows, tile):
        self.prompt = prompt
        self.grid = (lead, rows // tile) if prompt else (1, 1)
        self.block = (1, tile) if prompt else (lead, rows)

    def act(self, width):
        return pl.BlockSpec(self.block + (width,), lambda b, t, *_: (b, t, 0))

    def mod(self, k):
        if self.prompt:
            return pl.BlockSpec((None, None, 1, self.dm), lambda b, t, *_: (b, k, 0, 0))
        return pl.BlockSpec((None, self.block[1], self.dm), lambda b, t, *_: (k, 0, 0))

    def pos(self):
        if self.prompt:
            return pl.BlockSpec((1, self.block[1], LANES), lambda b, t, *_: (0, t, 0))
        return pl.BlockSpec((self.block[0], 1, LANES), lambda b, t, *_: (0, 0, 0))

    dm = 1024


def _full(a):
    nd = a.ndim
    return pl.BlockSpec(a.shape, lambda *_: (0,) * nd)


def _sds(shape, dtype=F32):
    return jax.ShapeDtypeStruct(shape, dtype)


def _ada_kernel(c_ref, w_ref, b_ref, o_ref):
    c = c_ref[...]
    o_ref[...] = _dot(_silu(c).astype(BF16), w_ref[...].astype(BF16)) + b_ref[...]


def _ada(c_all, ada_w, ada_b):
    depth, dm, n = ada_w.shape
    m = c_all.shape[0]
    tn = dm
    return pl.pallas_call(
        _ada_kernel,
        grid=(depth, n // tn),
        in_specs=[pl.BlockSpec((m, dm), lambda l, j: (0, 0)),
                  pl.BlockSpec((None, dm, tn), lambda l, j: (l, 0, j)),
                  pl.BlockSpec((None, 1, tn), lambda l, j: (l, 0, j))],
        out_specs=pl.BlockSpec((None, m, tn), lambda l, j: (l, 0, j)),
        out_shape=_sds((depth, m, n)),
        compiler_params=_cp("parallel", "parallel"),
    )(c_all, ada_w, ada_b.reshape(depth, 1, n))


def _ffn_kernel(*refs, final):
    if final:
        x_ref, sh_ref, sc_ref, gt_ref, g_ref, w1_ref, w3_ref, w2_ref, fn_ref, o_ref = refs
    else:
        x_ref, sh_ref, sc_ref, gt_ref, g_ref, w1_ref, w3_ref, w2_ref, o_ref = refs
    h = _modulated(x_ref, g_ref, sh_ref, sc_ref)
    dff = w1_ref.shape[1]
    y = None
    for c0 in range(0, dff, FF_SUB):
        cols = slice(c0, min(c0 + FF_SUB, dff))
        a = _dot(h, w1_ref[:, cols])
        b = _dot(h, w3_ref[:, cols])
        part = _dot((_silu(a) * b).astype(BF16), w2_ref[cols, :])
        y = part if y is None else y + part
    x = x_ref[...]
    xn = x + 0.5 * gt_ref[...] * y.reshape(x.shape)
    o_ref[...] = _rms(xn, fn_ref[...]) if final else xn


def _resident(a, lead):
    idx = tuple(lead) + (0, 0)
    return pl.BlockSpec((None,) * len(lead) + a.shape[-2:], lambda *_: idx, pipeline_mode=pl.Buffered(1))


def _ffn(grp, x, mod, k0, g, w1, w3, w2, which, final_g=None):
    dm = x.shape[-1]
    final = final_g is not None
    ins = [x, mod, mod, mod, g, w1, w3, w2] + ([final_g] if final else [])
    specs = [grp.act(dm), grp.mod(k0), grp.mod(k0 + 1), grp.mod(k0 + 2), _full(g),
             _resident(w1, which), _resident(w3, which), _resident(w2, which)] + ([_full(final_g)] if final else [])
    return pl.pallas_call(
        functools.partial(_ffn_kernel, final=final),
        grid=grp.grid,
        in_specs=specs,
        out_specs=grp.act(dm),
        out_shape=_sds(x.shape),
        compiler_params=_cp("parallel", "parallel"),
    )(*ins)


def _log_sigmoid(x):
    return jnp.minimum(x, 0.0) - jnp.log(1.0 + jnp.exp(-jnp.abs(x)))


def _even_in_kernel(x_ref, sh_ref, sc_ref, g_ref, wq_ref, wg_ref, w2_ref, gb_ref, wu_ref,
                    q_ref, k_ref, v_ref, r_ref, la_ref, u_ref):
    h = _modulated(x_ref, g_ref, sh_ref, sc_ref)
    lead = x_ref.shape[:2]
    hk = GLA_HEADS * GLA_DK
    hv = GLA_HEADS * GLA_DV
    z = _dot(h, wq_ref[...])
    q_ref[...] = (z[:, :hk] * GLA_DK ** -0.5).reshape(lead + (hk,))
    k_ref[...] = z[:, hk:2 * hk].reshape(lead + (hk,))
    v_ref[...] = z[:, 2 * hk:2 * hk + hv].reshape(lead + (hv,))
    r_ref[...] = z[:, 2 * hk + hv:].reshape(lead + (hv,))
    g_low = _dot(h, wg_ref[...]).astype(BF16)
    gate = _dot(g_low, w2_ref[...]) + gb_ref[...]
    la_ref[...] = (_log_sigmoid(gate) / GLA_TAU).reshape(lead + (hk,))
    u_ref[...] = _dot(h, wu_ref[...]).reshape(lead + (POOL_WIDTH,))


def _even_in(grp, x, mod, g, wq, wg, w2, gb, wu):
    hk = GLA_HEADS * GLA_DK
    hv = GLA_HEADS * GLA_DV
    widths = (hk, hk, hv, hv, hk, POOL_WIDTH)
    return pl.pallas_call(
        _even_in_kernel,
        grid=grp.grid,
        in_specs=[grp.act(x.shape[-1]), grp.mod(3), grp.mod(4), _full(g), _full(wq), _full(wg),
                  _full(w2), _full(gb), _full(wu)],
        out_specs=[grp.act(w) for w in widths],
        out_shape=[_sds(x.shape[:2] + (w,)) for w in widths],
        compiler_params=_cp("parallel", "parallel"),
    )(x, mod, mod, g, wq, wg, w2, gb, wu)


def _cumsum_rows(tril, x):
    hi = x.astype(BF16)
    r1 = x - hi.astype(F32)
    mid = r1.astype(BF16)
    lo = (r1 - mid.astype(F32)).astype(BF16)
    return _dot(tril, hi) + _dot(tril, mid) + _dot(tril, lo)


def _gla_kernel(*refs, nbb, tt, has_s0):
    if has_s0:
        q_ref, k_ref, v_ref, la_ref, s0_ref, o_ref, so_ref, s_scr = refs
    else:
        q_ref, k_ref, v_ref, la_ref, o_ref, so_ref, s_scr = refs
    t = pl.program_id(1)
    c_rows = GLA_CHUNK

    @pl.when(t == 0)
    def _():
        s_scr[...] = s0_ref[...] if has_s0 else jnp.zeros_like(s_scr)

    row = lax.broadcasted_iota(jnp.int32, (c_rows, c_rows), 0)
    col = lax.broadcasted_iota(jnp.int32, (c_rows, c_rows), 1)
    causal = col <= row
    tril = jnp.where(causal, 1.0, 0.0).astype(BF16)
    lane = lax.broadcasted_iota(jnp.int32, (c_rows, LANES), 1)
    srow = lax.broadcasted_iota(jnp.int32, (LANES, LANES), 0)

    pairs = GLA_HEADS // 2
    combos = [(n, p) for n in range(nbb) for p in range(pairs)]
    heads = [(n, p, hh) for n, p in combos for hh in range(2)]

    def chunk(c, carry):
        rows = pl.ds(pl.multiple_of(c * c_rows, c_rows), c_rows)
        wide = lambda ref: jnp.concatenate([ref[n, rows, :] for n in range(nbb)], axis=-1)
        slab = lambda x, n, p: x[:, LANES * (n * pairs + p):LANES * (n * pairs + p + 1)]
        vcols = lambda p, hh: slice(GLA_DV * (2 * p + hh), GLA_DV * (2 * p + hh + 1))
        b = _cumsum_rows(tril, wide(la_ref))
        b_last = b[c_rows - 1:c_rows, :]
        k = wide(k_ref)
        qi = wide(q_ref) * jnp.exp(b)
        ki = (k * jnp.exp(-b)).astype(BF16)
        kd = (k * jnp.exp(b_last - b)).astype(BF16)
        grow = jnp.exp(b_last)
        qh = {(n, p, hh): jnp.where((lane < GLA_DK) if hh == 0 else (lane >= GLA_DK), slab(qi, n, p), 0.0).astype(BF16)
              for n, p, hh in heads}
        att = {(n, p, hh): _dot_nt(qh[n, p, hh], slab(ki, n, p)) for n, p, hh in heads}
        vh = {(n, p, hh): v_ref[n, rows, vcols(p, hh)].astype(BF16) for n, p, hh in heads}
        s_old = {(n, p): s_scr[n, p] for n, p in combos}
        from_state = {(n, p, hh): _dot(qh[n, p, hh], s_old[n, p].astype(BF16)) for n, p, hh in heads}
        upd = {(n, p, hh): _dot_tn(slab(kd, n, p), vh[n, p, hh]) for n, p, hh in heads}
        for n, p, hh in heads:
            within = _dot(jnp.where(causal, att[n, p, hh], 0.0).astype(BF16), vh[n, p, hh])
            o_ref[n, rows, vcols(p, hh)] = from_state[n, p, hh] + within
        for n, p in combos:
            decay = jnp.transpose(jnp.broadcast_to(slab(grow, n, p), (LANES, LANES)))
            s_scr[n, p] = decay * s_old[n, p] + jnp.where(srow < GLA_DK, upd[n, p, 0], upd[n, p, 1])
        return carry

    lax.fori_loop(0, tt // c_rows, chunk, 0, unroll=min(2, tt // c_rows))

    @pl.when(t == pl.num_programs(1) - 1)
    def _():
        so_ref[...] = s_scr[...]


def _gla(q, k, v, la, s0, nbb, tt):
    nb, nt, hk = q.shape
    hv = v.shape[-1]
    has_s0 = s0 is not None
    act = lambda w: pl.BlockSpec((nbb, tt, w), lambda b, t: (b, t, 0))
    st = pl.BlockSpec((nbb, GLA_HEADS // 2, LANES, GLA_DV), lambda b, t: (b, 0, 0, 0))
    return pl.pallas_call(
        functools.partial(_gla_kernel, nbb=nbb, tt=tt, has_s0=has_s0),
        grid=(nb // nbb, nt // tt),
        in_specs=[act(hk), act(hk), act(hv), act(hk)] + ([st] if has_s0 else []),
        out_specs=[act(hv), st],
        out_shape=[_sds((nb, nt, hv)), _sds((nb, GLA_HEADS // 2, LANES, GLA_DV))],
        scratch_shapes=[pltpu.VMEM((nbb, GLA_HEADS // 2, LANES, GLA_DV), F32)],
        compiler_params=_cp("parallel", "arbitrary"),
    )(*([q, k, v, la] + ([s0] if has_s0 else [])))


def _gla_gate(o, r, on):
    outs = []
    for h in range(GLA_HEADS):
        sl = slice(GLA_DV * h, GLA_DV * (h + 1))
        outs.append(_rms(o[:, sl], on[:, sl]))
    return jnp.concatenate(outs, axis=-1) * _silu(r)


def _even_mix(og, pooled, pw_ref, ps_ref, wo_ref):
    mixed = [_dot(pooled[g].astype(BF16), pw_ref[g]) for g in range(len(POOL_WINDOWS))]
    mixed = jnp.concatenate(mixed, axis=-1) * ps_ref[...]
    n_o = og.shape[-1]
    return _dot(og.astype(BF16), wo_ref[:n_o, :]) + _dot(mixed.astype(BF16), wo_ref[n_o:, :])


def _even_out_prompt_kernel(x_ref, o_ref, r_ref, u_ref, gt_ref, on_ref, pw_ref, ps_ref, wo_ref,
                            xo_ref, ho_ref, hb, *, tm, p0):
    t = pl.program_id(1)

    @pl.when(t == 0)
    def _():
        hb[0:POOL_PAD, :] = jnp.zeros((POOL_PAD, POOL_WIDTH), F32)

    hb[POOL_PAD:POOL_PAD + tm, :] = u_ref[0]
    block = min(EVEN_BLOCK, tm)
    for q0 in range(0, tm, block):
        rows = slice(q0, q0 + block)
        pos = t * tm + q0 + lax.broadcasted_iota(jnp.int32, (block, 1), 0)
        pooled = []
        for g, w in enumerate(POOL_WINDOWS):
            sl = slice(POOL_GROUP * g, POOL_GROUP * (g + 1))
            first = POOL_PAD + q0 - (w - 1)
            win = hb[first:first + block, sl]
            for j in range(w - 2, -1, -1):
                win = win + hb[POOL_PAD + q0 - j:POOL_PAD + q0 - j + block, sl]
            cnt = jnp.minimum(p0 + pos + 1, w).astype(F32)
            pooled.append(win / cnt - u_ref[0, rows, sl])
        og = _gla_gate(o_ref[0, rows, :], r_ref[0, rows, :], on_ref[...])
        y = _even_mix(og, pooled, pw_ref, ps_ref, wo_ref)
        xo_ref[0, rows, :] = x_ref[0, rows, :] + gt_ref[...] * y
    hb[0:POOL_PAD, :] = hb[tm:tm + POOL_PAD, :]

    @pl.when(t == pl.num_programs(1) - 1)
    def _():
        ho_ref[0] = hb[0:POOL_PAD, :]


def _even_out_sample_kernel(x_ref, o_ref, r_ref, u_ref, hi_ref, gt_ref, on_ref, pw_ref, ps_ref, wo_ref,
                            xo_ref, hn_ref, *, p0):
    steps, nb = u_ref.shape[:2]

    def slab(i, sl):
        return hi_ref[i, :, sl] if i < POOL_HIST else u_ref[i - POOL_HIST, :, sl]

    pooled = []
    for g, w in enumerate(POOL_WINDOWS):
        sl = slice(POOL_GROUP * g, POOL_GROUP * (g + 1))
        rows = []
        for t in range(steps):
            win = slab(POOL_HIST + t - (w - 1), sl)
            for j in range(w - 2, -1, -1):
                win = win + slab(POOL_HIST + t - j, sl)
            rows.append(win / float(min(p0 + t + 1, w)) - u_ref[t, :, sl])
        pooled.append(jnp.concatenate(rows, axis=0))
    flat = lambda ref: ref[...].reshape(steps * nb, ref.shape[-1])
    og = _gla_gate(flat(o_ref), flat(r_ref), on_ref[...])
    x = x_ref[...]
    xo_ref[...] = x + gt_ref[...] * _even_mix(og, pooled, pw_ref, ps_ref, wo_ref).reshape(x.shape)
    for i in range(POOL_HIST):
        hn_ref[i] = slab(steps + i, slice(None))


def _even_out(grp, x, o, r, u, hist, mod, on, pw, ps, wo, p0):
    dm = x.shape[-1]
    common = [mod, on, pw, ps, wo]
    common_specs = [grp.mod(5), _full(on), _full(pw), _full(ps), _full(wo)]
    acts = [grp.act(dm), grp.act(o.shape[-1]), grp.act(r.shape[-1]), grp.act(u.shape[-1])]
    if grp.prompt:
        tm = grp.block[1]
        return pl.pallas_call(
            functools.partial(_even_out_prompt_kernel, tm=tm, p0=p0),
            grid=grp.grid,
            in_specs=acts + common_specs,
            out_specs=[grp.act(dm), pl.BlockSpec((1, POOL_PAD, POOL_WIDTH), lambda b, t: (b, 0, 0))],
            out_shape=[_sds(x.shape), _sds((x.shape[0], POOL_PAD, POOL_WIDTH))],
            scratch_shapes=[pltpu.VMEM((POOL_PAD + tm, POOL_WIDTH), F32)],
            compiler_params=_cp("parallel", "arbitrary"),
        )(x, o, r, u, *common)
    return pl.pallas_call(
        functools.partial(_even_out_sample_kernel, p0=p0),
        grid=grp.grid,
        in_specs=acts + [_full(hist)] + common_specs,
        out_specs=[grp.act(dm), _full(hist)],
        out_shape=[_sds(x.shape), _sds(hist.shape)],
        compiler_params=_cp("parallel", "arbitrary"),
    )(x, o, r, u, hist, *common)


def _rope_slab(x, cs, s1, s2, lead):
    shp = lead + (LANES,)
    back = pltpu.roll(x, LANES - MLA_ROPE // 2, 1).reshape(shp)
    fwd = pltpu.roll(x, MLA_ROPE // 2, 1).reshape(shp)
    out = x.reshape(shp) * cs + back * s1 + fwd * s2
    return out.reshape(x.shape)


def _odd_in_kernel(*refs, sample):
    (x_ref, sh_ref, sc_ref, g_ref, wcq_ref, wckv_ref, wkr_ref, wga_ref, wgg_ref, qn_ref, wuq_ref,
     kvn_ref, cs_ref, s1_ref, s2_ref) = refs[:15]
    if sample:
        wkl_ref, ckv_ref, kr_ref, uc_ref, q_ref, ql_ref = refs[15:]
    else:
        wuk_ref, wuvt_ref, ckv_ref, kr_ref, uc_ref, q_ref, k_ref, vt_ref = refs[15:]
    lead = x_ref.shape[:2]
    h = _modulated(x_ref, g_ref, sh_ref, sc_ref)
    cs, s1, s2 = cs_ref[...], s1_ref[...], s2_ref[...]

    cq = _rms(_dot(h, wcq_ref[...]), qn_ref[...]).astype(BF16)
    q = _dot(cq, wuq_ref[...]) * (MLA_SCALE * LOG2E)
    q = jnp.concatenate(
        [_rope_slab(q[:, LANES * i:LANES * (i + 1)], cs, s1, s2, lead) for i in range(MLA_HEADS)], axis=-1)
    q_ref[...] = q.astype(BF16).reshape(lead + (MLA_HEADS * LANES,))

    ckv = _rms(_dot(h, wckv_ref[...]), kvn_ref[...])
    ckv_ref[...] = ckv.reshape(lead + (MLA_KV_RANK,))
    kr = _rope_slab(_dot(h, wkr_ref[...]), cs, s1, s2, lead)
    kr_ref[...] = kr.reshape(lead + (LANES,))
    uc_ref[...] = (_dot(h, wga_ref[...]) * jax.nn.sigmoid(_dot(h, wgg_ref[...]))).reshape(lead + (CONV_CH,))

    if sample:
        qb = q.astype(BF16)
        for i in range(MLA_HEADS):
            ql = _dot(qb[:, LANES * i:LANES * (i + 1)], wkl_ref[i])
            ql_ref[:, :, MLA_KV_RANK * i:MLA_KV_RANK * (i + 1)] = ql.astype(BF16).reshape(lead + (MLA_KV_RANK,))
    else:
        cb = ckv.astype(BF16)
        kn = _dot(cb, wuk_ref[...])
        kn = jnp.concatenate([kn[:, LANES * i:LANES * (i + 1)] + kr for i in range(MLA_HEADS)], axis=-1)
        k_ref[...] = kn.astype(BF16).reshape(lead + (MLA_HEADS * LANES,))
        one_row = lax.broadcasted_iota(jnp.int32, (MLA_HEADS, VT_ROWS, 1), 1) == MLA_V
        ones = jnp.where(one_row, 1.0, 0.0).reshape(MLA_HEADS * VT_ROWS, 1)
        vt_ref[0] = (_dot_nt(wuvt_ref[...], cb) + ones).astype(BF16)


def _odd_in(grp, x, mod, g, w, tabs, sample):
    lead = x.shape[:2]
    hl = MLA_HEADS * LANES
    ins = [x, mod, mod, g, w['wcq'], w['wckv'], w['wkr'], w['wga'], w['wgg'], w['qn'], w['wuq'], w['kvn'], *tabs]
    specs = [grp.act(x.shape[-1]), grp.mod(3), grp.mod(4)] + [_full(a) for a in ins[3:12]] + [grp.pos()] * 3
    outs = [(MLA_KV_RANK, F32), (LANES, F32), (CONV_CH, F32), (hl, BF16)]
    if sample:
        ins.append(w['wkl'])
        outs.append((MLA_HEADS * MLA_KV_RANK, BF16))
    else:
        ins += [w['wuk'], w['wuvt']]
        outs.append((hl, BF16))
    specs += [_full(a) for a in ins[15:]]
    out_specs = [grp.act(wd) for wd, _ in outs]
    out_shape = [_sds(lead + (wd,), dt) for wd, dt in outs]
    if not sample:
        vt_rows = MLA_HEADS * VT_ROWS
        out_specs.append(pl.BlockSpec((1, vt_rows, grp.block[1]), lambda b, t: (b, 0, t)))
        out_shape.append(_sds((lead[0], vt_rows, lead[1]), BF16))
    return pl.pallas_call(
        functools.partial(_odd_in_kernel, sample=sample),
        grid=grp.grid,
        in_specs=specs,
        out_specs=out_specs,
        out_shape=out_shape,
        compiler_params=_cp("parallel", "parallel"),
    )(*ins)


def _attn_prompt_kernel(qi_ref, ki_ref, q_ref, k_ref, vt_ref, o_ref, m_scr, acc_scr, s_scr, p_scr, *, tq):
    step = pl.program_id(1)
    qi = qi_ref[step]
    ki = ki_ref[step]

    @pl.when(ki == 0)
    def _():
        m_scr[...] = jnp.full_like(m_scr, NEG)
        acc_scr[...] = jnp.zeros_like(acc_scr)

    def update(diagonal):
        width = s_scr.shape[2]
        items = [(h, c) for h in range(MLA_HEADS) for c in range(0, tq, width)]

        def keys_used(c):
            return c + width if diagonal else tq

        def scores(item, buf):
            h, c = item
            nk = keys_used(c)
            sl = slice(LANES * h, LANES * (h + 1))
            st = _dot_nt(k_ref[0, :nk, sl], q_ref[0, c:c + width, sl])
            if diagonal:
                key = lax.broadcasted_iota(jnp.int32, (nk, width), 0)
                qry = lax.broadcasted_iota(jnp.int32, (nk, width), 1) + c
                st = jnp.where(key <= qry, st, NEG)
            s_scr[buf, :nk] = st

        def absorb(item, buf):
            h, c = item
            nk = keys_used(c)
            vr = slice(VT_ROWS * h, VT_ROWS * (h + 1))
            qc = slice(c, c + width)
            m_prev = m_scr[h:h + 1, qc]
            m_new = jnp.maximum(m_prev, jnp.max(s_scr[buf, :nk], axis=0, keepdims=True))
            p_scr[buf, :nk] = jnp.exp2(s_scr[buf, :nk] - m_new).astype(BF16)
            acc_scr[vr, qc] = (jnp.exp2(m_prev - m_new) * acc_scr[vr, qc]
                               + _dot(vt_ref[0, vr, :nk], p_scr[buf, :nk]))
            m_scr[h:h + 1, qc] = m_new

        depth = s_scr.shape[0]
        for i in range(depth - 1):
            scores(items[i], i)
        for i, item in enumerate(items):
            if i + depth - 1 < len(items):
                scores(items[i + depth - 1], (i + depth - 1) % depth)
            absorb(item, i % depth)

    @pl.when(ki < qi)
    def _():
        update(False)

    @pl.when(ki == qi)
    def _():
        update(True)
        for h in range(0, MLA_HEADS, 2):
            pair = []
            for hh in (h, h + 1):
                acc = acc_scr[VT_ROWS * hh:VT_ROWS * (hh + 1), :]
                pair.append(acc[:MLA_V] / acc[MLA_V:MLA_V + 1])
            o_ref[0, :, MLA_V * h:MLA_V * (h + 2)] = jnp.transpose(jnp.concatenate(pair, axis=0)).astype(BF16)


def _attn_prompt(q, k, vt, tq):
    nb, nt, hl = q.shape
    vt_rows = vt.shape[1]
    n_out = MLA_HEADS * MLA_V
    pairs = [(i, j) for i in range(nt // tq) for j in range(i + 1)]
    qi_tab = jnp.asarray([p[0] for p in pairs], jnp.int32)
    ki_tab = jnp.asarray([p[1] for p in pairs], jnp.int32)
    return pl.pallas_call(
        functools.partial(_attn_prompt_kernel, tq=tq),
        grid_spec=pltpu.PrefetchScalarGridSpec(
            num_scalar_prefetch=2,
            grid=(nb, len(pairs)),
            in_specs=[pl.BlockSpec((1, tq, hl), lambda b, s, qt, kt: (b, qt[s], 0)),
                      pl.BlockSpec((1, tq, hl), lambda b, s, qt, kt: (b, kt[s], 0)),
                      pl.BlockSpec((1, vt_rows, tq), lambda b, s, qt, kt: (b, 0, kt[s]))],
            out_specs=pl.BlockSpec((1, tq, n_out), lambda b, s, qt, kt: (b, qt[s], 0)),
            scratch_shapes=[pltpu.VMEM((MLA_HEADS, tq), F32), pltpu.VMEM((vt_rows, tq), F32),
                            pltpu.VMEM((ATTN_DEPTH, tq, min(ATTN_QCHUNK, tq)), F32),
                            pltpu.VMEM((ATTN_DEPTH, tq, min(ATTN_QCHUNK, tq)), BF16)]),
        out_shape=_sds((nb, nt, n_out), BF16),
        compiler_params=_cp("parallel", "arbitrary"),
    )(qi_tab, ki_tab, q, k, vt)


def _attn_sample_kernel(pt_ref, ql_ref, qr_ref, cn_ref, kn_ref, ckv_hbm, kr_hbm, o_ref,
                        ckv_buf, kr_buf, pg_scr, s_scr, p_scr, sem, *, npg, layer):
    b = pl.program_id(0)
    nb = pl.num_programs(0)
    slot = lax.rem(b, 2)
    ql = ql_ref[0]
    qr = qr_ref[0]
    rows = ql.shape[0]

    def page_copies(page, i, sl):
        return (pltpu.make_async_copy(ckv_hbm.at[layer, page], ckv_buf.at[sl, i], sem.at[0, sl]),
                pltpu.make_async_copy(kr_hbm.at[layer, page], kr_buf.at[sl, i], sem.at[1, sl]))

    def start_page(batch, i, sl):
        for cp in page_copies(pt_ref[batch, i], i, sl):
            cp.start()

    def start_first(i, carry):
        start_page(0, i, 0)
        return carry

    @pl.when(b == 0)
    def _():
        lax.fori_loop(0, npg, start_first, 0)

    for i in range(npg):
        for cp in page_copies(0, i, slot):
            cp.wait()

    def score_pages(prefetch):
        for i in range(npg):
            if prefetch:
                for k in range(2 * i, min(2 * i + 2, npg)):
                    start_page(b + 1, k, 1 - slot)
            page = ckv_buf[slot, i].astype(BF16)
            pg_scr[i] = page
            s_scr[i] = _dot_nt(ql, page) + _dot(qr, kr_buf[slot, i].astype(BF16))

    @pl.when(b + 1 < nb)
    def _():
        score_pages(True)

    @pl.when(b + 1 == nb)
    def _():
        score_pages(False)

    n_new = cn_ref.shape[1]
    pad = jnp.zeros((LANES - n_new, MLA_KV_RANK), F32)
    cn = jnp.concatenate([cn_ref[0], pad], axis=0).astype(BF16)
    kn = jnp.concatenate([kn_ref[0], pad[:, :MLA_ROPE]], axis=0).astype(BF16)
    step = lax.shift_right_logical(lax.broadcasted_iota(jnp.int32, (rows, LANES), 0), MLA_HEADS.bit_length() - 1)
    col = lax.broadcasted_iota(jnp.int32, (rows, LANES), 1)
    s_new = jnp.where(col <= step, _dot_nt(ql, cn) + _dot_nt(qr, kn), NEG)

    s = s_scr[...]
    m = jnp.maximum(jnp.max(jnp.max(s, axis=0), axis=-1, keepdims=True),
                    jnp.max(s_new, axis=-1, keepdims=True))
    p = jnp.exp2(s - m)
    p_new = jnp.exp2(s_new - m)
    denom = jnp.sum(jnp.sum(p, axis=0), axis=-1, keepdims=True) + jnp.sum(p_new, axis=-1, keepdims=True)
    p_scr[...] = p.astype(BF16)

    def weigh_page(i, acc):
        return acc + _dot(p_scr[i], pg_scr[i])

    acc = lax.fori_loop(0, npg, weigh_page, _dot(p_new.astype(BF16), cn), unroll=True)
    o_ref[0] = acc / denom


def _attn_sample(page_table, ql, qr, cn, kn, cache_ckv, cache_kr, layer):
    nb, rows, rank = ql.shape
    npg = page_table.shape[1]
    page, rope = cache_ckv.shape[2], cache_kr.shape[2]
    assert page == LANES, "one cache page must fill one lane tile of scores"
    per_b = lambda a: pl.BlockSpec((1,) + a.shape[1:], lambda b, pt: (b, 0, 0))
    hbm = pl.BlockSpec(memory_space=pl.ANY)
    return pl.pallas_call(
        functools.partial(_attn_sample_kernel, npg=npg, layer=layer),
        grid_spec=pltpu.PrefetchScalarGridSpec(
            num_scalar_prefetch=1,
            grid=(nb,),
            in_specs=[per_b(ql), per_b(qr), per_b(cn), per_b(kn), hbm, hbm],
            out_specs=pl.BlockSpec((1, rows, rank), lambda b, pt: (b, 0, 0)),
            scratch_shapes=[pltpu.VMEM((2, npg, page, rank), F32),
                            pltpu.VMEM((2, npg, rope, page), F32),
                            pltpu.VMEM((npg, page, rank), BF16),
                            pltpu.VMEM((npg, rows, page), F32),
                            pltpu.VMEM((npg, rows, page), BF16),
                            pltpu.SemaphoreType.DMA((2, 2))]),
        out_shape=_sds((nb, rows, rank)),
        compiler_params=_cp("arbitrary"),
    )(page_table, ql, qr, cn, kn, cache_ckv, cache_kr)


def _conv_norm_act(cv, cb_ref, lg_ref, lb_ref):
    cv = cv + cb_ref[...]
    mu = jnp.mean(cv, axis=-1, keepdims=True)
    d = cv - mu
    y = d * lax.rsqrt(jnp.mean(d * d, axis=-1, keepdims=True) + EPS)
    return _silu(y * lg_ref[...] + lb_ref[...])


def _odd_out_prompt_kernel(x_ref, a_ref, uc_ref, gt_ref, cw_ref, cb_ref, lg_ref, lb_ref, woa_ref, woc_ref,
                           xo_ref, ho_ref, hb, hs, cvb, *, tm):
    t = pl.program_id(1)

    @pl.when(t == 0)
    def _():
        hb[0:CONV_PAD, :] = jnp.zeros((CONV_PAD, CONV_CH), F32)

    hb[CONV_PAD:CONV_PAD + tm, :] = uc_ref[0]
    for sft in range(1, SUBLANES):
        hs[sft - 1] = hb[sft:sft + hs.shape[1], :]
    base = CONV_PAD - CONV_HIST

    block = min(CONV_BLOCK, tm)
    for q0 in range(0, tm, block):
        ya = _dot(a_ref[0, q0:q0 + block, :], woa_ref[...])
        for r0 in range(q0, q0 + block, CONV_ROWS):
            acc = None
            for j in range(CONV_WIDTH):
                whole, sft = divmod(base + j, SUBLANES)
                rows = slice(r0 + whole * SUBLANES, r0 + whole * SUBLANES + CONV_ROWS)
                term = cw_ref[j:j + 1, :] * (hb[rows, :] if sft == 0 else hs[sft - 1, rows, :])
                acc = term if acc is None else acc + term
            cvb[r0:r0 + CONV_ROWS, :] = acc
        cv = _conv_norm_act(cvb[q0:q0 + block, :], cb_ref, lg_ref, lb_ref)
        y = ya + _dot(cv.astype(BF16), woc_ref[...])
        xo_ref[0, q0:q0 + block, :] = x_ref[0, q0:q0 + block, :] + gt_ref[...] * y
    hb[0:CONV_PAD, :] = hb[tm:tm + CONV_PAD, :]

    @pl.when(t == pl.num_programs(1) - 1)
    def _():
        ho_ref[0] = hb[0:CONV_PAD, :]


def _odd_out_sample_kernel(x_ref, lat_ref, uc_ref, hi_ref, gt_ref, cw_ref, cb_ref, lg_ref, lb_ref, wuv_ref,
                           woa_ref, woc_ref, xo_ref, hn_ref):
    steps, nb = uc_ref.shape[:2]

    def slab(i):
        return hi_ref[i] if i < CONV_HIST else uc_ref[i - CONV_HIST]

    rows = []
    for t in range(steps):
        cv = cw_ref[0:1, :] * slab(t)
        for j in range(1, CONV_WIDTH):
            cv = cv + cw_ref[j:j + 1, :] * slab(t + j)
        rows.append(cv)
    cv = _conv_norm_act(jnp.concatenate(rows, axis=0), cb_ref, lg_ref, lb_ref)
    lat = lat_ref[...].reshape(steps * nb, lat_ref.shape[-1]).astype(BF16)
    attn = _dot(lat, wuv_ref[...]).astype(BF16)
    y = _dot(attn, woa_ref[...]) + _dot(cv.astype(BF16), woc_ref[...])
    x = x_ref[...]
    xo_ref[...] = x + gt_ref[...] * y.reshape(x.shape)
    for i in range(CONV_HIST):
        hn_ref[i] = slab(steps + i)


def _odd_out(grp, x, a, uc, hist, mod, w):
    dm = x.shape[-1]
    conv = [w['cw'], w['cb'], w['lg'], w['lb']]
    if grp.prompt:
        tm = grp.block[1]
        ins = [x, a, uc, mod] + conv + [w['woa'], w['woc']]
        return pl.pallas_call(
            functools.partial(_odd_out_prompt_kernel, tm=tm),
            grid=grp.grid,
            in_specs=[grp.act(dm), grp.act(a.shape[-1]), grp.act(CONV_CH), grp.mod(5)] + [_full(v) for v in ins[4:]],
            out_specs=[grp.act(dm), pl.BlockSpec((1, CONV_PAD, CONV_CH), lambda b, t: (b, 0, 0))],
            out_shape=[_sds(x.shape), _sds((x.shape[0], CONV_PAD, CONV_CH))],
            scratch_shapes=[pltpu.VMEM((CONV_PAD + tm, CONV_CH), F32),
                            pltpu.VMEM((SUBLANES - 1, CONV_PAD + tm - SUBLANES, CONV_CH), F32),
                            pltpu.VMEM((tm, CONV_CH), F32)],
            compiler_params=_cp("parallel", "arbitrary"),
        )(*ins)
    ins = [x, a, uc, hist, mod] + conv + [w['wuv_bd'], w['woa'], w['woc']]
    return pl.pallas_call(
        _odd_out_sample_kernel,
        grid=grp.grid,
        in_specs=[grp.act(dm), grp.act(a.shape[-1]), grp.act(CONV_CH), _full(hist), grp.mod(5)]
        + [_full(v) for v in ins[5:]],
        out_specs=[grp.act(dm), _full(hist)],
        out_shape=[_sds(x.shape), _sds(hist.shape)],
        compiler_params=_cp("parallel", "arbitrary"),
    )(*ins)


def _head_pad(w, heads, width, offset=0):
    kdim = w.shape[0]
    w = w.reshape(kdim, heads, width)
    w = jnp.pad(w, ((0, 0), (0, 0), (offset, LANES - width - offset)))
    return w.reshape(kdim, heads * LANES)


def _rope_tables(pos):
    half = MLA_ROPE // 2
    freqs = ROPE_BASE ** (-jnp.arange(half, dtype=F32) / half)
    ang = pos.astype(F32)[..., None] * freqs
    cos, sin = jnp.cos(ang), jnp.sin(ang)
    zeros = jnp.zeros_like(cos)
    lead = jnp.ones(pos.shape + (ROPE_LANE0,), F32)
    tail = jnp.zeros(pos.shape + (LANES - ROPE_LANE0 - MLA_ROPE,), F32)
    cs = jnp.concatenate([lead, cos, cos, tail], axis=-1)
    s1 = jnp.concatenate([0 * lead, -sin, zeros, tail], axis=-1)
    s2 = jnp.concatenate([0 * lead, zeros, sin, tail], axis=-1)
    return cs, s1, s2


def _even_weights(w_in, gate_w2, gate_b, out_norm, pool_w, pool_scale, w_out):
    hk = GLA_HEADS * GLA_DK
    hv = GLA_HEADS * GLA_DV
    n_main = 2 * hk + 2 * hv
    wg = jnp.pad(w_in[:, n_main:n_main + GLA_GATE_RANK], ((0, 0), (0, LANES - GLA_GATE_RANK)))
    w2 = jnp.pad(gate_w2, ((0, LANES - GLA_GATE_RANK), (0, 0)))
    return dict(wq=w_in[:, :n_main].astype(BF16), wg=wg.astype(BF16), w2=w2.astype(BF16),
                gb=gate_b[None], wu=w_in[:, n_main + GLA_GATE_RANK:].astype(BF16),
                on=out_norm[None], pw=pool_w.astype(BF16), ps=pool_scale[None], wo=w_out.astype(BF16))


def _odd_weights(w_in, q_norm, w_uq, kv_norm, w_uk, w_uv, conv_w, conv_b, ln_g, ln_b, w_out):
    c0, c1, c2 = MLA_Q_RANK, MLA_Q_RANK + MLA_KV_RANK, MLA_Q_RANK + MLA_KV_RANK + MLA_ROPE
    n_attn = MLA_HEADS * MLA_V
    wkr = jnp.pad(w_in[:, c1:c2], ((0, 0), (ROPE_LANE0, LANES - ROPE_LANE0 - MLA_ROPE)))
    wuk = w_uk.reshape(MLA_KV_RANK, MLA_HEADS * MLA_NOPE)
    wkl = jnp.pad(jnp.transpose(w_uk, (1, 2, 0)), ((0, 0), (0, LANES - MLA_NOPE), (0, 0)))
    eye = jnp.eye(MLA_HEADS, dtype=F32)
    wuv_bd = (eye[:, None, :, None] * jnp.transpose(w_uv, (1, 0, 2))[:, :, None, :]).reshape(
        MLA_HEADS * MLA_KV_RANK, n_attn)
    wuvt = jnp.pad(jnp.transpose(w_uv, (1, 2, 0)), ((0, 0), (0, VT_ROWS - MLA_V), (0, 0))).reshape(
        MLA_HEADS * VT_ROWS, MLA_KV_RANK)
    return dict(wcq=w_in[:, :c0].astype(BF16), wckv=w_in[:, c0:c1].astype(BF16), wkr=wkr.astype(BF16),
                wga=w_in[:, c2:c2 + CONV_CH].astype(BF16), wgg=w_in[:, c2 + CONV_CH:].astype(BF16),
                qn=q_norm[None], wuq=_head_pad(w_uq, MLA_HEADS, MLA_NOPE + MLA_ROPE).astype(BF16),
                kvn=kv_norm[None], wuk=_head_pad(wuk, MLA_HEADS, MLA_NOPE).astype(BF16),
                wuvt=wuvt.astype(BF16), wkl=wkl.astype(BF16),
                wuv_bd=wuv_bd.astype(BF16), cw=conv_w, cb=conv_b[None], lg=ln_g[None], lb=ln_b[None],
                woa=w_out[:n_attn].astype(BF16), woc=w_out[n_attn:].astype(BF16))


def _tm(x):
    return jnp.swapaxes(x, 0, 1)


def kernel(x_prompt, x_sample, state_gla, state_pool, cache_ckv, cache_krope, state_conv, page_table, c_prompt, c_sample, ada_w, ada_b, norm_g, ffn_w1, ffn_w3, ffn_w2, ev_w_in, ev_gate_w2, ev_gate_b, ev_out_norm, ev_pool_w, ev_pool_scale, ev_w_out, od_w_in, od_q_norm, od_w_uq, od_kv_norm, od_w_uk, od_w_uv, od_conv_w, od_conv_b, od_conv_norm_g, od_conv_norm_b, od_w_out, final_norm):
    nbp, seq, dm = x_prompt.shape
    nbs, steps, _ = x_sample.shape
    depth = ada_w.shape[0]
    past_len = page_table.shape[1] * cache_ckv.shape[2]
    tile = min(TOKEN_TILE, seq)
    gp = _Group(True, nbp, seq, tile)
    gf = _Group(True, nbp, seq, min(FFN_TOKEN_TILE, seq))
    gs = _Group(False, steps, nbs, nbs)

    n_c = nbp + nbs
    c_all = jnp.pad(jnp.concatenate([c_prompt, c_sample], axis=0), ((0, -n_c % 8), (0, 0)))
    mod = _ada(c_all, ada_w, ada_b)
    mod_p = mod[:, :nbp].reshape(depth, nbp, N_MOD, 1, dm)
    mod_s = jnp.swapaxes(mod[:, nbp:n_c].reshape(depth, nbs, N_MOD, dm), 1, 2)

    w1, w3, w2 = ffn_w1.astype(BF16), ffn_w3.astype(BF16), ffn_w2.astype(BF16)
    tabs_p = _rope_tables(jnp.arange(seq)[None])
    tabs_s = _rope_tables(past_len + jnp.arange(steps)[:, None])

    xp = x_prompt
    xs = _tm(x_sample)
    gla_p, gla_s, pool_p, pool_s, ckv_p, ckv_s, kr_p, kr_s, conv_p, conv_s = ([] for _ in range(10))
    gla_tt = min(TOKEN_TILE, seq)

    for layer in range(depth):
        i = layer // 2
        ng = norm_g[layer]
        last = layer == depth - 1
        xp = _ffn(gf, xp, mod_p[layer], 0, ng[0:1], w1, w3, w2, (layer, 0))
        xs = _ffn(gs, xs, mod_s[layer], 0, ng[0:1], w1, w3, w2, (layer, 0))
        if layer % 2 == 0:
            w = _even_weights(ev_w_in[i], ev_gate_w2[i], ev_gate_b[i], ev_out_norm[i], ev_pool_w[i],
                              ev_pool_scale[i], ev_w_out[i])
            proj = (ng[1:2], w['wq'], w['wg'], w['w2'], w['gb'], w['wu'])
            out_w = (w['on'], w['pw'], w['ps'], w['wo'])
            q, k, v, r, la, u = _even_in(gp, xp, mod_p[layer], *proj)
            o, s_fin = _gla(q, k, v, la, None, nbp, gla_tt)
            xp, hist = _even_out(gp, xp, o, r, u, None, mod_p[layer], *out_w, 0)
            gla_p.append(s_fin.reshape(nbp, GLA_HEADS, GLA_DK, GLA_DV))
            pool_p.append(hist[:, POOL_PAD - POOL_HIST:])
            q, k, v, r, la, u = _even_in(gs, xs, mod_s[layer], *proj)
            chunked = lambda a: jnp.pad(_tm(a), ((0, 0), (0, GLA_CHUNK - steps), (0, 0)))
            s0 = state_gla[i].reshape(nbs, GLA_HEADS // 2, LANES, GLA_DV)
            o, s_fin = _gla(chunked(q), chunked(k), chunked(v), chunked(la), s0,
                            GLA_SAMPLE_BATCHES if nbs % GLA_SAMPLE_BATCHES == 0 else 1, GLA_CHUNK)
            xs, hist = _even_out(gs, xs, _tm(o[:, :steps]), r, u, _tm(state_pool[i]), mod_s[layer], *out_w, past_len)
            gla_s.append(s_fin.reshape(nbs, GLA_HEADS, GLA_DK, GLA_DV))
            pool_s.append(_tm(hist))
        else:
            w = _odd_weights(od_w_in[i], od_q_norm[i], od_w_uq[i], od_kv_norm[i], od_w_uk[i], od_w_uv[i],
                             od_conv_w[i], od_conv_b[i], od_conv_norm_g[i], od_conv_norm_b[i], od_w_out[i])
            rope_lanes = slice(ROPE_LANE0, ROPE_LANE0 + MLA_ROPE)
            ckv, kr, uc, q, k, v = _odd_in(gp, xp, mod_p[layer], ng[1:2], w, tabs_p, False)
            attn = _attn_prompt(q, k, v, min(ATTN_TILE, seq))
            xp, hist = _odd_out(gp, xp, attn, uc, None, mod_p[layer], w)
            ckv_p.append(ckv)
            kr_p.append(kr[..., rope_lanes])
            conv_p.append(hist[:, CONV_PAD - CONV_HIST:])
            ckv, kr, uc, q, ql = _odd_in(gs, xs, mod_s[layer], ng[1:2], w, tabs_s, True)
            kr = kr[..., rope_lanes]
            qr = q.reshape(steps, nbs, MLA_HEADS, LANES)[..., rope_lanes]
            qr = _tm(qr).reshape(nbs, steps * MLA_HEADS, MLA_ROPE)
            ql = _tm(ql).reshape(nbs, steps * MLA_HEADS, MLA_KV_RANK)
            pad8 = lambda a: jnp.pad(_tm(a), ((0, 0), (0, -steps % 8), (0, 0)))
            cache_kr_t = jnp.swapaxes(cache_krope, 2, 3)
            lat = _attn_sample(page_table, ql, qr, pad8(ckv), pad8(kr), cache_ckv, cache_kr_t, i)
            lat = _tm(lat.reshape(nbs, steps, MLA_HEADS * MLA_KV_RANK))
            xs, hist = _odd_out(gs, xs, lat, uc, _tm(state_conv[i]), mod_s[layer], w)
            ckv_s.append(_tm(ckv))
            kr_s.append(_tm(kr))
            conv_s.append(_tm(hist))
        fin = final_norm[None] if last else None
        xp = _ffn(gf, xp, mod_p[layer], 6, ng[2:3], w1, w3, w2, (layer, 1), fin)
        xs = _ffn(gs, xs, mod_s[layer], 6, ng[2:3], w1, w3, w2, (layer, 1), fin)

    st = jnp.stack
    return (xp, _tm(xs), st(gla_p), st(gla_s), st(pool_p), st(pool_s), st(ckv_p), st(ckv_s),
            st(kr_p), st(kr_s), st(conv_p), st(conv_s))
```

```python
import functools

import jax
import jax.numpy as jnp
import numpy as np
from jax import lax
from jax.experimental import pallas as pl
from jax.experimental.pallas import tpu as pltpu

F32 = jnp.float32
BF16 = jnp.bfloat16

EPS = 1e-6
NEG = -1e30
N_MOD = 9
GLA_HEADS = 4
GLA_DK = 64
GLA_DV = 128
GLA_GATE_RANK = 16
GLA_TAU = 16.0
GLA_CHUNK = 64
POOL_WINDOWS = (2, 4, 8, 16)
POOL_GROUP = 128
POOL_WIDTH = 512
POOL_HIST = 15
MLA_HEADS = 8
MLA_Q_RANK = 384
MLA_KV_RANK = 256
MLA_NOPE = 64
MLA_ROPE = 32
MLA_V = 64
MLA_SCALE = (MLA_NOPE + MLA_ROPE) ** -0.5
LOG2E = 1.4426950408889634
ROPE_BASE = 10000.0
CONV_WIDTH = 31
CONV_CH = 512
CONV_HIST = 30

LANES = 128
SUBLANES = 8
CONV_ROWS = 32
CONV_BLOCK = 256
EVEN_BLOCK = 256
ROPE_LANE0 = 64
VT_ROWS = 80
POOL_PAD = 16
CONV_PAD = 32
VMEM_LIMIT = 52 * 2 ** 20
TOKEN_TILE = 1024
ATTN_TILE = 1024
ATTN_DEPTH = 3
ATTN_QCHUNK = 512
FFN_TOKEN_TILE = 1024
FF_SUB = 1024
GLA_SAMPLE_BATCHES = 8


def _cp(*sem):
    return pltpu.CompilerParams(dimension_semantics=sem, vmem_limit_bytes=VMEM_LIMIT)


def _dot(a, b):
    return jnp.dot(a, b, preferred_element_type=F32)


def _dot_nt(a, b):
    return lax.dot_general(a, b, (((1,), (1,)), ((), ())), preferred_element_type=F32)


def _dot_tn(a, b):
    return lax.dot_general(a, b, (((0,), (0,)), ((), ())), preferred_element_type=F32)


def _silu(x):
    return x * jax.nn.sigmoid(x)


def _rms(x, g):
    return x * lax.rsqrt(jnp.mean(x * x, axis=-1, keepdims=True) + EPS) * g


def _modulated(x_ref, g_ref, sh_ref, sc_ref):
    x = x_ref[...]
    h = _rms(x, g_ref[...]) * (1.0 + sc_ref[...]) + sh_ref[...]
    return h.reshape(x.shape[0] * x.shape[1], x.shape[2]).astype(BF16)


class _Group:
    def __init__(self, prompt, lead, rows, tile):
        self.prompt = prompt
        self.grid = (lead, rows // tile) if prompt else (1, 1)
        self.block = (1, tile) if prompt else (lead, rows)

    def act(self, width):
        return pl.BlockSpec(self.block + (width,), lambda b, t, *_: (b, t, 0))

    def mod(self, k):
        if self.prompt:
            return pl.BlockSpec((None, None, 1, self.dm), lambda b, t, *_: (b, k, 0, 0))
        return pl.BlockSpec((None, self.block[1], self.dm), lambda b, t, *_: (k, 0, 0))

    def pos(self):
        if self.prompt:
            return pl.BlockSpec((1, self.block[1], LANES), lambda b, t, *_: (0, t, 0))
        return pl.BlockSpec((self.block[0], 1, LANES), lambda b, t, *_: (0, 0, 0))

    dm = 1024


def _full(a):
    nd = a.ndim
    return pl.BlockSpec(a.shape, lambda *_: (0,) * nd)


def _sds(shape, dtype=F32):
    return jax.ShapeDtypeStruct(shape, dtype)


def _ada_kernel(c_ref, w_ref, b_ref, o_ref):
    c = c_ref[...]
    o_ref[...] = _dot(_silu(c).astype(BF16), w_ref[...].astype(BF16)) + b_ref[...]


def _ada(c_all, ada_w, ada_b):
    depth, dm, n = ada_w.shape
    m = c_all.shape[0]
    tn = dm
    return pl.pallas_call(
        _ada_kernel,
        grid=(depth, n // tn),
        in_specs=[pl.BlockSpec((m, dm), lambda l, j: (0, 0)),
                  pl.BlockSpec((None, dm, tn), lambda l, j: (l, 0, j)),
                  pl.BlockSpec((None, 1, tn), lambda l, j: (l, 0, j))],
        out_specs=pl.BlockSpec((None, m, tn), lambda l, j: (l, 0, j)),
        out_shape=_sds((depth, m, n)),
        compiler_params=_cp("parallel", "parallel"),
    )(c_all, ada_w, ada_b.reshape(depth, 1, n))


def _ffn_kernel(*refs, final):
    if final:
        x_ref, sh_ref, sc_ref, gt_ref, g_ref, w1_ref, w3_ref, w2_ref, fn_ref, o_ref = refs
    else:
        x_ref, sh_ref, sc_ref, gt_ref, g_ref, w1_ref, w3_ref, w2_ref, o_ref = refs
    h = _modulated(x_ref, g_ref, sh_ref, sc_ref)
    dff = w1_ref.shape[1]
    y = None
    for c0 in range(0, dff, FF_SUB):
        cols = slice(c0, min(c0 + FF_SUB, dff))
        a = _dot(h, w1_ref[:, cols])
        b = _dot(h, w3_ref[:, cols])
        part = _dot((_silu(a) * b).astype(BF16), w2_ref[cols, :])
        y = part if y is None else y + part
    x = x_ref[...]
    xn = x + 0.5 * gt_ref[...] * y.reshape(x.shape)
    o_ref[...] = _rms(xn, fn_ref[...]) if final else xn


def _resident(a, lead):
    idx = tuple(lead) + (0, 0)
    return pl.BlockSpec((None,) * len(lead) + a.shape[-2:], lambda *_: idx, pipeline_mode=pl.Buffered(1))


def _ffn(grp, x, mod, k0, g, w1, w3, w2, which, final_g=None):
    dm = x.shape[-1]
    final = final_g is not None
    ins = [x, mod, mod, mod, g, w1, w3, w2] + ([final_g] if final else [])
    specs = [grp.act(dm), grp.mod(k0), grp.mod(k0 + 1), grp.mod(k0 + 2), _full(g),
             _resident(w1, which), _resident(w3, which), _resident(w2, which)] + ([_full(final_g)] if final else [])
    return pl.pallas_call(
        functools.partial(_ffn_kernel, final=final),
        grid=grp.grid,
        in_specs=specs,
        out_specs=grp.act(dm),
        out_shape=_sds(x.shape),
        compiler_params=_cp("parallel", "parallel"),
    )(*ins)


def _log_sigmoid(x):
    return jnp.minimum(x, 0.0) - jnp.log(1.0 + jnp.exp(-jnp.abs(x)))


def _even_in_kernel(x_ref, sh_ref, sc_ref, g_ref, wq_ref, wg_ref, w2_ref, gb_ref, wu_ref,
                    q_ref, k_ref, v_ref, r_ref, la_ref, u_ref):
    h = _modulated(x_ref, g_ref, sh_ref, sc_ref)
    lead = x_ref.shape[:2]
    hk = GLA_HEADS * GLA_DK
    hv = GLA_HEADS * GLA_DV
    z = _dot(h, wq_ref[...])
    q_ref[...] = (z[:, :hk] * GLA_DK ** -0.5).reshape(lead + (hk,))
    k_ref[...] = z[:, hk:2 * hk].reshape(lead + (hk,))
    v_ref[...] = z[:, 2 * hk:2 * hk + hv].reshape(lead + (hv,))
    r_ref[...] = z[:, 2 * hk + hv:].reshape(lead + (hv,))
    g_low = _dot(h, wg_ref[...]).astype(BF16)
    gate = _dot(g_low, w2_ref[...]) + gb_ref[...]
    la_ref[...] = (_log_sigmoid(gate) / GLA_TAU).reshape(lead + (hk,))
    u_ref[...] = _dot(h, wu_ref[...]).reshape(lead + (POOL_WIDTH,))


def _even_in(grp, x, mod, g, wq, wg, w2, gb, wu):
    hk = GLA_HEADS * GLA_DK
    hv = GLA_HEADS * GLA_DV
    widths = (hk, hk, hv, hv, hk, POOL_WIDTH)
    return pl.pallas_call(
        _even_in_kernel,
        grid=grp.grid,
        in_specs=[grp.act(x.shape[-1]), grp.mod(3), grp.mod(4), _full(g), _full(wq), _full(wg),
                  _full(w2), _full(gb), _full(wu)],
        out_specs=[grp.act(w) for w in widths],
        out_shape=[_sds(x.shape[:2] + (w,)) for w in widths],
        compiler_params=_cp("parallel", "parallel"),
    )(x, mod, mod, g, wq, wg, w2, gb, wu)


def _cumsum_rows(tril, x):
    hi = x.astype(BF16)
    r1 = x - hi.astype(F32)
    mid = r1.astype(BF16)
    lo = (r1 - mid.astype(F32)).astype(BF16)
    return _dot(tril, hi) + _dot(tril, mid) + _dot(tril, lo)


def _gla_kernel(*refs, nbb, tt, has_s0):
    if has_s0:
        q_ref, k_ref, v_ref, la_ref, s0_ref, o_ref, so_ref, s_scr = refs
    else:
        q_ref, k_ref, v_ref, la_ref, o_ref, so_ref, s_scr = refs
    t = pl.program_id(1)
    c_rows = GLA_CHUNK

    @pl.when(t == 0)
    def _():
        s_scr[...] = s0_ref[...] if has_s0 else jnp.zeros_like(s_scr)

    row = lax.broadcasted_iota(jnp.int32, (c_rows, c_rows), 0)
    col = lax.broadcasted_iota(jnp.int32, (c_rows, c_rows), 1)
    causal = col <= row
    tril = jnp.where(causal, 1.0, 0.0).astype(BF16)
    lane = lax.broadcasted_iota(jnp.int32, (c_rows, LANES), 1)
    srow = lax.broadcasted_iota(jnp.int32, (LANES, LANES), 0)

    pairs = GLA_HEADS // 2
    combos = [(n, p) for n in range(nbb) for p in range(pairs)]
    heads = [(n, p, hh) for n, p in combos for hh in range(2)]

    def chunk(c, carry):
        short = tt < c_rows
        rows = slice(0, tt) if short else pl.ds(pl.multiple_of(c * c_rows, c_rows), c_rows)

        def take(ref, n, cols):
            x = ref[n, rows, cols]
            return jnp.concatenate([x, jnp.zeros((c_rows - tt, x.shape[1]), F32)], axis=0) if short else x

        wide = lambda ref: jnp.concatenate([take(ref, n, slice(None)) for n in range(nbb)], axis=-1)
        slab = lambda x, n, p: x[:, LANES * (n * pairs + p):LANES * (n * pairs + p + 1)]
        vcols = lambda p, hh: slice(GLA_DV * (2 * p + hh), GLA_DV * (2 * p + hh + 1))
        b = _cumsum_rows(tril, wide(la_ref))
        b_last = b[c_rows - 1:c_rows, :]
        k = wide(k_ref)
        qi = wide(q_ref) * jnp.exp(b)
        ki = (k * jnp.exp(-b)).astype(BF16)
        kd = (k * jnp.exp(b_last - b)).astype(BF16)
        grow = jnp.exp(b_last)
        qh = {(n, p, hh): jnp.where((lane < GLA_DK) if hh == 0 else (lane >= GLA_DK), slab(qi, n, p), 0.0).astype(BF16)
              for n, p, hh in heads}
        att = {(n, p, hh): _dot_nt(qh[n, p, hh], slab(ki, n, p)) for n, p, hh in heads}
        vh = {(n, p, hh): take(v_ref, n, vcols(p, hh)).astype(BF16) for n, p, hh in heads}
        s_old = {(n, p): s_scr[n, p] for n, p in combos}
        from_state = {(n, p, hh): _dot(qh[n, p, hh], s_old[n, p].astype(BF16)) for n, p, hh in heads}
        upd = {(n, p, hh): _dot_tn(slab(kd, n, p), vh[n, p, hh]) for n, p, hh in heads}
        for n, p, hh in heads:
            within = _dot(jnp.where(causal, att[n, p, hh], 0.0).astype(BF16), vh[n, p, hh])
            o_ref[n, rows, vcols(p, hh)] = (from_state[n, p, hh] + within)[:min(tt, c_rows)]
        for n, p in combos:
            decay = jnp.transpose(jnp.broadcast_to(slab(grow, n, p), (LANES, LANES)))
            s_scr[n, p] = decay * s_old[n, p] + jnp.where(srow < GLA_DK, upd[n, p, 0], upd[n, p, 1])
        return carry

    n_chunks = max(1, tt // c_rows)
    lax.fori_loop(0, n_chunks, chunk, 0, unroll=min(2, n_chunks))

    @pl.when(t == pl.num_programs(1) - 1)
    def _():
        so_ref[...] = s_scr[...]


def _gla(q, k, v, la, s0, nbb, tt):
    nb, nt, hk = q.shape
    hv = v.shape[-1]
    has_s0 = s0 is not None
    act = lambda w: pl.BlockSpec((nbb, tt, w), lambda b, t: (b, t, 0))
    st = pl.BlockSpec((nbb, GLA_HEADS // 2, LANES, GLA_DV), lambda b, t: (b, 0, 0, 0))
    return pl.pallas_call(
        functools.partial(_gla_kernel, nbb=nbb, tt=tt, has_s0=has_s0),
        grid=(nb // nbb, nt // tt),
        in_specs=[act(hk), act(hk), act(hv), act(hk)] + ([st] if has_s0 else []),
        out_specs=[act(hv), st],
        out_shape=[_sds((nb, nt, hv)), _sds((nb, GLA_HEADS // 2, LANES, GLA_DV))],
        scratch_shapes=[pltpu.VMEM((nbb, GLA_HEADS // 2, LANES, GLA_DV), F32)],
        compiler_params=_cp("parallel", "arbitrary"),
    )(*([q, k, v, la] + ([s0] if has_s0 else [])))


def _gla_gate(o, r, on):
    outs = []
    for h in range(GLA_HEADS):
        sl = slice(GLA_DV * h, GLA_DV * (h + 1))
        outs.append(_rms(o[:, sl], on[:, sl]))
    return jnp.concatenate(outs, axis=-1) * _silu(r)


def _even_mix(og, pooled, pw_ref, ps_ref, wo_ref):
    mixed = [_dot(pooled[g].astype(BF16), pw_ref[g]) for g in range(len(POOL_WINDOWS))]
    mixed = jnp.concatenate(mixed, axis=-1) * ps_ref[...]
    n_o = og.shape[-1]
    return _dot(og.astype(BF16), wo_ref[:n_o, :]) + _dot(mixed.astype(BF16), wo_ref[n_o:, :])


def _even_out_prompt_kernel(x_ref, o_ref, r_ref, u_ref, gt_ref, on_ref, pw_ref, ps_ref, wo_ref,
                            xo_ref, ho_ref, hb, *, tm, p0):
    t = pl.program_id(1)

    @pl.when(t == 0)
    def _():
        hb[0:POOL_PAD, :] = jnp.zeros((POOL_PAD, POOL_WIDTH), F32)

    hb[POOL_PAD:POOL_PAD + tm, :] = u_ref[0]
    block = min(EVEN_BLOCK, tm)
    for q0 in range(0, tm, block):
        rows = slice(q0, q0 + block)
        pos = t * tm + q0 + lax.broadcasted_iota(jnp.int32, (block, 1), 0)
        pooled = []
        for g, w in enumerate(POOL_WINDOWS):
            sl = slice(POOL_GROUP * g, POOL_GROUP * (g + 1))
            first = POOL_PAD + q0 - (w - 1)
            win = hb[first:first + block, sl]
            for j in range(w - 2, -1, -1):
                win = win + hb[POOL_PAD + q0 - j:POOL_PAD + q0 - j + block, sl]
            cnt = jnp.minimum(p0 + pos + 1, w).astype(F32)
            pooled.append(win / cnt - u_ref[0, rows, sl])
        og = _gla_gate(o_ref[0, rows, :], r_ref[0, rows, :], on_ref[...])
        y = _even_mix(og, pooled, pw_ref, ps_ref, wo_ref)
        xo_ref[0, rows, :] = x_ref[0, rows, :] + gt_ref[...] * y
    hb[0:POOL_PAD, :] = hb[tm:tm + POOL_PAD, :]

    @pl.when(t == pl.num_programs(1) - 1)
    def _():
        ho_ref[0] = hb[0:POOL_PAD, :]


def _even_out_sample_kernel(x_ref, o_ref, r_ref, u_ref, hi_ref, gt_ref, on_ref, pw_ref, ps_ref, wo_ref,
                            xo_ref, hn_ref, *, p0):
    steps, nb = u_ref.shape[:2]

    def slab(i, sl):
        return hi_ref[i, :, sl] if i < POOL_HIST else u_ref[i - POOL_HIST, :, sl]

    pooled = []
    for g, w in enumerate(POOL_WINDOWS):
        sl = slice(POOL_GROUP * g, POOL_GROUP * (g + 1))
        rows = []
        for t in range(steps):
            win = slab(POOL_HIST + t - (w - 1), sl)
            for j in range(w - 2, -1, -1):
                win = win + slab(POOL_HIST + t - j, sl)
            rows.append(win / float(min(p0 + t + 1, w)) - u_ref[t, :, sl])
        pooled.append(jnp.concatenate(rows, axis=0))
    flat = lambda ref: ref[...].reshape(steps * nb, ref.shape[-1])
    og = _gla_gate(flat(o_ref), flat(r_ref), on_ref[...])
    x = x_ref[...]
    xo_ref[...] = x + gt_ref[...] * _even_mix(og, pooled, pw_ref, ps_ref, wo_ref).reshape(x.shape)
    for i in range(POOL_HIST):
        hn_ref[i] = slab(steps + i, slice(None))


def _even_out(grp, x, o, r, u, hist, mod, on, pw, ps, wo, p0):
    dm = x.shape[-1]
    common = [mod, on, pw, ps, wo]
    common_specs = [grp.mod(5), _full(on), _full(pw), _full(ps), _full(wo)]
    acts = [grp.act(dm), grp.act(o.shape[-1]), grp.act(r.shape[-1]), grp.act(u.shape[-1])]
    if grp.prompt:
        tm = grp.block[1]
        return pl.pallas_call(
            functools.partial(_even_out_prompt_kernel, tm=tm, p0=p0),
            grid=grp.grid,
            in_specs=acts + common_specs,
            out_specs=[grp.act(dm), pl.BlockSpec((1, POOL_PAD, POOL_WIDTH), lambda b, t: (b, 0, 0))],
            out_shape=[_sds(x.shape), _sds((x.shape[0], POOL_PAD, POOL_WIDTH))],
            scratch_shapes=[pltpu.VMEM((POOL_PAD + tm, POOL_WIDTH), F32)],
            compiler_params=_cp("parallel", "arbitrary"),
        )(x, o, r, u, *common)
    return pl.pallas_call(
        functools.partial(_even_out_sample_kernel, p0=p0),
        grid=grp.grid,
        in_specs=acts + [_full(hist)] + common_specs,
        out_specs=[grp.act(dm), _full(hist)],
        out_shape=[_sds(x.shape), _sds(hist.shape)],
        compiler_params=_cp("parallel", "arbitrary"),
    )(x, o, r, u, hist, *common)


def _rope_slab(x, cs, s1, s2, lead):
    shp = lead + (LANES,)
    back = pltpu.roll(x, LANES - MLA_ROPE // 2, 1).reshape(shp)
    fwd = pltpu.roll(x, MLA_ROPE // 2, 1).reshape(shp)
    out = x.reshape(shp) * cs + back * s1 + fwd * s2
    return out.reshape(x.shape)


def _odd_in_kernel(*refs, sample):
    (x_ref, sh_ref, sc_ref, g_ref, wcq_ref, wckv_ref, wkr_ref, wga_ref, wgg_ref, qn_ref, wuq_ref,
     kvn_ref, cs_ref, s1_ref, s2_ref) = refs[:15]
    if sample:
        wkl_ref, ckv_ref, kr_ref, uc_ref, q_ref, ql_ref = refs[15:]
    else:
        wuk_ref, wuvt_ref, ckv_ref, kr_ref, uc_ref, q_ref, k_ref, vt_ref = refs[15:]
    lead = x_ref.shape[:2]
    h = _modulated(x_ref, g_ref, sh_ref, sc_ref)
    cs, s1, s2 = cs_ref[...], s1_ref[...], s2_ref[...]

    cq = _rms(_dot(h, wcq_ref[...]), qn_ref[...]).astype(BF16)
    q = _dot(cq, wuq_ref[...]) * (MLA_SCALE * LOG2E)
    q = jnp.concatenate(
        [_rope_slab(q[:, LANES * i:LANES * (i + 1)], cs, s1, s2, lead) for i in range(MLA_HEADS)], axis=-1)
    q_ref[...] = q.astype(BF16).reshape(lead + (MLA_HEADS * LANES,))

    ckv = _rms(_dot(h, wckv_ref[...]), kvn_ref[...])
    ckv_ref[...] = ckv.reshape(lead + (MLA_KV_RANK,))
    kr = _rope_slab(_dot(h, wkr_ref[...]), cs, s1, s2, lead)
    kr_ref[...] = kr.reshape(lead + (LANES,))
    uc_ref[...] = (_dot(h, wga_ref[...]) * jax.nn.sigmoid(_dot(h, wgg_ref[...]))).reshape(lead + (CONV_CH,))

    if sample:
        qb = q.astype(BF16)
        for i in range(MLA_HEADS):
            ql = _dot(qb[:, LANES * i:LANES * (i + 1)], wkl_ref[i])
            ql_ref[:, :, MLA_KV_RANK * i:MLA_KV_RANK * (i + 1)] = ql.astype(BF16).reshape(lead + (MLA_KV_RANK,))
    else:
        cb = ckv.astype(BF16)
        kn = _dot(cb, wuk_ref[...])
        kn = jnp.concatenate([kn[:, LANES * i:LANES * (i + 1)] + kr for i in range(MLA_HEADS)], axis=-1)
        k_ref[...] = kn.astype(BF16).reshape(lead + (MLA_HEADS * LANES,))
        one_row = lax.broadcasted_iota(jnp.int32, (MLA_HEADS, VT_ROWS, 1), 1) == MLA_V
        ones = jnp.where(one_row, 1.0, 0.0).reshape(MLA_HEADS * VT_ROWS, 1)
        vt_ref[0] = (_dot_nt(wuvt_ref[...], cb) + ones).astype(BF16)


def _odd_in(grp, x, mod, g, w, tabs, sample):
    lead = x.shape[:2]
    hl = MLA_HEADS * LANES
    ins = [x, mod, mod, g, w['wcq'], w['wckv'], w['wkr'], w['wga'], w['wgg'], w['qn'], w['wuq'], w['kvn'], *tabs]
    specs = [grp.act(x.shape[-1]), grp.mod(3), grp.mod(4)] + [_full(a) for a in ins[3:12]] + [grp.pos()] * 3
    outs = [(MLA_KV_RANK, F32), (LANES, F32), (CONV_CH, F32), (hl, BF16)]
    if sample:
        ins.append(w['wkl'])
        outs.append((MLA_HEADS * MLA_KV_RANK, BF16))
    else:
        ins += [w['wuk'], w['wuvt']]
        outs.append((hl, BF16))
    specs += [_full(a) for a in ins[15:]]
    out_specs = [grp.act(wd) for wd, _ in outs]
    out_shape = [_sds(lead + (wd,), dt) for wd, dt in outs]
    if not sample:
        vt_rows = MLA_HEADS * VT_ROWS
        out_specs.append(pl.BlockSpec((1, vt_rows, grp.block[1]), lambda b, t: (b, 0, t)))
        out_shape.append(_sds((lead[0], vt_rows, lead[1]), BF16))
    return pl.pallas_call(
        functools.partial(_odd_in_kernel, sample=sample),
        grid=grp.grid,
        in_specs=specs,
        out_specs=out_specs,
        out_shape=out_shape,
        compiler_params=_cp("parallel", "parallel"),
    )(*ins)


def _attn_prompt_kernel(qi_ref, ki_ref, q_ref, k_ref, vt_ref, o_ref, m_scr, acc_scr, s_scr, p_scr, *, tq):
    step = pl.program_id(1)
    qi = qi_ref[step]
    ki = ki_ref[step]

    @pl.when(ki == 0)
    def _():
        m_scr[...] = jnp.full_like(m_scr, NEG)
        acc_scr[...] = jnp.zeros_like(acc_scr)

    def update(diagonal):
        width = s_scr.shape[2]
        items = [(h, c) for h in range(MLA_HEADS) for c in range(0, tq, width)]

        def keys_used(c):
            return c + width if diagonal else tq

        def scores(item, buf):
            h, c = item
            nk = keys_used(c)
            sl = slice(LANES * h, LANES * (h + 1))
            st = _dot_nt(k_ref[0, :nk, sl], q_ref[0, c:c + width, sl])
            if diagonal:
                key = lax.broadcasted_iota(jnp.int32, (nk, width), 0)
                qry = lax.broadcasted_iota(jnp.int32, (nk, width), 1) + c
                st = jnp.where(key <= qry, st, NEG)
            s_scr[buf, :nk] = st

        def absorb(item, buf):
            h, c = item
            nk = keys_used(c)
            vr = slice(VT_ROWS * h, VT_ROWS * (h + 1))
            qc = slice(c, c + width)
            m_prev = m_scr[h:h + 1, qc]
            m_new = jnp.maximum(m_prev, jnp.max(s_scr[buf, :nk], axis=0, keepdims=True))
            p_scr[buf, :nk] = jnp.exp2(s_scr[buf, :nk] - m_new).astype(BF16)
            acc_scr[vr, qc] = (jnp.exp2(m_prev - m_new) * acc_scr[vr, qc]
                               + _dot(vt_ref[0, vr, :nk], p_scr[buf, :nk]))
            m_scr[h:h + 1, qc] = m_new

        depth = s_scr.shape[0]
        for i in range(depth - 1):
            scores(items[i], i)
        for i, item in enumerate(items):
            if i + depth - 1 < len(items):
                scores(items[i + depth - 1], (i + depth - 1) % depth)
            absorb(item, i % depth)

    @pl.when(ki < qi)
    def _():
        update(False)

    @pl.when(ki == qi)
    def _():
        update(True)
        for h in range(0, MLA_HEADS, 2):
            pair = []
            for hh in (h, h + 1):
                acc = acc_scr[VT_ROWS * hh:VT_ROWS * (hh + 1), :]
                pair.append(acc[:MLA_V] / acc[MLA_V:MLA_V + 1])
            o_ref[0, :, MLA_V * h:MLA_V * (h + 2)] = jnp.transpose(jnp.concatenate(pair, axis=0)).astype(BF16)


def _attn_prompt(q, k, vt, tq):
    nb, nt, hl = q.shape
    vt_rows = vt.shape[1]
    n_out = MLA_HEADS * MLA_V
    pairs = [(i, j) for i in range(nt // tq) for j in range(i + 1)]
    qi_tab = jnp.asarray([p[0] for p in pairs], jnp.int32)
    ki_tab = jnp.asarray([p[1] for p in pairs], jnp.int32)
    return pl.pallas_call(
        functools.partial(_attn_prompt_kernel, tq=tq),
        grid_spec=pltpu.PrefetchScalarGridSpec(
            num_scalar_prefetch=2,
            grid=(nb, len(pairs)),
            in_specs=[pl.BlockSpec((1, tq, hl), lambda b, s, qt, kt: (b, qt[s], 0)),
                      pl.BlockSpec((1, tq, hl), lambda b, s, qt, kt: (b, kt[s], 0)),
                      pl.BlockSpec((1, vt_rows, tq), lambda b, s, qt, kt: (b, 0, kt[s]))],
            out_specs=pl.BlockSpec((1, tq, n_out), lambda b, s, qt, kt: (b, qt[s], 0)),
            scratch_shapes=[pltpu.VMEM((MLA_HEADS, tq), F32), pltpu.VMEM((vt_rows, tq), F32),
                            pltpu.VMEM((ATTN_DEPTH, tq, min(ATTN_QCHUNK, tq)), F32),
                            pltpu.VMEM((ATTN_DEPTH, tq, min(ATTN_QCHUNK, tq)), BF16)]),
        out_shape=_sds((nb, nt, n_out), BF16),
        compiler_params=_cp("parallel", "arbitrary"),
    )(qi_tab, ki_tab, q, k, vt)


def _attn_sample_kernel(pt_ref, ql_ref, qr_ref, cn_ref, kn_ref, ckv_hbm, kr_hbm, o_ref,
                        ckv_buf, kr_buf, pg_scr, s_scr, p_scr, sem, *, npg, layer):
    b = pl.program_id(0)
    nb = pl.num_programs(0)
    slot = lax.rem(b, 2)
    ql = ql_ref[0]
    qr = qr_ref[0]
    rows = ql.shape[0]

    def page_copies(page, i, sl):
        return (pltpu.make_async_copy(ckv_hbm.at[layer, page], ckv_buf.at[sl, i], sem.at[0, sl]),
                pltpu.make_async_copy(kr_hbm.at[layer, page], kr_buf.at[sl, i], sem.at[1, sl]))

    def start_page(batch, i, sl):
        for cp in page_copies(pt_ref[batch, i], i, sl):
            cp.start()

    def start_first(i, carry):
        start_page(0, i, 0)
        return carry

    @pl.when(b == 0)
    def _():
        lax.fori_loop(0, npg, start_first, 0)

    for i in range(npg):
        for cp in page_copies(0, i, slot):
            cp.wait()

    def score_pages(prefetch):
        for i in range(npg):
            if prefetch:
                for k in range(2 * i, min(2 * i + 2, npg)):
                    start_page(b + 1, k, 1 - slot)
            page = ckv_buf[slot, i].astype(BF16)
            pg_scr[i] = page
            s_scr[i] = _dot_nt(ql, page) + _dot(qr, kr_buf[slot, i].astype(BF16))

    @pl.when(b + 1 < nb)
    def _():
        score_pages(True)

    @pl.when(b + 1 == nb)
    def _():
        score_pages(False)

    n_new = cn_ref.shape[1]
    pad = jnp.zeros((LANES - n_new, MLA_KV_RANK), F32)
    cn = jnp.concatenate([cn_ref[0], pad], axis=0).astype(BF16)
    kn = jnp.concatenate([kn_ref[0], pad[:, :MLA_ROPE]], axis=0).astype(BF16)
    step = lax.shift_right_logical(lax.broadcasted_iota(jnp.int32, (rows, LANES), 0), MLA_HEADS.bit_length() - 1)
    col = lax.broadcasted_iota(jnp.int32, (rows, LANES), 1)
    s_new = jnp.where(col <= step, _dot_nt(ql, cn) + _dot_nt(qr, kn), NEG)

    s = s_scr[...]
    m = jnp.maximum(jnp.max(jnp.max(s, axis=0), axis=-1, keepdims=True),
                    jnp.max(s_new, axis=-1, keepdims=True))
    p = jnp.exp2(s - m)
    p_new = jnp.exp2(s_new - m)
    denom = jnp.sum(jnp.sum(p, axis=0), axis=-1, keepdims=True) + jnp.sum(p_new, axis=-1, keepdims=True)
    p_scr[...] = p.astype(BF16)

    def weigh_page(i, acc):
        return acc + _dot(p_scr[i], pg_scr[i])

    acc = lax.fori_loop(0, npg, weigh_page, _dot(p_new.astype(BF16), cn), unroll=True)
    o_ref[0] = acc / denom


def _attn_sample(page_table, ql, qr, cn, kn, cache_ckv, cache_kr, layer):
    nb, rows, rank = ql.shape
    npg = page_table.shape[1]
    page, rope = cache_ckv.shape[2], cache_kr.shape[2]
    assert page == LANES, "one cache page must fill one lane tile of scores"
    per_b = lambda a: pl.BlockSpec((1,) + a.shape[1:], lambda b, pt: (b, 0, 0))
    hbm = pl.BlockSpec(memory_space=pl.ANY)
    return pl.pallas_call(
        functools.partial(_attn_sample_kernel, npg=npg, layer=layer),
        grid_spec=pltpu.PrefetchScalarGridSpec(
            num_scalar_prefetch=1,
            grid=(nb,),
            in_specs=[per_b(ql), per_b(qr), per_b(cn), per_b(kn), hbm, hbm],
            out_specs=pl.BlockSpec((1, rows, rank), lambda b, pt: (b, 0, 0)),
            scratch_shapes=[pltpu.VMEM((2, npg, page, rank), F32),
                            pltpu.VMEM((2, npg, rope, page), F32),
                            pltpu.VMEM((npg, page, rank), BF16),
                            pltpu.VMEM((npg, rows, page), F32),
                            pltpu.VMEM((npg, rows, page), BF16),
                            pltpu.SemaphoreType.DMA((2, 2))]),
        out_shape=_sds((nb, rows, rank)),
        compiler_params=_cp("arbitrary"),
    )(page_table, ql, qr, cn, kn, cache_ckv, cache_kr)


def _conv_norm_act(cv, cb_ref, lg_ref, lb_ref):
    cv = cv + cb_ref[...]
    mu = jnp.mean(cv, axis=-1, keepdims=True)
    d = cv - mu
    y = d * lax.rsqrt(jnp.mean(d * d, axis=-1, keepdims=True) + EPS)
    return _silu(y * lg_ref[...] + lb_ref[...])


def _odd_out_prompt_kernel(x_ref, a_ref, uc_ref, gt_ref, cw_ref, cb_ref, lg_ref, lb_ref, woa_ref, woc_ref,
                           xo_ref, ho_ref, hb, hs, cvb, *, tm):
    t = pl.program_id(1)

    @pl.when(t == 0)
    def _():
        hb[0:CONV_PAD, :] = jnp.zeros((CONV_PAD, CONV_CH), F32)

    hb[CONV_PAD:CONV_PAD + tm, :] = uc_ref[0]
    for sft in range(1, SUBLANES):
        hs[sft - 1] = hb[sft:sft + hs.shape[1], :]
    base = CONV_PAD - CONV_HIST

    block = min(CONV_BLOCK, tm)
    for q0 in range(0, tm, block):
        ya = _dot(a_ref[0, q0:q0 + block, :], woa_ref[...])
        for r0 in range(q0, q0 + block, CONV_ROWS):
            acc = None
            for j in range(CONV_WIDTH):
                whole, sft = divmod(base + j, SUBLANES)
                rows = slice(r0 + whole * SUBLANES, r0 + whole * SUBLANES + CONV_ROWS)
                term = cw_ref[j:j + 1, :] * (hb[rows, :] if sft == 0 else hs[sft - 1, rows, :])
                acc = term if acc is None else acc + term
            cvb[r0:r0 + CONV_ROWS, :] = acc
        cv = _conv_norm_act(cvb[q0:q0 + block, :], cb_ref, lg_ref, lb_ref)
        y = ya + _dot(cv.astype(BF16), woc_ref[...])
        xo_ref[0, q0:q0 + block, :] = x_ref[0, q0:q0 + block, :] + gt_ref[...] * y
    hb[0:CONV_PAD, :] = hb[tm:tm + CONV_PAD, :]

    @pl.when(t == pl.num_programs(1) - 1)
    def _():
        ho_ref[0] = hb[0:CONV_PAD, :]


def _odd_out_sample_kernel(x_ref, lat_ref, uc_ref, hi_ref, gt_ref, cw_ref, cb_ref, lg_ref, lb_ref, wuv_ref,
                           woa_ref, woc_ref, xo_ref, hn_ref):
    steps, nb = uc_ref.shape[:2]

    def slab(i):
        return hi_ref[i] if i < CONV_HIST else uc_ref[i - CONV_HIST]

    rows = []
    for t in range(steps):
        cv = cw_ref[0:1, :] * slab(t)
        for j in range(1, CONV_WIDTH):
            cv = cv + cw_ref[j:j + 1, :] * slab(t + j)
        rows.append(cv)
    cv = _conv_norm_act(jnp.concatenate(rows, axis=0), cb_ref, lg_ref, lb_ref)
    lat = lat_ref[...].reshape(steps * nb, lat_ref.shape[-1]).astype(BF16)
    attn = _dot(lat, wuv_ref[...]).astype(BF16)
    y = _dot(attn, woa_ref[...]) + _dot(cv.astype(BF16), woc_ref[...])
    x = x_ref[...]
    xo_ref[...] = x + gt_ref[...] * y.reshape(x.shape)
    for i in range(CONV_HIST):
        hn_ref[i] = slab(steps + i)


def _odd_out(grp, x, a, uc, hist, mod, w):
    dm = x.shape[-1]
    conv = [w['cw'], w['cb'], w['lg'], w['lb']]
    if grp.prompt:
        tm = grp.block[1]
        ins = [x, a, uc, mod] + conv + [w['woa'], w['woc']]
        return pl.pallas_call(
            functools.partial(_odd_out_prompt_kernel, tm=tm),
            grid=grp.grid,
            in_specs=[grp.act(dm), grp.act(a.shape[-1]), grp.act(CONV_CH), grp.mod(5)] + [_full(v) for v in ins[4:]],
            out_specs=[grp.act(dm), pl.BlockSpec((1, CONV_PAD, CONV_CH), lambda b, t: (b, 0, 0))],
            out_shape=[_sds(x.shape), _sds((x.shape[0], CONV_PAD, CONV_CH))],
            scratch_shapes=[pltpu.VMEM((CONV_PAD + tm, CONV_CH), F32),
                            pltpu.VMEM((SUBLANES - 1, CONV_PAD + tm - SUBLANES, CONV_CH), F32),
                            pltpu.VMEM((tm, CONV_CH), F32)],
            compiler_params=_cp("parallel", "arbitrary"),
        )(*ins)
    ins = [x, a, uc, hist, mod] + conv + [w['wuv_bd'], w['woa'], w['woc']]
    return pl.pallas_call(
        _odd_out_sample_kernel,
        grid=grp.grid,
        in_specs=[grp.act(dm), grp.act(a.shape[-1]), grp.act(CONV_CH), _full(hist), grp.mod(5)]
        + [_full(v) for v in ins[5:]],
        out_specs=[grp.act(dm), _full(hist)],
        out_shape=[_sds(x.shape), _sds(hist.shape)],
        compiler_params=_cp("parallel", "arbitrary"),
    )(*ins)


def _head_pad(w, heads, width, offset=0):
    kdim = w.shape[0]
    w = w.reshape(kdim, heads, width)
    w = jnp.pad(w, ((0, 0), (0, 0), (offset, LANES - width - offset)))
    return w.reshape(kdim, heads * LANES)


def _rope_tables(pos):
    half = MLA_ROPE // 2
    freqs = ROPE_BASE ** (-np.arange(half, dtype=np.float64) / half)
    ang = np.asarray(pos, np.float64)[..., None] * freqs
    cos, sin = jnp.asarray(np.cos(ang), F32), jnp.asarray(np.sin(ang), F32)
    zeros = jnp.zeros_like(cos)
    lead = jnp.ones(pos.shape + (ROPE_LANE0,), F32)
    tail = jnp.zeros(pos.shape + (LANES - ROPE_LANE0 - MLA_ROPE,), F32)
    cs = jnp.concatenate([lead, cos, cos, tail], axis=-1)
    s1 = jnp.concatenate([0 * lead, -sin, zeros, tail], axis=-1)
    s2 = jnp.concatenate([0 * lead, zeros, sin, tail], axis=-1)
    return cs, s1, s2


def _even_weights(w_in, gate_w2, gate_b, out_norm, pool_w, pool_scale, w_out):
    hk = GLA_HEADS * GLA_DK
    hv = GLA_HEADS * GLA_DV
    n_main = 2 * hk + 2 * hv
    wg = jnp.pad(w_in[:, n_main:n_main + GLA_GATE_RANK], ((0, 0), (0, LANES - GLA_GATE_RANK)))
    w2 = jnp.pad(gate_w2, ((0, LANES - GLA_GATE_RANK), (0, 0)))
    return dict(wq=w_in[:, :n_main].astype(BF16), wg=wg.astype(BF16), w2=w2.astype(BF16),
                gb=gate_b[None], wu=w_in[:, n_main + GLA_GATE_RANK:].astype(BF16),
                on=out_norm[None], pw=pool_w.astype(BF16), ps=pool_scale[None], wo=w_out.astype(BF16))


def _odd_weights(w_in, q_norm, w_uq, kv_norm, w_uk, w_uv, conv_w, conv_b, ln_g, ln_b, w_out):
    c0, c1, c2 = MLA_Q_RANK, MLA_Q_RANK + MLA_KV_RANK, MLA_Q_RANK + MLA_KV_RANK + MLA_ROPE
    n_attn = MLA_HEADS * MLA_V
    wkr = jnp.pad(w_in[:, c1:c2], ((0, 0), (ROPE_LANE0, LANES - ROPE_LANE0 - MLA_ROPE)))
    wuk = w_uk.reshape(MLA_KV_RANK, MLA_HEADS * MLA_NOPE)
    wkl = jnp.pad(jnp.transpose(w_uk, (1, 2, 0)), ((0, 0), (0, LANES - MLA_NOPE), (0, 0)))
    eye = jnp.eye(MLA_HEADS, dtype=F32)
    wuv_bd = (eye[:, None, :, None] * jnp.transpose(w_uv, (1, 0, 2))[:, :, None, :]).reshape(
        MLA_HEADS * MLA_KV_RANK, n_attn)
    wuvt = jnp.pad(jnp.transpose(w_uv, (1, 2, 0)), ((0, 0), (0, VT_ROWS - MLA_V), (0, 0))).reshape(
        MLA_HEADS * VT_ROWS, MLA_KV_RANK)
    return dict(wcq=w_in[:, :c0].astype(BF16), wckv=w_in[:, c0:c1].astype(BF16), wkr=wkr.astype(BF16),
                wga=w_in[:, c2:c2 + CONV_CH].astype(BF16), wgg=w_in[:, c2 + CONV_CH:].astype(BF16),
                qn=q_norm[None], wuq=_head_pad(w_uq, MLA_HEADS, MLA_NOPE + MLA_ROPE).astype(BF16),
                kvn=kv_norm[None], wuk=_head_pad(wuk, MLA_HEADS, MLA_NOPE).astype(BF16),
                wuvt=wuvt.astype(BF16), wkl=wkl.astype(BF16),
                wuv_bd=wuv_bd.astype(BF16), cw=conv_w, cb=conv_b[None], lg=ln_g[None], lb=ln_b[None],
                woa=w_out[:n_attn].astype(BF16), woc=w_out[n_attn:].astype(BF16))


def _tm(x):
    return jnp.swapaxes(x, 0, 1)


def kernel(x_prompt, x_sample, state_gla, state_pool, cache_ckv, cache_krope, state_conv, page_table, c_prompt, c_sample, ada_w, ada_b, norm_g, ffn_w1, ffn_w3, ffn_w2, ev_w_in, ev_gate_w2, ev_gate_b, ev_out_norm, ev_pool_w, ev_pool_scale, ev_w_out, od_w_in, od_q_norm, od_w_uq, od_kv_norm, od_w_uk, od_w_uv, od_conv_w, od_conv_b, od_conv_norm_g, od_conv_norm_b, od_w_out, final_norm):
    nbp, seq, dm = x_prompt.shape
    nbs, steps, _ = x_sample.shape
    depth = ada_w.shape[0]
    past_len = page_table.shape[1] * cache_ckv.shape[2]
    tile = min(TOKEN_TILE, seq)
    gp = _Group(True, nbp, seq, tile)
    gf = _Group(True, nbp, seq, min(FFN_TOKEN_TILE, seq))
    gs = _Group(False, steps, nbs, nbs)

    n_c = nbp + nbs
    c_all = jnp.pad(jnp.concatenate([c_prompt, c_sample], axis=0), ((0, -n_c % 8), (0, 0)))
    mod = _ada(c_all, ada_w, ada_b)
    mod_p = mod[:, :nbp].reshape(depth, nbp, N_MOD, 1, dm)
    mod_s = jnp.swapaxes(mod[:, nbp:n_c].reshape(depth, nbs, N_MOD, dm), 1, 2)

    w1, w3, w2 = ffn_w1.astype(BF16), ffn_w3.astype(BF16), ffn_w2.astype(BF16)
    tabs_p = _rope_tables(np.arange(seq)[None])
    tabs_s = _rope_tables(past_len + np.arange(steps)[:, None])

    xp = x_prompt
    xs = _tm(x_sample)
    gla_p, gla_s, pool_p, pool_s, ckv_p, ckv_s, kr_p, kr_s, conv_p, conv_s = ([] for _ in range(10))
    gla_tt = min(TOKEN_TILE, seq)

    for layer in range(depth):
        i = layer // 2
        ng = norm_g[layer]
        last = layer == depth - 1
        xp = _ffn(gf, xp, mod_p[layer], 0, ng[0:1], w1, w3, w2, (layer, 0))
        xs = _ffn(gs, xs, mod_s[layer], 0, ng[0:1], w1, w3, w2, (layer, 0))
        if layer % 2 == 0:
            w = _even_weights(ev_w_in[i], ev_gate_w2[i], ev_gate_b[i], ev_out_norm[i], ev_pool_w[i],
                              ev_pool_scale[i], ev_w_out[i])
            proj = (ng[1:2], w['wq'], w['wg'], w['w2'], w['gb'], w['wu'])
            out_w = (w['on'], w['pw'], w['ps'], w['wo'])
            q, k, v, r, la, u = _even_in(gp, xp, mod_p[layer], *proj)
            o, s_fin = _gla(q, k, v, la, None, nbp, gla_tt)
            xp, hist = _even_out(gp, xp, o, r, u, None, mod_p[layer], *out_w, 0)
            gla_p.append(s_fin.reshape(nbp, GLA_HEADS, GLA_DK, GLA_DV))
            pool_p.append(hist[:, POOL_PAD - POOL_HIST:])
            q, k, v, r, la, u = _even_in(gs, xs, mod_s[layer], *proj)
            rows8 = steps + (-steps % SUBLANES)
            chunked = lambda a: jnp.pad(_tm(a), ((0, 0), (0, rows8 - steps), (0, 0)))
            s0 = state_gla[i].reshape(nbs, GLA_HEADS // 2, LANES, GLA_DV)
            o, s_fin = _gla(chunked(q), chunked(k), chunked(v), chunked(la), s0,
                            GLA_SAMPLE_BATCHES if nbs % GLA_SAMPLE_BATCHES == 0 else 1, rows8)
            xs, hist = _even_out(gs, xs, _tm(o[:, :steps]), r, u, _tm(state_pool[i]), mod_s[layer], *out_w, past_len)
            gla_s.append(s_fin.reshape(nbs, GLA_HEADS, GLA_DK, GLA_DV))
            pool_s.append(_tm(hist))
        else:
            w = _odd_weights(od_w_in[i], od_q_norm[i], od_w_uq[i], od_kv_norm[i], od_w_uk[i], od_w_uv[i],
                             od_conv_w[i], od_conv_b[i], od_conv_norm_g[i], od_conv_norm_b[i], od_w_out[i])
            rope_lanes = slice(ROPE_LANE0, ROPE_LANE0 + MLA_ROPE)
            ckv, kr, uc, q, k, v = _odd_in(gp, xp, mod_p[layer], ng[1:2], w, tabs_p, False)
            attn = _attn_prompt(q, k, v, min(ATTN_TILE, seq))
            xp, hist = _odd_out(gp, xp, attn, uc, None, mod_p[layer], w)
            ckv_p.append(ckv)
            kr_p.append(kr[..., rope_lanes])
            conv_p.append(hist[:, CONV_PAD - CONV_HIST:])
            ckv, kr, uc, q, ql = _odd_in(gs, xs, mod_s[layer], ng[1:2], w, tabs_s, True)
            kr = kr[..., rope_lanes]
            qr = q.reshape(steps, nbs, MLA_HEADS, LANES)[..., rope_lanes]
            qr = _tm(qr).reshape(nbs, steps * MLA_HEADS, MLA_ROPE)
            ql = _tm(ql).reshape(nbs, steps * MLA_HEADS, MLA_KV_RANK)
            pad8 = lambda a: jnp.pad(_tm(a), ((0, 0), (0, -steps % 8), (0, 0)))
            cache_kr_t = jnp.swapaxes(cache_krope, 2, 3)
            lat = _attn_sample(page_table, ql, qr, pad8(ckv), pad8(kr), cache_ckv, cache_kr_t, i)
            lat = _tm(lat.reshape(nbs, steps, MLA_HEADS * MLA_KV_RANK))
            xs, hist = _odd_out(gs, xs, lat, uc, _tm(state_conv[i]), mod_s[layer], w)
            ckv_s.append(_tm(ckv))
            kr_s.append(_tm(kr))
            conv_s.append(_tm(hist))
        fin = final_norm[None] if last else None
        xp = _ffn(gf, xp, mod_p[layer], 6, ng[2:3], w1, w3, w2, (layer, 1), fin)
        xs = _ffn(gs, xs, mod_s[layer], 6, ng[2:3], w1, w3, w2, (layer, 1), fin)

    st = jnp.stack
    return (xp, _tm(xs), st(gla_p), st(gla_s), st(pool_p), st(pool_s), st(ckv_p), st(ckv_s),
            st(kr_p), st(kr_s), st(conv_p), st(conv_s))
```

```python
import functools

import jax
import jax.numpy as jnp
import numpy as np
from jax import lax
from jax.experimental import pallas as pl
from jax.experimental.pallas import tpu as pltpu

F32 = jnp.float32
BF16 = jnp.bfloat16

EPS = 1e-6
NEG = -1e30
N_MOD = 9
GLA_HEADS = 4
GLA_DK = 64
GLA_DV = 128
GLA_GATE_RANK = 16
GLA_TAU = 16.0
GLA_CHUNK = 64
POOL_WINDOWS = (2, 4, 8, 16)
POOL_GROUP = 128
POOL_WIDTH = 512
POOL_HIST = 15
MLA_HEADS = 8
MLA_Q_RANK = 384
MLA_KV_RANK = 256
MLA_NOPE = 64
MLA_ROPE = 32
MLA_V = 64
MLA_SCALE = (MLA_NOPE + MLA_ROPE) ** -0.5
LOG2E = 1.4426950408889634
ROPE_BASE = 10000.0
CONV_WIDTH = 31
CONV_CH = 512
CONV_HIST = 30

LANES = 128
SUBLANES = 8
CONV_ROWS = 32
CONV_BLOCK = 256
EVEN_BLOCK = 256
ROPE_LANE0 = 64
VT_ROWS = 80
POOL_PAD = 16
CONV_PAD = 32
VMEM_LIMIT = 52 * 2 ** 20
TOKEN_TILE = 1024
ATTN_TILE = 1024
ATTN_DEPTH = 3
ATTN_QCHUNK = 512
FFN_TOKEN_TILE = 1024
FF_SUB = 1024
GLA_SAMPLE_BATCHES = 8


def _cp(*sem):
    return pltpu.CompilerParams(dimension_semantics=sem, vmem_limit_bytes=VMEM_LIMIT)


def _dot(a, b):
    return jnp.dot(a, b, preferred_element_type=F32)


def _dot_nt(a, b):
    return lax.dot_general(a, b, (((1,), (1,)), ((), ())), preferred_element_type=F32)


def _dot_tn(a, b):
    return lax.dot_general(a, b, (((0,), (0,)), ((), ())), preferred_element_type=F32)


def _silu(x):
    return x * jax.nn.sigmoid(x)


def _rms(x, g):
    return x * lax.rsqrt(jnp.mean(x * x, axis=-1, keepdims=True) + EPS) * g


def _modulated(x_ref, g_ref, sh_ref, sc_ref):
    x = x_ref[...]
    h = _rms(x, g_ref[...]) * (1.0 + sc_ref[...]) + sh_ref[...]
    return h.reshape(x.shape[0] * x.shape[1], x.shape[2]).astype(BF16)


class _Group:
    def __init__(self, prompt, lead, rows, tile):
        self.prompt = prompt
        self.grid = (lead, rows // tile) if prompt else (1, 1)
        self.block = (1, tile) if prompt else (lead, rows)

    def act(self, width):
        return pl.BlockSpec(self.block + (width,), lambda b, t, *_: (b, t, 0))

    def mod(self, k):
        if self.prompt:
            return pl.BlockSpec((None, None, 1, self.dm), lambda b, t, *_: (b, k, 0, 0))
        return pl.BlockSpec((None, self.block[1], self.dm), lambda b, t, *_: (k, 0, 0))

    def pos(self):
        if self.prompt:
            return pl.BlockSpec((1, self.block[1], LANES), lambda b, t, *_: (0, t, 0))
        return pl.BlockSpec((self.block[0], 1, LANES), lambda b, t, *_: (0, 0, 0))

    dm = 1024


def _full(a):
    nd = a.ndim
    return pl.BlockSpec(a.shape, lambda *_: (0,) * nd)


def _sds(shape, dtype=F32):
    return jax.ShapeDtypeStruct(shape, dtype)


def _ada_kernel(c_ref, w_ref, b_ref, o_ref):
    c = c_ref[...]
    o_ref[...] = _dot(_silu(c).astype(BF16), w_ref[...].astype(BF16)) + b_ref[...]


def _ada(c_all, ada_w, ada_b):
    depth, dm, n = ada_w.shape
    m = c_all.shape[0]
    tn = dm
    return pl.pallas_call(
        _ada_kernel,
        grid=(depth, n // tn),
        in_specs=[pl.BlockSpec((m, dm), lambda l, j: (0, 0)),
                  pl.BlockSpec((None, dm, tn), lambda l, j: (l, 0, j)),
                  pl.BlockSpec((None, 1, tn), lambda l, j: (l, 0, j))],
        out_specs=pl.BlockSpec((None, m, tn), lambda l, j: (l, 0, j)),
        out_shape=_sds((depth, m, n)),
        compiler_params=_cp("parallel", "parallel"),
    )(c_all, ada_w, ada_b.reshape(depth, 1, n))


def _ffn_kernel(*refs, final):
    xp_ref, shp_ref, scp_ref, gtp_ref, xs_ref, shs_ref, scs_ref, gts_ref, g_ref, w1_ref, w3_ref, w2_ref = refs[:12]
    fn_ref = refs[12] if final else None
    op_ref, os_ref = refs[-2:]
    last = pl.program_id(0) == pl.num_programs(0) - 1

    @pl.when(jnp.logical_not(last))
    def _():
        _ffn_tile(xp_ref, shp_ref, scp_ref, gtp_ref, g_ref, w1_ref, w3_ref, w2_ref, fn_ref, op_ref)

    @pl.when(last)
    def _():
        _ffn_tile(xs_ref, shs_ref, scs_ref, gts_ref, g_ref, w1_ref, w3_ref, w2_ref, fn_ref, os_ref)


def _ffn_tile(x_ref, sh_ref, sc_ref, gt_ref, g_ref, w1_ref, w3_ref, w2_ref, fn_ref, o_ref):
    final = fn_ref is not None
    h = _modulated(x_ref, g_ref, sh_ref, sc_ref)
    dff = w1_ref.shape[1]
    y = None
    for c0 in range(0, dff, FF_SUB):
        cols = slice(c0, min(c0 + FF_SUB, dff))
        a = _dot(h, w1_ref[:, cols])
        b = _dot(h, w3_ref[:, cols])
        part = _dot((_silu(a) * b).astype(BF16), w2_ref[cols, :])
        y = part if y is None else y + part
    x = x_ref[...]
    xn = x + 0.5 * gt_ref[...] * y.reshape(x.shape)
    o_ref[...] = _rms(xn, fn_ref[...]) if final else xn


def _resident(a, lead):
    idx = tuple(lead) + (0, 0)
    return pl.BlockSpec((None,) * len(lead) + a.shape[-2:], lambda *_: idx, pipeline_mode=pl.Buffered(1))


def _ffn(xp, xs, mod_p, mod_s, k0, g, w1, w3, w2, which, final_g=None):
    nbp, seq, dm = xp.shape
    tm = min(FFN_TOKEN_TILE, seq)
    nt = seq // tm
    n_prompt = nbp * nt
    final = final_g is not None

    def tile(s):
        s = jnp.minimum(s, n_prompt - 1)
        return s // nt, s % nt

    xp_spec = pl.BlockSpec((1, tm, dm), lambda s: tile(s) + (0,))
    xs_spec = pl.BlockSpec(xs.shape, lambda s: (0, 0, 0), pipeline_mode=pl.Buffered(1))
    modp = lambda k: pl.BlockSpec((None, None, 1, dm), lambda s: (tile(s)[0], k, 0, 0))
    mods = lambda k: pl.BlockSpec((None,) + mod_s.shape[1:], lambda s: (k, 0, 0), pipeline_mode=pl.Buffered(1))
    ins = [xp, mod_p, mod_p, mod_p, xs, mod_s, mod_s, mod_s, g, w1, w3, w2] + ([final_g] if final else [])
    specs = [xp_spec, modp(k0), modp(k0 + 1), modp(k0 + 2), xs_spec, mods(k0), mods(k0 + 1), mods(k0 + 2), _full(g),
             _resident(w1, which), _resident(w3, which), _resident(w2, which)] + ([_full(final_g)] if final else [])
    return pl.pallas_call(
        functools.partial(_ffn_kernel, final=final),
        grid=(n_prompt + 1,),
        in_specs=specs,
        out_specs=[xp_spec, pl.BlockSpec(xs.shape, lambda s: (0, 0, 0))],
        out_shape=[_sds(xp.shape), _sds(xs.shape)],
        compiler_params=_cp("arbitrary"),
    )(*ins)


def _log_sigmoid(x):
    return jnp.minimum(x, 0.0) - jnp.log(1.0 + jnp.exp(-jnp.abs(x)))


def _even_in_kernel(x_ref, sh_ref, sc_ref, g_ref, wq_ref, wg_ref, w2_ref, gb_ref, wu_ref,
                    q_ref, k_ref, v_ref, r_ref, la_ref, u_ref):
    h = _modulated(x_ref, g_ref, sh_ref, sc_ref)
    lead = x_ref.shape[:2]
    hk = GLA_HEADS * GLA_DK
    hv = GLA_HEADS * GLA_DV
    z = _dot(h, wq_ref[...])
    q_ref[...] = (z[:, :hk] * GLA_DK ** -0.5).reshape(lead + (hk,))
    k_ref[...] = z[:, hk:2 * hk].reshape(lead + (hk,))
    v_ref[...] = z[:, 2 * hk:2 * hk + hv].reshape(lead + (hv,))
    r_ref[...] = z[:, 2 * hk + hv:].reshape(lead + (hv,))
    g_low = _dot(h, wg_ref[...]).astype(BF16)
    gate = _dot(g_low, w2_ref[...]) + gb_ref[...]
    la_ref[...] = (_log_sigmoid(gate) / GLA_TAU).reshape(lead + (hk,))
    u_ref[...] = _dot(h, wu_ref[...]).reshape(lead + (POOL_WIDTH,))


def _even_in(grp, x, mod, g, wq, wg, w2, gb, wu):
    hk = GLA_HEADS * GLA_DK
    hv = GLA_HEADS * GLA_DV
    widths = (hk, hk, hv, hv, hk, POOL_WIDTH)
    return pl.pallas_call(
        _even_in_kernel,
        grid=grp.grid,
        in_specs=[grp.act(x.shape[-1]), grp.mod(3), grp.mod(4), _full(g), _full(wq), _full(wg),
                  _full(w2), _full(gb), _full(wu)],
        out_specs=[grp.act(w) for w in widths],
        out_shape=[_sds(x.shape[:2] + (w,)) for w in widths],
        compiler_params=_cp("parallel", "parallel"),
    )(x, mod, mod, g, wq, wg, w2, gb, wu)


def _cumsum_rows(tril, x):
    hi = x.astype(BF16)
    r1 = x - hi.astype(F32)
    mid = r1.astype(BF16)
    lo = (r1 - mid.astype(F32)).astype(BF16)
    return _dot(tril, hi) + _dot(tril, mid) + _dot(tril, lo)


def _gla_kernel(*refs, nbb, tt, has_s0):
    if has_s0:
        q_ref, k_ref, v_ref, la_ref, s0_ref, o_ref, so_ref, s_scr = refs
    else:
        q_ref, k_ref, v_ref, la_ref, o_ref, so_ref, s_scr = refs
    t = pl.program_id(1)
    c_rows = GLA_CHUNK

    @pl.when(t == 0)
    def _():
        s_scr[...] = s0_ref[...] if has_s0 else jnp.zeros_like(s_scr)

    row = lax.broadcasted_iota(jnp.int32, (c_rows, c_rows), 0)
    col = lax.broadcasted_iota(jnp.int32, (c_rows, c_rows), 1)
    causal = col <= row
    tril = jnp.where(causal, 1.0, 0.0).astype(BF16)
    lane = lax.broadcasted_iota(jnp.int32, (c_rows, LANES), 1)
    srow = lax.broadcasted_iota(jnp.int32, (LANES, LANES), 0)

    pairs = GLA_HEADS // 2
    combos = [(n, p) for n in range(nbb) for p in range(pairs)]
    heads = [(n, p, hh) for n, p in combos for hh in range(2)]

    def chunk(c, carry):
        short = tt < c_rows
        rows = slice(0, tt) if short else pl.ds(pl.multiple_of(c * c_rows, c_rows), c_rows)

        def take(ref, n, cols):
            x = ref[n, rows, cols]
            return jnp.concatenate([x, jnp.zeros((c_rows - tt, x.shape[1]), F32)], axis=0) if short else x

        wide = lambda ref: jnp.concatenate([take(ref, n, slice(None)) for n in range(nbb)], axis=-1)
        slab = lambda x, n, p: x[:, LANES * (n * pairs + p):LANES * (n * pairs + p + 1)]
        vcols = lambda p, hh: slice(GLA_DV * (2 * p + hh), GLA_DV * (2 * p + hh + 1))
        b = _cumsum_rows(tril, wide(la_ref))
        b_last = b[c_rows - 1:c_rows, :]
        k = wide(k_ref)
        qi = wide(q_ref) * jnp.exp(b)
        ki = (k * jnp.exp(-b)).astype(BF16)
        kd = (k * jnp.exp(b_last - b)).astype(BF16)
        grow = jnp.exp(b_last)
        qh = {(n, p, hh): jnp.where((lane < GLA_DK) if hh == 0 else (lane >= GLA_DK), slab(qi, n, p), 0.0).astype(BF16)
              for n, p, hh in heads}
        att = {(n, p, hh): _dot_nt(qh[n, p, hh], slab(ki, n, p)) for n, p, hh in heads}
        vh = {(n, p, hh): take(v_ref, n, vcols(p, hh)).astype(BF16) for n, p, hh in heads}
        s_old = {(n, p): s_scr[n, p] for n, p in combos}
        from_state = {(n, p, hh): _dot(qh[n, p, hh], s_old[n, p].astype(BF16)) for n, p, hh in heads}
        upd = {(n, p, hh): _dot_tn(slab(kd, n, p), vh[n, p, hh]) for n, p, hh in heads}
        for n, p, hh in heads:
            within = _dot(jnp.where(causal, att[n, p, hh], 0.0).astype(BF16), vh[n, p, hh])
            o_ref[n, rows, vcols(p, hh)] = (from_state[n, p, hh] + within)[:min(tt, c_rows)]
        for n, p in combos:
            decay = jnp.transpose(jnp.broadcast_to(slab(grow, n, p), (LANES, LANES)))
            s_scr[n, p] = decay * s_old[n, p] + jnp.where(srow < GLA_DK, upd[n, p, 0], upd[n, p, 1])
        return carry

    n_chunks = max(1, tt // c_rows)
    lax.fori_loop(0, n_chunks, chunk, 0, unroll=min(2, n_chunks))

    @pl.when(t == pl.num_programs(1) - 1)
    def _():
        so_ref[...] = s_scr[...]


def _gla(q, k, v, la, s0, nbb, tt):
    nb, nt, hk = q.shape
    hv = v.shape[-1]
    has_s0 = s0 is not None
    act = lambda w: pl.BlockSpec((nbb, tt, w), lambda b, t: (b, t, 0))
    st = pl.BlockSpec((nbb, GLA_HEADS // 2, LANES, GLA_DV), lambda b, t: (b, 0, 0, 0))
    return pl.pallas_call(
        functools.partial(_gla_kernel, nbb=nbb, tt=tt, has_s0=has_s0),
        grid=(nb // nbb, nt // tt),
        in_specs=[act(hk), act(hk), act(hv), act(hk)] + ([st] if has_s0 else []),
        out_specs=[act(hv), st],
        out_shape=[_sds((nb, nt, hv)), _sds((nb, GLA_HEADS // 2, LANES, GLA_DV))],
        scratch_shapes=[pltpu.VMEM((nbb, GLA_HEADS // 2, LANES, GLA_DV), F32)],
        compiler_params=_cp("parallel", "arbitrary"),
    )(*([q, k, v, la] + ([s0] if has_s0 else [])))


def _gla_gate(o, r, on):
    outs = []
    for h in range(GLA_HEADS):
        sl = slice(GLA_DV * h, GLA_DV * (h + 1))
        outs.append(_rms(o[:, sl], on[:, sl]))
    return jnp.concatenate(outs, axis=-1) * _silu(r)


def _even_mix(og, pooled, pw_ref, ps_ref, wo_ref):
    mixed = [_dot(pooled[g].astype(BF16), pw_ref[g]) for g in range(len(POOL_WINDOWS))]
    mixed = jnp.concatenate(mixed, axis=-1) * ps_ref[...]
    n_o = og.shape[-1]
    return _dot(og.astype(BF16), wo_ref[:n_o, :]) + _dot(mixed.astype(BF16), wo_ref[n_o:, :])


def _even_out_prompt_kernel(x_ref, o_ref, r_ref, u_ref, gt_ref, on_ref, pw_ref, ps_ref, wo_ref,
                            xo_ref, ho_ref, hb, *, tm, p0):
    t = pl.program_id(1)

    @pl.when(t == 0)
    def _():
        hb[0:POOL_PAD, :] = jnp.zeros((POOL_PAD, POOL_WIDTH), F32)

    hb[POOL_PAD:POOL_PAD + tm, :] = u_ref[0]
    block = min(EVEN_BLOCK, tm)
    for q0 in range(0, tm, block):
        rows = slice(q0, q0 + block)
        pos = t * tm + q0 + lax.broadcasted_iota(jnp.int32, (block, 1), 0)
        pooled = []
        for g, w in enumerate(POOL_WINDOWS):
            sl = slice(POOL_GROUP * g, POOL_GROUP * (g + 1))
            first = POOL_PAD + q0 - (w - 1)
            win = hb[first:first + block, sl]
            for j in range(w - 2, -1, -1):
                win = win + hb[POOL_PAD + q0 - j:POOL_PAD + q0 - j + block, sl]
            cnt = jnp.minimum(p0 + pos + 1, w).astype(F32)
            pooled.append(win / cnt - u_ref[0, rows, sl])
        og = _gla_gate(o_ref[0, rows, :], r_ref[0, rows, :], on_ref[...])
        y = _even_mix(og, pooled, pw_ref, ps_ref, wo_ref)
        xo_ref[0, rows, :] = x_ref[0, rows, :] + gt_ref[...] * y
    hb[0:POOL_PAD, :] = hb[tm:tm + POOL_PAD, :]

    @pl.when(t == pl.num_programs(1) - 1)
    def _():
        ho_ref[0] = hb[0:POOL_PAD, :]


def _even_out_sample_kernel(x_ref, o_ref, r_ref, u_ref, hi_ref, gt_ref, on_ref, pw_ref, ps_ref, wo_ref,
                            xo_ref, hn_ref, *, p0):
    steps, nb = u_ref.shape[:2]

    def slab(i, sl):
        return hi_ref[i, :, sl] if i < POOL_HIST else u_ref[i - POOL_HIST, :, sl]

    pooled = []
    for g, w in enumerate(POOL_WINDOWS):
        sl = slice(POOL_GROUP * g, POOL_GROUP * (g + 1))
        rows = []
        for t in range(steps):
            win = slab(POOL_HIST + t - (w - 1), sl)
            for j in range(w - 2, -1, -1):
                win = win + slab(POOL_HIST + t - j, sl)
            rows.append(win / float(min(p0 + t + 1, w)) - u_ref[t, :, sl])
        pooled.append(jnp.concatenate(rows, axis=0))
    flat = lambda ref: ref[...].reshape(steps * nb, ref.shape[-1])
    og = _gla_gate(flat(o_ref), flat(r_ref), on_ref[...])
    x = x_ref[...]
    xo_ref[...] = x + gt_ref[...] * _even_mix(og, pooled, pw_ref, ps_ref, wo_ref).reshape(x.shape)
    for i in range(POOL_HIST):
        hn_ref[i] = slab(steps + i, slice(None))


def _even_out(grp, x, o, r, u, hist, mod, on, pw, ps, wo, p0):
    dm = x.shape[-1]
    common = [mod, on, pw, ps, wo]
    common_specs = [grp.mod(5), _full(on), _full(pw), _full(ps), _full(wo)]
    acts = [grp.act(dm), grp.act(o.shape[-1]), grp.act(r.shape[-1]), grp.act(u.shape[-1])]
    if grp.prompt:
        tm = grp.block[1]
        return pl.pallas_call(
            functools.partial(_even_out_prompt_kernel, tm=tm, p0=p0),
            grid=grp.grid,
            in_specs=acts + common_specs,
            out_specs=[grp.act(dm), pl.BlockSpec((1, POOL_PAD, POOL_WIDTH), lambda b, t: (b, 0, 0))],
            out_shape=[_sds(x.shape), _sds((x.shape[0], POOL_PAD, POOL_WIDTH))],
            scratch_shapes=[pltpu.VMEM((POOL_PAD + tm, POOL_WIDTH), F32)],
            compiler_params=_cp("parallel", "arbitrary"),
        )(x, o, r, u, *common)
    return pl.pallas_call(
        functools.partial(_even_out_sample_kernel, p0=p0),
        grid=grp.grid,
        in_specs=acts + [_full(hist)] + common_specs,
        out_specs=[grp.act(dm), _full(hist)],
        out_shape=[_sds(x.shape), _sds(hist.shape)],
        compiler_params=_cp("parallel", "arbitrary"),
    )(x, o, r, u, hist, *common)


def _rope_slab(x, cs, s1, s2, lead):
    shp = lead + (LANES,)
    back = pltpu.roll(x, LANES - MLA_ROPE // 2, 1).reshape(shp)
    fwd = pltpu.roll(x, MLA_ROPE // 2, 1).reshape(shp)
    out = x.reshape(shp) * cs + back * s1 + fwd * s2
    return out.reshape(x.shape)


def _odd_in_kernel(*refs, sample):
    (x_ref, sh_ref, sc_ref, g_ref, wcq_ref, wckv_ref, wkr_ref, wga_ref, wgg_ref, qn_ref, wuq_ref,
     kvn_ref, cs_ref, s1_ref, s2_ref) = refs[:15]
    if sample:
        wkl_ref, ckv_ref, kr_ref, uc_ref, q_ref, ql_ref = refs[15:]
    else:
        wuk_ref, wuvt_ref, ckv_ref, kr_ref, uc_ref, q_ref, k_ref, vt_ref = refs[15:]
    lead = x_ref.shape[:2]
    h = _modulated(x_ref, g_ref, sh_ref, sc_ref)
    cs, s1, s2 = cs_ref[...], s1_ref[...], s2_ref[...]

    cq = _rms(_dot(h, wcq_ref[...]), qn_ref[...]).astype(BF16)
    q = _dot(cq, wuq_ref[...]) * (MLA_SCALE * LOG2E)
    q = jnp.concatenate(
        [_rope_slab(q[:, LANES * i:LANES * (i + 1)], cs, s1, s2, lead) for i in range(MLA_HEADS)], axis=-1)
    q_ref[...] = q.astype(BF16).reshape(lead + (MLA_HEADS * LANES,))

    ckv = _rms(_dot(h, wckv_ref[...]), kvn_ref[...])
    ckv_ref[...] = ckv.reshape(lead + (MLA_KV_RANK,))
    kr = _rope_slab(_dot(h, wkr_ref[...]), cs, s1, s2, lead)
    kr_ref[...] = kr.reshape(lead + (LANES,))
    uc_ref[...] = (_dot(h, wga_ref[...]) * jax.nn.sigmoid(_dot(h, wgg_ref[...]))).reshape(lead + (CONV_CH,))

    if sample:
        qb = q.astype(BF16)
        for i in range(MLA_HEADS):
            ql = _dot(qb[:, LANES * i:LANES * (i + 1)], wkl_ref[i])
            ql_ref[:, :, MLA_KV_RANK * i:MLA_KV_RANK * (i + 1)] = ql.astype(BF16).reshape(lead + (MLA_KV_RANK,))
    else:
        cb = ckv.astype(BF16)
        kn = _dot(cb, wuk_ref[...])
        kn = jnp.concatenate([kn[:, LANES * i:LANES * (i + 1)] + kr for i in range(MLA_HEADS)], axis=-1)
        k_ref[...] = kn.astype(BF16).reshape(lead + (MLA_HEADS * LANES,))
        one_row = lax.broadcasted_iota(jnp.int32, (MLA_HEADS, VT_ROWS, 1), 1) == MLA_V
        ones = jnp.where(one_row, 1.0, 0.0).reshape(MLA_HEADS * VT_ROWS, 1)
        vt_ref[0] = (_dot_nt(wuvt_ref[...], cb) + ones).astype(BF16)


def _odd_in(grp, x, mod, g, w, tabs, sample):
    lead = x.shape[:2]
    hl = MLA_HEADS * LANES
    ins = [x, mod, mod, g, w['wcq'], w['wckv'], w['wkr'], w['wga'], w['wgg'], w['qn'], w['wuq'], w['kvn'], *tabs]
    specs = [grp.act(x.shape[-1]), grp.mod(3), grp.mod(4)] + [_full(a) for a in ins[3:12]] + [grp.pos()] * 3
    outs = [(MLA_KV_RANK, F32), (LANES, F32), (CONV_CH, F32), (hl, BF16)]
    if sample:
        ins.append(w['wkl'])
        outs.append((MLA_HEADS * MLA_KV_RANK, BF16))
    else:
        ins += [w['wuk'], w['wuvt']]
        outs.append((hl, BF16))
    specs += [_full(a) for a in ins[15:]]
    out_specs = [grp.act(wd) for wd, _ in outs]
    out_shape = [_sds(lead + (wd,), dt) for wd, dt in outs]
    if not sample:
        vt_rows = MLA_HEADS * VT_ROWS
        out_specs.append(pl.BlockSpec((1, vt_rows, grp.block[1]), lambda b, t: (b, 0, t)))
        out_shape.append(_sds((lead[0], vt_rows, lead[1]), BF16))
    return pl.pallas_call(
        functools.partial(_odd_in_kernel, sample=sample),
        grid=grp.grid,
        in_specs=specs,
        out_specs=out_specs,
        out_shape=out_shape,
        compiler_params=_cp("parallel", "parallel"),
    )(*ins)


def _attn_prompt_kernel(qi_ref, ki_ref, q_ref, k_ref, vt_ref, o_ref, m_scr, acc_scr, s_scr, p_scr, *, tq):
    step = pl.program_id(1)
    qi = qi_ref[step]
    ki = ki_ref[step]

    @pl.when(ki == 0)
    def _():
        m_scr[...] = jnp.full_like(m_scr, NEG)
        acc_scr[...] = jnp.zeros_like(acc_scr)

    def update(diagonal):
        width = s_scr.shape[2]
        items = [(h, c) for h in range(MLA_HEADS) for c in range(0, tq, width)]

        def keys_used(c):
            return c + width if diagonal else tq

        def scores(item, buf):
            h, c = item
            nk = keys_used(c)
            sl = slice(LANES * h, LANES * (h + 1))
            st = _dot_nt(k_ref[0, :nk, sl], q_ref[0, c:c + width, sl])
            if diagonal:
                key = lax.broadcasted_iota(jnp.int32, (nk, width), 0)
                qry = lax.broadcasted_iota(jnp.int32, (nk, width), 1) + c
                st = jnp.where(key <= qry, st, NEG)
            s_scr[buf, :nk] = st

        def absorb(item, buf):
            h, c = item
            nk = keys_used(c)
            vr = slice(VT_ROWS * h, VT_ROWS * (h + 1))
            qc = slice(c, c + width)
            m_prev = m_scr[h:h + 1, qc]
            m_new = jnp.maximum(m_prev, jnp.max(s_scr[buf, :nk], axis=0, keepdims=True))
            p_scr[buf, :nk] = jnp.exp2(s_scr[buf, :nk] - m_new).astype(BF16)
            acc_scr[vr, qc] = (jnp.exp2(m_prev - m_new) * acc_scr[vr, qc]
                               + _dot(vt_ref[0, vr, :nk], p_scr[buf, :nk]))
            m_scr[h:h + 1, qc] = m_new

        depth = s_scr.shape[0]
        for i in range(depth - 1):
            scores(items[i], i)
        for i, item in enumerate(items):
            if i + depth - 1 < len(items):
                scores(items[i + depth - 1], (i + depth - 1) % depth)
            absorb(item, i % depth)

    @pl.when(ki < qi)
    def _():
        update(False)

    @pl.when(ki == qi)
    def _():
        update(True)
        for h in range(0, MLA_HEADS, 2):
            pair = []
            for hh in (h, h + 1):
                acc = acc_scr[VT_ROWS * hh:VT_ROWS * (hh + 1), :]
                pair.append(acc[:MLA_V] / acc[MLA_V:MLA_V + 1])
            o_ref[0, :, MLA_V * h:MLA_V * (h + 2)] = jnp.transpose(jnp.concatenate(pair, axis=0)).astype(BF16)


def _attn_prompt(q, k, vt, tq):
    nb, nt, hl = q.shape
    vt_rows = vt.shape[1]
    n_out = MLA_HEADS * MLA_V
    pairs = [(i, j) for i in range(nt // tq) for j in range(i + 1)]
    qi_tab = jnp.asarray([p[0] for p in pairs], jnp.int32)
    ki_tab = jnp.asarray([p[1] for p in pairs], jnp.int32)
    return pl.pallas_call(
        functools.partial(_attn_prompt_kernel, tq=tq),
        grid_spec=pltpu.PrefetchScalarGridSpec(
            num_scalar_prefetch=2,
            grid=(nb, len(pairs)),
            in_specs=[pl.BlockSpec((1, tq, hl), lambda b, s, qt, kt: (b, qt[s], 0)),
                      pl.BlockSpec((1, tq, hl), lambda b, s, qt, kt: (b, kt[s], 0)),
                      pl.BlockSpec((1, vt_rows, tq), lambda b, s, qt, kt: (b, 0, kt[s]))],
            out_specs=pl.BlockSpec((1, tq, n_out), lambda b, s, qt, kt: (b, qt[s], 0)),
            scratch_shapes=[pltpu.VMEM((MLA_HEADS, tq), F32), pltpu.VMEM((vt_rows, tq), F32),
                            pltpu.VMEM((ATTN_DEPTH, tq, min(ATTN_QCHUNK, tq)), F32),
                            pltpu.VMEM((ATTN_DEPTH, tq, min(ATTN_QCHUNK, tq)), BF16)]),
        out_shape=_sds((nb, nt, n_out), BF16),
        compiler_params=_cp("parallel", "arbitrary"),
    )(qi_tab, ki_tab, q, k, vt)


def _attn_sample_kernel(pt_ref, ql_ref, qr_ref, cn_ref, kn_ref, ckv_hbm, kr_hbm, o_ref,
                        ckv_buf, kr_buf, pg_scr, s_scr, p_scr, sem, *, npg, layer):
    b = pl.program_id(0)
    nb = pl.num_programs(0)
    slot = lax.rem(b, 2)
    ql = ql_ref[0]
    qr = qr_ref[0]
    rows = ql.shape[0]

    def page_copies(page, i, sl):
        return (pltpu.make_async_copy(ckv_hbm.at[layer, page], ckv_buf.at[sl, i], sem.at[0, sl]),
                pltpu.make_async_copy(kr_hbm.at[layer, page], kr_buf.at[sl, i], sem.at[1, sl]))

    def start_page(batch, i, sl):
        for cp in page_copies(pt_ref[batch, i], i, sl):
            cp.start()

    def start_first(i, carry):
        start_page(0, i, 0)
        return carry

    @pl.when(b == 0)
    def _():
        lax.fori_loop(0, npg, start_first, 0)

    for i in range(npg):
        for cp in page_copies(0, i, slot):
            cp.wait()

    def score_pages(prefetch):
        for i in range(npg):
            if prefetch:
                for k in range(2 * i, min(2 * i + 2, npg)):
                    start_page(b + 1, k, 1 - slot)
            page = ckv_buf[slot, i].astype(BF16)
            pg_scr[i] = page
            s_scr[i] = _dot_nt(ql, page) + _dot(qr, kr_buf[slot, i].astype(BF16))

    @pl.when(b + 1 < nb)
    def _():
        score_pages(True)

    @pl.when(b + 1 == nb)
    def _():
        score_pages(False)

    n_new = cn_ref.shape[1]
    pad = jnp.zeros((LANES - n_new, MLA_KV_RANK), F32)
    cn = jnp.concatenate([cn_ref[0], pad], axis=0).astype(BF16)
    kn = jnp.concatenate([kn_ref[0], pad[:, :MLA_ROPE]], axis=0).astype(BF16)
    step = lax.shift_right_logical(lax.broadcasted_iota(jnp.int32, (rows, LANES), 0), MLA_HEADS.bit_length() - 1)
    col = lax.broadcasted_iota(jnp.int32, (rows, LANES), 1)
    s_new = jnp.where(col <= step, _dot_nt(ql, cn) + _dot_nt(qr, kn), NEG)

    s = s_scr[...]
    m = jnp.maximum(jnp.max(jnp.max(s, axis=0), axis=-1, keepdims=True),
                    jnp.max(s_new, axis=-1, keepdims=True))
    p = jnp.exp2(s - m)
    p_new = jnp.exp2(s_new - m)
    denom = jnp.sum(jnp.sum(p, axis=0), axis=-1, keepdims=True) + jnp.sum(p_new, axis=-1, keepdims=True)
    p_scr[...] = p.astype(BF16)

    def weigh_page(i, acc):
        return acc + _dot(p_scr[i], pg_scr[i])

    acc = lax.fori_loop(0, npg, weigh_page, _dot(p_new.astype(BF16), cn), unroll=True)
    o_ref[0] = acc / denom


def _attn_sample(page_table, ql, qr, cn, kn, cache_ckv, cache_kr, layer):
    nb, rows, rank = ql.shape
    npg = page_table.shape[1]
    page, rope = cache_ckv.shape[2], cache_kr.shape[2]
    assert page == LANES, "one cache page must fill one lane tile of scores"
    per_b = lambda a: pl.BlockSpec((1,) + a.shape[1:], lambda b, pt: (b, 0, 0))
    hbm = pl.BlockSpec(memory_space=pl.ANY)
    return pl.pallas_call(
        functools.partial(_attn_sample_kernel, npg=npg, layer=layer),
        grid_spec=pltpu.PrefetchScalarGridSpec(
            num_scalar_prefetch=1,
            grid=(nb,),
            in_specs=[per_b(ql), per_b(qr), per_b(cn), per_b(kn), hbm, hbm],
            out_specs=pl.BlockSpec((1, rows, rank), lambda b, pt: (b, 0, 0)),
            scratch_shapes=[pltpu.VMEM((2, npg, page, rank), F32),
                            pltpu.VMEM((2, npg, rope, page), F32),
                            pltpu.VMEM((npg, page, rank), BF16),
                            pltpu.VMEM((npg, rows, page), F32),
                            pltpu.VMEM((npg, rows, page), BF16),
                            pltpu.SemaphoreType.DMA((2, 2))]),
        out_shape=_sds((nb, rows, rank)),
        compiler_params=_cp("arbitrary"),
    )(page_table, ql, qr, cn, kn, cache_ckv, cache_kr)


def _conv_norm_act(cv, cb_ref, lg_ref, lb_ref):
    cv = cv + cb_ref[...]
    mu = jnp.mean(cv, axis=-1, keepdims=True)
    d = cv - mu
    y = d * lax.rsqrt(jnp.mean(d * d, axis=-1, keepdims=True) + EPS)
    return _silu(y * lg_ref[...] + lb_ref[...])


def _odd_out_prompt_kernel(x_ref, a_ref, uc_ref, gt_ref, cw_ref, cb_ref, lg_ref, lb_ref, woa_ref, woc_ref,
                           xo_ref, ho_ref, hb, hs, cvb, *, tm):
    t = pl.program_id(1)

    @pl.when(t == 0)
    def _():
        hb[0:CONV_PAD, :] = jnp.zeros((CONV_PAD, CONV_CH), F32)

    hb[CONV_PAD:CONV_PAD + tm, :] = uc_ref[0]
    for sft in range(1, SUBLANES):
        hs[sft - 1] = hb[sft:sft + hs.shape[1], :]
    base = CONV_PAD - CONV_HIST

    block = min(CONV_BLOCK, tm)
    for q0 in range(0, tm, block):
        ya = _dot(a_ref[0, q0:q0 + block, :], woa_ref[...])
        for r0 in range(q0, q0 + block, CONV_ROWS):
            acc = None
            for j in range(CONV_WIDTH):
                whole, sft = divmod(base + j, SUBLANES)
                rows = slice(r0 + whole * SUBLANES, r0 + whole * SUBLANES + CONV_ROWS)
                term = cw_ref[j:j + 1, :] * (hb[rows, :] if sft == 0 else hs[sft - 1, rows, :])
                acc = term if acc is None else acc + term
            cvb[r0:r0 + CONV_ROWS, :] = acc
        cv = _conv_norm_act(cvb[q0:q0 + block, :], cb_ref, lg_ref, lb_ref)
        y = ya + _dot(cv.astype(BF16), woc_ref[...])
        xo_ref[0, q0:q0 + block, :] = x_ref[0, q0:q0 + block, :] + gt_ref[...] * y
    hb[0:CONV_PAD, :] = hb[tm:tm + CONV_PAD, :]

    @pl.when(t == pl.num_programs(1) - 1)
    def _():
        ho_ref[0] = hb[0:CONV_PAD, :]


def _odd_out_sample_kernel(x_ref, lat_ref, uc_ref, hi_ref, gt_ref, cw_ref, cb_ref, lg_ref, lb_ref, wuv_ref,
                           woa_ref, woc_ref, xo_ref, hn_ref):
    steps, nb = uc_ref.shape[:2]

    def slab(i):
        return hi_ref[i] if i < CONV_HIST else uc_ref[i - CONV_HIST]

    rows = []
    for t in range(steps):
        cv = cw_ref[0:1, :] * slab(t)
        for j in range(1, CONV_WIDTH):
            cv = cv + cw_ref[j:j + 1, :] * slab(t + j)
        rows.append(cv)
    cv = _conv_norm_act(jnp.concatenate(rows, axis=0), cb_ref, lg_ref, lb_ref)
    lat = lat_ref[...].reshape(steps * nb, lat_ref.shape[-1]).astype(BF16)
    attn = _dot(lat, wuv_ref[...]).astype(BF16)
    y = _dot(attn, woa_ref[...]) + _dot(cv.astype(BF16), woc_ref[...])
    x = x_ref[...]
    xo_ref[...] = x + gt_ref[...] * y.reshape(x.shape)
    for i in range(CONV_HIST):
        hn_ref[i] = slab(steps + i)


def _odd_out(grp, x, a, uc, hist, mod, w):
    dm = x.shape[-1]
    conv = [w['cw'], w['cb'], w['lg'], w['lb']]
    if grp.prompt:
        tm = grp.block[1]
        ins = [x, a, uc, mod] + conv + [w['woa'], w['woc']]
        return pl.pallas_call(
            functools.partial(_odd_out_prompt_kernel, tm=tm),
            grid=grp.grid,
            in_specs=[grp.act(dm), grp.act(a.shape[-1]), grp.act(CONV_CH), grp.mod(5)] + [_full(v) for v in ins[4:]],
            out_specs=[grp.act(dm), pl.BlockSpec((1, CONV_PAD, CONV_CH), lambda b, t: (b, 0, 0))],
            out_shape=[_sds(x.shape), _sds((x.shape[0], CONV_PAD, CONV_CH))],
            scratch_shapes=[pltpu.VMEM((CONV_PAD + tm, CONV_CH), F32),
                            pltpu.VMEM((SUBLANES - 1, CONV_PAD + tm - SUBLANES, CONV_CH), F32),
                            pltpu.VMEM((tm, CONV_CH), F32)],
            compiler_params=_cp("parallel", "arbitrary"),
        )(*ins)
    ins = [x, a, uc, hist, mod] + conv + [w['wuv_bd'], w['woa'], w['woc']]
    return pl.pallas_call(
        _odd_out_sample_kernel,
        grid=grp.grid,
        in_specs=[grp.act(dm), grp.act(a.shape[-1]), grp.act(CONV_CH), _full(hist), grp.mod(5)]
        + [_full(v) for v in ins[5:]],
        out_specs=[grp.act(dm), _full(hist)],
        out_shape=[_sds(x.shape), _sds(hist.shape)],
        compiler_params=_cp("parallel", "arbitrary"),
    )(*ins)


def _head_pad(w, heads, width, offset=0):
    kdim = w.shape[0]
    w = w.reshape(kdim, heads, width)
    w = jnp.pad(w, ((0, 0), (0, 0), (offset, LANES - width - offset)))
    return w.reshape(kdim, heads * LANES)


def _rope_tables(pos):
    half = MLA_ROPE // 2
    freqs = ROPE_BASE ** (-np.arange(half, dtype=np.float64) / half)
    ang = np.asarray(pos, np.float64)[..., None] * freqs
    cos, sin = jnp.asarray(np.cos(ang), F32), jnp.asarray(np.sin(ang), F32)
    zeros = jnp.zeros_like(cos)
    lead = jnp.ones(pos.shape + (ROPE_LANE0,), F32)
    tail = jnp.zeros(pos.shape + (LANES - ROPE_LANE0 - MLA_ROPE,), F32)
    cs = jnp.concatenate([lead, cos, cos, tail], axis=-1)
    s1 = jnp.concatenate([0 * lead, -sin, zeros, tail], axis=-1)
    s2 = jnp.concatenate([0 * lead, zeros, sin, tail], axis=-1)
    return cs, s1, s2


def _even_weights(w_in, gate_w2, gate_b, out_norm, pool_w, pool_scale, w_out):
    hk = GLA_HEADS * GLA_DK
    hv = GLA_HEADS * GLA_DV
    n_main = 2 * hk + 2 * hv
    wg = jnp.pad(w_in[:, n_main:n_main + GLA_GATE_RANK], ((0, 0), (0, LANES - GLA_GATE_RANK)))
    w2 = jnp.pad(gate_w2, ((0, LANES - GLA_GATE_RANK), (0, 0)))
    return dict(wq=w_in[:, :n_main].astype(BF16), wg=wg.astype(BF16), w2=w2.astype(BF16),
                gb=gate_b[None], wu=w_in[:, n_main + GLA_GATE_RANK:].astype(BF16),
                on=out_norm[None], pw=pool_w.astype(BF16), ps=pool_scale[None], wo=w_out.astype(BF16))


def _odd_weights(w_in, q_norm, w_uq, kv_norm, w_uk, w_uv, conv_w, conv_b, ln_g, ln_b, w_out):
    c0, c1, c2 = MLA_Q_RANK, MLA_Q_RANK + MLA_KV_RANK, MLA_Q_RANK + MLA_KV_RANK + MLA_ROPE
    n_attn = MLA_HEADS * MLA_V
    wkr = jnp.pad(w_in[:, c1:c2], ((0, 0), (ROPE_LANE0, LANES - ROPE_LANE0 - MLA_ROPE)))
    wuk = w_uk.reshape(MLA_KV_RANK, MLA_HEADS * MLA_NOPE)
    wkl = jnp.pad(jnp.transpose(w_uk, (1, 2, 0)), ((0, 0), (0, LANES - MLA_NOPE), (0, 0)))
    eye = jnp.eye(MLA_HEADS, dtype=F32)
    wuv_bd = (eye[:, None, :, None] * jnp.transpose(w_uv, (1, 0, 2))[:, :, None, :]).reshape(
        MLA_HEADS * MLA_KV_RANK, n_attn)
    wuvt = jnp.pad(jnp.transpose(w_uv, (1, 2, 0)), ((0, 0), (0, VT_ROWS - MLA_V), (0, 0))).reshape(
        MLA_HEADS * VT_ROWS, MLA_KV_RANK)
    return dict(wcq=w_in[:, :c0].astype(BF16), wckv=w_in[:, c0:c1].astype(BF16), wkr=wkr.astype(BF16),
                wga=w_in[:, c2:c2 + CONV_CH].astype(BF16), wgg=w_in[:, c2 + CONV_CH:].astype(BF16),
                qn=q_norm[None], wuq=_head_pad(w_uq, MLA_HEADS, MLA_NOPE + MLA_ROPE).astype(BF16),
                kvn=kv_norm[None], wuk=_head_pad(wuk, MLA_HEADS, MLA_NOPE).astype(BF16),
                wuvt=wuvt.astype(BF16), wkl=wkl.astype(BF16),
                wuv_bd=wuv_bd.astype(BF16), cw=conv_w, cb=conv_b[None], lg=ln_g[None], lb=ln_b[None],
                woa=w_out[:n_attn].astype(BF16), woc=w_out[n_attn:].astype(BF16))


def _tm(x):
    return jnp.swapaxes(x, 0, 1)


def kernel(x_prompt, x_sample, state_gla, state_pool, cache_ckv, cache_krope, state_conv, page_table, c_prompt, c_sample, ada_w, ada_b, norm_g, ffn_w1, ffn_w3, ffn_w2, ev_w_in, ev_gate_w2, ev_gate_b, ev_out_norm, ev_pool_w, ev_pool_scale, ev_w_out, od_w_in, od_q_norm, od_w_uq, od_kv_norm, od_w_uk, od_w_uv, od_conv_w, od_conv_b, od_conv_norm_g, od_conv_norm_b, od_w_out, final_norm):
    nbp, seq, dm = x_prompt.shape
    nbs, steps, _ = x_sample.shape
    depth = ada_w.shape[0]
    past_len = page_table.shape[1] * cache_ckv.shape[2]
    tile = min(TOKEN_TILE, seq)
    gp = _Group(True, nbp, seq, tile)
    gs = _Group(False, steps, nbs, nbs)

    n_c = nbp + nbs
    c_all = jnp.pad(jnp.concatenate([c_prompt, c_sample], axis=0), ((0, -n_c % 8), (0, 0)))
    mod = _ada(c_all, ada_w, ada_b)
    mod_p = mod[:, :nbp].reshape(depth, nbp, N_MOD, 1, dm)
    mod_s = jnp.swapaxes(mod[:, nbp:n_c].reshape(depth, nbs, N_MOD, dm), 1, 2)

    w1, w3, w2 = ffn_w1.astype(BF16), ffn_w3.astype(BF16), ffn_w2.astype(BF16)
    tabs_p = _rope_tables(np.arange(seq)[None])
    tabs_s = _rope_tables(past_len + np.arange(steps)[:, None])

    xp = x_prompt
    xs = _tm(x_sample)
    gla_p, gla_s, pool_p, pool_s, ckv_p, ckv_s, kr_p, kr_s, conv_p, conv_s = ([] for _ in range(10))
    gla_tt = min(TOKEN_TILE, seq)

    for layer in range(depth):
        i = layer // 2
        ng = norm_g[layer]
        last = layer == depth - 1
        xp, xs = _ffn(xp, xs, mod_p[layer], mod_s[layer], 0, ng[0:1], w1, w3, w2, (layer, 0))
        if layer % 2 == 0:
            w = _even_weights(ev_w_in[i], ev_gate_w2[i], ev_gate_b[i], ev_out_norm[i], ev_pool_w[i],
                              ev_pool_scale[i], ev_w_out[i])
            proj = (ng[1:2], w['wq'], w['wg'], w['w2'], w['gb'], w['wu'])
            out_w = (w['on'], w['pw'], w['ps'], w['wo'])
            q, k, v, r, la, u = _even_in(gp, xp, mod_p[layer], *proj)
            o, s_fin = _gla(q, k, v, la, None, nbp, gla_tt)
            xp, hist = _even_out(gp, xp, o, r, u, None, mod_p[layer], *out_w, 0)
            gla_p.append(s_fin.reshape(nbp, GLA_HEADS, GLA_DK, GLA_DV))
            pool_p.append(hist[:, POOL_PAD - POOL_HIST:])
            q, k, v, r, la, u = _even_in(gs, xs, mod_s[layer], *proj)
            rows8 = steps + (-steps % SUBLANES)
            chunked = lambda a: jnp.pad(_tm(a), ((0, 0), (0, rows8 - steps), (0, 0)))
            s0 = state_gla[i].reshape(nbs, GLA_HEADS // 2, LANES, GLA_DV)
            o, s_fin = _gla(chunked(q), chunked(k), chunked(v), chunked(la), s0,
                            GLA_SAMPLE_BATCHES if nbs % GLA_SAMPLE_BATCHES == 0 else 1, rows8)
            xs, hist = _even_out(gs, xs, _tm(o[:, :steps]), r, u, _tm(state_pool[i]), mod_s[layer], *out_w, past_len)
            gla_s.append(s_fin.reshape(nbs, GLA_HEADS, GLA_DK, GLA_DV))
            pool_s.append(_tm(hist))
        else:
            w = _odd_weights(od_w_in[i], od_q_norm[i], od_w_uq[i], od_kv_norm[i], od_w_uk[i], od_w_uv[i],
                             od_conv_w[i], od_conv_b[i], od_conv_norm_g[i], od_conv_norm_b[i], od_w_out[i])
            rope_lanes = slice(ROPE_LANE0, ROPE_LANE0 + MLA_ROPE)
            ckv, kr, uc, q, k, v = _odd_in(gp, xp, mod_p[layer], ng[1:2], w, tabs_p, False)
            attn = _attn_prompt(q, k, v, min(ATTN_TILE, seq))
            xp, hist = _odd_out(gp, xp, attn, uc, None, mod_p[layer], w)
            ckv_p.append(ckv)
            kr_p.append(kr[..., rope_lanes])
            conv_p.append(hist[:, CONV_PAD - CONV_HIST:])
            ckv, kr, uc, q, ql = _odd_in(gs, xs, mod_s[layer], ng[1:2], w, tabs_s, True)
            kr = kr[..., rope_lanes]
            qr = q.reshape(steps, nbs, MLA_HEADS, LANES)[..., rope_lanes]
            qr = _tm(qr).reshape(nbs, steps * MLA_HEADS, MLA_ROPE)
            ql = _tm(ql).reshape(nbs, steps * MLA_HEADS, MLA_KV_RANK)
            pad8 = lambda a: jnp.pad(_tm(a), ((0, 0), (0, -steps % 8), (0, 0)))
            cache_kr_t = jnp.swapaxes(cache_krope, 2, 3)
            lat = _attn_sample(page_table, ql, qr, pad8(ckv), pad8(kr), cache_ckv, cache_kr_t, i)
            lat = _tm(lat.reshape(nbs, steps, MLA_HEADS * MLA_KV_RANK))
            xs, hist = _odd_out(gs, xs, lat, uc, _tm(state_conv[i]), mod_s[layer], w)
            ckv_s.append(_tm(ckv))
            kr_s.append(_tm(kr))
            conv_s.append(_tm(hist))
        fin = final_norm[None] if last else None
        xp, xs = _ffn(xp, xs, mod_p[layer], mod_s[layer], 6, ng[2:3], w1, w3, w2, (layer, 1), fin)

    st = jnp.stack
    return (xp, _tm(xs), st(gla_p), st(gla_s), st(pool_p), st(pool_s), st(ckv_p), st(ckv_s),
            st(kr_p), st(kr_s), st(conv_p), st(conv_s))
```

```python
import functools

import jax
import jax.numpy as jnp
import numpy as np
from jax import lax
from jax.experimental import pallas as pl
from jax.experimental.pallas import tpu as pltpu

F32 = jnp.float32
BF16 = jnp.bfloat16

EPS = 1e-6
NEG = -1e30
N_MOD = 9
GLA_HEADS = 4
GLA_DK = 64
GLA_DV = 128
GLA_GATE_RANK = 16
GLA_TAU = 16.0
GLA_CHUNK = 64
POOL_WINDOWS = (2, 4, 8, 16)
POOL_GROUP = 128
POOL_WIDTH = 512
POOL_HIST = 15
MLA_HEADS = 8
MLA_Q_RANK = 384
MLA_KV_RANK = 256
MLA_NOPE = 64
MLA_ROPE = 32
MLA_V = 64
MLA_SCALE = (MLA_NOPE + MLA_ROPE) ** -0.5
LOG2E = 1.4426950408889634
ROPE_BASE = 10000.0
CONV_WIDTH = 31
CONV_CH = 512
CONV_HIST = 30

LANES = 128
SUBLANES = 8
CONV_ROWS = 32
CONV_BLOCK = 256
EVEN_BLOCK = 256
ROPE_LANE0 = 64
VT_ROWS = 80
POOL_PAD = 16
CONV_PAD = 32
VMEM_LIMIT = 52 * 2 ** 20
TOKEN_TILE = 1024
ATTN_TILE = 1024
ATTN_KEYS = 1024
ATTN_DEPTH = 3
ATTN_QCHUNK = 512
FFN_TOKEN_TILE = 1024
FF_SUB = 1024
GLA_SAMPLE_BATCHES = 8


def _cp(*sem):
    return pltpu.CompilerParams(dimension_semantics=sem, vmem_limit_bytes=VMEM_LIMIT)


def _dot(a, b):
    return jnp.dot(a, b, preferred_element_type=F32)


def _dot_nt(a, b):
    return lax.dot_general(a, b, (((1,), (1,)), ((), ())), preferred_element_type=F32)


def _dot_tn(a, b):
    return lax.dot_general(a, b, (((0,), (0,)), ((), ())), preferred_element_type=F32)


def _silu(x):
    return x * jax.nn.sigmoid(x)


def _rms(x, g):
    return x * lax.rsqrt(jnp.mean(x * x, axis=-1, keepdims=True) + EPS) * g


def _modulated(x_ref, g_ref, sh_ref, sc_ref):
    x = x_ref[...]
    h = _rms(x, g_ref[...]) * (1.0 + sc_ref[...]) + sh_ref[...]
    return h.reshape(x.shape[0] * x.shape[1], x.shape[2]).astype(BF16)


class _Group:
    def __init__(self, prompt, lead, rows, tile):
        self.prompt = prompt
        self.grid = (lead, rows // tile) if prompt else (1, 1)
        self.block = (1, tile) if prompt else (lead, rows)

    def act(self, width):
        return pl.BlockSpec(self.block + (width,), lambda b, t, *_: (b, t, 0))

    def mod(self, k):
        if self.prompt:
            return pl.BlockSpec((None, None, 1, self.dm), lambda b, t, *_: (k, b, 0, 0))
        return pl.BlockSpec((None, self.block[1], self.dm), lambda b, t, *_: (k, 0, 0))

    def pos(self):
        if self.prompt:
            return pl.BlockSpec((1, self.block[1], LANES), lambda b, t, *_: (0, t, 0))
        return pl.BlockSpec((self.block[0], 1, LANES), lambda b, t, *_: (0, 0, 0))

    dm = 1024


def _full(a):
    nd = a.ndim
    return pl.BlockSpec(a.shape, lambda *_: (0,) * nd)


def _sds(shape, dtype=F32):
    return jax.ShapeDtypeStruct(shape, dtype)


def _ada_kernel(c_ref, w_ref, b_ref, o_ref):
    c = c_ref[...]
    o_ref[...] = _dot(_silu(c).astype(BF16), w_ref[...].astype(BF16)) + b_ref[...]


def _ada(c_all, ada_w, ada_b):
    depth, dm, n = ada_w.shape
    m = c_all.shape[0]
    tn = dm
    return pl.pallas_call(
        _ada_kernel,
        grid=(depth, n // tn),
        in_specs=[pl.BlockSpec((m, dm), lambda l, j: (0, 0)),
                  pl.BlockSpec((None, dm, tn), lambda l, j: (l, 0, j)),
                  pl.BlockSpec((None, 1, tn), lambda l, j: (l, 0, j))],
        out_specs=pl.BlockSpec((None, None, m, tn), lambda l, j: (l, j, 0, 0)),
        out_shape=_sds((depth, n // tn, m, tn)),
        compiler_params=_cp("parallel", "parallel"),
    )(c_all, ada_w, ada_b.reshape(depth, 1, n))


def _ffn_kernel(*refs, final):
    if final:
        x_ref, sh_ref, sc_ref, gt_ref, g_ref, w1_ref, w3_ref, w2_ref, fn_ref, o_ref = refs
    else:
        x_ref, sh_ref, sc_ref, gt_ref, g_ref, w1_ref, w3_ref, w2_ref, o_ref = refs
    h = _modulated(x_ref, g_ref, sh_ref, sc_ref)
    dff = w1_ref.shape[1]
    y = None
    for c0 in range(0, dff, FF_SUB):
        cols = slice(c0, min(c0 + FF_SUB, dff))
        a = _dot(h, w1_ref[:, cols])
        b = _dot(h, w3_ref[:, cols])
        part = _dot((_silu(a) * b).astype(BF16), w2_ref[cols, :])
        y = part if y is None else y + part
    x = x_ref[...]
    xn = x + 0.5 * gt_ref[...] * y.reshape(x.shape)
    o_ref[...] = _rms(xn, fn_ref[...]) if final else xn


def _resident(a, lead):
    idx = tuple(lead) + (0, 0)
    return pl.BlockSpec((None,) * len(lead) + a.shape[-2:], lambda *_: idx, pipeline_mode=pl.Buffered(1))


def _ffn(grp, x, mod, k0, g, w1, w3, w2, which, final_g=None):
    dm = x.shape[-1]
    final = final_g is not None
    ins = [x, mod, mod, mod, g, w1, w3, w2] + ([final_g] if final else [])
    specs = [grp.act(dm), grp.mod(k0), grp.mod(k0 + 1), grp.mod(k0 + 2), _full(g),
             _resident(w1, which), _resident(w3, which), _resident(w2, which)] + ([_full(final_g)] if final else [])
    return pl.pallas_call(
        functools.partial(_ffn_kernel, final=final),
        grid=grp.grid,
        in_specs=specs,
        out_specs=grp.act(dm),
        out_shape=_sds(x.shape),
        compiler_params=_cp("parallel", "parallel"),
    )(*ins)


def _log_sigmoid(x):
    return jnp.minimum(x, 0.0) - jnp.log(1.0 + jnp.exp(-jnp.abs(x)))


def _even_in_kernel(x_ref, sh_ref, sc_ref, g_ref, wq_ref, wg_ref, w2_ref, gb_ref, wu_ref,
                    q_ref, k_ref, v_ref, r_ref, la_ref, u_ref):
    h = _modulated(x_ref, g_ref, sh_ref, sc_ref)
    lead = x_ref.shape[:2]
    hk = GLA_HEADS * GLA_DK
    hv = GLA_HEADS * GLA_DV
    z = _dot(h, wq_ref[...])
    q_ref[...] = (z[:, :hk] * GLA_DK ** -0.5).reshape(lead + (hk,))
    k_ref[...] = z[:, hk:2 * hk].reshape(lead + (hk,))
    v_ref[...] = z[:, 2 * hk:2 * hk + hv].reshape(lead + (hv,))
    r_ref[...] = z[:, 2 * hk + hv:].reshape(lead + (hv,))
    g_low = _dot(h, wg_ref[...]).astype(BF16)
    gate = _dot(g_low, w2_ref[...]) + gb_ref[...]
    la_ref[...] = (_log_sigmoid(gate) / GLA_TAU).reshape(lead + (hk,))
    u_ref[...] = _dot(h, wu_ref[...]).reshape(lead + (POOL_WIDTH,))


def _even_in(grp, x, mod, g, wq, wg, w2, gb, wu):
    hk = GLA_HEADS * GLA_DK
    hv = GLA_HEADS * GLA_DV
    widths = (hk, hk, hv, hv, hk, POOL_WIDTH)
    return pl.pallas_call(
        _even_in_kernel,
        grid=grp.grid,
        in_specs=[grp.act(x.shape[-1]), grp.mod(3), grp.mod(4), _full(g), _full(wq), _full(wg),
                  _full(w2), _full(gb), _full(wu)],
        out_specs=[grp.act(w) for w in widths],
        out_shape=[_sds(x.shape[:2] + (w,)) for w in widths],
        compiler_params=_cp("parallel", "parallel"),
    )(x, mod, mod, g, wq, wg, w2, gb, wu)


def _cumsum_rows(tril, x):
    hi = x.astype(BF16)
    r1 = x - hi.astype(F32)
    mid = r1.astype(BF16)
    lo = (r1 - mid.astype(F32)).astype(BF16)
    return _dot(tril, hi) + _dot(tril, mid) + _dot(tril, lo)


def _gla_kernel(*refs, nbb, tt, has_s0):
    if has_s0:
        q_ref, k_ref, v_ref, la_ref, s0_ref, o_ref, so_ref, s_scr = refs
    else:
        q_ref, k_ref, v_ref, la_ref, o_ref, so_ref, s_scr = refs
    t = pl.program_id(1)
    c_rows = GLA_CHUNK

    @pl.when(t == 0)
    def _():
        s_scr[...] = s0_ref[...] if has_s0 else jnp.zeros_like(s_scr)

    row = lax.broadcasted_iota(jnp.int32, (c_rows, c_rows), 0)
    col = lax.broadcasted_iota(jnp.int32, (c_rows, c_rows), 1)
    causal = col <= row
    tril = jnp.where(causal, 1.0, 0.0).astype(BF16)
    lane = lax.broadcasted_iota(jnp.int32, (c_rows, LANES), 1)
    srow = lax.broadcasted_iota(jnp.int32, (LANES, LANES), 0)

    pairs = GLA_HEADS // 2
    combos = [(n, p) for n in range(nbb) for p in range(pairs)]
    heads = [(n, p, hh) for n, p in combos for hh in range(2)]

    def chunk(c, carry):
        short = tt < c_rows
        rows = slice(0, tt) if short else pl.ds(pl.multiple_of(c * c_rows, c_rows), c_rows)

        def take(ref, n, cols):
            x = ref[n, rows, cols]
            return jnp.concatenate([x, jnp.zeros((c_rows - tt, x.shape[1]), F32)], axis=0) if short else x

        wide = lambda ref: jnp.concatenate([take(ref, n, slice(None)) for n in range(nbb)], axis=-1)
        slab = lambda x, n, p: x[:, LANES * (n * pairs + p):LANES * (n * pairs + p + 1)]
        vcols = lambda p, hh: slice(GLA_DV * (2 * p + hh), GLA_DV * (2 * p + hh + 1))
        b = _cumsum_rows(tril, wide(la_ref))
        b_last = b[c_rows - 1:c_rows, :]
        k = wide(k_ref)
        qi = wide(q_ref) * jnp.exp(b)
        ki = (k * jnp.exp(-b)).astype(BF16)
        kd = (k * jnp.exp(b_last - b)).astype(BF16)
        grow = jnp.exp(b_last)
        qh = {(n, p, hh): jnp.where((lane < GLA_DK) if hh == 0 else (lane >= GLA_DK), slab(qi, n, p), 0.0).astype(BF16)
              for n, p, hh in heads}
        att = {(n, p, hh): _dot_nt(qh[n, p, hh], slab(ki, n, p)) for n, p, hh in heads}
        vh = {(n, p, hh): take(v_ref, n, vcols(p, hh)).astype(BF16) for n, p, hh in heads}
        s_old = {(n, p): s_scr[n, p] for n, p in combos}
        from_state = {(n, p, hh): _dot(qh[n, p, hh], s_old[n, p].astype(BF16)) for n, p, hh in heads}
        upd = {(n, p, hh): _dot_tn(slab(kd, n, p), vh[n, p, hh]) for n, p, hh in heads}
        for n, p, hh in heads:
            within = _dot(jnp.where(causal, att[n, p, hh], 0.0).astype(BF16), vh[n, p, hh])
            o_ref[n, rows, vcols(p, hh)] = (from_state[n, p, hh] + within)[:min(tt, c_rows)]
        for n, p in combos:
            decay = jnp.transpose(jnp.broadcast_to(slab(grow, n, p), (LANES, LANES)))
            s_scr[n, p] = decay * s_old[n, p] + jnp.where(srow < GLA_DK, upd[n, p, 0], upd[n, p, 1])
        return carry

    n_chunks = max(1, tt // c_rows)
    lax.fori_loop(0, n_chunks, chunk, 0, unroll=min(2, n_chunks))

    @pl.when(t == pl.num_programs(1) - 1)
    def _():
        so_ref[...] = s_scr[...]


def _gla(q, k, v, la, s0, nbb, tt):
    nb, nt, hk = q.shape
    hv = v.shape[-1]
    has_s0 = s0 is not None
    act = lambda w: pl.BlockSpec((nbb, tt, w), lambda b, t: (b, t, 0))
    st = pl.BlockSpec((nbb, GLA_HEADS // 2, LANES, GLA_DV), lambda b, t: (b, 0, 0, 0))
    return pl.pallas_call(
        functools.partial(_gla_kernel, nbb=nbb, tt=tt, has_s0=has_s0),
        grid=(nb // nbb, nt // tt),
        in_specs=[act(hk), act(hk), act(hv), act(hk)] + ([st] if has_s0 else []),
        out_specs=[act(hv), st],
        out_shape=[_sds((nb, nt, hv)), _sds((nb, GLA_HEADS // 2, LANES, GLA_DV))],
        scratch_shapes=[pltpu.VMEM((nbb, GLA_HEADS // 2, LANES, GLA_DV), F32)],
        compiler_params=_cp("parallel", "arbitrary"),
    )(*([q, k, v, la] + ([s0] if has_s0 else [])))


def _gla_gate(o, r, on):
    outs = []
    for h in range(GLA_HEADS):
        sl = slice(GLA_DV * h, GLA_DV * (h + 1))
        outs.append(_rms(o[:, sl], on[:, sl]))
    return jnp.concatenate(outs, axis=-1) * _silu(r)


def _even_mix(og, pooled, pw_ref, ps_ref, wo_ref):
    mixed = [_dot(pooled[g].astype(BF16), pw_ref[g]) for g in range(len(POOL_WINDOWS))]
    mixed = jnp.concatenate(mixed, axis=-1) * ps_ref[...]
    n_o = og.shape[-1]
    return _dot(og.astype(BF16), wo_ref[:n_o, :]) + _dot(mixed.astype(BF16), wo_ref[n_o:, :])


def _even_out_prompt_kernel(x_ref, o_ref, r_ref, u_ref, gt_ref, on_ref, pw_ref, ps_ref, wo_ref,
                            xo_ref, ho_ref, hb, *, tm, p0):
    t = pl.program_id(1)

    @pl.when(t == 0)
    def _():
        hb[0:POOL_PAD, :] = jnp.zeros((POOL_PAD, POOL_WIDTH), F32)

    hb[POOL_PAD:POOL_PAD + tm, :] = u_ref[0]
    block = min(EVEN_BLOCK, tm)
    for q0 in range(0, tm, block):
        rows = slice(q0, q0 + block)
        pos = t * tm + q0 + lax.broadcasted_iota(jnp.int32, (block, 1), 0)
        pooled = []
        for g, w in enumerate(POOL_WINDOWS):
            sl = slice(POOL_GROUP * g, POOL_GROUP * (g + 1))
            first = POOL_PAD + q0 - (w - 1)
            win = hb[first:first + block, sl]
            for j in range(w - 2, -1, -1):
                win = win + hb[POOL_PAD + q0 - j:POOL_PAD + q0 - j + block, sl]
            cnt = jnp.minimum(p0 + pos + 1, w).astype(F32)
            pooled.append(win / cnt - u_ref[0, rows, sl])
        og = _gla_gate(o_ref[0, rows, :], r_ref[0, rows, :], on_ref[...])
        y = _even_mix(og, pooled, pw_ref, ps_ref, wo_ref)
        xo_ref[0, rows, :] = x_ref[0, rows, :] + gt_ref[...] * y
    hb[0:POOL_PAD, :] = hb[tm:tm + POOL_PAD, :]

    @pl.when(t == pl.num_programs(1) - 1)
    def _():
        ho_ref[0] = hb[0:POOL_PAD, :]


def _even_out_sample_kernel(x_ref, o_ref, r_ref, u_ref, hi_ref, gt_ref, on_ref, pw_ref, ps_ref, wo_ref,
                            xo_ref, hn_ref, *, p0):
    steps, nb = u_ref.shape[:2]

    def slab(i, sl):
        return hi_ref[i, :, sl] if i < POOL_HIST else u_ref[i - POOL_HIST, :, sl]

    pooled = []
    for g, w in enumerate(POOL_WINDOWS):
        sl = slice(POOL_GROUP * g, POOL_GROUP * (g + 1))
        rows = []
        for t in range(steps):
            win = slab(POOL_HIST + t - (w - 1), sl)
            for j in range(w - 2, -1, -1):
                win = win + slab(POOL_HIST + t - j, sl)
            rows.append(win / float(min(p0 + t + 1, w)) - u_ref[t, :, sl])
        pooled.append(jnp.concatenate(rows, axis=0))
    flat = lambda ref: ref[...].reshape(steps * nb, ref.shape[-1])
    og = _gla_gate(flat(o_ref), flat(r_ref), on_ref[...])
    x = x_ref[...]
    xo_ref[...] = x + gt_ref[...] * _even_mix(og, pooled, pw_ref, ps_ref, wo_ref).reshape(x.shape)
    for i in range(POOL_HIST):
        hn_ref[i] = slab(steps + i, slice(None))


def _even_out(grp, x, o, r, u, hist, mod, on, pw, ps, wo, p0):
    dm = x.shape[-1]
    common = [mod, on, pw, ps, wo]
    common_specs = [grp.mod(5), _full(on), _full(pw), _full(ps), _full(wo)]
    acts = [grp.act(dm), grp.act(o.shape[-1]), grp.act(r.shape[-1]), grp.act(u.shape[-1])]
    if grp.prompt:
        tm = grp.block[1]
        return pl.pallas_call(
            functools.partial(_even_out_prompt_kernel, tm=tm, p0=p0),
            grid=grp.grid,
            in_specs=acts + common_specs,
            out_specs=[grp.act(dm), pl.BlockSpec((1, POOL_PAD, POOL_WIDTH), lambda b, t: (b, 0, 0))],
            out_shape=[_sds(x.shape), _sds((x.shape[0], POOL_PAD, POOL_WIDTH))],
            scratch_shapes=[pltpu.VMEM((POOL_PAD + tm, POOL_WIDTH), F32)],
            compiler_params=_cp("parallel", "arbitrary"),
        )(x, o, r, u, *common)
    return pl.pallas_call(
        functools.partial(_even_out_sample_kernel, p0=p0),
        grid=grp.grid,
        in_specs=acts + [_full(hist)] + common_specs,
        out_specs=[grp.act(dm), _full(hist)],
        out_shape=[_sds(x.shape), _sds(hist.shape)],
        compiler_params=_cp("parallel", "arbitrary"),
    )(x, o, r, u, hist, *common)


def _rope_slab(x, cs, s1, s2, lead):
    shp = lead + (LANES,)
    back = pltpu.roll(x, LANES - MLA_ROPE // 2, 1).reshape(shp)
    fwd = pltpu.roll(x, MLA_ROPE // 2, 1).reshape(shp)
    out = x.reshape(shp) * cs + back * s1 + fwd * s2
    return out.reshape(x.shape)


def _odd_in_kernel(*refs, sample):
    (x_ref, sh_ref, sc_ref, g_ref, wcq_ref, wckv_ref, wkr_ref, wga_ref, wgg_ref, qn_ref, wuq_ref,
     kvn_ref, cs_ref, s1_ref, s2_ref) = refs[:15]
    if sample:
        wkl_ref, ckv_ref, kr_ref, uc_ref, q_ref, ql_ref = refs[15:]
    else:
        wuk_ref, wuvt_ref, ckv_ref, kr_ref, uc_ref, q_ref, k_ref, vt_ref = refs[15:]
    lead = x_ref.shape[:2]
    h = _modulated(x_ref, g_ref, sh_ref, sc_ref)
    cs, s1, s2 = cs_ref[...], s1_ref[...], s2_ref[...]

    cq = _rms(_dot(h, wcq_ref[...]), qn_ref[...]).astype(BF16)
    q = _dot(cq, wuq_ref[...]) * (MLA_SCALE * LOG2E)
    q = jnp.concatenate(
        [_rope_slab(q[:, LANES * i:LANES * (i + 1)], cs, s1, s2, lead) for i in range(MLA_HEADS)], axis=-1)
    q_ref[...] = q.astype(BF16).reshape(lead + (MLA_HEADS * LANES,))

    ckv = _rms(_dot(h, wckv_ref[...]), kvn_ref[...])
    ckv_ref[...] = ckv.reshape(lead + (MLA_KV_RANK,))
    kr = _rope_slab(_dot(h, wkr_ref[...]), cs, s1, s2, lead)
    kr_ref[...] = kr.reshape(lead + (LANES,))
    uc_ref[...] = (_dot(h, wga_ref[...]) * jax.nn.sigmoid(_dot(h, wgg_ref[...]))).reshape(lead + (CONV_CH,))

    if sample:
        qb = q.astype(BF16)
        for i in range(MLA_HEADS):
            ql = _dot(qb[:, LANES * i:LANES * (i + 1)], wkl_ref[i])
            ql_ref[:, :, MLA_KV_RANK * i:MLA_KV_RANK * (i + 1)] = ql.astype(BF16).reshape(lead + (MLA_KV_RANK,))
    else:
        cb = ckv.astype(BF16)
        kn = _dot(cb, wuk_ref[...])
        kn = jnp.concatenate([kn[:, LANES * i:LANES * (i + 1)] + kr for i in range(MLA_HEADS)], axis=-1)
        k_ref[...] = kn.astype(BF16).reshape(lead + (MLA_HEADS * LANES,))
        one_row = lax.broadcasted_iota(jnp.int32, (MLA_HEADS, VT_ROWS, 1), 1) == MLA_V
        ones = jnp.where(one_row, 1.0, 0.0).reshape(MLA_HEADS * VT_ROWS, 1)
        vt_ref[0] = (_dot_nt(wuvt_ref[...], cb) + ones).astype(BF16)


def _odd_in(grp, x, mod, g, w, tabs, sample):
    lead = x.shape[:2]
    hl = MLA_HEADS * LANES
    ins = [x, mod, mod, g, w['wcq'], w['wckv'], w['wkr'], w['wga'], w['wgg'], w['qn'], w['wuq'], w['kvn'], *tabs]
    specs = [grp.act(x.shape[-1]), grp.mod(3), grp.mod(4)] + [_full(a) for a in ins[3:12]] + [grp.pos()] * 3
    outs = [(MLA_KV_RANK, F32), (LANES, F32), (CONV_CH, F32), (hl, BF16)]
    if sample:
        ins.append(w['wkl'])
        outs.append((MLA_HEADS * MLA_KV_RANK, BF16))
    else:
        ins += [w['wuk'], w['wuvt']]
        outs.append((hl, BF16))
    specs += [_full(a) for a in ins[15:]]
    out_specs = [grp.act(wd) for wd, _ in outs]
    out_shape = [_sds(lead + (wd,), dt) for wd, dt in outs]
    if not sample:
        vt_rows = MLA_HEADS * VT_ROWS
        out_specs.append(pl.BlockSpec((1, vt_rows, grp.block[1]), lambda b, t: (b, 0, t)))
        out_shape.append(_sds((lead[0], vt_rows, lead[1]), BF16))
    return pl.pallas_call(
        functools.partial(_odd_in_kernel, sample=sample),
        grid=grp.grid,
        in_specs=specs,
        out_specs=out_specs,
        out_shape=out_shape,
        compiler_params=_cp("parallel", "parallel"),
    )(*ins)


def _attn_prompt_kernel(qi_ref, ki_ref, q_ref, k_ref, vt_ref, o_ref, m_scr, acc_scr, s_scr, p_scr, *, tq):
    step = pl.program_id(1)
    qi = qi_ref[step]
    ki = ki_ref[step]

    @pl.when(ki == 0)
    def _():
        m_scr[...] = jnp.full_like(m_scr, NEG)
        acc_scr[...] = jnp.zeros_like(acc_scr)

    tk = k_ref.shape[1]
    lead = qi * tq - ki * tk

    def update(lead_rows):
        diagonal = lead_rows is not None
        width = s_scr.shape[2]
        items = [(h, c) for h in range(MLA_HEADS) for c in range(0, tq, width)]

        def keys_used(c):
            return min(tk, lead_rows + c + width) if diagonal else tk

        def scores(item, buf):
            h, c = item
            nk = keys_used(c)
            sl = slice(LANES * h, LANES * (h + 1))
            st = _dot_nt(k_ref[0, :nk, sl], q_ref[0, c:c + width, sl])
            if diagonal:
                key = lax.broadcasted_iota(jnp.int32, (nk, width), 0)
                qry = lax.broadcasted_iota(jnp.int32, (nk, width), 1) + (lead_rows + c)
                st = jnp.where(key <= qry, st, NEG)
            s_scr[buf, :nk] = st

        def absorb(item, buf):
            h, c = item
            nk = keys_used(c)
            vr = slice(VT_ROWS * h, VT_ROWS * (h + 1))
            qc = slice(c, c + width)
            m_prev = m_scr[h:h + 1, qc]
            m_new = jnp.maximum(m_prev, jnp.max(s_scr[buf, :nk], axis=0, keepdims=True))
            p_scr[buf, :nk] = jnp.exp2(s_scr[buf, :nk] - m_new).astype(BF16)
            acc_scr[vr, qc] = (jnp.exp2(m_prev - m_new) * acc_scr[vr, qc]
                               + _dot(vt_ref[0, vr, :nk], p_scr[buf, :nk]))
            m_scr[h:h + 1, qc] = m_new

        depth = s_scr.shape[0]
        for i in range(depth - 1):
            scores(items[i], i)
        for i, item in enumerate(items):
            if i + depth - 1 < len(items):
                scores(items[i + depth - 1], (i + depth - 1) % depth)
            absorb(item, i % depth)

    def finish():
        for h in range(0, MLA_HEADS, 2):
            pair = []
            for hh in (h, h + 1):
                acc = acc_scr[VT_ROWS * hh:VT_ROWS * (hh + 1), :]
                pair.append(acc[:MLA_V] / acc[MLA_V:MLA_V + 1])
            o_ref[0, :, MLA_V * h:MLA_V * (h + 2)] = jnp.transpose(jnp.concatenate(pair, axis=0)).astype(BF16)

    @pl.when(lead >= tk)
    def _():
        update(None)

    for lead_rows in range(0, tk, tq):
        @pl.when(lead == lead_rows)
        def _():
            update(lead_rows)
            finish()


def _attn_prompt(q, k, vt, tq, tk):
    nb, nt, hl = q.shape
    vt_rows = vt.shape[1]
    n_out = MLA_HEADS * MLA_V
    assert nt % tq == 0 and nt % tk == 0 and tk % tq == 0
    width = min(ATTN_QCHUNK, tq)
    pairs = [(i, j) for i in range(nt // tq) for j in range((i * tq + tq - 1) // tk + 1)]
    qi_tab = jnp.asarray([p[0] for p in pairs], jnp.int32)
    ki_tab = jnp.asarray([p[1] for p in pairs], jnp.int32)
    return pl.pallas_call(
        functools.partial(_attn_prompt_kernel, tq=tq),
        grid_spec=pltpu.PrefetchScalarGridSpec(
            num_scalar_prefetch=2,
            grid=(nb, len(pairs)),
            in_specs=[pl.BlockSpec((1, tq, hl), lambda b, s, qt, kt: (b, qt[s], 0)),
                      pl.BlockSpec((1, tk, hl), lambda b, s, qt, kt: (b, kt[s], 0)),
                      pl.BlockSpec((1, vt_rows, tk), lambda b, s, qt, kt: (b, 0, kt[s]))],
            out_specs=pl.BlockSpec((1, tq, n_out), lambda b, s, qt, kt: (b, qt[s], 0)),
            scratch_shapes=[pltpu.VMEM((MLA_HEADS, tq), F32), pltpu.VMEM((vt_rows, tq), F32),
                            pltpu.VMEM((ATTN_DEPTH, tk, width), F32),
                            pltpu.VMEM((ATTN_DEPTH, tk, width), BF16)]),
        out_shape=_sds((nb, nt, n_out), BF16),
        compiler_params=_cp("parallel", "arbitrary"),
    )(qi_tab, ki_tab, q, k, vt)


def _attn_sample_kernel(pt_ref, ql_ref, qr_ref, cn_ref, kn_ref, ckv_hbm, kr_hbm, o_ref,
                        ckv_buf, kr_buf, pg_scr, s_scr, p_scr, sem, *, npg, layer):
    b = pl.program_id(0)
    nb = pl.num_programs(0)
    slot = lax.rem(b, 2)
    ql = ql_ref[0]
    qr = qr_ref[0]
    rows = ql.shape[0]

    def page_copies(page, i, sl):
        return (pltpu.make_async_copy(ckv_hbm.at[layer, page], ckv_buf.at[sl, i], sem.at[0, sl]),
                pltpu.make_async_copy(kr_hbm.at[layer, page], kr_buf.at[sl, i], sem.at[1, sl]))

    def start_page(batch, i, sl):
        for cp in page_copies(pt_ref[batch, i], i, sl):
            cp.start()

    def start_first(i, carry):
        start_page(0, i, 0)
        return carry

    @pl.when(b == 0)
    def _():
        lax.fori_loop(0, npg, start_first, 0)

    for i in range(npg):
        for cp in page_copies(0, i, slot):
            cp.wait()

    def score_pages(prefetch):
        for i in range(npg):
            if prefetch:
                for k in range(2 * i, min(2 * i + 2, npg)):
                    start_page(b + 1, k, 1 - slot)
            page = ckv_buf[slot, i].astype(BF16)
            pg_scr[i] = page
            s_scr[i] = _dot_nt(ql, page) + _dot(qr, kr_buf[slot, i].astype(BF16))

    @pl.when(b + 1 < nb)
    def _():
        score_pages(True)

    @pl.when(b + 1 == nb)
    def _():
        score_pages(False)

    n_new = cn_ref.shape[1]
    pad = jnp.zeros((LANES - n_new, MLA_KV_RANK), F32)
    cn = jnp.concatenate([cn_ref[0], pad], axis=0).astype(BF16)
    kn = jnp.concatenate([kn_ref[0], pad[:, :MLA_ROPE]], axis=0).astype(BF16)
    step = lax.shift_right_logical(lax.broadcasted_iota(jnp.int32, (rows, LANES), 0), MLA_HEADS.bit_length() - 1)
    col = lax.broadcasted_iota(jnp.int32, (rows, LANES), 1)
    s_new = jnp.where(col <= step, _dot_nt(ql, cn) + _dot_nt(qr, kn), NEG)

    s = s_scr[...]
    m = jnp.maximum(jnp.max(jnp.max(s, axis=0), axis=-1, keepdims=True),
                    jnp.max(s_new, axis=-1, keepdims=True))
    p = jnp.exp2(s - m)
    p_new = jnp.exp2(s_new - m)
    denom = jnp.sum(jnp.sum(p, axis=0), axis=-1, keepdims=True) + jnp.sum(p_new, axis=-1, keepdims=True)
    p_scr[...] = p.astype(BF16)

    def weigh_page(i, acc):
        return acc + _dot(p_scr[i], pg_scr[i])

    acc = lax.fori_loop(0, npg, weigh_page, _dot(p_new.astype(BF16), cn), unroll=True)
    o_ref[0] = acc / denom


def _attn_sample(page_table, ql, qr, cn, kn, cache_ckv, cache_kr, layer):
    nb, rows, rank = ql.shape
    npg = page_table.shape[1]
    page, rope = cache_ckv.shape[2], cache_kr.shape[2]
    assert page == LANES, "one cache page must fill one lane tile of scores"
    per_b = lambda a: pl.BlockSpec((1,) + a.shape[1:], lambda b, pt: (b, 0, 0))
    hbm = pl.BlockSpec(memory_space=pl.ANY)
    return pl.pallas_call(
        functools.partial(_attn_sample_kernel, npg=npg, layer=layer),
        grid_spec=pltpu.PrefetchScalarGridSpec(
            num_scalar_prefetch=1,
            grid=(nb,),
            in_specs=[per_b(ql), per_b(qr), per_b(cn), per_b(kn), hbm, hbm],
            out_specs=pl.BlockSpec((1, rows, rank), lambda b, pt: (b, 0, 0)),
            scratch_shapes=[pltpu.VMEM((2, npg, page, rank), F32),
                            pltpu.VMEM((2, npg, rope, page), F32),
                            pltpu.VMEM((npg, page, rank), BF16),
                            pltpu.VMEM((npg, rows, page), F32),
                            pltpu.VMEM((npg, rows, page), BF16),
                            pltpu.SemaphoreType.DMA((2, 2))]),
        out_shape=_sds((nb, rows, rank)),
        compiler_params=_cp("arbitrary"),
    )(page_table, ql, qr, cn, kn, cache_ckv, cache_kr)


def _conv_norm_act(cv, cb_ref, lg_ref, lb_ref):
    cv = cv + cb_ref[...]
    mu = jnp.mean(cv, axis=-1, keepdims=True)
    d = cv - mu
    y = d * lax.rsqrt(jnp.mean(d * d, axis=-1, keepdims=True) + EPS)
    return _silu(y * lg_ref[...] + lb_ref[...])


def _odd_out_prompt_kernel(x_ref, a_ref, uc_ref, gt_ref, cw_ref, cb_ref, lg_ref, lb_ref, woa_ref, woc_ref,
                           xo_ref, ho_ref, hb, hs, cvb, *, tm):
    t = pl.program_id(1)

    @pl.when(t == 0)
    def _():
        hb[0:CONV_PAD, :] = jnp.zeros((CONV_PAD, CONV_CH), F32)

    hb[CONV_PAD:CONV_PAD + tm, :] = uc_ref[0]
    for sft in range(1, SUBLANES):
        hs[sft - 1] = hb[sft:sft + hs.shape[1], :]
    base = CONV_PAD - CONV_HIST

    block = min(CONV_BLOCK, tm)
    for q0 in range(0, tm, block):
        ya = _dot(a_ref[0, q0:q0 + block, :], woa_ref[...])
        for r0 in range(q0, q0 + block, CONV_ROWS):
            acc = None
            for j in range(CONV_WIDTH):
                whole, sft = divmod(base + j, SUBLANES)
                rows = slice(r0 + whole * SUBLANES, r0 + whole * SUBLANES + CONV_ROWS)
                term = cw_ref[j:j + 1, :] * (hb[rows, :] if sft == 0 else hs[sft - 1, rows, :])
                acc = term if acc is None else acc + term
            cvb[r0:r0 + CONV_ROWS, :] = acc
        cv = _conv_norm_act(cvb[q0:q0 + block, :], cb_ref, lg_ref, lb_ref)
        y = ya + _dot(cv.astype(BF16), woc_ref[...])
        xo_ref[0, q0:q0 + block, :] = x_ref[0, q0:q0 + block, :] + gt_ref[...] * y
    hb[0:CONV_PAD, :] = hb[tm:tm + CONV_PAD, :]

    @pl.when(t == pl.num_programs(1) - 1)
    def _():
        ho_ref[0] = hb[0:CONV_PAD, :]


def _odd_out_sample_kernel(x_ref, lat_ref, uc_ref, hi_ref, gt_ref, cw_ref, cb_ref, lg_ref, lb_ref, wuv_ref,
                           woa_ref, woc_ref, xo_ref, hn_ref):
    steps, nb = uc_ref.shape[:2]

    def slab(i):
        return hi_ref[i] if i < CONV_HIST else uc_ref[i - CONV_HIST]

    rows = []
    for t in range(steps):
        cv = cw_ref[0:1, :] * slab(t)
        for j in range(1, CONV_WIDTH):
            cv = cv + cw_ref[j:j + 1, :] * slab(t + j)
        rows.append(cv)
    cv = _conv_norm_act(jnp.concatenate(rows, axis=0), cb_ref, lg_ref, lb_ref)
    lat = lat_ref[...].reshape(steps * nb, lat_ref.shape[-1]).astype(BF16)
    attn = _dot(lat, wuv_ref[...]).astype(BF16)
    y = _dot(attn, woa_ref[...]) + _dot(cv.astype(BF16), woc_ref[...])
    x = x_ref[...]
    xo_ref[...] = x + gt_ref[...] * y.reshape(x.shape)
    for i in range(CONV_HIST):
        hn_ref[i] = slab(steps + i)


def _odd_out(grp, x, a, uc, hist, mod, w):
    dm = x.shape[-1]
    conv = [w['cw'], w['cb'], w['lg'], w['lb']]
    if grp.prompt:
        tm = grp.block[1]
        ins = [x, a, uc, mod] + conv + [w['woa'], w['woc']]
        return pl.pallas_call(
            functools.partial(_odd_out_prompt_kernel, tm=tm),
            grid=grp.grid,
            in_specs=[grp.act(dm), grp.act(a.shape[-1]), grp.act(CONV_CH), grp.mod(5)] + [_full(v) for v in ins[4:]],
            out_specs=[grp.act(dm), pl.BlockSpec((1, CONV_PAD, CONV_CH), lambda b, t: (b, 0, 0))],
            out_shape=[_sds(x.shape), _sds((x.shape[0], CONV_PAD, CONV_CH))],
            scratch_shapes=[pltpu.VMEM((CONV_PAD + tm, CONV_CH), F32),
                            pltpu.VMEM((SUBLANES - 1, CONV_PAD + tm - SUBLANES, CONV_CH), F32),
                            pltpu.VMEM((tm, CONV_CH), F32)],
            compiler_params=_cp("parallel", "arbitrary"),
        )(*ins)
    ins = [x, a, uc, hist, mod] + conv + [w['wuv_bd'], w['woa'], w['woc']]
    return pl.pallas_call(
        _odd_out_sample_kernel,
        grid=grp.grid,
        in_specs=[grp.act(dm), grp.act(a.shape[-1]), grp.act(CONV_CH), _full(hist), grp.mod(5)]
        + [_full(v) for v in ins[5:]],
        out_specs=[grp.act(dm), _full(hist)],
        out_shape=[_sds(x.shape), _sds(hist.shape)],
        compiler_params=_cp("parallel", "arbitrary"),
    )(*ins)


def _head_pad(w, heads, width, offset=0):
    kdim = w.shape[0]
    w = w.reshape(kdim, heads, width)
    w = jnp.pad(w, ((0, 0), (0, 0), (offset, LANES - width - offset)))
    return w.reshape(kdim, heads * LANES)


def _rope_tables(pos):
    half = MLA_ROPE // 2
    freqs = ROPE_BASE ** (-np.arange(half, dtype=np.float64) / half)
    ang = np.asarray(pos, np.float64)[..., None] * freqs
    cos, sin = jnp.asarray(np.cos(ang), F32), jnp.asarray(np.sin(ang), F32)
    zeros = jnp.zeros_like(cos)
    lead = jnp.ones(pos.shape + (ROPE_LANE0,), F32)
    tail = jnp.zeros(pos.shape + (LANES - ROPE_LANE0 - MLA_ROPE,), F32)
    cs = jnp.concatenate([lead, cos, cos, tail], axis=-1)
    s1 = jnp.concatenate([0 * lead, -sin, zeros, tail], axis=-1)
    s2 = jnp.concatenate([0 * lead, zeros, sin, tail], axis=-1)
    return cs, s1, s2


def _even_weights(w_in, gate_w2, gate_b, out_norm, pool_w, pool_scale, w_out):
    hk = GLA_HEADS * GLA_DK
    hv = GLA_HEADS * GLA_DV
    n_main = 2 * hk + 2 * hv
    wg = jnp.pad(w_in[:, n_main:n_main + GLA_GATE_RANK], ((0, 0), (0, LANES - GLA_GATE_RANK)))
    w2 = jnp.pad(gate_w2, ((0, LANES - GLA_GATE_RANK), (0, 0)))
    return dict(wq=w_in[:, :n_main].astype(BF16), wg=wg.astype(BF16), w2=w2.astype(BF16),
                gb=gate_b[None], wu=w_in[:, n_main + GLA_GATE_RANK:].astype(BF16),
                on=out_norm[None], pw=pool_w.astype(BF16), ps=pool_scale[None], wo=w_out.astype(BF16))


def _odd_weights(w_in, q_norm, w_uq, kv_norm, w_uk, w_uv, conv_w, conv_b, ln_g, ln_b, w_out):
    c0, c1, c2 = MLA_Q_RANK, MLA_Q_RANK + MLA_KV_RANK, MLA_Q_RANK + MLA_KV_RANK + MLA_ROPE
    n_attn = MLA_HEADS * MLA_V
    wkr = jnp.pad(w_in[:, c1:c2], ((0, 0), (ROPE_LANE0, LANES - ROPE_LANE0 - MLA_ROPE)))
    wuk = w_uk.reshape(MLA_KV_RANK, MLA_HEADS * MLA_NOPE)
    wkl = jnp.pad(jnp.transpose(w_uk, (1, 2, 0)), ((0, 0), (0, LANES - MLA_NOPE), (0, 0)))
    eye = jnp.eye(MLA_HEADS, dtype=F32)
    wuv_bd = (eye[:, None, :, None] * jnp.transpose(w_uv, (1, 0, 2))[:, :, None, :]).reshape(
        MLA_HEADS * MLA_KV_RANK, n_attn)
    wuvt = jnp.pad(jnp.transpose(w_uv, (1, 2, 0)), ((0, 0), (0, VT_ROWS - MLA_V), (0, 0))).reshape(
        MLA_HEADS * VT_ROWS, MLA_KV_RANK)
    return dict(wcq=w_in[:, :c0].astype(BF16), wckv=w_in[:, c0:c1].astype(BF16), wkr=wkr.astype(BF16),
                wga=w_in[:, c2:c2 + CONV_CH].astype(BF16), wgg=w_in[:, c2 + CONV_CH:].astype(BF16),
                qn=q_norm[None], wuq=_head_pad(w_uq, MLA_HEADS, MLA_NOPE + MLA_ROPE).astype(BF16),
                kvn=kv_norm[None], wuk=_head_pad(wuk, MLA_HEADS, MLA_NOPE).astype(BF16),
                wuvt=wuvt.astype(BF16), wkl=wkl.astype(BF16),
                wuv_bd=wuv_bd.astype(BF16), cw=conv_w, cb=conv_b[None], lg=ln_g[None], lb=ln_b[None],
                woa=w_out[:n_attn].astype(BF16), woc=w_out[n_attn:].astype(BF16))


def _tm(x):
    return jnp.swapaxes(x, 0, 1)


def kernel(x_prompt, x_sample, state_gla, state_pool, cache_ckv, cache_krope, state_conv, page_table, c_prompt, c_sample, ada_w, ada_b, norm_g, ffn_w1, ffn_w3, ffn_w2, ev_w_in, ev_gate_w2, ev_gate_b, ev_out_norm, ev_pool_w, ev_pool_scale, ev_w_out, od_w_in, od_q_norm, od_w_uq, od_kv_norm, od_w_uk, od_w_uv, od_conv_w, od_conv_b, od_conv_norm_g, od_conv_norm_b, od_w_out, final_norm):
    nbp, seq, dm = x_prompt.shape
    nbs, steps, _ = x_sample.shape
    depth = ada_w.shape[0]
    past_len = page_table.shape[1] * cache_ckv.shape[2]
    tile = min(TOKEN_TILE, seq)
    gp = _Group(True, nbp, seq, tile)
    gf = _Group(True, nbp, seq, min(FFN_TOKEN_TILE, seq))
    gs = _Group(False, steps, nbs, nbs)

    n_c = nbp + nbs
    c_all = jnp.pad(jnp.concatenate([c_sample, c_prompt], axis=0), ((0, -n_c % SUBLANES), (0, 0)))
    mod = _ada(c_all, ada_w, ada_b)
    mod_p = mod[:, :, nbs:n_c].reshape(depth, N_MOD, nbp, 1, dm)
    mod_s = mod if nbs % SUBLANES == 0 else mod[:, :, :nbs]

    w1, w3, w2 = ffn_w1.astype(BF16), ffn_w3.astype(BF16), ffn_w2.astype(BF16)
    tabs_p = _rope_tables(np.arange(seq)[None])
    tabs_s = _rope_tables(past_len + np.arange(steps)[:, None])

    xp = x_prompt
    xs = _tm(x_sample)
    gla_p, gla_s, pool_p, pool_s, ckv_p, ckv_s, kr_p, kr_s, conv_p, conv_s = ([] for _ in range(10))
    gla_tt = min(TOKEN_TILE, seq)

    for layer in range(depth):
        i = layer // 2
        ng = norm_g[layer]
        last = layer == depth - 1
        xp = _ffn(gf, xp, mod_p[layer], 0, ng[0:1], w1, w3, w2, (layer, 0))
        xs = _ffn(gs, xs, mod_s[layer], 0, ng[0:1], w1, w3, w2, (layer, 0))
        if layer % 2 == 0:
            w = _even_weights(ev_w_in[i], ev_gate_w2[i], ev_gate_b[i], ev_out_norm[i], ev_pool_w[i],
                              ev_pool_scale[i], ev_w_out[i])
            proj = (ng[1:2], w['wq'], w['wg'], w['w2'], w['gb'], w['wu'])
            out_w = (w['on'], w['pw'], w['ps'], w['wo'])
            q, k, v, r, la, u = _even_in(gp, xp, mod_p[layer], *proj)
            o, s_fin = _gla(q, k, v, la, None, nbp, gla_tt)
            xp, hist = _even_out(gp, xp, o, r, u, None, mod_p[layer], *out_w, 0)
            gla_p.append(s_fin.reshape(nbp, GLA_HEADS, GLA_DK, GLA_DV))
            pool_p.append(hist[:, POOL_PAD - POOL_HIST:])
            q, k, v, r, la, u = _even_in(gs, xs, mod_s[layer], *proj)
            rows8 = steps + (-steps % SUBLANES)
            chunked = lambda a: jnp.pad(_tm(a), ((0, 0), (0, rows8 - steps), (0, 0)))
            s0 = state_gla[i].reshape(nbs, GLA_HEADS // 2, LANES, GLA_DV)
            o, s_fin = _gla(chunked(q), chunked(k), chunked(v), chunked(la), s0,
                            GLA_SAMPLE_BATCHES if nbs % GLA_SAMPLE_BATCHES == 0 else 1, rows8)
            xs, hist = _even_out(gs, xs, _tm(o[:, :steps]), r, u, _tm(state_pool[i]), mod_s[layer], *out_w, past_len)
            gla_s.append(s_fin.reshape(nbs, GLA_HEADS, GLA_DK, GLA_DV))
            pool_s.append(_tm(hist))
        else:
            w = _odd_weights(od_w_in[i], od_q_norm[i], od_w_uq[i], od_kv_norm[i], od_w_uk[i], od_w_uv[i],
                             od_conv_w[i], od_conv_b[i], od_conv_norm_g[i], od_conv_norm_b[i], od_w_out[i])
            rope_lanes = slice(ROPE_LANE0, ROPE_LANE0 + MLA_ROPE)
            ckv, kr, uc, q, k, v = _odd_in(gp, xp, mod_p[layer], ng[1:2], w, tabs_p, False)
            attn = _attn_prompt(q, k, v, min(ATTN_TILE, seq), min(ATTN_KEYS, seq))
            xp, hist = _odd_out(gp, xp, attn, uc, None, mod_p[layer], w)
            ckv_p.append(ckv)
            kr_p.append(kr[..., rope_lanes])
            conv_p.append(hist[:, CONV_PAD - CONV_HIST:])
            ckv, kr, uc, q, ql = _odd_in(gs, xs, mod_s[layer], ng[1:2], w, tabs_s, True)
            kr = kr[..., rope_lanes]
            qr = q.reshape(steps, nbs, MLA_HEADS, LANES)[..., rope_lanes]
            qr = _tm(qr).reshape(nbs, steps * MLA_HEADS, MLA_ROPE)
            ql = _tm(ql).reshape(nbs, steps * MLA_HEADS, MLA_KV_RANK)
            pad8 = lambda a: jnp.pad(_tm(a), ((0, 0), (0, -steps % 8), (0, 0)))
            cache_kr_t = jnp.swapaxes(cache_krope, 2, 3)
            lat = _attn_sample(page_table, ql, qr, pad8(ckv), pad8(kr), cache_ckv, cache_kr_t, i)
            lat = _tm(lat.reshape(nbs, steps, MLA_HEADS * MLA_KV_RANK))
            xs, hist = _odd_out(gs, xs, lat, uc, _tm(state_conv[i]), mod_s[layer], w)
            ckv_s.append(_tm(ckv))
            kr_s.append(_tm(kr))
            conv_s.append(_tm(hist))
        fin = final_norm[None] if last else None
        xp = _ffn(gf, xp, mod_p[layer], 6, ng[2:3], w1, w3, w2, (layer, 1), fin)
        xs = _ffn(gs, xs, mod_s[layer], 6, ng[2:3], w1, w3, w2, (layer, 1), fin)

    st = jnp.stack
    return (xp, _tm(xs), st(gla_p), st(gla_s), st(pool_p), st(pool_s), st(ckv_p), st(ckv_s),
            st(kr_p), st(kr_s), st(conv_p), st(conv_s))
```

```python
import functools

import jax
import jax.numpy as jnp
import numpy as np
from jax import lax
from jax.experimental import pallas as pl
from jax.experimental.pallas import tpu as pltpu

F32 = jnp.float32
BF16 = jnp.bfloat16

EPS = 1e-6
NEG = -1e30
N_MOD = 9
GLA_HEADS = 4
GLA_DK = 64
GLA_DV = 128
GLA_GATE_RANK = 16
GLA_TAU = 16.0
GLA_CHUNK = 64
POOL_WINDOWS = (2, 4, 8, 16)
POOL_GROUP = 128
POOL_WIDTH = 512
POOL_HIST = 15
MLA_HEADS = 8
MLA_Q_RANK = 384
MLA_KV_RANK = 256
MLA_NOPE = 64
MLA_ROPE = 32
MLA_V = 64
MLA_SCALE = (MLA_NOPE + MLA_ROPE) ** -0.5
LOG2E = 1.4426950408889634
ROPE_BASE = 10000.0
CONV_WIDTH = 31
CONV_CH = 512
CONV_HIST = 30

LANES = 128
SUBLANES = 8
CONV_ROWS = 32
CONV_BLOCK = 256
EVEN_BLOCK = 256
ROPE_LANE0 = 64
VT_ROWS = 80
POOL_PAD = 16
assert all(w & (w - 1) == 0 and w <= POOL_PAD for w in POOL_WINDOWS)
CONV_PAD = 32
VMEM_LIMIT = 52 * 2 ** 20
TOKEN_TILE = 1024
ATTN_TILE = 1024
ATTN_KEYS = 1024
ATTN_DEPTH = 3
ATTN_QCHUNK = 512
FFN_TOKEN_TILE = 1024
FF_SUB = 1024
GLA_SAMPLE_BATCHES = 8


def _cp(*sem):
    return pltpu.CompilerParams(dimension_semantics=sem, vmem_limit_bytes=VMEM_LIMIT)


def _dot(a, b):
    return jnp.dot(a, b, preferred_element_type=F32)


def _dot_nt(a, b):
    return lax.dot_general(a, b, (((1,), (1,)), ((), ())), preferred_element_type=F32)


def _dot_tn(a, b):
    return lax.dot_general(a, b, (((0,), (0,)), ((), ())), preferred_element_type=F32)


def _silu(x):
    return x * jax.nn.sigmoid(x)


def _rms(x, g):
    return x * lax.rsqrt(jnp.mean(x * x, axis=-1, keepdims=True) + EPS) * g


def _modulated(x_ref, g_ref, sh_ref, sc_ref):
    x = x_ref[...]
    h = _rms(x, g_ref[...]) * (1.0 + sc_ref[...]) + sh_ref[...]
    return h.reshape(x.shape[0] * x.shape[1], x.shape[2]).astype(BF16)


class _Group:
    def __init__(self, prompt, lead, rows, tile, dm):
        self.prompt = prompt
        self.dm = dm
        self.grid = (lead, rows // tile) if prompt else (1, 1)
        self.block = (1, tile) if prompt else (lead, rows)

    def act(self, width):
        return pl.BlockSpec(self.block + (width,), lambda b, t, *_: (b, t, 0))

    def mod(self, k):
        if self.prompt:
            return pl.BlockSpec((None, None, 1, self.dm), lambda b, t, *_: (k, b, 0, 0))
        return pl.BlockSpec((None, self.block[1], self.dm), lambda b, t, *_: (k, 0, 0))

    def pos(self):
        if self.prompt:
            return pl.BlockSpec((1, self.block[1], LANES), lambda b, t, *_: (0, t, 0))
        return pl.BlockSpec((self.block[0], 1, LANES), lambda b, t, *_: (0, 0, 0))


def _full(a):
    nd = a.ndim
    return pl.BlockSpec(a.shape, lambda *_: (0,) * nd)


def _sds(shape, dtype=F32):
    return jax.ShapeDtypeStruct(shape, dtype)


def _ada_kernel(c_ref, w_ref, b_ref, o_ref):
    c = c_ref[...]
    o_ref[...] = _dot(_silu(c).astype(BF16), w_ref[...].astype(BF16)) + b_ref[...]


def _ada(c_all, ada_w, ada_b):
    depth, dm, n = ada_w.shape
    m = c_all.shape[0]
    tn = dm
    return pl.pallas_call(
        _ada_kernel,
        grid=(depth, n // tn),
        in_specs=[pl.BlockSpec((m, dm), lambda l, j: (0, 0)),
                  pl.BlockSpec((None, dm, tn), lambda l, j: (l, 0, j)),
                  pl.BlockSpec((None, 1, tn), lambda l, j: (l, 0, j))],
        out_specs=pl.BlockSpec((None, None, m, tn), lambda l, j: (l, j, 0, 0)),
        out_shape=_sds((depth, n // tn, m, tn)),
        compiler_params=_cp("parallel", "parallel"),
    )(c_all, ada_w, ada_b.reshape(depth, 1, n))


def _ffn_kernel(*refs, final):
    if final:
        x_ref, sh_ref, sc_ref, gt_ref, g_ref, w1_ref, w3_ref, w2_ref, fn_ref, o_ref = refs
    else:
        x_ref, sh_ref, sc_ref, gt_ref, g_ref, w1_ref, w3_ref, w2_ref, o_ref = refs
    h = _modulated(x_ref, g_ref, sh_ref, sc_ref)
    dff = w1_ref.shape[1]
    y = None
    for c0 in range(0, dff, FF_SUB):
        cols = slice(c0, min(c0 + FF_SUB, dff))
        a = _dot(h, w1_ref[:, cols])
        b = _dot(h, w3_ref[:, cols])
        part = _dot((_silu(a) * b).astype(BF16), w2_ref[cols, :])
        y = part if y is None else y + part
    x = x_ref[...]
    xn = x + 0.5 * gt_ref[...] * y.reshape(x.shape)
    o_ref[...] = _rms(xn, fn_ref[...]) if final else xn


def _resident(a, lead):
    idx = tuple(lead) + (0, 0)
    return pl.BlockSpec((None,) * len(lead) + a.shape[-2:], lambda *_: idx, pipeline_mode=pl.Buffered(1))


def _ffn(grp, x, mod, k0, g, w1, w3, w2, which, final_g=None):
    dm = x.shape[-1]
    final = final_g is not None
    ins = [x, mod, mod, mod, g, w1, w3, w2] + ([final_g] if final else [])
    specs = [grp.act(dm), grp.mod(k0), grp.mod(k0 + 1), grp.mod(k0 + 2), _full(g),
             _resident(w1, which), _resident(w3, which), _resident(w2, which)] + ([_full(final_g)] if final else [])
    return pl.pallas_call(
        functools.partial(_ffn_kernel, final=final),
        grid=grp.grid,
        in_specs=specs,
        out_specs=grp.act(dm),
        out_shape=_sds(x.shape),
        compiler_params=_cp("parallel", "parallel"),
    )(*ins)


def _log_sigmoid(x):
    return jnp.minimum(x, 0.0) - jnp.log(1.0 + jnp.exp(-jnp.abs(x)))


def _even_in_kernel(x_ref, sh_ref, sc_ref, g_ref, wq_ref, wg_ref, w2_ref, gb_ref, wu_ref,
                    q_ref, k_ref, v_ref, r_ref, la_ref, u_ref):
    h = _modulated(x_ref, g_ref, sh_ref, sc_ref)
    lead = x_ref.shape[:2]
    hk = GLA_HEADS * GLA_DK
    hv = GLA_HEADS * GLA_DV
    z = _dot(h, wq_ref[...])
    q_ref[...] = (z[:, :hk] * GLA_DK ** -0.5).reshape(lead + (hk,))
    k_ref[...] = z[:, hk:2 * hk].reshape(lead + (hk,))
    v_ref[...] = z[:, 2 * hk:2 * hk + hv].reshape(lead + (hv,))
    r_ref[...] = z[:, 2 * hk + hv:].reshape(lead + (hv,))
    g_low = _dot(h, wg_ref[...]).astype(BF16)
    gate = _dot(g_low, w2_ref[...]) + gb_ref[...]
    la_ref[...] = (_log_sigmoid(gate) / GLA_TAU).reshape(lead + (hk,))
    u_ref[...] = _dot(h, wu_ref[...]).reshape(lead + (POOL_WIDTH,))


def _even_in(grp, x, mod, g, wq, wg, w2, gb, wu):
    hk = GLA_HEADS * GLA_DK
    hv = GLA_HEADS * GLA_DV
    widths = (hk, hk, hv, hv, hk, POOL_WIDTH)
    return pl.pallas_call(
        _even_in_kernel,
        grid=grp.grid,
        in_specs=[grp.act(x.shape[-1]), grp.mod(3), grp.mod(4), _full(g), _full(wq), _full(wg),
                  _full(w2), _full(gb), _full(wu)],
        out_specs=[grp.act(w) for w in widths],
        out_shape=[_sds(x.shape[:2] + (w,)) for w in widths],
        compiler_params=_cp("parallel", "parallel"),
    )(x, mod, mod, g, wq, wg, w2, gb, wu)


def _cumsum_rows(tril, x):
    hi = x.astype(BF16)
    r1 = x - hi.astype(F32)
    mid = r1.astype(BF16)
    lo = (r1 - mid.astype(F32)).astype(BF16)
    return _dot(tril, hi) + _dot(tril, mid) + _dot(tril, lo)


def _gla_kernel(*refs, nbb, tt, has_s0):
    if has_s0:
        q_ref, k_ref, v_ref, la_ref, s0_ref, o_ref, so_ref, s_scr = refs
    else:
        q_ref, k_ref, v_ref, la_ref, o_ref, so_ref, s_scr = refs
    t = pl.program_id(1)
    c_rows = GLA_CHUNK

    @pl.when(t == 0)
    def _():
        s_scr[...] = s0_ref[...] if has_s0 else jnp.zeros_like(s_scr)

    row = lax.broadcasted_iota(jnp.int32, (c_rows, c_rows), 0)
    col = lax.broadcasted_iota(jnp.int32, (c_rows, c_rows), 1)
    causal = col <= row
    tril = jnp.where(causal, 1.0, 0.0).astype(BF16)
    lane = lax.broadcasted_iota(jnp.int32, (c_rows, LANES), 1)
    srow = lax.broadcasted_iota(jnp.int32, (LANES, LANES), 0)

    pairs = GLA_HEADS // 2
    combos = [(n, p) for n in range(nbb) for p in range(pairs)]
    heads = [(n, p, hh) for n, p in combos for hh in range(2)]

    def chunk(c, carry):
        short = tt < c_rows
        rows = slice(0, tt) if short else pl.ds(pl.multiple_of(c * c_rows, c_rows), c_rows)

        def take(ref, n, cols):
            x = ref[n, rows, cols]
            return jnp.concatenate([x, jnp.zeros((c_rows - tt, x.shape[1]), F32)], axis=0) if short else x

        wide = lambda ref: jnp.concatenate([take(ref, n, slice(None)) for n in range(nbb)], axis=-1)
        slab = lambda x, n, p: x[:, LANES * (n * pairs + p):LANES * (n * pairs + p + 1)]
        vcols = lambda p, hh: slice(GLA_DV * (2 * p + hh), GLA_DV * (2 * p + hh + 1))
        b = _cumsum_rows(tril, wide(la_ref))
        b_last = b[c_rows - 1:c_rows, :]
        k = wide(k_ref)
        qi = wide(q_ref) * jnp.exp(b)
        ki = (k * jnp.exp(-b)).astype(BF16)
        kd = (k * jnp.exp(b_last - b)).astype(BF16)
        grow = jnp.exp(b_last)
        qh = {(n, p, hh): jnp.where((lane < GLA_DK) if hh == 0 else (lane >= GLA_DK), slab(qi, n, p), 0.0).astype(BF16)
              for n, p, hh in heads}
        att = {(n, p, hh): _dot_nt(qh[n, p, hh], slab(ki, n, p)) for n, p, hh in heads}
        vh = {(n, p, hh): take(v_ref, n, vcols(p, hh)).astype(BF16) for n, p, hh in heads}
        s_old = {(n, p): s_scr[n, p] for n, p in combos}
        from_state = {(n, p, hh): _dot(qh[n, p, hh], s_old[n, p].astype(BF16)) for n, p, hh in heads}
        upd = {(n, p, hh): _dot_tn(slab(kd, n, p), vh[n, p, hh]) for n, p, hh in heads}
        for n, p, hh in heads:
            within = _dot(jnp.where(causal, att[n, p, hh], 0.0).astype(BF16), vh[n, p, hh])
            o_ref[n, rows, vcols(p, hh)] = (from_state[n, p, hh] + within)[:min(tt, c_rows)]
        for n, p in combos:
            decay = jnp.transpose(jnp.broadcast_to(slab(grow, n, p), (LANES, LANES)))
            s_scr[n, p] = decay * s_old[n, p] + jnp.where(srow < GLA_DK, upd[n, p, 0], upd[n, p, 1])
        return carry

    n_chunks = max(1, tt // c_rows)
    lax.fori_loop(0, n_chunks, chunk, 0, unroll=min(2, n_chunks))

    @pl.when(t == pl.num_programs(1) - 1)
    def _():
        so_ref[...] = s_scr[...]


def _gla(q, k, v, la, s0, nbb, tt):
    nb, nt, hk = q.shape
    hv = v.shape[-1]
    has_s0 = s0 is not None
    act = lambda w: pl.BlockSpec((nbb, tt, w), lambda b, t: (b, t, 0))
    st = pl.BlockSpec((nbb, GLA_HEADS // 2, LANES, GLA_DV), lambda b, t: (b, 0, 0, 0))
    return pl.pallas_call(
        functools.partial(_gla_kernel, nbb=nbb, tt=tt, has_s0=has_s0),
        grid=(nb // nbb, nt // tt),
        in_specs=[act(hk), act(hk), act(hv), act(hk)] + ([st] if has_s0 else []),
        out_specs=[act(hv), st],
        out_shape=[_sds((nb, nt, hv)), _sds((nb, GLA_HEADS // 2, LANES, GLA_DV))],
        scratch_shapes=[pltpu.VMEM((nbb, GLA_HEADS // 2, LANES, GLA_DV), F32)],
        compiler_params=_cp("parallel", "arbitrary"),
    )(*([q, k, v, la] + ([s0] if has_s0 else [])))


def _gla_gate(o, r, on):
    outs = []
    for h in range(GLA_HEADS):
        sl = slice(GLA_DV * h, GLA_DV * (h + 1))
        outs.append(_rms(o[:, sl], on[:, sl]))
    return jnp.concatenate(outs, axis=-1) * _silu(r)


def _even_mix(og, pooled, pw_ref, ps_ref, wo_ref):
    mixed = [_dot(pooled[g].astype(BF16), pw_ref[g]) for g in range(len(POOL_WINDOWS))]
    mixed = jnp.concatenate(mixed, axis=-1) * ps_ref[...]
    n_o = og.shape[-1]
    return _dot(og.astype(BF16), wo_ref[:n_o, :]) + _dot(mixed.astype(BF16), wo_ref[n_o:, :])


def _even_out_prompt_kernel(x_ref, o_ref, r_ref, u_ref, gt_ref, on_ref, pw_ref, ps_ref, wo_ref,
                            xo_ref, ho_ref, hb, *, tm, p0):
    t = pl.program_id(1)

    @pl.when(t == 0)
    def _():
        hb[0:POOL_PAD, :] = jnp.zeros((POOL_PAD, POOL_WIDTH), F32)

    hb[POOL_PAD:POOL_PAD + tm, :] = u_ref[0]
    block = min(EVEN_BLOCK, tm)
    for q0 in range(0, tm, block):
        rows = slice(q0, q0 + block)
        pos = t * tm + q0 + lax.broadcasted_iota(jnp.int32, (block, 1), 0)
        pooled = []
        for g, w in enumerate(POOL_WINDOWS):
            sl = slice(POOL_GROUP * g, POOL_GROUP * (g + 1))
            win = hb[q0:q0 + POOL_PAD + block, sl]
            span = 1
            while span < w:
                win = win + pltpu.roll(win, span, 0)
                span *= 2
            cnt = jnp.minimum(p0 + pos + 1, w).astype(F32)
            pooled.append(win[POOL_PAD:] / cnt - u_ref[0, rows, sl])
        og = _gla_gate(o_ref[0, rows, :], r_ref[0, rows, :], on_ref[...])
        y = _even_mix(og, pooled, pw_ref, ps_ref, wo_ref)
        xo_ref[0, rows, :] = x_ref[0, rows, :] + gt_ref[...] * y
    hb[0:POOL_PAD, :] = hb[tm:tm + POOL_PAD, :]

    @pl.when(t == pl.num_programs(1) - 1)
    def _():
        ho_ref[0] = hb[0:POOL_PAD, :]


def _even_out_sample_kernel(x_ref, o_ref, r_ref, u_ref, hi_ref, gt_ref, on_ref, pw_ref, ps_ref, wo_ref,
                            xo_ref, hn_ref, *, p0):
    steps, nb = u_ref.shape[:2]

    def slab(i, sl):
        return hi_ref[i, :, sl] if i < POOL_HIST else u_ref[i - POOL_HIST, :, sl]

    pooled = []
    for g, w in enumerate(POOL_WINDOWS):
        sl = slice(POOL_GROUP * g, POOL_GROUP * (g + 1))
        rows = []
        for t in range(steps):
            win = slab(POOL_HIST + t - (w - 1), sl)
            for j in range(w - 2, -1, -1):
                win = win + slab(POOL_HIST + t - j, sl)
            rows.append(win / float(min(p0 + t + 1, w)) - u_ref[t, :, sl])
        pooled.append(jnp.concatenate(rows, axis=0))
    flat = lambda ref: ref[...].reshape(steps * nb, ref.shape[-1])
    og = _gla_gate(flat(o_ref), flat(r_ref), on_ref[...])
    x = x_ref[...]
    xo_ref[...] = x + gt_ref[...] * _even_mix(og, pooled, pw_ref, ps_ref, wo_ref).reshape(x.shape)
    for i in range(POOL_HIST):
        hn_ref[i] = slab(steps + i, slice(None))


def _even_out(grp, x, o, r, u, hist, mod, on, pw, ps, wo, p0):
    dm = x.shape[-1]
    common = [mod, on, pw, ps, wo]
    common_specs = [grp.mod(5), _full(on), _full(pw), _full(ps), _full(wo)]
    acts = [grp.act(dm), grp.act(o.shape[-1]), grp.act(r.shape[-1]), grp.act(u.shape[-1])]
    if grp.prompt:
        tm = grp.block[1]
        return pl.pallas_call(
            functools.partial(_even_out_prompt_kernel, tm=tm, p0=p0),
            grid=grp.grid,
            in_specs=acts + common_specs,
            out_specs=[grp.act(dm), pl.BlockSpec((1, POOL_PAD, POOL_WIDTH), lambda b, t: (b, 0, 0))],
            out_shape=[_sds(x.shape), _sds((x.shape[0], POOL_PAD, POOL_WIDTH))],
            scratch_shapes=[pltpu.VMEM((POOL_PAD + tm, POOL_WIDTH), F32)],
            compiler_params=_cp("parallel", "arbitrary"),
        )(x, o, r, u, *common)
    return pl.pallas_call(
        functools.partial(_even_out_sample_kernel, p0=p0),
        grid=grp.grid,
        in_specs=acts + [_full(hist)] + common_specs,
        out_specs=[grp.act(dm), _full(hist)],
        out_shape=[_sds(x.shape), _sds(hist.shape)],
        compiler_params=_cp("parallel", "arbitrary"),
    )(x, o, r, u, hist, *common)


def _rope_slab(x, cs, s1, s2, lead):
    shp = lead + (LANES,)
    back = pltpu.roll(x, LANES - MLA_ROPE // 2, 1).reshape(shp)
    fwd = pltpu.roll(x, MLA_ROPE // 2, 1).reshape(shp)
    out = x.reshape(shp) * cs + back * s1 + fwd * s2
    return out.reshape(x.shape)


def _odd_in_kernel(*refs, sample):
    (x_ref, sh_ref, sc_ref, g_ref, wcq_ref, wckv_ref, wkr_ref, wga_ref, wgg_ref, qn_ref, wuq_ref,
     kvn_ref, cs_ref, s1_ref, s2_ref) = refs[:15]
    if sample:
        wkl_ref, ckv_ref, kr_ref, uc_ref, q_ref, ql_ref = refs[15:]
    else:
        wuk_ref, wuvt_ref, ckv_ref, kr_ref, uc_ref, q_ref, k_ref, vt_ref = refs[15:]
    lead = x_ref.shape[:2]
    h = _modulated(x_ref, g_ref, sh_ref, sc_ref)
    cs, s1, s2 = cs_ref[...], s1_ref[...], s2_ref[...]

    cq = _rms(_dot(h, wcq_ref[...]), qn_ref[...]).astype(BF16)
    q = _dot(cq, wuq_ref[...]) * (MLA_SCALE * LOG2E)
    q = jnp.concatenate(
        [_rope_slab(q[:, LANES * i:LANES * (i + 1)], cs, s1, s2, lead) for i in range(MLA_HEADS)], axis=-1)
    q_ref[...] = q.astype(BF16).reshape(lead + (MLA_HEADS * LANES,))

    ckv = _rms(_dot(h, wckv_ref[...]), kvn_ref[...])
    ckv_ref[...] = ckv.reshape(lead + (MLA_KV_RANK,))
    kr = _rope_slab(_dot(h, wkr_ref[...]), cs, s1, s2, lead)
    kr_ref[...] = kr.reshape(lead + (LANES,))
    uc_ref[...] = (_dot(h, wga_ref[...]) * jax.nn.sigmoid(_dot(h, wgg_ref[...]))).reshape(lead + (CONV_CH,))

    if sample:
        qb = q.astype(BF16)
        for i in range(MLA_HEADS):
            ql = _dot(qb[:, LANES * i:LANES * (i + 1)], wkl_ref[i])
            ql_ref[:, :, MLA_KV_RANK * i:MLA_KV_RANK * (i + 1)] = ql.astype(BF16).reshape(lead + (MLA_KV_RANK,))
    else:
        cb = ckv.astype(BF16)
        kn = _dot(cb, wuk_ref[...])
        kn = jnp.concatenate([kn[:, LANES * i:LANES * (i + 1)] + kr for i in range(MLA_HEADS)], axis=-1)
        k_ref[...] = kn.astype(BF16).reshape(lead + (MLA_HEADS * LANES,))
        one_row = lax.broadcasted_iota(jnp.int32, (MLA_HEADS, VT_ROWS, 1), 1) == MLA_V
        ones = jnp.where(one_row, 1.0, 0.0).reshape(MLA_HEADS * VT_ROWS, 1)
        vt_ref[0] = (_dot_nt(wuvt_ref[...], cb) + ones).astype(BF16)


def _odd_in(grp, x, mod, g, w, tabs, sample):
    lead = x.shape[:2]
    hl = MLA_HEADS * LANES
    ins = [x, mod, mod, g, w['wcq'], w['wckv'], w['wkr'], w['wga'], w['wgg'], w['qn'], w['wuq'], w['kvn'], *tabs]
    specs = [grp.act(x.shape[-1]), grp.mod(3), grp.mod(4)] + [_full(a) for a in ins[3:12]] + [grp.pos()] * 3
    outs = [(MLA_KV_RANK, F32), (LANES, F32), (CONV_CH, F32), (hl, BF16)]
    if sample:
        ins.append(w['wkl'])
        outs.append((MLA_HEADS * MLA_KV_RANK, BF16))
    else:
        ins += [w['wuk'], w['wuvt']]
        outs.append((hl, BF16))
    specs += [_full(a) for a in ins[15:]]
    out_specs = [grp.act(wd) for wd, _ in outs]
    out_shape = [_sds(lead + (wd,), dt) for wd, dt in outs]
    if not sample:
        vt_rows = MLA_HEADS * VT_ROWS
        out_specs.append(pl.BlockSpec((1, vt_rows, grp.block[1]), lambda b, t: (b, 0, t)))
        out_shape.append(_sds((lead[0], vt_rows, lead[1]), BF16))
    return pl.pallas_call(
        functools.partial(_odd_in_kernel, sample=sample),
        grid=grp.grid,
        in_specs=specs,
        out_specs=out_specs,
        out_shape=out_shape,
        compiler_params=_cp("parallel", "parallel"),
    )(*ins)


def _attn_prompt_kernel(qi_ref, ki_ref, q_ref, k_ref, vt_ref, o_ref, m_scr, acc_scr, s_scr, p_scr, *, tq):
    step = pl.program_id(1)
    qi = qi_ref[step]
    ki = ki_ref[step]

    @pl.when(ki == 0)
    def _():
        m_scr[...] = jnp.full_like(m_scr, NEG)
        acc_scr[...] = jnp.zeros_like(acc_scr)

    tk = k_ref.shape[1]
    lead = qi * tq - ki * tk

    def update(lead_rows):
        diagonal = lead_rows is not None
        width = s_scr.shape[2]
        items = [(h, c) for h in range(MLA_HEADS) for c in range(0, tq, width)]

        def keys_used(c):
            return min(tk, lead_rows + c + width) if diagonal else tk

        def scores(item, buf):
            h, c = item
            nk = keys_used(c)
            sl = slice(LANES * h, LANES * (h + 1))
            st = _dot_nt(k_ref[0, :nk, sl], q_ref[0, c:c + width, sl])
            if diagonal:
                key = lax.broadcasted_iota(jnp.int32, (nk, width), 0)
                qry = lax.broadcasted_iota(jnp.int32, (nk, width), 1) + (lead_rows + c)
                st = jnp.where(key <= qry, st, NEG)
            s_scr[buf, :nk] = st

        def absorb(item, buf):
            h, c = item
            nk = keys_used(c)
            vr = slice(VT_ROWS * h, VT_ROWS * (h + 1))
            qc = slice(c, c + width)
            m_prev = m_scr[h:h + 1, qc]
            m_new = jnp.maximum(m_prev, jnp.max(s_scr[buf, :nk], axis=0, keepdims=True))
            p_scr[buf, :nk] = jnp.exp2(s_scr[buf, :nk] - m_new).astype(BF16)
            acc_scr[vr, qc] = (jnp.exp2(m_prev - m_new) * acc_scr[vr, qc]
                               + _dot(vt_ref[0, vr, :nk], p_scr[buf, :nk]))
            m_scr[h:h + 1, qc] = m_new

        depth = s_scr.shape[0]
        for i in range(depth - 1):
            scores(items[i], i)
        for i, item in enumerate(items):
            if i + depth - 1 < len(items):
                scores(items[i + depth - 1], (i + depth - 1) % depth)
            absorb(item, i % depth)

    def finish():
        for h in range(0, MLA_HEADS, 2):
            pair = []
            for hh in (h, h + 1):
                acc = acc_scr[VT_ROWS * hh:VT_ROWS * (hh + 1), :]
                pair.append(acc[:MLA_V] / acc[MLA_V:MLA_V + 1])
            o_ref[0, :, MLA_V * h:MLA_V * (h + 2)] = jnp.transpose(jnp.concatenate(pair, axis=0)).astype(BF16)

    @pl.when(lead >= tk)
    def _():
        update(None)

    for lead_rows in range(0, tk, tq):
        @pl.when(lead == lead_rows)
        def _():
            update(lead_rows)
            finish()


def _attn_prompt(q, k, vt, tq, tk):
    nb, nt, hl = q.shape
    vt_rows = vt.shape[1]
    n_out = MLA_HEADS * MLA_V
    assert nt % tq == 0 and nt % tk == 0 and tk % tq == 0
    width = min(ATTN_QCHUNK, tq)
    pairs = [(i, j) for i in range(nt // tq) for j in range((i * tq + tq - 1) // tk + 1)]
    qi_tab = jnp.asarray([p[0] for p in pairs], jnp.int32)
    ki_tab = jnp.asarray([p[1] for p in pairs], jnp.int32)
    return pl.pallas_call(
        functools.partial(_attn_prompt_kernel, tq=tq),
        grid_spec=pltpu.PrefetchScalarGridSpec(
            num_scalar_prefetch=2,
            grid=(nb, len(pairs)),
            in_specs=[pl.BlockSpec((1, tq, hl), lambda b, s, qt, kt: (b, qt[s], 0)),
                      pl.BlockSpec((1, tk, hl), lambda b, s, qt, kt: (b, kt[s], 0)),
                      pl.BlockSpec((1, vt_rows, tk), lambda b, s, qt, kt: (b, 0, kt[s]))],
            out_specs=pl.BlockSpec((1, tq, n_out), lambda b, s, qt, kt: (b, qt[s], 0)),
            scratch_shapes=[pltpu.VMEM((MLA_HEADS, tq), F32), pltpu.VMEM((vt_rows, tq), F32),
                            pltpu.VMEM((ATTN_DEPTH, tk, width), F32),
                            pltpu.VMEM((ATTN_DEPTH, tk, width), BF16)]),
        out_shape=_sds((nb, nt, n_out), BF16),
        compiler_params=_cp("parallel", "arbitrary"),
    )(qi_tab, ki_tab, q, k, vt)


def _attn_sample_kernel(pt_ref, ql_ref, qr_ref, cn_ref, kn_ref, ckv_hbm, kr_hbm, o_ref,
                        ckv_buf, kr_buf, pg_scr, s_scr, p_scr, sem, *, npg, layer):
    b = pl.program_id(0)
    nb = pl.num_programs(0)
    slot = lax.rem(b, 2)
    ql = ql_ref[0]
    qr = qr_ref[0]
    rows = ql.shape[0]

    def page_copies(page, i, sl):
        return (pltpu.make_async_copy(ckv_hbm.at[layer, page], ckv_buf.at[sl, i], sem.at[0, sl]),
                pltpu.make_async_copy(kr_hbm.at[layer, page], kr_buf.at[sl, i], sem.at[1, sl]))

    def start_page(batch, i, sl):
        for cp in page_copies(pt_ref[batch, i], i, sl):
            cp.start()

    def start_first(i, carry):
        start_page(0, i, 0)
        return carry

    @pl.when(b == 0)
    def _():
        lax.fori_loop(0, npg, start_first, 0)

    for i in range(npg):
        for cp in page_copies(0, i, slot):
            cp.wait()

    def score_pages(prefetch):
        for i in range(npg):
            if prefetch:
                for k in range(2 * i, min(2 * i + 2, npg)):
                    start_page(b + 1, k, 1 - slot)
            page = ckv_buf[slot, i].astype(BF16)
            pg_scr[i] = page
            s_scr[i] = _dot_nt(ql, page) + _dot(qr, kr_buf[slot, i].astype(BF16))

    @pl.when(b + 1 < nb)
    def _():
        score_pages(True)

    @pl.when(b + 1 == nb)
    def _():
        score_pages(False)

    n_new = cn_ref.shape[1]
    pad = jnp.zeros((LANES - n_new, MLA_KV_RANK), F32)
    cn = jnp.concatenate([cn_ref[0], pad], axis=0).astype(BF16)
    kn = jnp.concatenate([kn_ref[0], pad[:, :MLA_ROPE]], axis=0).astype(BF16)
    step = lax.shift_right_logical(lax.broadcasted_iota(jnp.int32, (rows, LANES), 0), MLA_HEADS.bit_length() - 1)
    col = lax.broadcasted_iota(jnp.int32, (rows, LANES), 1)
    s_new = jnp.where(col <= step, _dot_nt(ql, cn) + _dot_nt(qr, kn), NEG)

    s = s_scr[...]
    m = jnp.maximum(jnp.max(jnp.max(s, axis=0), axis=-1, keepdims=True),
                    jnp.max(s_new, axis=-1, keepdims=True))
    p = jnp.exp2(s - m)
    p_new = jnp.exp2(s_new - m)
    denom = jnp.sum(jnp.sum(p, axis=0), axis=-1, keepdims=True) + jnp.sum(p_new, axis=-1, keepdims=True)
    p_scr[...] = p.astype(BF16)

    def weigh_page(i, acc):
        return acc + _dot(p_scr[i], pg_scr[i])

    acc = lax.fori_loop(0, npg, weigh_page, _dot(p_new.astype(BF16), cn), unroll=True)
    o_ref[0] = acc / denom


def _attn_sample(page_table, ql, qr, cn, kn, cache_ckv, cache_kr, layer):
    nb, rows, rank = ql.shape
    npg = page_table.shape[1]
    page, rope = cache_ckv.shape[2], cache_kr.shape[2]
    assert page == LANES, "one cache page must fill one lane tile of scores"
    per_b = lambda a: pl.BlockSpec((1,) + a.shape[1:], lambda b, pt: (b, 0, 0))
    hbm = pl.BlockSpec(memory_space=pl.ANY)
    return pl.pallas_call(
        functools.partial(_attn_sample_kernel, npg=npg, layer=layer),
        grid_spec=pltpu.PrefetchScalarGridSpec(
            num_scalar_prefetch=1,
            grid=(nb,),
            in_specs=[per_b(ql), per_b(qr), per_b(cn), per_b(kn), hbm, hbm],
            out_specs=pl.BlockSpec((1, rows, rank), lambda b, pt: (b, 0, 0)),
            scratch_shapes=[pltpu.VMEM((2, npg, page, rank), F32),
                            pltpu.VMEM((2, npg, rope, page), F32),
                            pltpu.VMEM((npg, page, rank), BF16),
                            pltpu.VMEM((npg, rows, page), F32),
                            pltpu.VMEM((npg, rows, page), BF16),
                            pltpu.SemaphoreType.DMA((2, 2))]),
        out_shape=_sds((nb, rows, rank)),
        compiler_params=_cp("arbitrary"),
    )(page_table, ql, qr, cn, kn, cache_ckv, cache_kr)


def _conv_norm_act(cv, cb_ref, lg_ref, lb_ref):
    cv = cv + cb_ref[...]
    mu = jnp.mean(cv, axis=-1, keepdims=True)
    d = cv - mu
    y = d * lax.rsqrt(jnp.mean(d * d, axis=-1, keepdims=True) + EPS)
    return _silu(y * lg_ref[...] + lb_ref[...])


def _odd_out_prompt_kernel(x_ref, a_ref, uc_ref, gt_ref, cw_ref, cb_ref, lg_ref, lb_ref, woa_ref, woc_ref,
                           xo_ref, ho_ref, hb, hs, cvb, *, tm):
    t = pl.program_id(1)

    @pl.when(t == 0)
    def _():
        hb[0:CONV_PAD, :] = jnp.zeros((CONV_PAD, CONV_CH), F32)

    hb[CONV_PAD:CONV_PAD + tm, :] = uc_ref[0]
    for sft in range(1, SUBLANES):
        hs[sft - 1] = hb[sft:sft + hs.shape[1], :]
    base = CONV_PAD - CONV_HIST

    block = min(CONV_BLOCK, tm)
    for q0 in range(0, tm, block):
        ya = _dot(a_ref[0, q0:q0 + block, :], woa_ref[...])
        for r0 in range(q0, q0 + block, CONV_ROWS):
            acc = None
            for j in range(CONV_WIDTH):
                whole, sft = divmod(base + j, SUBLANES)
                rows = slice(r0 + whole * SUBLANES, r0 + whole * SUBLANES + CONV_ROWS)
                term = cw_ref[j:j + 1, :] * (hb[rows, :] if sft == 0 else hs[sft - 1, rows, :])
                acc = term if acc is None else acc + term
            cvb[r0:r0 + CONV_ROWS, :] = acc
        cv = _conv_norm_act(cvb[q0:q0 + block, :], cb_ref, lg_ref, lb_ref)
        y = ya + _dot(cv.astype(BF16), woc_ref[...])
        xo_ref[0, q0:q0 + block, :] = x_ref[0, q0:q0 + block, :] + gt_ref[...] * y
    hb[0:CONV_PAD, :] = hb[tm:tm + CONV_PAD, :]

    @pl.when(t == pl.num_programs(1) - 1)
    def _():
        ho_ref[0] = hb[0:CONV_PAD, :]


def _odd_out_sample_kernel(x_ref, lat_ref, uc_ref, hi_ref, gt_ref, cw_ref, cb_ref, lg_ref, lb_ref, wuv_ref,
                           woa_ref, woc_ref, xo_ref, hn_ref):
    steps, nb = uc_ref.shape[:2]

    def slab(i):
        return hi_ref[i] if i < CONV_HIST else uc_ref[i - CONV_HIST]

    rows = []
    for t in range(steps):
        cv = cw_ref[0:1, :] * slab(t)
        for j in range(1, CONV_WIDTH):
            cv = cv + cw_ref[j:j + 1, :] * slab(t + j)
        rows.append(cv)
    cv = _conv_norm_act(jnp.concatenate(rows, axis=0), cb_ref, lg_ref, lb_ref)
    lat = lat_ref[...].reshape(steps * nb, lat_ref.shape[-1]).astype(BF16)
    attn = _dot(lat, wuv_ref[...]).astype(BF16)
    y = _dot(attn, woa_ref[...]) + _dot(cv.astype(BF16), woc_ref[...])
    x = x_ref[...]
    xo_ref[...] = x + gt_ref[...] * y.reshape(x.shape)
    for i in range(CONV_HIST):
        hn_ref[i] = slab(steps + i)


def _odd_out(grp, x, a, uc, hist, mod, w):
    dm = x.shape[-1]
    conv = [w['cw'], w['cb'], w['lg'], w['lb']]
    if grp.prompt:
        tm = grp.block[1]
        ins = [x, a, uc, mod] + conv + [w['woa'], w['woc']]
        return pl.pallas_call(
            functools.partial(_odd_out_prompt_kernel, tm=tm),
            grid=grp.grid,
            in_specs=[grp.act(dm), grp.act(a.shape[-1]), grp.act(CONV_CH), grp.mod(5)] + [_full(v) for v in ins[4:]],
            out_specs=[grp.act(dm), pl.BlockSpec((1, CONV_PAD, CONV_CH), lambda b, t: (b, 0, 0))],
            out_shape=[_sds(x.shape), _sds((x.shape[0], CONV_PAD, CONV_CH))],
            scratch_shapes=[pltpu.VMEM((CONV_PAD + tm, CONV_CH), F32),
                            pltpu.VMEM((SUBLANES - 1, CONV_PAD + tm - SUBLANES, CONV_CH), F32),
                            pltpu.VMEM((tm, CONV_CH), F32)],
            compiler_params=_cp("parallel", "arbitrary"),
        )(*ins)
    ins = [x, a, uc, hist, mod] + conv + [w['wuv_bd'], w['woa'], w['woc']]
    return pl.pallas_call(
        _odd_out_sample_kernel,
        grid=grp.grid,
        in_specs=[grp.act(dm), grp.act(a.shape[-1]), grp.act(CONV_CH), _full(hist), grp.mod(5)]
        + [_full(v) for v in ins[5:]],
        out_specs=[grp.act(dm), _full(hist)],
        out_shape=[_sds(x.shape), _sds(hist.shape)],
        compiler_params=_cp("parallel", "arbitrary"),
    )(*ins)


def _head_pad(w, heads, width, offset=0):
    kdim = w.shape[0]
    w = w.reshape(kdim, heads, width)
    w = jnp.pad(w, ((0, 0), (0, 0), (offset, LANES - width - offset)))
    return w.reshape(kdim, heads * LANES)


def _rope_tables(pos):
    half = MLA_ROPE // 2
    freqs = ROPE_BASE ** (-np.arange(half, dtype=np.float64) / half)
    ang = np.asarray(pos, np.float64)[..., None] * freqs
    cos, sin = jnp.asarray(np.cos(ang), F32), jnp.asarray(np.sin(ang), F32)
    zeros = jnp.zeros_like(cos)
    lead = jnp.ones(pos.shape + (ROPE_LANE0,), F32)
    tail = jnp.zeros(pos.shape + (LANES - ROPE_LANE0 - MLA_ROPE,), F32)
    cs = jnp.concatenate([lead, cos, cos, tail], axis=-1)
    s1 = jnp.concatenate([0 * lead, -sin, zeros, tail], axis=-1)
    s2 = jnp.concatenate([0 * lead, zeros, sin, tail], axis=-1)
    return cs, s1, s2


def _even_weights(w_in, gate_w2, gate_b, out_norm, pool_w, pool_scale, w_out):
    hk = GLA_HEADS * GLA_DK
    hv = GLA_HEADS * GLA_DV
    n_main = 2 * hk + 2 * hv
    wg = jnp.pad(w_in[:, n_main:n_main + GLA_GATE_RANK], ((0, 0), (0, LANES - GLA_GATE_RANK)))
    w2 = jnp.pad(gate_w2, ((0, LANES - GLA_GATE_RANK), (0, 0)))
    return dict(wq=w_in[:, :n_main].astype(BF16), wg=wg.astype(BF16), w2=w2.astype(BF16),
                gb=gate_b[None], wu=w_in[:, n_main + GLA_GATE_RANK:].astype(BF16),
                on=out_norm[None], pw=pool_w.astype(BF16), ps=pool_scale[None], wo=w_out.astype(BF16))


def _odd_weights(w_in, q_norm, w_uq, kv_norm, w_uk, w_uv, conv_w, conv_b, ln_g, ln_b, w_out):
    c0, c1, c2 = MLA_Q_RANK, MLA_Q_RANK + MLA_KV_RANK, MLA_Q_RANK + MLA_KV_RANK + MLA_ROPE
    n_attn = MLA_HEADS * MLA_V
    wkr = jnp.pad(w_in[:, c1:c2], ((0, 0), (ROPE_LANE0, LANES - ROPE_LANE0 - MLA_ROPE)))
    wuk = w_uk.reshape(MLA_KV_RANK, MLA_HEADS * MLA_NOPE)
    wkl = jnp.pad(jnp.transpose(w_uk, (1, 2, 0)), ((0, 0), (0, LANES - MLA_NOPE), (0, 0)))
    eye = jnp.eye(MLA_HEADS, dtype=F32)
    wuv_bd = (eye[:, None, :, None] * jnp.transpose(w_uv, (1, 0, 2))[:, :, None, :]).reshape(
        MLA_HEADS * MLA_KV_RANK, n_attn)
    wuvt = jnp.pad(jnp.transpose(w_uv, (1, 2, 0)), ((0, 0), (0, VT_ROWS - MLA_V), (0, 0))).reshape(
        MLA_HEADS * VT_ROWS, MLA_KV_RANK)
    return dict(wcq=w_in[:, :c0].astype(BF16), wckv=w_in[:, c0:c1].astype(BF16), wkr=wkr.astype(BF16),
                wga=w_in[:, c2:c2 + CONV_CH].astype(BF16), wgg=w_in[:, c2 + CONV_CH:].astype(BF16),
                qn=q_norm[None], wuq=_head_pad(w_uq, MLA_HEADS, MLA_NOPE + MLA_ROPE).astype(BF16),
                kvn=kv_norm[None], wuk=_head_pad(wuk, MLA_HEADS, MLA_NOPE).astype(BF16),
                wuvt=wuvt.astype(BF16), wkl=wkl.astype(BF16),
                wuv_bd=wuv_bd.astype(BF16), cw=conv_w, cb=conv_b[None], lg=ln_g[None], lb=ln_b[None],
                woa=w_out[:n_attn].astype(BF16), woc=w_out[n_attn:].astype(BF16))


def _tm(x):
    return jnp.swapaxes(x, 0, 1)


def kernel(x_prompt, x_sample, state_gla, state_pool, cache_ckv, cache_krope, state_conv, page_table, c_prompt, c_sample, ada_w, ada_b, norm_g, ffn_w1, ffn_w3, ffn_w2, ev_w_in, ev_gate_w2, ev_gate_b, ev_out_norm, ev_pool_w, ev_pool_scale, ev_w_out, od_w_in, od_q_norm, od_w_uq, od_kv_norm, od_w_uk, od_w_uv, od_conv_w, od_conv_b, od_conv_norm_g, od_conv_norm_b, od_w_out, final_norm):
    nbp, seq, dm = x_prompt.shape
    nbs, steps, _ = x_sample.shape
    depth = ada_w.shape[0]
    past_len = page_table.shape[1] * cache_ckv.shape[2]
    tile = min(TOKEN_TILE, seq)
    gp = _Group(True, nbp, seq, tile, dm)
    gf = _Group(True, nbp, seq, min(FFN_TOKEN_TILE, seq), dm)
    gs = _Group(False, steps, nbs, nbs, dm)

    n_c = nbp + nbs
    c_all = jnp.pad(jnp.concatenate([c_sample, c_prompt], axis=0), ((0, -n_c % SUBLANES), (0, 0)))
    mod = _ada(c_all, ada_w, ada_b)
    mod_p = mod[:, :, nbs:n_c].reshape(depth, N_MOD, nbp, 1, dm)
    mod_s = mod if nbs % SUBLANES == 0 else mod[:, :, :nbs]

    w1, w3, w2 = ffn_w1.astype(BF16), ffn_w3.astype(BF16), ffn_w2.astype(BF16)
    tabs_p = _rope_tables(np.arange(seq)[None])
    tabs_s = _rope_tables(past_len + np.arange(steps)[:, None])

    xp = x_prompt
    xs = _tm(x_sample)
    gla_p, gla_s, pool_p, pool_s, ckv_p, ckv_s, kr_p, kr_s, conv_p, conv_s = ([] for _ in range(10))
    gla_tt = min(TOKEN_TILE, seq)

    for layer in range(depth):
        i = layer // 2
        ng = norm_g[layer]
        last = layer == depth - 1
        xp = _ffn(gf, xp, mod_p[layer], 0, ng[0:1], w1, w3, w2, (layer, 0))
        xs = _ffn(gs, xs, mod_s[layer], 0, ng[0:1], w1, w3, w2, (layer, 0))
        if layer % 2 == 0:
            w = _even_weights(ev_w_in[i], ev_gate_w2[i], ev_gate_b[i], ev_out_norm[i], ev_pool_w[i],
                              ev_pool_scale[i], ev_w_out[i])
            proj = (ng[1:2], w['wq'], w['wg'], w['w2'], w['gb'], w['wu'])
            out_w = (w['on'], w['pw'], w['ps'], w['wo'])
            q, k, v, r, la, u = _even_in(gp, xp, mod_p[layer], *proj)
            o, s_fin = _gla(q, k, v, la, None, nbp, gla_tt)
            xp, hist = _even_out(gp, xp, o, r, u, None, mod_p[layer], *out_w, 0)
            gla_p.append(s_fin.reshape(nbp, GLA_HEADS, GLA_DK, GLA_DV))
            pool_p.append(hist[:, POOL_PAD - POOL_HIST:])
            q, k, v, r, la, u = _even_in(gs, xs, mod_s[layer], *proj)
            rows8 = steps + (-steps % SUBLANES)
            chunked = lambda a: jnp.pad(_tm(a), ((0, 0), (0, rows8 - steps), (0, 0)))
            s0 = state_gla[i].reshape(nbs, GLA_HEADS // 2, LANES, GLA_DV)
            o, s_fin = _gla(chunked(q), chunked(k), chunked(v), chunked(la), s0,
                            GLA_SAMPLE_BATCHES if nbs % GLA_SAMPLE_BATCHES == 0 else 1, rows8)
            xs, hist = _even_out(gs, xs, _tm(o[:, :steps]), r, u, _tm(state_pool[i]), mod_s[layer], *out_w, past_len)
            gla_s.append(s_fin.reshape(nbs, GLA_HEADS, GLA_DK, GLA_DV))
            pool_s.append(_tm(hist))
        else:
            w = _odd_weights(od_w_in[i], od_q_norm[i], od_w_uq[i], od_kv_norm[i], od_w_uk[i], od_w_uv[i],
                             od_conv_w[i], od_conv_b[i], od_conv_norm_g[i], od_conv_norm_b[i], od_w_out[i])
            rope_lanes = slice(ROPE_LANE0, ROPE_LANE0 + MLA_ROPE)
            ckv, kr, uc, q, k, v = _odd_in(gp, xp, mod_p[layer], ng[1:2], w, tabs_p, False)
            attn = _attn_prompt(q, k, v, min(ATTN_TILE, seq), min(ATTN_KEYS, seq))
            xp, hist = _odd_out(gp, xp, attn, uc, None, mod_p[layer], w)
            ckv_p.append(ckv)
            kr_p.append(kr[..., rope_lanes])
            conv_p.append(hist[:, CONV_PAD - CONV_HIST:])
            ckv, kr, uc, q, ql = _odd_in(gs, xs, mod_s[layer], ng[1:2], w, tabs_s, True)
            kr = kr[..., rope_lanes]
            qr = q.reshape(steps, nbs, MLA_HEADS, LANES)[..., rope_lanes]
            qr = _tm(qr).reshape(nbs, steps * MLA_HEADS, MLA_ROPE)
            ql = _tm(ql).reshape(nbs, steps * MLA_HEADS, MLA_KV_RANK)
            pad8 = lambda a: jnp.pad(_tm(a), ((0, 0), (0, -steps % 8), (0, 0)))
            cache_kr_t = jnp.swapaxes(cache_krope, 2, 3)
            lat = _attn_sample(page_table, ql, qr, pad8(ckv), pad8(kr), cache_ckv, cache_kr_t, i)
            lat = _tm(lat.reshape(nbs, steps, MLA_HEADS * MLA_KV_RANK))
            xs, hist = _odd_out(gs, xs, lat, uc, _tm(state_conv[i]), mod_s[layer], w)
            ckv_s.append(_tm(ckv))
            kr_s.append(_tm(kr))
            conv_s.append(_tm(hist))
        fin = final_norm[None] if last else None
        xp = _ffn(gf, xp, mod_p[layer], 6, ng[2:3], w1, w3, w2, (layer, 1), fin)
        xs = _ffn(gs, xs, mod_s[layer], 6, ng[2:3], w1, w3, w2, (layer, 1), fin)

    st = jnp.stack
    return (xp, _tm(xs), st(gla_p), st(gla_s), st(pool_p), st(pool_s), st(ckv_p), st(ckv_s),
            st(kr_p), st(kr_s), st(conv_p), st(conv_s))
```

```python
import functools

import jax
import jax.numpy as jnp
import numpy as np
from jax import lax
from jax.experimental import pallas as pl
from jax.experimental.pallas import tpu as pltpu

F32 = jnp.float32
BF16 = jnp.bfloat16

EPS = 1e-6
NEG = -1e30
N_MOD = 9
GLA_HEADS = 4
GLA_DK = 64
GLA_DV = 128
GLA_GATE_RANK = 16
GLA_TAU = 16.0
GLA_CHUNK = 64
POOL_WINDOWS = (2, 4, 8, 16)
POOL_GROUP = 128
POOL_WIDTH = 512
POOL_HIST = 15
MLA_HEADS = 8
MLA_Q_RANK = 384
MLA_KV_RANK = 256
MLA_NOPE = 64
MLA_ROPE = 32
MLA_V = 64
MLA_SCALE = (MLA_NOPE + MLA_ROPE) ** -0.5
LOG2E = 1.4426950408889634
ROPE_BASE = 10000.0
CONV_WIDTH = 31
CONV_CH = 512
CONV_HIST = 30

LANES = 128
SUBLANES = 8
CONV_ROWS = 32
CONV_BLOCK = 256
EVEN_BLOCK = 256
ROPE_LANE0 = 64
VT_ROWS = 80
POOL_PAD = 16
assert all(w & (w - 1) == 0 and w <= POOL_PAD for w in POOL_WINDOWS)
CONV_PAD = 32
VMEM_LIMIT = 52 * 2 ** 20
TOKEN_TILE = 1024
ATTN_TILE = 1024
ATTN_KEYS = 1024
ATTN_DEPTH = 3
ATTN_QCHUNK = 512
FFN_TOKEN_TILE = 1024
FF_SUB = 1024
GLA_SAMPLE_BATCHES = 8


def _cp(*sem):
    return pltpu.CompilerParams(dimension_semantics=sem, vmem_limit_bytes=VMEM_LIMIT)


def _dot(a, b):
    return jnp.dot(a, b, preferred_element_type=F32)


def _dot_nt(a, b):
    return lax.dot_general(a, b, (((1,), (1,)), ((), ())), preferred_element_type=F32)


def _dot_tn(a, b):
    return lax.dot_general(a, b, (((0,), (0,)), ((), ())), preferred_element_type=F32)


def _silu(x):
    return x * jax.nn.sigmoid(x)


def _rms(x, g):
    return x * lax.rsqrt(jnp.mean(x * x, axis=-1, keepdims=True) + EPS) * g


def _modulated(x_ref, g_ref, sh_ref, sc_ref):
    x = x_ref[...]
    h = _rms(x, g_ref[...]) * (1.0 + sc_ref[...]) + sh_ref[...]
    return h.reshape(x.shape[0] * x.shape[1], x.shape[2]).astype(BF16)


class _Group:
    def __init__(self, prompt, lead, rows, tile, dm):
        self.prompt = prompt
        self.dm = dm
        self.grid = (lead, rows // tile) if prompt else (1, 1)
        self.block = (1, tile) if prompt else (lead, rows)

    def act(self, width):
        return pl.BlockSpec(self.block + (width,), lambda b, t, *_: (b, t, 0))

    def mod(self, k):
        if self.prompt:
            return pl.BlockSpec((None, None, 1, self.dm), lambda b, t, *_: (k, b, 0, 0))
        return pl.BlockSpec((None, self.block[1], self.dm), lambda b, t, *_: (k, 0, 0))

    def pos(self):
        if self.prompt:
            return pl.BlockSpec((1, self.block[1], LANES), lambda b, t, *_: (0, t, 0))
        return pl.BlockSpec((self.block[0], 1, LANES), lambda b, t, *_: (0, 0, 0))


def _full(a):
    nd = a.ndim
    return pl.BlockSpec(a.shape, lambda *_: (0,) * nd)


def _sds(shape, dtype=F32):
    return jax.ShapeDtypeStruct(shape, dtype)


def _ada_kernel(c_ref, w_ref, b_ref, o_ref):
    c = c_ref[...]
    o_ref[...] = _dot(_silu(c).astype(BF16), w_ref[...].astype(BF16)) + b_ref[...]


def _ada(c_all, ada_w, ada_b):
    depth, dm, n = ada_w.shape
    m = c_all.shape[0]
    tn = dm
    return pl.pallas_call(
        _ada_kernel,
        grid=(depth, n // tn),
        in_specs=[pl.BlockSpec((m, dm), lambda l, j: (0, 0)),
                  pl.BlockSpec((None, dm, tn), lambda l, j: (l, 0, j)),
                  pl.BlockSpec((None, 1, tn), lambda l, j: (l, 0, j))],
        out_specs=pl.BlockSpec((None, None, m, tn), lambda l, j: (l, j, 0, 0)),
        out_shape=_sds((depth, n // tn, m, tn)),
        compiler_params=_cp("parallel", "parallel"),
    )(c_all, ada_w, ada_b.reshape(depth, 1, n))


def _ffn_kernel(*refs, final):
    if final:
        x_ref, sh_ref, sc_ref, gt_ref, g_ref, w1_ref, w3_ref, w2_ref, fn_ref, o_ref = refs
    else:
        x_ref, sh_ref, sc_ref, gt_ref, g_ref, w1_ref, w3_ref, w2_ref, o_ref = refs
    h = _modulated(x_ref, g_ref, sh_ref, sc_ref)
    dff = w1_ref.shape[1]
    y = None
    for c0 in range(0, dff, FF_SUB):
        cols = slice(c0, min(c0 + FF_SUB, dff))
        a = _dot(h, w1_ref[:, cols])
        b = _dot(h, w3_ref[:, cols])
        part = _dot((_silu(a) * b).astype(BF16), w2_ref[cols, :])
        y = part if y is None else y + part
    x = x_ref[...]
    xn = x + 0.5 * gt_ref[...] * y.reshape(x.shape)
    o_ref[...] = _rms(xn, fn_ref[...]) if final else xn


def _resident(a, lead):
    idx = tuple(lead) + (0, 0)
    return pl.BlockSpec((None,) * len(lead) + a.shape[-2:], lambda *_: idx, pipeline_mode=pl.Buffered(1))


def _ffn(grp, x, mod, k0, g, w1, w3, w2, which, final_g=None):
    dm = x.shape[-1]
    final = final_g is not None
    ins = [x, mod, mod, mod, g, w1, w3, w2] + ([final_g] if final else [])
    specs = [grp.act(dm), grp.mod(k0), grp.mod(k0 + 1), grp.mod(k0 + 2), _full(g),
             _resident(w1, which), _resident(w3, which), _resident(w2, which)] + ([_full(final_g)] if final else [])
    return pl.pallas_call(
        functools.partial(_ffn_kernel, final=final),
        grid=grp.grid,
        in_specs=specs,
        out_specs=grp.act(dm),
        out_shape=_sds(x.shape),
        compiler_params=_cp("parallel", "parallel"),
    )(*ins)


def _log_sigmoid(x):
    return jnp.minimum(x, 0.0) - jnp.log(1.0 + jnp.exp(-jnp.abs(x)))


def _even_in_kernel(x_ref, sh_ref, sc_ref, g_ref, wq_ref, wg_ref, w2_ref, gb_ref, wu_ref,
                    q_ref, k_ref, v_ref, r_ref, la_ref, u_ref):
    h = _modulated(x_ref, g_ref, sh_ref, sc_ref)
    lead = x_ref.shape[:2]
    hk = GLA_HEADS * GLA_DK
    hv = GLA_HEADS * GLA_DV
    z = _dot(h, wq_ref[...])
    q_ref[...] = (z[:, :hk] * GLA_DK ** -0.5).reshape(lead + (hk,))
    k_ref[...] = z[:, hk:2 * hk].reshape(lead + (hk,))
    v_ref[...] = z[:, 2 * hk:2 * hk + hv].astype(BF16).reshape(lead + (hv,))
    r_ref[...] = z[:, 2 * hk + hv:].reshape(lead + (hv,))
    g_low = _dot(h, wg_ref[...]).astype(BF16)
    gate = _dot(g_low, w2_ref[...]) + gb_ref[...]
    la_ref[...] = (_log_sigmoid(gate) / GLA_TAU).reshape(lead + (hk,))
    u_ref[...] = _dot(h, wu_ref[...]).reshape(lead + (POOL_WIDTH,))


def _even_in(grp, x, mod, g, wq, wg, w2, gb, wu):
    hk = GLA_HEADS * GLA_DK
    hv = GLA_HEADS * GLA_DV
    widths = (hk, hk, hv, hv, hk, POOL_WIDTH)
    return pl.pallas_call(
        _even_in_kernel,
        grid=grp.grid,
        in_specs=[grp.act(x.shape[-1]), grp.mod(3), grp.mod(4), _full(g), _full(wq), _full(wg),
                  _full(w2), _full(gb), _full(wu)],
        out_specs=[grp.act(w) for w in widths],
        out_shape=[_sds(x.shape[:2] + (w,), BF16 if i == 2 else F32) for i, w in enumerate(widths)],
        compiler_params=_cp("parallel", "parallel"),
    )(x, mod, mod, g, wq, wg, w2, gb, wu)


def _cumsum_rows(tril, x):
    hi = x.astype(BF16)
    r1 = x - hi.astype(F32)
    mid = r1.astype(BF16)
    lo = (r1 - mid.astype(F32)).astype(BF16)
    return _dot(tril, hi) + _dot(tril, mid) + _dot(tril, lo)


def _gla_kernel(*refs, nbb, tt, has_s0):
    if has_s0:
        q_ref, k_ref, v_ref, la_ref, s0_ref, o_ref, so_ref, s_scr = refs
    else:
        q_ref, k_ref, v_ref, la_ref, o_ref, so_ref, s_scr = refs
    t = pl.program_id(1)
    c_rows = GLA_CHUNK

    @pl.when(t == 0)
    def _():
        s_scr[...] = s0_ref[...] if has_s0 else jnp.zeros_like(s_scr)

    row = lax.broadcasted_iota(jnp.int32, (c_rows, c_rows), 0)
    col = lax.broadcasted_iota(jnp.int32, (c_rows, c_rows), 1)
    causal = col <= row
    tril = jnp.where(causal, 1.0, 0.0).astype(BF16)
    lane = lax.broadcasted_iota(jnp.int32, (c_rows, LANES), 1)
    srow = lax.broadcasted_iota(jnp.int32, (LANES, LANES), 0)

    pairs = GLA_HEADS // 2
    combos = [(n, p) for n in range(nbb) for p in range(pairs)]
    heads = [(n, p, hh) for n, p in combos for hh in range(2)]

    def chunk(c, carry):
        short = tt < c_rows
        rows = slice(0, tt) if short else pl.ds(pl.multiple_of(c * c_rows, c_rows), c_rows)

        def take(ref, n, cols):
            x = ref[n, rows, cols]
            return jnp.concatenate([x, jnp.zeros((c_rows - tt, x.shape[1]), F32)], axis=0) if short else x

        wide = lambda ref: jnp.concatenate([take(ref, n, slice(None)) for n in range(nbb)], axis=-1)
        slab = lambda x, n, p: x[:, LANES * (n * pairs + p):LANES * (n * pairs + p + 1)]
        vcols = lambda p, hh: slice(GLA_DV * (2 * p + hh), GLA_DV * (2 * p + hh + 1))
        b = _cumsum_rows(tril, wide(la_ref))
        b_last = b[c_rows - 1:c_rows, :]
        k = wide(k_ref)
        qi = wide(q_ref) * jnp.exp(b)
        ki = (k * jnp.exp(-b)).astype(BF16)
        kd = (k * jnp.exp(b_last - b)).astype(BF16)
        grow = jnp.exp(b_last)
        qh = {(n, p, hh): jnp.where((lane < GLA_DK) if hh == 0 else (lane >= GLA_DK), slab(qi, n, p), 0.0).astype(BF16)
              for n, p, hh in heads}
        att = {(n, p, hh): _dot_nt(qh[n, p, hh], slab(ki, n, p)) for n, p, hh in heads}
        vh = {(n, p, hh): take(v_ref, n, vcols(p, hh)).astype(BF16) for n, p, hh in heads}
        s_old = {(n, p): s_scr[n, p] for n, p in combos}
        from_state = {(n, p, hh): _dot(qh[n, p, hh], s_old[n, p].astype(BF16)) for n, p, hh in heads}
        upd = {(n, p, hh): _dot_tn(slab(kd, n, p), vh[n, p, hh]) for n, p, hh in heads}
        for n, p, hh in heads:
            within = _dot(jnp.where(causal, att[n, p, hh], 0.0).astype(BF16), vh[n, p, hh])
            o_ref[n, rows, vcols(p, hh)] = (from_state[n, p, hh] + within)[:min(tt, c_rows)]
        for n, p in combos:
            decay = jnp.transpose(jnp.broadcast_to(slab(grow, n, p), (LANES, LANES)))
            s_scr[n, p] = decay * s_old[n, p] + jnp.where(srow < GLA_DK, upd[n, p, 0], upd[n, p, 1])
        return carry

    n_chunks = max(1, tt // c_rows)
    lax.fori_loop(0, n_chunks, chunk, 0, unroll=min(2, n_chunks))

    @pl.when(t == pl.num_programs(1) - 1)
    def _():
        so_ref[...] = s_scr[...]


def _gla(q, k, v, la, s0, nbb, tt):
    nb, nt, hk = q.shape
    hv = v.shape[-1]
    has_s0 = s0 is not None
    act = lambda w: pl.BlockSpec((nbb, tt, w), lambda b, t: (b, t, 0))
    st = pl.BlockSpec((nbb, GLA_HEADS // 2, LANES, GLA_DV), lambda b, t: (b, 0, 0, 0))
    return pl.pallas_call(
        functools.partial(_gla_kernel, nbb=nbb, tt=tt, has_s0=has_s0),
        grid=(nb // nbb, nt // tt),
        in_specs=[act(hk), act(hk), act(hv), act(hk)] + ([st] if has_s0 else []),
        out_specs=[act(hv), st],
        out_shape=[_sds((nb, nt, hv)), _sds((nb, GLA_HEADS // 2, LANES, GLA_DV))],
        scratch_shapes=[pltpu.VMEM((nbb, GLA_HEADS // 2, LANES, GLA_DV), F32)],
        compiler_params=_cp("parallel", "arbitrary"),
    )(*([q, k, v, la] + ([s0] if has_s0 else [])))


def _gla_gate(o, r, on):
    outs = []
    for h in range(GLA_HEADS):
        sl = slice(GLA_DV * h, GLA_DV * (h + 1))
        outs.append(_rms(o[:, sl], on[:, sl]))
    return jnp.concatenate(outs, axis=-1) * _silu(r)


def _even_mix(og, pooled, pw_ref, ps_ref, wo_ref):
    mixed = [_dot(pooled[g].astype(BF16), pw_ref[g]) for g in range(len(POOL_WINDOWS))]
    mixed = jnp.concatenate(mixed, axis=-1) * ps_ref[...]
    n_o = og.shape[-1]
    return _dot(og.astype(BF16), wo_ref[:n_o, :]) + _dot(mixed.astype(BF16), wo_ref[n_o:, :])


def _even_out_prompt_kernel(x_ref, o_ref, r_ref, u_ref, gt_ref, on_ref, pw_ref, ps_ref, wo_ref,
                            xo_ref, ho_ref, hb, *, tm, p0):
    t = pl.program_id(1)

    @pl.when(t == 0)
    def _():
        hb[0:POOL_PAD, :] = jnp.zeros((POOL_PAD, POOL_WIDTH), F32)

    hb[POOL_PAD:POOL_PAD + tm, :] = u_ref[0]
    block = min(EVEN_BLOCK, tm)
    for q0 in range(0, tm, block):
        rows = slice(q0, q0 + block)
        pos = t * tm + q0 + lax.broadcasted_iota(jnp.int32, (block, 1), 0)
        pooled = []
        for g, w in enumerate(POOL_WINDOWS):
            sl = slice(POOL_GROUP * g, POOL_GROUP * (g + 1))
            win = hb[q0:q0 + POOL_PAD + block, sl]
            span = 1
            while span < w:
                win = win + pltpu.roll(win, span, 0)
                span *= 2
            cnt = jnp.minimum(p0 + pos + 1, w).astype(F32)
            pooled.append(win[POOL_PAD:] / cnt - u_ref[0, rows, sl])
        og = _gla_gate(o_ref[0, rows, :], r_ref[0, rows, :], on_ref[...])
        y = _even_mix(og, pooled, pw_ref, ps_ref, wo_ref)
        xo_ref[0, rows, :] = x_ref[0, rows, :] + gt_ref[...] * y
    hb[0:POOL_PAD, :] = hb[tm:tm + POOL_PAD, :]

    @pl.when(t == pl.num_programs(1) - 1)
    def _():
        ho_ref[0] = hb[0:POOL_PAD, :]


def _even_out_sample_kernel(x_ref, o_ref, r_ref, u_ref, hi_ref, gt_ref, on_ref, pw_ref, ps_ref, wo_ref,
                            xo_ref, hn_ref, *, p0):
    steps, nb = u_ref.shape[:2]

    def slab(i, sl):
        return hi_ref[i, :, sl] if i < POOL_HIST else u_ref[i - POOL_HIST, :, sl]

    pooled = []
    for g, w in enumerate(POOL_WINDOWS):
        sl = slice(POOL_GROUP * g, POOL_GROUP * (g + 1))
        rows = []
        for t in range(steps):
            win = slab(POOL_HIST + t - (w - 1), sl)
            for j in range(w - 2, -1, -1):
                win = win + slab(POOL_HIST + t - j, sl)
            rows.append(win / float(min(p0 + t + 1, w)) - u_ref[t, :, sl])
        pooled.append(jnp.concatenate(rows, axis=0))
    flat = lambda ref: ref[...].reshape(steps * nb, ref.shape[-1])
    og = _gla_gate(flat(o_ref), flat(r_ref), on_ref[...])
    x = x_ref[...]
    xo_ref[...] = x + gt_ref[...] * _even_mix(og, pooled, pw_ref, ps_ref, wo_ref).reshape(x.shape)
    for i in range(POOL_HIST):
        hn_ref[i] = slab(steps + i, slice(None))


def _even_out(grp, x, o, r, u, hist, mod, on, pw, ps, wo, p0):
    dm = x.shape[-1]
    common = [mod, on, pw, ps, wo]
    common_specs = [grp.mod(5), _full(on), _full(pw), _full(ps), _full(wo)]
    acts = [grp.act(dm), grp.act(o.shape[-1]), grp.act(r.shape[-1]), grp.act(u.shape[-1])]
    if grp.prompt:
        tm = grp.block[1]
        return pl.pallas_call(
            functools.partial(_even_out_prompt_kernel, tm=tm, p0=p0),
            grid=grp.grid,
            in_specs=acts + common_specs,
            out_specs=[grp.act(dm), pl.BlockSpec((1, POOL_PAD, POOL_WIDTH), lambda b, t: (b, 0, 0))],
            out_shape=[_sds(x.shape), _sds((x.shape[0], POOL_PAD, POOL_WIDTH))],
            scratch_shapes=[pltpu.VMEM((POOL_PAD + tm, POOL_WIDTH), F32)],
            compiler_params=_cp("parallel", "arbitrary"),
        )(x, o, r, u, *common)
    return pl.pallas_call(
        functools.partial(_even_out_sample_kernel, p0=p0),
        grid=grp.grid,
        in_specs=acts + [_full(hist)] + common_specs,
        out_specs=[grp.act(dm), _full(hist)],
        out_shape=[_sds(x.shape), _sds(hist.shape)],
        compiler_params=_cp("parallel", "arbitrary"),
    )(x, o, r, u, hist, *common)


def _rope_slab(x, cs, s1, s2, lead):
    shp = lead + (LANES,)
    back = pltpu.roll(x, LANES - MLA_ROPE // 2, 1).reshape(shp)
    fwd = pltpu.roll(x, MLA_ROPE // 2, 1).reshape(shp)
    out = x.reshape(shp) * cs + back * s1 + fwd * s2
    return out.reshape(x.shape)


def _odd_in_kernel(*refs, sample):
    (x_ref, sh_ref, sc_ref, g_ref, wcq_ref, wckv_ref, wkr_ref, wga_ref, wgg_ref, qn_ref, wuq_ref,
     kvn_ref, cs_ref, s1_ref, s2_ref) = refs[:15]
    if sample:
        wkl_ref, ckv_ref, kr_ref, uc_ref, q_ref, ql_ref = refs[15:]
    else:
        wuk_ref, wuvt_ref, ckv_ref, kr_ref, uc_ref, q_ref, k_ref, vt_ref = refs[15:]
    lead = x_ref.shape[:2]
    h = _modulated(x_ref, g_ref, sh_ref, sc_ref)
    cs, s1, s2 = cs_ref[...], s1_ref[...], s2_ref[...]

    cq = _rms(_dot(h, wcq_ref[...]), qn_ref[...]).astype(BF16)
    q = _dot(cq, wuq_ref[...]) * (MLA_SCALE * LOG2E)
    q = jnp.concatenate(
        [_rope_slab(q[:, LANES * i:LANES * (i + 1)], cs, s1, s2, lead) for i in range(MLA_HEADS)], axis=-1)
    q_ref[...] = q.astype(BF16).reshape(lead + (MLA_HEADS * LANES,))

    ckv = _rms(_dot(h, wckv_ref[...]), kvn_ref[...])
    ckv_ref[...] = ckv.reshape(lead + (MLA_KV_RANK,))
    kr = _rope_slab(_dot(h, wkr_ref[...]), cs, s1, s2, lead)
    kr_ref[...] = kr.reshape(lead + (LANES,))
    uc_ref[...] = (_dot(h, wga_ref[...]) * jax.nn.sigmoid(_dot(h, wgg_ref[...]))).reshape(lead + (CONV_CH,))

    if sample:
        qb = q.astype(BF16)
        for i in range(MLA_HEADS):
            ql = _dot(qb[:, LANES * i:LANES * (i + 1)], wkl_ref[i])
            ql_ref[:, :, MLA_KV_RANK * i:MLA_KV_RANK * (i + 1)] = ql.astype(BF16).reshape(lead + (MLA_KV_RANK,))
    else:
        cb = ckv.astype(BF16)
        kn = _dot(cb, wuk_ref[...])
        kn = jnp.concatenate([kn[:, LANES * i:LANES * (i + 1)] + kr for i in range(MLA_HEADS)], axis=-1)
        k_ref[...] = kn.astype(BF16).reshape(lead + (MLA_HEADS * LANES,))
        one_row = lax.broadcasted_iota(jnp.int32, (MLA_HEADS, VT_ROWS, 1), 1) == MLA_V
        ones = jnp.where(one_row, 1.0, 0.0).reshape(MLA_HEADS * VT_ROWS, 1)
        vt_ref[0] = (_dot_nt(wuvt_ref[...], cb) + ones).astype(BF16)


def _odd_in(grp, x, mod, g, w, tabs, sample):
    lead = x.shape[:2]
    hl = MLA_HEADS * LANES
    ins = [x, mod, mod, g, w['wcq'], w['wckv'], w['wkr'], w['wga'], w['wgg'], w['qn'], w['wuq'], w['kvn'], *tabs]
    specs = [grp.act(x.shape[-1]), grp.mod(3), grp.mod(4)] + [_full(a) for a in ins[3:12]] + [grp.pos()] * 3
    outs = [(MLA_KV_RANK, F32), (LANES, F32), (CONV_CH, F32), (hl, BF16)]
    if sample:
        ins.append(w['wkl'])
        outs.append((MLA_HEADS * MLA_KV_RANK, BF16))
    else:
        ins += [w['wuk'], w['wuvt']]
        outs.append((hl, BF16))
    specs += [_full(a) for a in ins[15:]]
    out_specs = [grp.act(wd) for wd, _ in outs]
    out_shape = [_sds(lead + (wd,), dt) for wd, dt in outs]
    if not sample:
        vt_rows = MLA_HEADS * VT_ROWS
        out_specs.append(pl.BlockSpec((1, vt_rows, grp.block[1]), lambda b, t: (b, 0, t)))
        out_shape.append(_sds((lead[0], vt_rows, lead[1]), BF16))
    return pl.pallas_call(
        functools.partial(_odd_in_kernel, sample=sample),
        grid=grp.grid,
        in_specs=specs,
        out_specs=out_specs,
        out_shape=out_shape,
        compiler_params=_cp("parallel", "parallel"),
    )(*ins)


def _attn_prompt_kernel(qi_ref, ki_ref, q_ref, k_ref, vt_ref, o_ref, m_scr, acc_scr, s_scr, p_scr, *, tq):
    step = pl.program_id(1)
    qi = qi_ref[step]
    ki = ki_ref[step]

    @pl.when(ki == 0)
    def _():
        m_scr[...] = jnp.full_like(m_scr, NEG)
        acc_scr[...] = jnp.zeros_like(acc_scr)

    tk = k_ref.shape[1]
    lead = qi * tq - ki * tk

    def update(lead_rows):
        diagonal = lead_rows is not None
        width = s_scr.shape[2]
        items = [(h, c) for h in range(MLA_HEADS) for c in range(0, tq, width)]

        def keys_used(c):
            return min(tk, lead_rows + c + width) if diagonal else tk

        def scores(item, buf):
            h, c = item
            nk = keys_used(c)
            sl = slice(LANES * h, LANES * (h + 1))
            st = _dot_nt(k_ref[0, :nk, sl], q_ref[0, c:c + width, sl])
            if diagonal:
                key = lax.broadcasted_iota(jnp.int32, (nk, width), 0)
                qry = lax.broadcasted_iota(jnp.int32, (nk, width), 1) + (lead_rows + c)
                st = jnp.where(key <= qry, st, NEG)
            s_scr[buf, :nk] = st

        def absorb(item, buf):
            h, c = item
            nk = keys_used(c)
            vr = slice(VT_ROWS * h, VT_ROWS * (h + 1))
            qc = slice(c, c + width)
            m_prev = m_scr[h:h + 1, qc]
            m_new = jnp.maximum(m_prev, jnp.max(s_scr[buf, :nk], axis=0, keepdims=True))
            p_scr[buf, :nk] = jnp.exp2(s_scr[buf, :nk] - m_new).astype(BF16)
            acc_scr[vr, qc] = (jnp.exp2(m_prev - m_new) * acc_scr[vr, qc]
                               + _dot(vt_ref[0, vr, :nk], p_scr[buf, :nk]))
            m_scr[h:h + 1, qc] = m_new

        depth = s_scr.shape[0]
        for i in range(depth - 1):
            scores(items[i], i)
        for i, item in enumerate(items):
            if i + depth - 1 < len(items):
                scores(items[i + depth - 1], (i + depth - 1) % depth)
            absorb(item, i % depth)

    def finish():
        for h in range(0, MLA_HEADS, 2):
            pair = []
            for hh in (h, h + 1):
                acc = acc_scr[VT_ROWS * hh:VT_ROWS * (hh + 1), :]
                pair.append(acc[:MLA_V] / acc[MLA_V:MLA_V + 1])
            o_ref[0, :, MLA_V * h:MLA_V * (h + 2)] = jnp.transpose(jnp.concatenate(pair, axis=0)).astype(BF16)

    @pl.when(lead >= tk)
    def _():
        update(None)

    for lead_rows in range(0, tk, tq):
        @pl.when(lead == lead_rows)
        def _():
            update(lead_rows)
            finish()


def _attn_prompt(q, k, vt, tq, tk):
    nb, nt, hl = q.shape
    vt_rows = vt.shape[1]
    n_out = MLA_HEADS * MLA_V
    assert nt % tq == 0 and nt % tk == 0 and tk % tq == 0
    width = min(ATTN_QCHUNK, tq)
    pairs = [(i, j) for i in range(nt // tq) for j in range((i * tq + tq - 1) // tk + 1)]
    qi_tab = jnp.asarray([p[0] for p in pairs], jnp.int32)
    ki_tab = jnp.asarray([p[1] for p in pairs], jnp.int32)
    return pl.pallas_call(
        functools.partial(_attn_prompt_kernel, tq=tq),
        grid_spec=pltpu.PrefetchScalarGridSpec(
            num_scalar_prefetch=2,
            grid=(nb, len(pairs)),
            in_specs=[pl.BlockSpec((1, tq, hl), lambda b, s, qt, kt: (b, qt[s], 0)),
                      pl.BlockSpec((1, tk, hl), lambda b, s, qt, kt: (b, kt[s], 0)),
                      pl.BlockSpec((1, vt_rows, tk), lambda b, s, qt, kt: (b, 0, kt[s]))],
            out_specs=pl.BlockSpec((1, tq, n_out), lambda b, s, qt, kt: (b, qt[s], 0)),
            scratch_shapes=[pltpu.VMEM((MLA_HEADS, tq), F32), pltpu.VMEM((vt_rows, tq), F32),
                            pltpu.VMEM((ATTN_DEPTH, tk, width), F32),
                            pltpu.VMEM((ATTN_DEPTH, tk, width), BF16)]),
        out_shape=_sds((nb, nt, n_out), BF16),
        compiler_params=_cp("parallel", "arbitrary"),
    )(qi_tab, ki_tab, q, k, vt)


def _attn_sample_kernel(pt_ref, ql_ref, qr_ref, cn_ref, kn_ref, ckv_hbm, kr_hbm, o_ref,
                        ckv_buf, kr_buf, pg_scr, s_scr, p_scr, sem, *, npg, layer):
    b = pl.program_id(0)
    nb = pl.num_programs(0)
    slot = lax.rem(b, 2)
    ql = ql_ref[0]
    qr = qr_ref[0]
    rows = ql.shape[0]

    def page_copies(page, i, sl):
        return (pltpu.make_async_copy(ckv_hbm.at[layer, page], ckv_buf.at[sl, i], sem.at[0, sl]),
                pltpu.make_async_copy(kr_hbm.at[layer, page], kr_buf.at[sl, i], sem.at[1, sl]))

    def start_page(batch, i, sl):
        for cp in page_copies(pt_ref[batch, i], i, sl):
            cp.start()

    def start_first(i, carry):
        start_page(0, i, 0)
        return carry

    @pl.when(b == 0)
    def _():
        lax.fori_loop(0, npg, start_first, 0)

    for i in range(npg):
        for cp in page_copies(0, i, slot):
            cp.wait()

    def score_pages(prefetch):
        for i in range(npg):
            if prefetch:
                for k in range(2 * i, min(2 * i + 2, npg)):
                    start_page(b + 1, k, 1 - slot)
            page = ckv_buf[slot, i].astype(BF16)
            pg_scr[i] = page
            s_scr[i] = _dot_nt(ql, page) + _dot(qr, kr_buf[slot, i].astype(BF16))

    @pl.when(b + 1 < nb)
    def _():
        score_pages(True)

    @pl.when(b + 1 == nb)
    def _():
        score_pages(False)

    n_new = cn_ref.shape[1]
    pad = jnp.zeros((LANES - n_new, MLA_KV_RANK), F32)
    cn = jnp.concatenate([cn_ref[0], pad], axis=0).astype(BF16)
    kn = jnp.concatenate([kn_ref[0], pad[:, :MLA_ROPE]], axis=0).astype(BF16)
    step = lax.shift_right_logical(lax.broadcasted_iota(jnp.int32, (rows, LANES), 0), MLA_HEADS.bit_length() - 1)
    col = lax.broadcasted_iota(jnp.int32, (rows, LANES), 1)
    s_new = jnp.where(col <= step, _dot_nt(ql, cn) + _dot_nt(qr, kn), NEG)

    s = s_scr[...]
    m = jnp.maximum(jnp.max(jnp.max(s, axis=0), axis=-1, keepdims=True),
                    jnp.max(s_new, axis=-1, keepdims=True))
    p = jnp.exp2(s - m)
    p_new = jnp.exp2(s_new - m)
    denom = jnp.sum(jnp.sum(p, axis=0), axis=-1, keepdims=True) + jnp.sum(p_new, axis=-1, keepdims=True)
    p_scr[...] = p.astype(BF16)

    def weigh_page(i, acc):
        return acc + _dot(p_scr[i], pg_scr[i])

    acc = lax.fori_loop(0, npg, weigh_page, _dot(p_new.astype(BF16), cn), unroll=True)
    o_ref[0] = acc / denom


def _attn_sample(page_table, ql, qr, cn, kn, cache_ckv, cache_kr, layer):
    nb, rows, rank = ql.shape
    npg = page_table.shape[1]
    page, rope = cache_ckv.shape[2], cache_kr.shape[2]
    assert page == LANES, "one cache page must fill one lane tile of scores"
    per_b = lambda a: pl.BlockSpec((1,) + a.shape[1:], lambda b, pt: (b, 0, 0))
    hbm = pl.BlockSpec(memory_space=pl.ANY)
    return pl.pallas_call(
        functools.partial(_attn_sample_kernel, npg=npg, layer=layer),
        grid_spec=pltpu.PrefetchScalarGridSpec(
            num_scalar_prefetch=1,
            grid=(nb,),
            in_specs=[per_b(ql), per_b(qr), per_b(cn), per_b(kn), hbm, hbm],
            out_specs=pl.BlockSpec((1, rows, rank), lambda b, pt: (b, 0, 0)),
            scratch_shapes=[pltpu.VMEM((2, npg, page, rank), F32),
                            pltpu.VMEM((2, npg, rope, page), F32),
                            pltpu.VMEM((npg, page, rank), BF16),
                            pltpu.VMEM((npg, rows, page), F32),
                            pltpu.VMEM((npg, rows, page), BF16),
                            pltpu.SemaphoreType.DMA((2, 2))]),
        out_shape=_sds((nb, rows, rank)),
        compiler_params=_cp("arbitrary"),
    )(page_table, ql, qr, cn, kn, cache_ckv, cache_kr)


def _conv_norm_act(cv, cb_ref, lg_ref, lb_ref):
    cv = cv + cb_ref[...]
    mu = jnp.mean(cv, axis=-1, keepdims=True)
    d = cv - mu
    y = d * lax.rsqrt(jnp.mean(d * d, axis=-1, keepdims=True) + EPS)
    return _silu(y * lg_ref[...] + lb_ref[...])


def _odd_out_prompt_kernel(x_ref, a_ref, uc_ref, gt_ref, cw_ref, cb_ref, lg_ref, lb_ref, woa_ref, woc_ref,
                           xo_ref, ho_ref, hb, hs, cvb, *, tm):
    t = pl.program_id(1)

    @pl.when(t == 0)
    def _():
        hb[0:CONV_PAD, :] = jnp.zeros((CONV_PAD, CONV_CH), F32)

    hb[CONV_PAD:CONV_PAD + tm, :] = uc_ref[0]
    for sft in range(1, SUBLANES):
        hs[sft - 1] = hb[sft:sft + hs.shape[1], :]
    base = CONV_PAD - CONV_HIST

    block = min(CONV_BLOCK, tm)
    for q0 in range(0, tm, block):
        ya = _dot(a_ref[0, q0:q0 + block, :], woa_ref[...])
        for r0 in range(q0, q0 + block, CONV_ROWS):
            acc = None
            for j in range(CONV_WIDTH):
                whole, sft = divmod(base + j, SUBLANES)
                rows = slice(r0 + whole * SUBLANES, r0 + whole * SUBLANES + CONV_ROWS)
                term = cw_ref[j:j + 1, :] * (hb[rows, :] if sft == 0 else hs[sft - 1, rows, :])
                acc = term if acc is None else acc + term
            cvb[r0:r0 + CONV_ROWS, :] = acc
        cv = _conv_norm_act(cvb[q0:q0 + block, :], cb_ref, lg_ref, lb_ref)
        y = ya + _dot(cv.astype(BF16), woc_ref[...])
        xo_ref[0, q0:q0 + block, :] = x_ref[0, q0:q0 + block, :] + gt_ref[...] * y
    hb[0:CONV_PAD, :] = hb[tm:tm + CONV_PAD, :]

    @pl.when(t == pl.num_programs(1) - 1)
    def _():
        ho_ref[0] = hb[0:CONV_PAD, :]


def _odd_out_sample_kernel(x_ref, lat_ref, uc_ref, hi_ref, gt_ref, cw_ref, cb_ref, lg_ref, lb_ref, wuv_ref,
                           woa_ref, woc_ref, xo_ref, hn_ref):
    steps, nb = uc_ref.shape[:2]

    def slab(i):
        return hi_ref[i] if i < CONV_HIST else uc_ref[i - CONV_HIST]

    rows = []
    for t in range(steps):
        cv = cw_ref[0:1, :] * slab(t)
        for j in range(1, CONV_WIDTH):
            cv = cv + cw_ref[j:j + 1, :] * slab(t + j)
        rows.append(cv)
    cv = _conv_norm_act(jnp.concatenate(rows, axis=0), cb_ref, lg_ref, lb_ref)
    lat = lat_ref[...].reshape(steps * nb, lat_ref.shape[-1]).astype(BF16)
    attn = _dot(lat, wuv_ref[...]).astype(BF16)
    y = _dot(attn, woa_ref[...]) + _dot(cv.astype(BF16), woc_ref[...])
    x = x_ref[...]
    xo_ref[...] = x + gt_ref[...] * y.reshape(x.shape)
    for i in range(CONV_HIST):
        hn_ref[i] = slab(steps + i)


def _odd_out(grp, x, a, uc, hist, mod, w):
    dm = x.shape[-1]
    conv = [w['cw'], w['cb'], w['lg'], w['lb']]
    if grp.prompt:
        tm = grp.block[1]
        ins = [x, a, uc, mod] + conv + [w['woa'], w['woc']]
        return pl.pallas_call(
            functools.partial(_odd_out_prompt_kernel, tm=tm),
            grid=grp.grid,
            in_specs=[grp.act(dm), grp.act(a.shape[-1]), grp.act(CONV_CH), grp.mod(5)] + [_full(v) for v in ins[4:]],
            out_specs=[grp.act(dm), pl.BlockSpec((1, CONV_PAD, CONV_CH), lambda b, t: (b, 0, 0))],
            out_shape=[_sds(x.shape), _sds((x.shape[0], CONV_PAD, CONV_CH))],
            scratch_shapes=[pltpu.VMEM((CONV_PAD + tm, CONV_CH), F32),
                            pltpu.VMEM((SUBLANES - 1, CONV_PAD + tm - SUBLANES, CONV_CH), F32),
                            pltpu.VMEM((tm, CONV_CH), F32)],
            compiler_params=_cp("parallel", "arbitrary"),
        )(*ins)
    ins = [x, a, uc, hist, mod] + conv + [w['wuv_bd'], w['woa'], w['woc']]
    return pl.pallas_call(
        _odd_out_sample_kernel,
        grid=grp.grid,
        in_specs=[grp.act(dm), grp.act(a.shape[-1]), grp.act(CONV_CH), _full(hist), grp.mod(5)]
        + [_full(v) for v in ins[5:]],
        out_specs=[grp.act(dm), _full(hist)],
        out_shape=[_sds(x.shape), _sds(hist.shape)],
        compiler_params=_cp("parallel", "arbitrary"),
    )(*ins)


def _head_pad(w, heads, width, offset=0):
    kdim = w.shape[0]
    w = w.reshape(kdim, heads, width)
    w = jnp.pad(w, ((0, 0), (0, 0), (offset, LANES - width - offset)))
    return w.reshape(kdim, heads * LANES)


def _rope_tables(pos):
    half = MLA_ROPE // 2
    freqs = ROPE_BASE ** (-np.arange(half, dtype=np.float64) / half)
    ang = np.asarray(pos, np.float64)[..., None] * freqs
    cos, sin = jnp.asarray(np.cos(ang), F32), jnp.asarray(np.sin(ang), F32)
    zeros = jnp.zeros_like(cos)
    lead = jnp.ones(pos.shape + (ROPE_LANE0,), F32)
    tail = jnp.zeros(pos.shape + (LANES - ROPE_LANE0 - MLA_ROPE,), F32)
    cs = jnp.concatenate([lead, cos, cos, tail], axis=-1)
    s1 = jnp.concatenate([0 * lead, -sin, zeros, tail], axis=-1)
    s2 = jnp.concatenate([0 * lead, zeros, sin, tail], axis=-1)
    return cs, s1, s2


def _even_weights(w_in, gate_w2, gate_b, out_norm, pool_w, pool_scale, w_out):
    hk = GLA_HEADS * GLA_DK
    hv = GLA_HEADS * GLA_DV
    n_main = 2 * hk + 2 * hv
    wg = jnp.pad(w_in[:, n_main:n_main + GLA_GATE_RANK], ((0, 0), (0, LANES - GLA_GATE_RANK)))
    w2 = jnp.pad(gate_w2, ((0, LANES - GLA_GATE_RANK), (0, 0)))
    return dict(wq=w_in[:, :n_main].astype(BF16), wg=wg.astype(BF16), w2=w2.astype(BF16),
                gb=gate_b[None], wu=w_in[:, n_main + GLA_GATE_RANK:].astype(BF16),
                on=out_norm[None], pw=pool_w.astype(BF16), ps=pool_scale[None], wo=w_out.astype(BF16))


def _odd_weights(w_in, q_norm, w_uq, kv_norm, w_uk, w_uv, conv_w, conv_b, ln_g, ln_b, w_out):
    c0, c1, c2 = MLA_Q_RANK, MLA_Q_RANK + MLA_KV_RANK, MLA_Q_RANK + MLA_KV_RANK + MLA_ROPE
    n_attn = MLA_HEADS * MLA_V
    wkr = jnp.pad(w_in[:, c1:c2], ((0, 0), (ROPE_LANE0, LANES - ROPE_LANE0 - MLA_ROPE)))
    wuk = w_uk.reshape(MLA_KV_RANK, MLA_HEADS * MLA_NOPE)
    wkl = jnp.pad(jnp.transpose(w_uk, (1, 2, 0)), ((0, 0), (0, LANES - MLA_NOPE), (0, 0)))
    eye = jnp.eye(MLA_HEADS, dtype=F32)
    wuv_bd = (eye[:, None, :, None] * jnp.transpose(w_uv, (1, 0, 2))[:, :, None, :]).reshape(
        MLA_HEADS * MLA_KV_RANK, n_attn)
    wuvt = jnp.pad(jnp.transpose(w_uv, (1, 2, 0)), ((0, 0), (0, VT_ROWS - MLA_V), (0, 0))).reshape(
        MLA_HEADS * VT_ROWS, MLA_KV_RANK)
    return dict(wcq=w_in[:, :c0].astype(BF16), wckv=w_in[:, c0:c1].astype(BF16), wkr=wkr.astype(BF16),
                wga=w_in[:, c2:c2 + CONV_CH].astype(BF16), wgg=w_in[:, c2 + CONV_CH:].astype(BF16),
                qn=q_norm[None], wuq=_head_pad(w_uq, MLA_HEADS, MLA_NOPE + MLA_ROPE).astype(BF16),
                kvn=kv_norm[None], wuk=_head_pad(wuk, MLA_HEADS, MLA_NOPE).astype(BF16),
                wuvt=wuvt.astype(BF16), wkl=wkl.astype(BF16),
                wuv_bd=wuv_bd.astype(BF16), cw=conv_w, cb=conv_b[None], lg=ln_g[None], lb=ln_b[None],
                woa=w_out[:n_attn].astype(BF16), woc=w_out[n_attn:].astype(BF16))


def _tm(x):
    return jnp.swapaxes(x, 0, 1)


def kernel(x_prompt, x_sample, state_gla, state_pool, cache_ckv, cache_krope, state_conv, page_table, c_prompt, c_sample, ada_w, ada_b, norm_g, ffn_w1, ffn_w3, ffn_w2, ev_w_in, ev_gate_w2, ev_gate_b, ev_out_norm, ev_pool_w, ev_pool_scale, ev_w_out, od_w_in, od_q_norm, od_w_uq, od_kv_norm, od_w_uk, od_w_uv, od_conv_w, od_conv_b, od_conv_norm_g, od_conv_norm_b, od_w_out, final_norm):
    nbp, seq, dm = x_prompt.shape
    nbs, steps, _ = x_sample.shape
    depth = ada_w.shape[0]
    past_len = page_table.shape[1] * cache_ckv.shape[2]
    tile = min(TOKEN_TILE, seq)
    gp = _Group(True, nbp, seq, tile, dm)
    gf = _Group(True, nbp, seq, min(FFN_TOKEN_TILE, seq), dm)
    gs = _Group(False, steps, nbs, nbs, dm)

    n_c = nbp + nbs
    c_all = jnp.pad(jnp.concatenate([c_sample, c_prompt], axis=0), ((0, -n_c % SUBLANES), (0, 0)))
    mod = _ada(c_all, ada_w, ada_b)
    mod_p = mod[:, :, nbs:n_c].reshape(depth, N_MOD, nbp, 1, dm)
    mod_s = mod if nbs % SUBLANES == 0 else mod[:, :, :nbs]

    w1, w3, w2 = ffn_w1.astype(BF16), ffn_w3.astype(BF16), ffn_w2.astype(BF16)
    tabs_p = _rope_tables(np.arange(seq)[None])
    tabs_s = _rope_tables(past_len + np.arange(steps)[:, None])

    xp = x_prompt
    xs = _tm(x_sample)
    gla_p, gla_s, pool_p, pool_s, ckv_p, ckv_s, kr_p, kr_s, conv_p, conv_s = ([] for _ in range(10))
    gla_tt = min(TOKEN_TILE, seq)

    for layer in range(depth):
        i = layer // 2
        ng = norm_g[layer]
        last = layer == depth - 1
        xp = _ffn(gf, xp, mod_p[layer], 0, ng[0:1], w1, w3, w2, (layer, 0))
        xs = _ffn(gs, xs, mod_s[layer], 0, ng[0:1], w1, w3, w2, (layer, 0))
        if layer % 2 == 0:
            w = _even_weights(ev_w_in[i], ev_gate_w2[i], ev_gate_b[i], ev_out_norm[i], ev_pool_w[i],
                              ev_pool_scale[i], ev_w_out[i])
            proj = (ng[1:2], w['wq'], w['wg'], w['w2'], w['gb'], w['wu'])
            out_w = (w['on'], w['pw'], w['ps'], w['wo'])
            q, k, v, r, la, u = _even_in(gp, xp, mod_p[layer], *proj)
            o, s_fin = _gla(q, k, v, la, None, nbp, gla_tt)
            xp, hist = _even_out(gp, xp, o, r, u, None, mod_p[layer], *out_w, 0)
            gla_p.append(s_fin.reshape(nbp, GLA_HEADS, GLA_DK, GLA_DV))
            pool_p.append(hist[:, POOL_PAD - POOL_HIST:])
            q, k, v, r, la, u = _even_in(gs, xs, mod_s[layer], *proj)
            rows8 = steps + (-steps % SUBLANES)
            chunked = lambda a: jnp.pad(_tm(a), ((0, 0), (0, rows8 - steps), (0, 0)))
            s0 = state_gla[i].reshape(nbs, GLA_HEADS // 2, LANES, GLA_DV)
            o, s_fin = _gla(chunked(q), chunked(k), chunked(v.astype(F32)), chunked(la), s0,
                            GLA_SAMPLE_BATCHES if nbs % GLA_SAMPLE_BATCHES == 0 else 1, rows8)
            xs, hist = _even_out(gs, xs, _tm(o[:, :steps]), r, u, _tm(state_pool[i]), mod_s[layer], *out_w, past_len)
            gla_s.append(s_fin.reshape(nbs, GLA_HEADS, GLA_DK, GLA_DV))
            pool_s.append(_tm(hist))
        else:
            w = _odd_weights(od_w_in[i], od_q_norm[i], od_w_uq[i], od_kv_norm[i], od_w_uk[i], od_w_uv[i],
                             od_conv_w[i], od_conv_b[i], od_conv_norm_g[i], od_conv_norm_b[i], od_w_out[i])
            rope_lanes = slice(ROPE_LANE0, ROPE_LANE0 + MLA_ROPE)
            ckv, kr, uc, q, k, v = _odd_in(gp, xp, mod_p[layer], ng[1:2], w, tabs_p, False)
            attn = _attn_prompt(q, k, v, min(ATTN_TILE, seq), min(ATTN_KEYS, seq))
            xp, hist = _odd_out(gp, xp, attn, uc, None, mod_p[layer], w)
            ckv_p.append(ckv)
            kr_p.append(kr[..., rope_lanes])
            conv_p.append(hist[:, CONV_PAD - CONV_HIST:])
            ckv, kr, uc, q, ql = _odd_in(gs, xs, mod_s[layer], ng[1:2], w, tabs_s, True)
            kr = kr[..., rope_lanes]
            qr = q.reshape(steps, nbs, MLA_HEADS, LANES)[..., rope_lanes]
            qr = _tm(qr).reshape(nbs, steps * MLA_HEADS, MLA_ROPE)
            ql = _tm(ql).reshape(nbs, steps * MLA_HEADS, MLA_KV_RANK)
            pad8 = lambda a: jnp.pad(_tm(a), ((0, 0), (0, -steps % 8), (0, 0)))
            cache_kr_t = jnp.swapaxes(cache_krope, 2, 3)
            lat = _attn_sample(page_table, ql, qr, pad8(ckv), pad8(kr), cache_ckv, cache_kr_t, i)
            lat = _tm(lat.reshape(nbs, steps, MLA_HEADS * MLA_KV_RANK))
            xs, hist = _odd_out(gs, xs, lat, uc, _tm(state_conv[i]), mod_s[layer], w)
            ckv_s.append(_tm(ckv))
            kr_s.append(_tm(kr))
            conv_s.append(_tm(hist))
        fin = final_norm[None] if last else None
        xp = _ffn(gf, xp, mod_p[layer], 6, ng[2:3], w1, w3, w2, (layer, 1), fin)
        xs = _ffn(gs, xs, mod_s[layer], 6, ng[2:3], w1, w3, w2, (layer, 1), fin)

    st = jnp.stack
    return (xp, _tm(xs), st(gla_p), st(gla_s), st(pool_p), st(pool_s), st(ckv_p), st(ckv_s),
            st(kr_p), st(kr_s), st(conv_p), st(conv_s))
```

```python
import functools

import jax
import jax.numpy as jnp
import numpy as np
from jax import lax
from jax.experimental import pallas as pl
from jax.experimental.pallas import tpu as pltpu

F32 = jnp.float32
BF16 = jnp.bfloat16

EPS = 1e-6
NEG = -1e30
N_MOD = 9
GLA_HEADS = 4
GLA_DK = 64
GLA_DV = 128
GLA_GATE_RANK = 16
GLA_TAU = 16.0
GLA_CHUNK = 64
POOL_WINDOWS = (2, 4, 8, 16)
POOL_GROUP = 128
POOL_WIDTH = 512
POOL_HIST = 15
MLA_HEADS = 8
MLA_Q_RANK = 384
MLA_KV_RANK = 256
MLA_NOPE = 64
MLA_ROPE = 32
MLA_V = 64
MLA_SCALE = (MLA_NOPE + MLA_ROPE) ** -0.5
LOG2E = 1.4426950408889634
ROPE_BASE = 10000.0
CONV_WIDTH = 31
CONV_CH = 512
CONV_HIST = 30

LANES = 128
SUBLANES = 8
CONV_ROWS = 32
CONV_BLOCK = 256
EVEN_BLOCK = 256
ROPE_LANE0 = 64
VT_ROWS = 80
POOL_PAD = 16
assert all(w & (w - 1) == 0 and w <= POOL_PAD for w in POOL_WINDOWS)
CONV_PAD = 32
VMEM_LIMIT = 52 * 2 ** 20
TOKEN_TILE = 1024
ATTN_TILE = 1024
ATTN_KEYS = 1024
ATTN_DEPTH = 3
ATTN_QCHUNK = 512
FFN_TOKEN_TILE = 1024
FF_SUB = 1024
GLA_SAMPLE_BATCHES = 8


def _cp(*sem):
    return pltpu.CompilerParams(dimension_semantics=sem, vmem_limit_bytes=VMEM_LIMIT)


def _dot(a, b):
    return jnp.dot(a, b, preferred_element_type=F32)


def _dot_nt(a, b):
    return lax.dot_general(a, b, (((1,), (1,)), ((), ())), preferred_element_type=F32)


def _dot_tn(a, b):
    return lax.dot_general(a, b, (((0,), (0,)), ((), ())), preferred_element_type=F32)


def _silu(x):
    return x * jax.nn.sigmoid(x)


def _rms(x, g):
    return x * lax.rsqrt(jnp.mean(x * x, axis=-1, keepdims=True) + EPS) * g


def _modulated(x_ref, g_ref, sh_ref, sc_ref):
    x = x_ref[...]
    h = _rms(x, g_ref[...]) * (1.0 + sc_ref[...]) + sh_ref[...]
    return h.reshape(x.shape[0] * x.shape[1], x.shape[2]).astype(BF16)


class _Group:
    def __init__(self, prompt, lead, rows, tile, dm):
        self.prompt = prompt
        self.dm = dm
        self.grid = (lead, rows // tile) if prompt else (1, 1)
        self.block = (1, tile) if prompt else (lead, rows)

    def act(self, width):
        return pl.BlockSpec(self.block + (width,), lambda b, t, *_: (b, t, 0))

    def mod(self, k):
        if self.prompt:
            return pl.BlockSpec((None, None, 1, self.dm), lambda b, t, *_: (k, b, 0, 0))
        return pl.BlockSpec((None, self.block[1], self.dm), lambda b, t, *_: (k, 0, 0))

    def pos(self):
        if self.prompt:
            return pl.BlockSpec((1, self.block[1], LANES), lambda b, t, *_: (0, t, 0))
        return pl.BlockSpec((self.block[0], 1, LANES), lambda b, t, *_: (0, 0, 0))


def _full(a):
    nd = a.ndim
    return pl.BlockSpec(a.shape, lambda *_: (0,) * nd)


def _sds(shape, dtype=F32):
    return jax.ShapeDtypeStruct(shape, dtype)


def _ada_kernel(c_ref, w_ref, b_ref, o_ref):
    c = c_ref[...]
    o_ref[...] = _dot(_silu(c).astype(BF16), w_ref[...].astype(BF16)) + b_ref[...]


def _ada(c_all, ada_w, ada_b):
    depth, dm, n = ada_w.shape
    m = c_all.shape[0]
    tn = dm
    return pl.pallas_call(
        _ada_kernel,
        grid=(depth, n // tn),
        in_specs=[pl.BlockSpec((m, dm), lambda l, j: (0, 0)),
                  pl.BlockSpec((None, dm, tn), lambda l, j: (l, 0, j)),
                  pl.BlockSpec((None, 1, tn), lambda l, j: (l, 0, j))],
        out_specs=pl.BlockSpec((None, None, m, tn), lambda l, j: (l, j, 0, 0)),
        out_shape=_sds((depth, n // tn, m, tn)),
        compiler_params=_cp("parallel", "parallel"),
    )(c_all, ada_w, ada_b.reshape(depth, 1, n))


def _ffn_kernel(*refs, final):
    if final:
        x_ref, sh_ref, sc_ref, gt_ref, g_ref, w1_ref, w3_ref, w2_ref, fn_ref, o_ref = refs
    else:
        x_ref, sh_ref, sc_ref, gt_ref, g_ref, w1_ref, w3_ref, w2_ref, o_ref = refs
    h = _modulated(x_ref, g_ref, sh_ref, sc_ref)
    dff = w1_ref.shape[1]
    y = None
    for c0 in range(0, dff, FF_SUB):
        cols = slice(c0, min(c0 + FF_SUB, dff))
        a = _dot(h, w1_ref[:, cols])
        b = _dot(h, w3_ref[:, cols])
        part = _dot((_silu(a) * b).astype(BF16), w2_ref[cols, :])
        y = part if y is None else y + part
    x = x_ref[...]
    xn = x + 0.5 * gt_ref[...] * y.reshape(x.shape)
    o_ref[...] = _rms(xn, fn_ref[...]) if final else xn


def _resident(a, lead):
    idx = tuple(lead) + (0, 0)
    return pl.BlockSpec((None,) * len(lead) + a.shape[-2:], lambda *_: idx, pipeline_mode=pl.Buffered(1))


def _ffn(grp, x, mod, k0, g, w1, w3, w2, which, final_g=None):
    dm = x.shape[-1]
    final = final_g is not None
    ins = [x, mod, mod, mod, g, w1, w3, w2] + ([final_g] if final else [])
    specs = [grp.act(dm), grp.mod(k0), grp.mod(k0 + 1), grp.mod(k0 + 2), _full(g),
             _resident(w1, which), _resident(w3, which), _resident(w2, which)] + ([_full(final_g)] if final else [])
    return pl.pallas_call(
        functools.partial(_ffn_kernel, final=final),
        grid=grp.grid,
        in_specs=specs,
        out_specs=grp.act(dm),
        out_shape=_sds(x.shape),
        compiler_params=_cp("parallel", "parallel"),
    )(*ins)


def _log_sigmoid(x):
    return jnp.minimum(x, 0.0) - jnp.log(1.0 + jnp.exp(-jnp.abs(x)))


def _even_in_kernel(x_ref, sh_ref, sc_ref, g_ref, wq_ref, wg_ref, w2_ref, gb_ref, wu_ref,
                    q_ref, k_ref, v_ref, r_ref, la_ref, u_ref):
    h = _modulated(x_ref, g_ref, sh_ref, sc_ref)
    lead = x_ref.shape[:2]
    hk = GLA_HEADS * GLA_DK
    hv = GLA_HEADS * GLA_DV
    z = _dot(h, wq_ref[...])
    q_ref[...] = (z[:, :hk] * GLA_DK ** -0.5).reshape(lead + (hk,))
    k_ref[...] = z[:, hk:2 * hk].reshape(lead + (hk,))
    v_ref[...] = z[:, 2 * hk:2 * hk + hv].reshape(lead + (hv,))
    r_ref[...] = z[:, 2 * hk + hv:].reshape(lead + (hv,))
    g_low = _dot(h, wg_ref[...]).astype(BF16)
    gate = _dot(g_low, w2_ref[...]) + gb_ref[...]
    la_ref[...] = (_log_sigmoid(gate) / GLA_TAU).reshape(lead + (hk,))
    u_ref[...] = _dot(h, wu_ref[...]).reshape(lead + (POOL_WIDTH,))


def _even_in(grp, x, mod, g, wq, wg, w2, gb, wu):
    hk = GLA_HEADS * GLA_DK
    hv = GLA_HEADS * GLA_DV
    widths = (hk, hk, hv, hv, hk, POOL_WIDTH)
    return pl.pallas_call(
        _even_in_kernel,
        grid=grp.grid,
        in_specs=[grp.act(x.shape[-1]), grp.mod(3), grp.mod(4), _full(g), _full(wq), _full(wg),
                  _full(w2), _full(gb), _full(wu)],
        out_specs=[grp.act(w) for w in widths],
        out_shape=[_sds(x.shape[:2] + (w,)) for w in widths],
        compiler_params=_cp("parallel", "parallel"),
    )(x, mod, mod, g, wq, wg, w2, gb, wu)


def _cumsum_rows(tril, x):
    hi = x.astype(BF16)
    r1 = x - hi.astype(F32)
    mid = r1.astype(BF16)
    lo = (r1 - mid.astype(F32)).astype(BF16)
    return _dot(tril, hi) + _dot(tril, mid) + _dot(tril, lo)


def _gla_kernel(*refs, nbb, tt, has_s0):
    if has_s0:
        q_ref, k_ref, v_ref, la_ref, s0_ref, o_ref, so_ref, s_scr = refs
    else:
        q_ref, k_ref, v_ref, la_ref, o_ref, so_ref, s_scr = refs
    t = pl.program_id(1)
    c_rows = GLA_CHUNK

    @pl.when(t == 0)
    def _():
        s_scr[...] = s0_ref[...] if has_s0 else jnp.zeros_like(s_scr)

    row = lax.broadcasted_iota(jnp.int32, (c_rows, c_rows), 0)
    col = lax.broadcasted_iota(jnp.int32, (c_rows, c_rows), 1)
    causal = col <= row
    tril = jnp.where(causal, 1.0, 0.0).astype(BF16)
    lane = lax.broadcasted_iota(jnp.int32, (c_rows, LANES), 1)
    srow = lax.broadcasted_iota(jnp.int32, (LANES, LANES), 0)

    pairs = GLA_HEADS // 2
    combos = [(n, p) for n in range(nbb) for p in range(pairs)]
    heads = [(n, p, hh) for n, p in combos for hh in range(2)]

    def chunk(c, carry):
        short = tt < c_rows
        rows = slice(0, tt) if short else pl.ds(pl.multiple_of(c * c_rows, c_rows), c_rows)

        def take(ref, n, cols):
            x = ref[n, rows, cols]
            return jnp.concatenate([x, jnp.zeros((c_rows - tt, x.shape[1]), F32)], axis=0) if short else x

        wide = lambda ref: jnp.concatenate([take(ref, n, slice(None)) for n in range(nbb)], axis=-1)
        slab = lambda x, n, p: x[:, LANES * (n * pairs + p):LANES * (n * pairs + p + 1)]
        vcols = lambda p, hh: slice(GLA_DV * (2 * p + hh), GLA_DV * (2 * p + hh + 1))
        b = _cumsum_rows(tril, wide(la_ref))
        b_last = b[c_rows - 1:c_rows, :]
        k = wide(k_ref)
        qi = wide(q_ref) * jnp.exp(b)
        ki = (k * jnp.exp(-b)).astype(BF16)
        kd = (k * jnp.exp(b_last - b)).astype(BF16)
        grow = jnp.exp(b_last)
        qh = {(n, p, hh): jnp.where((lane < GLA_DK) if hh == 0 else (lane >= GLA_DK), slab(qi, n, p), 0.0).astype(BF16)
              for n, p, hh in heads}
        att = {(n, p, hh): _dot_nt(qh[n, p, hh], slab(ki, n, p)) for n, p, hh in heads}
        vh = {(n, p, hh): take(v_ref, n, vcols(p, hh)).astype(BF16) for n, p, hh in heads}
        s_old = {(n, p): s_scr[n, p] for n, p in combos}
        from_state = {(n, p, hh): _dot(qh[n, p, hh], s_old[n, p].astype(BF16)) for n, p, hh in heads}
        upd = {(n, p, hh): _dot_tn(slab(kd, n, p), vh[n, p, hh]) for n, p, hh in heads}
        for n, p, hh in heads:
            within = _dot(jnp.where(causal, att[n, p, hh], 0.0).astype(BF16), vh[n, p, hh])
            o_ref[n, rows, vcols(p, hh)] = (from_state[n, p, hh] + within)[:min(tt, c_rows)]
        for n, p in combos:
            decay = jnp.transpose(jnp.broadcast_to(slab(grow, n, p), (LANES, LANES)))
            s_scr[n, p] = decay * s_old[n, p] + jnp.where(srow < GLA_DK, upd[n, p, 0], upd[n, p, 1])
        return carry

    n_chunks = max(1, tt // c_rows)
    lax.fori_loop(0, n_chunks, chunk, 0, unroll=min(2, n_chunks))

    @pl.when(t == pl.num_programs(1) - 1)
    def _():
        so_ref[...] = s_scr[...]


def _gla(q, k, v, la, s0, nbb, tt):
    nb, nt, hk = q.shape
    hv = v.shape[-1]
    has_s0 = s0 is not None
    act = lambda w: pl.BlockSpec((nbb, tt, w), lambda b, t: (b, t, 0))
    st = pl.BlockSpec((nbb, GLA_HEADS // 2, LANES, GLA_DV), lambda b, t: (b, 0, 0, 0))
    return pl.pallas_call(
        functools.partial(_gla_kernel, nbb=nbb, tt=tt, has_s0=has_s0),
        grid=(nb // nbb, nt // tt),
        in_specs=[act(hk), act(hk), act(hv), act(hk)] + ([st] if has_s0 else []),
        out_specs=[act(hv), st],
        out_shape=[_sds((nb, nt, hv)), _sds((nb, GLA_HEADS // 2, LANES, GLA_DV))],
        scratch_shapes=[pltpu.VMEM((nbb, GLA_HEADS // 2, LANES, GLA_DV), F32)],
        compiler_params=_cp("parallel", "arbitrary"),
    )(*([q, k, v, la] + ([s0] if has_s0 else [])))


def _gla_gate(o, r, on):
    outs = []
    for h in range(GLA_HEADS):
        sl = slice(GLA_DV * h, GLA_DV * (h + 1))
        outs.append(_rms(o[:, sl], on[:, sl]))
    return jnp.concatenate(outs, axis=-1) * _silu(r)


def _even_mix(og, pooled, pw_ref, ps_ref, wo_ref):
    mixed = [_dot(pooled[g].astype(BF16), pw_ref[g]) for g in range(len(POOL_WINDOWS))]
    mixed = jnp.concatenate(mixed, axis=-1) * ps_ref[...]
    n_o = og.shape[-1]
    return _dot(og.astype(BF16), wo_ref[:n_o, :]) + _dot(mixed.astype(BF16), wo_ref[n_o:, :])


def _even_out_prompt_kernel(x_ref, o_ref, r_ref, u_ref, gt_ref, on_ref, pw_ref, ps_ref, wo_ref,
                            xo_ref, ho_ref, hb, *, tm, p0):
    t = pl.program_id(1)

    @pl.when(t == 0)
    def _():
        hb[0:POOL_PAD, :] = jnp.zeros((POOL_PAD, POOL_WIDTH), F32)

    hb[POOL_PAD:POOL_PAD + tm, :] = u_ref[0]
    block = min(EVEN_BLOCK, tm)
    for q0 in range(0, tm, block):
        rows = slice(q0, q0 + block)
        pos = t * tm + q0 + lax.broadcasted_iota(jnp.int32, (block, 1), 0)
        pooled = []
        for g, w in enumerate(POOL_WINDOWS):
            sl = slice(POOL_GROUP * g, POOL_GROUP * (g + 1))
            win = hb[q0:q0 + POOL_PAD + block, sl]
            span = 1
            while span < w:
                win = win + pltpu.roll(win, span, 0)
                span *= 2
            cnt = jnp.minimum(p0 + pos + 1, w).astype(F32)
            pooled.append(win[POOL_PAD:] / cnt - u_ref[0, rows, sl])
        og = _gla_gate(o_ref[0, rows, :], r_ref[0, rows, :], on_ref[...])
        y = _even_mix(og, pooled, pw_ref, ps_ref, wo_ref)
        xo_ref[0, rows, :] = x_ref[0, rows, :] + gt_ref[...] * y
    hb[0:POOL_PAD, :] = hb[tm:tm + POOL_PAD, :]

    @pl.when(t == pl.num_programs(1) - 1)
    def _():
        ho_ref[0] = hb[0:POOL_PAD, :]


def _even_out_sample_kernel(x_ref, o_ref, r_ref, u_ref, hi_ref, gt_ref, on_ref, pw_ref, ps_ref, wo_ref,
                            xo_ref, hn_ref, *, p0):
    steps, nb = u_ref.shape[:2]

    def slab(i, sl):
        return hi_ref[i, :, sl] if i < POOL_HIST else u_ref[i - POOL_HIST, :, sl]

    pooled = []
    for g, w in enumerate(POOL_WINDOWS):
        sl = slice(POOL_GROUP * g, POOL_GROUP * (g + 1))
        rows = []
        for t in range(steps):
            win = slab(POOL_HIST + t - (w - 1), sl)
            for j in range(w - 2, -1, -1):
                win = win + slab(POOL_HIST + t - j, sl)
            rows.append(win / float(min(p0 + t + 1, w)) - u_ref[t, :, sl])
        pooled.append(jnp.concatenate(rows, axis=0))
    flat = lambda ref: ref[...].reshape(steps * nb, ref.shape[-1])
    og = _gla_gate(flat(o_ref), flat(r_ref), on_ref[...])
    x = x_ref[...]
    xo_ref[...] = x + gt_ref[...] * _even_mix(og, pooled, pw_ref, ps_ref, wo_ref).reshape(x.shape)
    for i in range(POOL_HIST):
        hn_ref[i] = slab(steps + i, slice(None))


def _even_out(grp, x, o, r, u, hist, mod, on, pw, ps, wo, p0):
    dm = x.shape[-1]
    common = [mod, on, pw, ps, wo]
    common_specs = [grp.mod(5), _full(on), _full(pw), _full(ps), _full(wo)]
    acts = [grp.act(dm), grp.act(o.shape[-1]), grp.act(r.shape[-1]), grp.act(u.shape[-1])]
    if grp.prompt:
        tm = grp.block[1]
        return pl.pallas_call(
            functools.partial(_even_out_prompt_kernel, tm=tm, p0=p0),
            grid=grp.grid,
            in_specs=acts + common_specs,
            out_specs=[grp.act(dm), pl.BlockSpec((1, POOL_PAD, POOL_WIDTH), lambda b, t: (b, 0, 0))],
            out_shape=[_sds(x.shape), _sds((x.shape[0], POOL_PAD, POOL_WIDTH))],
            scratch_shapes=[pltpu.VMEM((POOL_PAD + tm, POOL_WIDTH), F32)],
            compiler_params=_cp("parallel", "arbitrary"),
        )(x, o, r, u, *common)
    return pl.pallas_call(
        functools.partial(_even_out_sample_kernel, p0=p0),
        grid=grp.grid,
        in_specs=acts + [_full(hist)] + common_specs,
        out_specs=[grp.act(dm), _full(hist)],
        out_shape=[_sds(x.shape), _sds(hist.shape)],
        compiler_params=_cp("parallel", "arbitrary"),
    )(x, o, r, u, hist, *common)


def _rope_slab(x, cs, s1, s2, lead):
    shp = lead + (LANES,)
    back = pltpu.roll(x, LANES - MLA_ROPE // 2, 1).reshape(shp)
    fwd = pltpu.roll(x, MLA_ROPE // 2, 1).reshape(shp)
    out = x.reshape(shp) * cs + back * s1 + fwd * s2
    return out.reshape(x.shape)


def _odd_in_kernel(*refs, sample):
    (x_ref, sh_ref, sc_ref, g_ref, wcq_ref, wckv_ref, wkr_ref, wga_ref, wgg_ref, qn_ref, wuq_ref,
     kvn_ref, cs_ref, s1_ref, s2_ref) = refs[:15]
    if sample:
        wkl_ref, ckv_ref, kr_ref, uc_ref, q_ref, ql_ref = refs[15:]
    else:
        wuk_ref, wuvt_ref, ckv_ref, kr_ref, uc_ref, q_ref, k_ref, vt_ref = refs[15:]
    lead = x_ref.shape[:2]
    h = _modulated(x_ref, g_ref, sh_ref, sc_ref)
    cs, s1, s2 = cs_ref[...], s1_ref[...], s2_ref[...]

    cq = _rms(_dot(h, wcq_ref[...]), qn_ref[...]).astype(BF16)
    q = _dot(cq, wuq_ref[...]) * (MLA_SCALE * LOG2E)
    q = jnp.concatenate(
        [_rope_slab(q[:, LANES * i:LANES * (i + 1)], cs, s1, s2, lead) for i in range(MLA_HEADS)], axis=-1)
    q_ref[...] = q.astype(BF16).reshape(lead + (MLA_HEADS * LANES,))

    ckv = _rms(_dot(h, wckv_ref[...]), kvn_ref[...])
    ckv_ref[...] = ckv.reshape(lead + (MLA_KV_RANK,))
    kr = _rope_slab(_dot(h, wkr_ref[...]), cs, s1, s2, lead)
    kr_ref[...] = kr.reshape(lead + (LANES,))
    uc_ref[...] = (_dot(h, wga_ref[...]) * jax.nn.sigmoid(_dot(h, wgg_ref[...]))).reshape(lead + (CONV_CH,))

    if sample:
        qb = q.astype(BF16)
        for i in range(MLA_HEADS):
            ql = _dot(qb[:, LANES * i:LANES * (i + 1)], wkl_ref[i])
            ql_ref[:, :, MLA_KV_RANK * i:MLA_KV_RANK * (i + 1)] = ql.astype(BF16).reshape(lead + (MLA_KV_RANK,))
    else:
        cb = ckv.astype(BF16)
        kn = _dot(cb, wuk_ref[...])
        kn = jnp.concatenate([kn[:, LANES * i:LANES * (i + 1)] + kr for i in range(MLA_HEADS)], axis=-1)
        k_ref[...] = kn.astype(BF16).reshape(lead + (MLA_HEADS * LANES,))
        one_row = lax.broadcasted_iota(jnp.int32, (MLA_HEADS, VT_ROWS, 1), 1) == MLA_V
        ones = jnp.where(one_row, 1.0, 0.0).reshape(MLA_HEADS * VT_ROWS, 1)
        vt_ref[0] = (_dot_nt(wuvt_ref[...], cb) + ones).astype(BF16)


def _odd_in(grp, x, mod, g, w, tabs, sample):
    lead = x.shape[:2]
    hl = MLA_HEADS * LANES
    ins = [x, mod, mod, g, w['wcq'], w['wckv'], w['wkr'], w['wga'], w['wgg'], w['qn'], w['wuq'], w['kvn'], *tabs]
    specs = [grp.act(x.shape[-1]), grp.mod(3), grp.mod(4)] + [_full(a) for a in ins[3:12]] + [grp.pos()] * 3
    outs = [(MLA_KV_RANK, F32), (LANES, F32), (CONV_CH, F32), (hl, BF16)]
    if sample:
        ins.append(w['wkl'])
        outs.append((MLA_HEADS * MLA_KV_RANK, BF16))
    else:
        ins += [w['wuk'], w['wuvt']]
        outs.append((hl, BF16))
    specs += [_full(a) for a in ins[15:]]
    out_specs = [grp.act(wd) for wd, _ in outs]
    out_shape = [_sds(lead + (wd,), dt) for wd, dt in outs]
    if not sample:
        vt_rows = MLA_HEADS * VT_ROWS
        out_specs.append(pl.BlockSpec((1, vt_rows, grp.block[1]), lambda b, t: (b, 0, t)))
        out_shape.append(_sds((lead[0], vt_rows, lead[1]), BF16))
    return pl.pallas_call(
        functools.partial(_odd_in_kernel, sample=sample),
        grid=grp.grid,
        in_specs=specs,
        out_specs=out_specs,
        out_shape=out_shape,
        compiler_params=_cp("parallel", "parallel"),
    )(*ins)


def _attn_prompt_kernel(qi_ref, ki_ref, q_ref, k_ref, vt_ref, o_ref, m_scr, acc_scr, s_scr, p_scr, *, tq):
    step = pl.program_id(1)
    qi = qi_ref[step]
    ki = ki_ref[step]

    @pl.when(ki == 0)
    def _():
        m_scr[...] = jnp.full_like(m_scr, NEG)
        acc_scr[...] = jnp.zeros_like(acc_scr)

    tk = k_ref.shape[1]
    lead = qi * tq - ki * tk

    def update(lead_rows):
        diagonal = lead_rows is not None
        width = s_scr.shape[2]
        items = [(h, c) for h in range(MLA_HEADS) for c in range(0, tq, width)]

        def keys_used(c):
            return min(tk, lead_rows + c + width) if diagonal else tk

        def scores(item, buf):
            h, c = item
            nk = keys_used(c)
            sl = slice(LANES * h, LANES * (h + 1))
            st = _dot_nt(k_ref[0, :nk, sl], q_ref[0, c:c + width, sl])
            if diagonal:
                key = lax.broadcasted_iota(jnp.int32, (nk, width), 0)
                qry = lax.broadcasted_iota(jnp.int32, (nk, width), 1) + (lead_rows + c)
                st = jnp.where(key <= qry, st, NEG)
            s_scr[buf, :nk] = st

        def absorb(item, buf):
            h, c = item
            nk = keys_used(c)
            vr = slice(VT_ROWS * h, VT_ROWS * (h + 1))
            qc = slice(c, c + width)
            m_prev = m_scr[h:h + 1, qc]
            m_new = jnp.maximum(m_prev, jnp.max(s_scr[buf, :nk], axis=0, keepdims=True))
            p_scr[buf, :nk] = jnp.exp2(s_scr[buf, :nk] - m_new).astype(BF16)
            acc_scr[vr, qc] = (jnp.exp2(m_prev - m_new) * acc_scr[vr, qc]
                               + _dot(vt_ref[0, vr, :nk], p_scr[buf, :nk]))
            m_scr[h:h + 1, qc] = m_new

        depth = s_scr.shape[0]
        for i in range(depth - 1):
            scores(items[i], i)
        for i, item in enumerate(items):
            if i + depth - 1 < len(items):
                scores(items[i + depth - 1], (i + depth - 1) % depth)
            absorb(item, i % depth)

    def finish():
        for h in range(0, MLA_HEADS, 2):
            pair = []
            for hh in (h, h + 1):
                acc = acc_scr[VT_ROWS * hh:VT_ROWS * (hh + 1), :]
                pair.append(acc[:MLA_V] / acc[MLA_V:MLA_V + 1])
            o_ref[0, :, MLA_V * h:MLA_V * (h + 2)] = jnp.transpose(jnp.concatenate(pair, axis=0)).astype(BF16)

    @pl.when(lead >= tk)
    def _():
        update(None)

    for lead_rows in range(0, tk, tq):
        @pl.when(lead == lead_rows)
        def _():
            update(lead_rows)
            finish()


def _attn_prompt(q, k, vt, tq, tk):
    nb, nt, hl = q.shape
    vt_rows = vt.shape[1]
    n_out = MLA_HEADS * MLA_V
    assert nt % tq == 0 and nt % tk == 0 and tk % tq == 0
    width = min(ATTN_QCHUNK, tq)
    pairs = [(i, j) for i in range(nt // tq) for j in range((i * tq + tq - 1) // tk + 1)]
    qi_tab = jnp.asarray([p[0] for p in pairs], jnp.int32)
    ki_tab = jnp.asarray([p[1] for p in pairs], jnp.int32)
    return pl.pallas_call(
        functools.partial(_attn_prompt_kernel, tq=tq),
        grid_spec=pltpu.PrefetchScalarGridSpec(
            num_scalar_prefetch=2,
            grid=(nb, len(pairs)),
            in_specs=[pl.BlockSpec((1, tq, hl), lambda b, s, qt, kt: (b, qt[s], 0)),
                      pl.BlockSpec((1, tk, hl), lambda b, s, qt, kt: (b, kt[s], 0)),
                      pl.BlockSpec((1, vt_rows, tk), lambda b, s, qt, kt: (b, 0, kt[s]))],
            out_specs=pl.BlockSpec((1, tq, n_out), lambda b, s, qt, kt: (b, qt[s], 0)),
            scratch_shapes=[pltpu.VMEM((MLA_HEADS, tq), F32), pltpu.VMEM((vt_rows, tq), F32),
                            pltpu.VMEM((ATTN_DEPTH, tk, width), F32),
                            pltpu.VMEM((ATTN_DEPTH, tk, width), BF16)]),
        out_shape=_sds((nb, nt, n_out), BF16),
        compiler_params=_cp("parallel", "arbitrary"),
    )(qi_tab, ki_tab, q, k, vt)


def _attn_sample_kernel(pt_ref, ql_ref, qr_ref, cn_ref, kn_ref, ckv_hbm, kr_hbm, o_ref,
                        ckv_buf, kr_buf, pg_scr, s_scr, p_scr, sem, *, npg, layer):
    b = pl.program_id(0)
    nb = pl.num_programs(0)
    slot = lax.rem(b, 2)
    ql = ql_ref[0]
    qr = qr_ref[0]
    rows = ql.shape[0]

    def page_copies(page, i, sl):
        return (pltpu.make_async_copy(ckv_hbm.at[layer, page], ckv_buf.at[sl, i], sem.at[0, sl]),
                pltpu.make_async_copy(kr_hbm.at[layer, page], kr_buf.at[sl, i], sem.at[1, sl]))

    def start_page(batch, i, sl):
        latent, rotary = page_copies(pt_ref[batch, i], i, sl)
        latent.start()
        rotary.start(priority=1)

    def start_first(i, carry):
        start_page(0, i, 0)
        return carry

    @pl.when(b == 0)
    def _():
        lax.fori_loop(0, npg, start_first, 0)

    for i in range(npg):
        for cp in page_copies(0, i, slot):
            cp.wait()

    def score_pages(prefetch):
        for i in range(npg):
            if prefetch:
                for k in range(2 * i, min(2 * i + 2, npg)):
                    start_page(b + 1, k, 1 - slot)
            page = ckv_buf[slot, i].astype(BF16)
            pg_scr[i] = page
            s_scr[i] = _dot_nt(ql, page) + _dot(qr, kr_buf[slot, i].astype(BF16))

    @pl.when(b + 1 < nb)
    def _():
        score_pages(True)

    @pl.when(b + 1 == nb)
    def _():
        score_pages(False)

    n_new = cn_ref.shape[1]
    pad = jnp.zeros((LANES - n_new, MLA_KV_RANK), F32)
    cn = jnp.concatenate([cn_ref[0], pad], axis=0).astype(BF16)
    kn = jnp.concatenate([kn_ref[0], pad[:, :MLA_ROPE]], axis=0).astype(BF16)
    step = lax.shift_right_logical(lax.broadcasted_iota(jnp.int32, (rows, LANES), 0), MLA_HEADS.bit_length() - 1)
    col = lax.broadcasted_iota(jnp.int32, (rows, LANES), 1)
    s_new = jnp.where(col <= step, _dot_nt(ql, cn) + _dot_nt(qr, kn), NEG)

    s = s_scr[...]
    m = jnp.maximum(jnp.max(jnp.max(s, axis=0), axis=-1, keepdims=True),
                    jnp.max(s_new, axis=-1, keepdims=True))
    p = jnp.exp2(s - m)
    p_new = jnp.exp2(s_new - m)
    denom = jnp.sum(jnp.sum(p, axis=0), axis=-1, keepdims=True) + jnp.sum(p_new, axis=-1, keepdims=True)
    p_scr[...] = p.astype(BF16)

    def weigh_page(i, acc):
        return acc + _dot(p_scr[i], pg_scr[i])

    acc = lax.fori_loop(0, npg, weigh_page, _dot(p_new.astype(BF16), cn), unroll=True)
    o_ref[0] = acc / denom


def _attn_sample(page_table, ql, qr, cn, kn, cache_ckv, cache_kr, layer):
    nb, rows, rank = ql.shape
    npg = page_table.shape[1]
    page, rope = cache_ckv.shape[2], cache_kr.shape[2]
    assert page == LANES, "one cache page must fill one lane tile of scores"
    per_b = lambda a: pl.BlockSpec((1,) + a.shape[1:], lambda b, pt: (b, 0, 0))
    hbm = pl.BlockSpec(memory_space=pl.ANY)
    return pl.pallas_call(
        functools.partial(_attn_sample_kernel, npg=npg, layer=layer),
        grid_spec=pltpu.PrefetchScalarGridSpec(
            num_scalar_prefetch=1,
            grid=(nb,),
            in_specs=[per_b(ql), per_b(qr), per_b(cn), per_b(kn), hbm, hbm],
            out_specs=pl.BlockSpec((1, rows, rank), lambda b, pt: (b, 0, 0)),
            scratch_shapes=[pltpu.VMEM((2, npg, page, rank), F32),
                            pltpu.VMEM((2, npg, rope, page), F32),
                            pltpu.VMEM((npg, page, rank), BF16),
                            pltpu.VMEM((npg, rows, page), F32),
                            pltpu.VMEM((npg, rows, page), BF16),
                            pltpu.SemaphoreType.DMA((2, 2))]),
        out_shape=_sds((nb, rows, rank)),
        compiler_params=_cp("arbitrary"),
    )(page_table, ql, qr, cn, kn, cache_ckv, cache_kr)


def _conv_norm_act(cv, cb_ref, lg_ref, lb_ref):
    cv = cv + cb_ref[...]
    mu = jnp.mean(cv, axis=-1, keepdims=True)
    d = cv - mu
    y = d * lax.rsqrt(jnp.mean(d * d, axis=-1, keepdims=True) + EPS)
    return _silu(y * lg_ref[...] + lb_ref[...])


def _odd_out_prompt_kernel(x_ref, a_ref, uc_ref, gt_ref, cw_ref, cb_ref, lg_ref, lb_ref, woa_ref, woc_ref,
                           xo_ref, ho_ref, hb, hs, cvb, *, tm):
    t = pl.program_id(1)

    @pl.when(t == 0)
    def _():
        hb[0:CONV_PAD, :] = jnp.zeros((CONV_PAD, CONV_CH), F32)

    hb[CONV_PAD:CONV_PAD + tm, :] = uc_ref[0]
    for sft in range(1, SUBLANES):
        hs[sft - 1] = hb[sft:sft + hs.shape[1], :]
    base = CONV_PAD - CONV_HIST

    block = min(CONV_BLOCK, tm)
    for q0 in range(0, tm, block):
        ya = _dot(a_ref[0, q0:q0 + block, :], woa_ref[...])
        for r0 in range(q0, q0 + block, CONV_ROWS):
            acc = None
            for j in range(CONV_WIDTH):
                whole, sft = divmod(base + j, SUBLANES)
                rows = slice(r0 + whole * SUBLANES, r0 + whole * SUBLANES + CONV_ROWS)
                term = cw_ref[j:j + 1, :] * (hb[rows, :] if sft == 0 else hs[sft - 1, rows, :])
                acc = term if acc is None else acc + term
            cvb[r0:r0 + CONV_ROWS, :] = acc
        cv = _conv_norm_act(cvb[q0:q0 + block, :], cb_ref, lg_ref, lb_ref)
        y = ya + _dot(cv.astype(BF16), woc_ref[...])
        xo_ref[0, q0:q0 + block, :] = x_ref[0, q0:q0 + block, :] + gt_ref[...] * y
    hb[0:CONV_PAD, :] = hb[tm:tm + CONV_PAD, :]

    @pl.when(t == pl.num_programs(1) - 1)
    def _():
        ho_ref[0] = hb[0:CONV_PAD, :]


def _odd_out_sample_kernel(x_ref, lat_ref, uc_ref, hi_ref, gt_ref, cw_ref, cb_ref, lg_ref, lb_ref, wuv_ref,
                           woa_ref, woc_ref, xo_ref, hn_ref):
    steps, nb = uc_ref.shape[:2]

    def slab(i):
        return hi_ref[i] if i < CONV_HIST else uc_ref[i - CONV_HIST]

    rows = []
    for t in range(steps):
        cv = cw_ref[0:1, :] * slab(t)
        for j in range(1, CONV_WIDTH):
            cv = cv + cw_ref[j:j + 1, :] * slab(t + j)
        rows.append(cv)
    cv = _conv_norm_act(jnp.concatenate(rows, axis=0), cb_ref, lg_ref, lb_ref)
    lat = lat_ref[...].reshape(steps * nb, lat_ref.shape[-1]).astype(BF16)
    attn = _dot(lat, wuv_ref[...]).astype(BF16)
    y = _dot(attn, woa_ref[...]) + _dot(cv.astype(BF16), woc_ref[...])
    x = x_ref[...]
    xo_ref[...] = x + gt_ref[...] * y.reshape(x.shape)
    for i in range(CONV_HIST):
        hn_ref[i] = slab(steps + i)


def _odd_out(grp, x, a, uc, hist, mod, w):
    dm = x.shape[-1]
    conv = [w['cw'], w['cb'], w['lg'], w['lb']]
    if grp.prompt:
        tm = grp.block[1]
        ins = [x, a, uc, mod] + conv + [w['woa'], w['woc']]
        return pl.pallas_call(
            functools.partial(_odd_out_prompt_kernel, tm=tm),
            grid=grp.grid,
            in_specs=[grp.act(dm), grp.act(a.shape[-1]), grp.act(CONV_CH), grp.mod(5)] + [_full(v) for v in ins[4:]],
            out_specs=[grp.act(dm), pl.BlockSpec((1, CONV_PAD, CONV_CH), lambda b, t: (b, 0, 0))],
            out_shape=[_sds(x.shape), _sds((x.shape[0], CONV_PAD, CONV_CH))],
            scratch_shapes=[pltpu.VMEM((CONV_PAD + tm, CONV_CH), F32),
                            pltpu.VMEM((SUBLANES - 1, CONV_PAD + tm - SUBLANES, CONV_CH), F32),
                            pltpu.VMEM((tm, CONV_CH), F32)],
            compiler_params=_cp("parallel", "arbitrary"),
        )(*ins)
    ins = [x, a, uc, hist, mod] + conv + [w['wuv_bd'], w['woa'], w['woc']]
    return pl.pallas_call(
        _odd_out_sample_kernel,
        grid=grp.grid,
        in_specs=[grp.act(dm), grp.act(a.shape[-1]), grp.act(CONV_CH), _full(hist), grp.mod(5)]
        + [_full(v) for v in ins[5:]],
        out_specs=[grp.act(dm), _full(hist)],
        out_shape=[_sds(x.shape), _sds(hist.shape)],
        compiler_params=_cp("parallel", "arbitrary"),
    )(*ins)


def _head_pad(w, heads, width, offset=0):
    kdim = w.shape[0]
    w = w.reshape(kdim, heads, width)
    w = jnp.pad(w, ((0, 0), (0, 0), (offset, LANES - width - offset)))
    return w.reshape(kdim, heads * LANES)


def _rope_tables(pos):
    half = MLA_ROPE // 2
    freqs = ROPE_BASE ** (-np.arange(half, dtype=np.float64) / half)
    ang = np.asarray(pos, np.float64)[..., None] * freqs
    cos, sin = jnp.asarray(np.cos(ang), F32), jnp.asarray(np.sin(ang), F32)
    zeros = jnp.zeros_like(cos)
    lead = jnp.ones(pos.shape + (ROPE_LANE0,), F32)
    tail = jnp.zeros(pos.shape + (LANES - ROPE_LANE0 - MLA_ROPE,), F32)
    cs = jnp.concatenate([lead, cos, cos, tail], axis=-1)
    s1 = jnp.concatenate([0 * lead, -sin, zeros, tail], axis=-1)
    s2 = jnp.concatenate([0 * lead, zeros, sin, tail], axis=-1)
    return cs, s1, s2


def _even_weights(w_in, gate_w2, gate_b, out_norm, pool_w, pool_scale, w_out):
    hk = GLA_HEADS * GLA_DK
    hv = GLA_HEADS * GLA_DV
    n_main = 2 * hk + 2 * hv
    wg = jnp.pad(w_in[:, n_main:n_main + GLA_GATE_RANK], ((0, 0), (0, LANES - GLA_GATE_RANK)))
    w2 = jnp.pad(gate_w2, ((0, LANES - GLA_GATE_RANK), (0, 0)))
    return dict(wq=w_in[:, :n_main].astype(BF16), wg=wg.astype(BF16), w2=w2.astype(BF16),
                gb=gate_b[None], wu=w_in[:, n_main + GLA_GATE_RANK:].astype(BF16),
                on=out_norm[None], pw=pool_w.astype(BF16), ps=pool_scale[None], wo=w_out.astype(BF16))


def _odd_weights(w_in, q_norm, w_uq, kv_norm, w_uk, w_uv, conv_w, conv_b, ln_g, ln_b, w_out):
    c0, c1, c2 = MLA_Q_RANK, MLA_Q_RANK + MLA_KV_RANK, MLA_Q_RANK + MLA_KV_RANK + MLA_ROPE
    n_attn = MLA_HEADS * MLA_V
    wkr = jnp.pad(w_in[:, c1:c2], ((0, 0), (ROPE_LANE0, LANES - ROPE_LANE0 - MLA_ROPE)))
    wuk = w_uk.reshape(MLA_KV_RANK, MLA_HEADS * MLA_NOPE)
    wkl = jnp.pad(jnp.transpose(w_uk, (1, 2, 0)), ((0, 0), (0, LANES - MLA_NOPE), (0, 0)))
    eye = jnp.eye(MLA_HEADS, dtype=F32)
    wuv_bd = (eye[:, None, :, None] * jnp.transpose(w_uv, (1, 0, 2))[:, :, None, :]).reshape(
        MLA_HEADS * MLA_KV_RANK, n_attn)
    wuvt = jnp.pad(jnp.transpose(w_uv, (1, 2, 0)), ((0, 0), (0, VT_ROWS - MLA_V), (0, 0))).reshape(
        MLA_HEADS * VT_ROWS, MLA_KV_RANK)
    return dict(wcq=w_in[:, :c0].astype(BF16), wckv=w_in[:, c0:c1].astype(BF16), wkr=wkr.astype(BF16),
                wga=w_in[:, c2:c2 + CONV_CH].astype(BF16), wgg=w_in[:, c2 + CONV_CH:].astype(BF16),
                qn=q_norm[None], wuq=_head_pad(w_uq, MLA_HEADS, MLA_NOPE + MLA_ROPE).astype(BF16),
                kvn=kv_norm[None], wuk=_head_pad(wuk, MLA_HEADS, MLA_NOPE).astype(BF16),
                wuvt=wuvt.astype(BF16), wkl=wkl.astype(BF16),
                wuv_bd=wuv_bd.astype(BF16), cw=conv_w, cb=conv_b[None], lg=ln_g[None], lb=ln_b[None],
                woa=w_out[:n_attn].astype(BF16), woc=w_out[n_attn:].astype(BF16))


def _tm(x):
    return jnp.swapaxes(x, 0, 1)


def kernel(x_prompt, x_sample, state_gla, state_pool, cache_ckv, cache_krope, state_conv, page_table, c_prompt, c_sample, ada_w, ada_b, norm_g, ffn_w1, ffn_w3, ffn_w2, ev_w_in, ev_gate_w2, ev_gate_b, ev_out_norm, ev_pool_w, ev_pool_scale, ev_w_out, od_w_in, od_q_norm, od_w_uq, od_kv_norm, od_w_uk, od_w_uv, od_conv_w, od_conv_b, od_conv_norm_g, od_conv_norm_b, od_w_out, final_norm):
    nbp, seq, dm = x_prompt.shape
    nbs, steps, _ = x_sample.shape
    depth = ada_w.shape[0]
    past_len = page_table.shape[1] * cache_ckv.shape[2]
    tile = min(TOKEN_TILE, seq)
    gp = _Group(True, nbp, seq, tile, dm)
    gf = _Group(True, nbp, seq, min(FFN_TOKEN_TILE, seq), dm)
    gs = _Group(False, steps, nbs, nbs, dm)

    n_c = nbp + nbs
    c_all = jnp.pad(jnp.concatenate([c_sample, c_prompt], axis=0), ((0, -n_c % SUBLANES), (0, 0)))
    mod = _ada(c_all, ada_w, ada_b)
    mod_p = mod[:, :, nbs:n_c].reshape(depth, N_MOD, nbp, 1, dm)
    mod_s = mod if nbs % SUBLANES == 0 else mod[:, :, :nbs]

    w1, w3, w2 = ffn_w1.astype(BF16), ffn_w3.astype(BF16), ffn_w2.astype(BF16)
    tabs_p = _rope_tables(np.arange(seq)[None])
    tabs_s = _rope_tables(past_len + np.arange(steps)[:, None])

    xp = x_prompt
    xs = _tm(x_sample)
    gla_p, gla_s, pool_p, pool_s, ckv_p, ckv_s, kr_p, kr_s, conv_p, conv_s = ([] for _ in range(10))
    gla_tt = min(TOKEN_TILE, seq)

    for layer in range(depth):
        i = layer // 2
        ng = norm_g[layer]
        last = layer == depth - 1
        xp = _ffn(gf, xp, mod_p[layer], 0, ng[0:1], w1, w3, w2, (layer, 0))
        xs = _ffn(gs, xs, mod_s[layer], 0, ng[0:1], w1, w3, w2, (layer, 0))
        if layer % 2 == 0:
            w = _even_weights(ev_w_in[i], ev_gate_w2[i], ev_gate_b[i], ev_out_norm[i], ev_pool_w[i],
                              ev_pool_scale[i], ev_w_out[i])
            proj = (ng[1:2], w['wq'], w['wg'], w['w2'], w['gb'], w['wu'])
            out_w = (w['on'], w['pw'], w['ps'], w['wo'])
            q, k, v, r, la, u = _even_in(gp, xp, mod_p[layer], *proj)
            o, s_fin = _gla(q, k, v, la, None, nbp, gla_tt)
            xp, hist = _even_out(gp, xp, o, r, u, None, mod_p[layer], *out_w, 0)
            gla_p.append(s_fin.reshape(nbp, GLA_HEADS, GLA_DK, GLA_DV))
            pool_p.append(hist[:, POOL_PAD - POOL_HIST:])
            q, k, v, r, la, u = _even_in(gs, xs, mod_s[layer], *proj)
            rows8 = steps + (-steps % SUBLANES)
            chunked = lambda a: jnp.pad(_tm(a), ((0, 0), (0, rows8 - steps), (0, 0)))
            s0 = state_gla[i].reshape(nbs, GLA_HEADS // 2, LANES, GLA_DV)
            o, s_fin = _gla(chunked(q), chunked(k), chunked(v), chunked(la), s0,
                            GLA_SAMPLE_BATCHES if nbs % GLA_SAMPLE_BATCHES == 0 else 1, rows8)
            xs, hist = _even_out(gs, xs, _tm(o[:, :steps]), r, u, _tm(state_pool[i]), mod_s[layer], *out_w, past_len)
            gla_s.append(s_fin.reshape(nbs, GLA_HEADS, GLA_DK, GLA_DV))
            pool_s.append(_tm(hist))
        else:
            w = _odd_weights(od_w_in[i], od_q_norm[i], od_w_uq[i], od_kv_norm[i], od_w_uk[i], od_w_uv[i],
                             od_conv_w[i], od_conv_b[i], od_conv_norm_g[i], od_conv_norm_b[i], od_w_out[i])
            rope_lanes = slice(ROPE_LANE0, ROPE_LANE0 + MLA_ROPE)
            ckv, kr, uc, q, k, v = _odd_in(gp, xp, mod_p[layer], ng[1:2], w, tabs_p, False)
            attn = _attn_prompt(q, k, v, min(ATTN_TILE, seq), min(ATTN_KEYS, seq))
            xp, hist = _odd_out(gp, xp, attn, uc, None, mod_p[layer], w)
            ckv_p.append(ckv)
            kr_p.append(kr[..., rope_lanes])
            conv_p.append(hist[:, CONV_PAD - CONV_HIST:])
            ckv, kr, uc, q, ql = _odd_in(gs, xs, mod_s[layer], ng[1:2], w, tabs_s, True)
            kr = kr[..., rope_lanes]
            qr = q.reshape(steps, nbs, MLA_HEADS, LANES)[..., rope_lanes]
            qr = _tm(qr).reshape(nbs, steps * MLA_HEADS, MLA_ROPE)
            ql = _tm(ql).reshape(nbs, steps * MLA_HEADS, MLA_KV_RANK)
            pad8 = lambda a: jnp.pad(_tm(a), ((0, 0), (0, -steps % 8), (0, 0)))
            cache_kr_t = jnp.swapaxes(cache_krope, 2, 3)
            lat = _attn_sample(page_table, ql, qr, pad8(ckv), pad8(kr), cache_ckv, cache_kr_t, i)
            lat = _tm(lat.reshape(nbs, steps, MLA_HEADS * MLA_KV_RANK))
            xs, hist = _odd_out(gs, xs, lat, uc, _tm(state_conv[i]), mod_s[layer], w)
            ckv_s.append(_tm(ckv))
            kr_s.append(_tm(kr))
            conv_s.append(_tm(hist))
        fin = final_norm[None] if last else None
        xp = _ffn(gf, xp, mod_p[layer], 6, ng[2:3], w1, w3, w2, (layer, 1), fin)
        xs = _ffn(gs, xs, mod_s[layer], 6, ng[2:3], w1, w3, w2, (layer, 1), fin)

    st = jnp.stack
    return (xp, _tm(xs), st(gla_p), st(gla_s), st(pool_p), st(pool_s), st(ckv_p), st(ckv_s),
            st(kr_p), st(kr_s), st(conv_p), st(conv_s))
```
